```python
import jax, jax.numpy as jnp
from jax import lax
import numpy as np

D_MODEL = 1024
BATCH = 16
SEQ = 256
DEPTH = 4
DEC_BATCH = 8
DEC_SEQ = 2048
PAST_LEN = 512

GRID_W = 64
BLK = 128
WINDOW = 128
HEAD_DIM = 64
A_WIDTH = D_MODEL // 4
A_GROUPS = 4
A_GROUP_DIM = A_WIDTH // A_GROUPS
B_WIDTH = D_MODEL - A_WIDTH
N_Q_HEADS = B_WIDTH // HEAD_DIM
N_KV_HEADS = 4
Q_PER_KV = N_Q_HEADS // N_KV_HEADS
KV_WIDTH = N_KV_HEADS * HEAD_DIM
IN_AB = A_WIDTH + B_WIDTH + 2 * KV_WIDTH
MIX_WIDTH = A_WIDTH + B_WIDTH
C_WIDTH = D_MODEL
C_GROUPS = 8
C_GROUP_DIM = C_WIDTH // C_GROUPS
CHUNK = 128
N_EXPERTS = 32
TOP_K = 4
D_FF = D_MODEL
SWIGLU_LIMIT = 7.0
SWIGLU_ALPHA = 1.702
ROPE_THETA = 10000.0
N_EVEN = (DEPTH + 1) // 2
N_ODD = DEPTH // 2
DEEPNORM_ALPHA = (2 * DEPTH) ** 0.25
DEEPNORM_BETA = (8 * DEPTH) ** -0.25
LN_EPS = 1e-6
NEG_INF = -1e30

kernel_name = 'hybrid_fnet_swa_gmlp_moe_diffusion_step'


def layer_norm(x, g=None, b=None):
    xf = x.astype(jnp.float32)
    mu = jnp.mean(xf, axis=-1, keepdims=True)
    xc = xf - mu
    var = jnp.mean(xc * xc, axis=-1, keepdims=True)
    y = xc * lax.rsqrt(var + LN_EPS)
    if g is not None:
        y = y * g.astype(jnp.float32) + b.astype(jnp.float32)
    return y.astype(x.dtype)


def rope_1d(x, pos):
    n = x.shape[-1] // 2
    freqs = ROPE_THETA ** (-jnp.arange(n, dtype=jnp.float32) / n)
    ang = pos[:, None] * freqs[None, :]
    cos = jnp.cos(ang)[:, None, :]
    sin = jnp.sin(ang)[:, None, :]
    xf = x.astype(jnp.float32)
    x1, x2 = xf[..., :n], xf[..., n:]
    return jnp.concatenate([x1 * cos - x2 * sin, x2 * cos + x1 * sin], axis=-1).astype(x.dtype)


def axial_rope(x):
    n_tok = x.shape[1]
    rows = n_tok // GRID_W
    pos_r = jnp.repeat(jnp.arange(rows, dtype=jnp.float32), GRID_W)
    pos_c = jnp.tile(jnp.arange(GRID_W, dtype=jnp.float32), rows)
    half = HEAD_DIM // 2
    return jnp.concatenate([rope_1d(x[..., :half], pos_r), rope_1d(x[..., half:], pos_c)], axis=-1)


def fourier_mix(a):
    return jnp.fft.fft2(a.astype(jnp.float32), axes=(1, 3), norm='ortho').real.astype(a.dtype)


def window_blocks(t):
    b, s = t.shape[:2]
    nb = s // BLK
    tp = jnp.pad(t, ((0, 0), (BLK, BLK), (0, 0), (0, 0))).reshape(b, nb + 2, BLK, t.shape[2], t.shape[3])
    win = jnp.concatenate([tp[:, :-2], tp[:, 1:-1], tp[:, 2:]], axis=2)
    return jnp.moveaxis(win, 1, 0)


def window_mask(n_tok):
    nb = n_tok // BLK
    blocks = jnp.arange(nb, dtype=jnp.int32)[:, None]
    qpos = blocks * BLK + jnp.arange(BLK, dtype=jnp.int32)[None, :]
    kpos = (blocks - 1) * BLK + jnp.arange(3 * BLK, dtype=jnp.int32)[None, :]
    inside = (kpos >= 0) & (kpos < n_tok)
    near = jnp.abs(qpos[:, :, None] - kpos[:, None, :]) <= WINDOW
    return inside[:, None, :] & near


def attend(qb, k_ctx, v_ctx, sink, k_loc=None, v_loc=None, m_loc=None):
    scale = HEAD_DIM ** -0.5
    b = qb.shape[0]
    s_ctx = jnp.einsum('bqkgd,bpkd->bkgqp', qb, k_ctx).astype(jnp.float32) * scale
    parts = [s_ctx]
    if k_loc is not None:
        s_loc = jnp.einsum('bqkgd,blkd->bkgql', qb, k_loc).astype(jnp.float32) * scale
        s_loc = jnp.where(m_loc[None, None, None], s_loc, NEG_INF)
        parts = [s_loc, s_ctx]
    sink_b = jnp.broadcast_to(sink.astype(jnp.float32).reshape(1, N_KV_HEADS, Q_PER_KV, 1, 1),
                              (b, N_KV_HEADS, Q_PER_KV, qb.shape[1], 1))
    p = jax.nn.softmax(jnp.concatenate(parts + [sink_b], axis=-1), axis=-1).astype(v_ctx.dtype)
    n_ctx = k_ctx.shape[1]
    if k_loc is None:
        return jnp.einsum('bkgqp,bpkd->bqkgd', p[..., :n_ctx], v_ctx)
    n_loc = k_loc.shape[1]
    out = jnp.einsum('bkgql,blkd->bqkgd', p[..., :n_loc], v_loc)
    return out + jnp.einsum('bkgqp,bpkd->bqkgd', p[..., n_loc:n_loc + n_ctx], v_ctx)


def mixer_ab(h, w_in, w_out, sink, k_ctx=None, v_ctx=None):
    b, s, _ = h.shape
    nb = s // BLK
    proj = h @ w_in
    a = proj[..., :A_WIDTH].reshape(b, s, A_GROUPS, A_GROUP_DIM)
    q = proj[..., A_WIDTH:A_WIDTH + B_WIDTH].reshape(b, s, N_Q_HEADS, HEAD_DIM)
    k = proj[..., A_WIDTH + B_WIDTH:A_WIDTH + B_WIDTH + KV_WIDTH].reshape(b, s, N_KV_HEADS, HEAD_DIM)
    v = proj[..., A_WIDTH + B_WIDTH + KV_WIDTH:].reshape(b, s, N_KV_HEADS, HEAD_DIM)
    y_a = fourier_mix(a).reshape(b, s, A_WIDTH)
    if k_ctx is None:
        q_blocks = jnp.moveaxis(q.reshape(b, nb, BLK, N_KV_HEADS, Q_PER_KV, HEAD_DIM), 1, 0)
        o = lax.map(lambda qb: attend(qb, k, v, sink), q_blocks)
        ctx_kv = (k, v)
    else:
        q = axial_rope(q)
        k = axial_rope(k)
        q_blocks = jnp.moveaxis(q.reshape(b, nb, BLK, N_KV_HEADS, Q_PER_KV, HEAD_DIM), 1, 0)
        xs = (q_blocks, window_blocks(k), window_blocks(v), window_mask(s))
        o = lax.map(lambda t: attend(t[0], k_ctx, v_ctx, sink, t[1], t[2], t[3]), xs)
        ctx_kv = None
    y_b = jnp.moveaxis(o, 0, 1).reshape(b, s, B_WIDTH)
    return jnp.concatenate([y_a, y_b], axis=-1) @ w_out, ctx_kv


def mixer_c(h, w_in, b_in, g_v, b_v, w_sp, b_sp, w_out):
    b, s, _ = h.shape
    z = jax.nn.gelu(h @ w_in + b_in, approximate=False)
    u, v = z[..., :C_WIDTH], z[..., C_WIDTH:]
    v = layer_norm(v, g_v, b_v)
    vc = v.reshape(b, s // CHUNK, CHUNK, C_GROUPS, C_GROUP_DIM)
    mixed = jnp.einsum('gpq,bnqgd->bnpgd', w_sp, vc) + b_sp.T[None, None, :, :, None]
    return (u * mixed.reshape(b, s, C_WIDTH)) @ w_out


def moe(h, w_r, b_r, w_gu, b_gu, w_dn, b_dn):
    b, s, d = h.shape
    n_tok = b * s
    xt = h.reshape(n_tok, d)
    logits = (xt @ w_r + b_r).astype(jnp.float32)
    top_vals, top_idx = lax.top_k(logits, TOP_K)
    gates = jax.nn.softmax(top_vals, axis=-1)
    flat_e = top_idx.reshape(-1)
    flat_g = gates.reshape(-1)
    order = jnp.argsort(flat_e)
    sorted_e = flat_e[order]
    tok = order // TOP_K
    counts = jnp.zeros((N_EXPERTS,), jnp.int32).at[flat_e].add(1)
    padded = (counts + BLK - 1) // BLK * BLK
    start = jnp.cumsum(counts) - counts
    pend = jnp.cumsum(padded)
    pstart = pend - padded
    n_assign = n_tok * TOP_K
    rank = jnp.arange(n_assign, dtype=jnp.int32) - start[sorted_e]
    dest = pstart[sorted_e] + rank
    n_rows = n_assign + N_EXPERTS * BLK
    n_blocks = n_rows // BLK
    buf = jnp.zeros((n_rows, d), h.dtype).at[dest].set(xt[tok])
    blk_e = jnp.minimum(jnp.searchsorted(pend, jnp.arange(n_blocks, dtype=jnp.int32) * BLK, side='right'),
                        N_EXPERTS - 1)

    def expert_block(args):
        xb, e = args
        gu = xb @ w_gu[e] + b_gu[e]
        gate = jnp.minimum(gu[..., :D_FF], SWIGLU_LIMIT)
        lin = jnp.clip(gu[..., D_FF:], -SWIGLU_LIMIT, SWIGLU_LIMIT)
        glu = gate * jax.nn.sigmoid(SWIGLU_ALPHA * gate)
        return ((lin + 1.0) * glu) @ w_dn[e] + b_dn[e]

    out = lax.map(expert_block, (buf.reshape(n_blocks, BLK, d), blk_e)).reshape(n_rows, d)
    y = jnp.zeros((n_tok, d), h.dtype).at[tok].add(out[dest] * flat_g[order][:, None].astype(h.dtype))
    return y.reshape(b, s, d)


def run_stream(x, cond, cache_k, cache_v, p):
    ks, vs = [], []
    for l in range(DEPTH):
        j = l // 2
        mod = (jax.nn.silu(cond) @ p['w_ada'][l] + p['b_ada'][l])[:, None, :]
        sh_m, sc_m, g_m, sh_f, sc_f, g_f = jnp.split(mod, 6, axis=-1)
        hm = layer_norm(x) * (1.0 + sc_m) + sh_m
        if l % 2 == 0:
            kc = None if cache_k is None else cache_k[:, j]
            vc = None if cache_v is None else cache_v[:, j]
            y, ctx_kv = mixer_ab(hm, p['w_in_ab'][j], p['w_out_ab'][j], p['sink_ab'][j], kc, vc)
            if ctx_kv is not None:
                ks.append(ctx_kv[0])
                vs.append(ctx_kv[1])
        else:
            y = mixer_c(hm, p['w_in_c'][j], p['b_in_c'][j], p['ln_v_g'][j], p['ln_v_b'][j],
                        p['w_sp'][j], p['b_sp'][j], p['w_out_c'][j])
        x = layer_norm(DEEPNORM_ALPHA * x + g_m * y, p['ln_mix_g'][l], p['ln_mix_b'][l])
        hf = layer_norm(x) * (1.0 + sc_f) + sh_f
        y = moe(hf, p['w_router'][l], p['b_router'][l], p['w_gu'][l], p['b_gu'][l], p['w_dn'][l], p['b_dn'][l])
        x = layer_norm(DEEPNORM_ALPHA * x + g_f * y, p['ln_ffn_g'][l], p['ln_ffn_b'][l])
    return x, ks, vs


def setup_inputs(seed: int = 0) -> dict:
    key = jax.random.key(seed)
    ks = jax.random.split(key, 32)

    def nrm(k, shape, scale):
        return jax.random.normal(k, shape, jnp.float32) * scale

    kv_shape = (DEC_BATCH, N_EVEN, PAST_LEN, N_KV_HEADS, HEAD_DIM)
    return {
        'x_prompt': nrm(ks[0], (BATCH, SEQ, D_MODEL), 1.0),
        'x_sample': nrm(ks[1], (DEC_BATCH, DEC_SEQ, D_MODEL), 1.0),
        'cache_k_ab': nrm(ks[2], kv_shape, 1.0),
        'cache_v_ab': nrm(ks[3], kv_shape, 1.0),
        'c': nrm(ks[4], (DEC_BATCH, D_MODEL), 1.0),
        'c_ctx': nrm(ks[5], (D_MODEL,), 1.0),
        'w_ada': nrm(ks[6], (DEPTH, D_MODEL, 6 * D_MODEL), 0.5 * D_MODEL ** -0.5),
        'b_ada': nrm(ks[7], (DEPTH, 6 * D_MODEL), 0.02),
        'ln_mix_g': 1.0 + nrm(ks[8], (DEPTH, D_MODEL), 0.02),
        'ln_mix_b': nrm(ks[9], (DEPTH, D_MODEL), 0.02),
        'ln_ffn_g': 1.0 + nrm(ks[10], (DEPTH, D_MODEL), 0.02),
        'ln_ffn_b': nrm(ks[11], (DEPTH, D_MODEL), 0.02),
        'w_in_ab': nrm(ks[12], (N_EVEN, D_MODEL, IN_AB), D_MODEL ** -0.5),
        'w_out_ab': nrm(ks[13], (N_EVEN, MIX_WIDTH, D_MODEL), DEEPNORM_BETA * MIX_WIDTH ** -0.5),
        'sink_ab': nrm(ks[14], (N_EVEN, N_Q_HEADS), 0.5),
        'w_in_c': nrm(ks[15], (N_ODD, D_MODEL, 2 * C_WIDTH), D_MODEL ** -0.5),
        'b_in_c': nrm(ks[16], (N_ODD, 2 * C_WIDTH), 0.02),
        'ln_v_g': 1.0 + nrm(ks[17], (N_ODD, C_WIDTH), 0.02),
        'ln_v_b': nrm(ks[18], (N_ODD, C_WIDTH), 0.02),
        'w_sp': nrm(ks[19], (N_ODD, C_GROUPS, CHUNK, CHUNK), 0.5 * CHUNK ** -0.5),
        'b_sp': 1.0 + nrm(ks[20], (N_ODD, C_GROUPS, CHUNK), 0.1),
        'w_out_c': nrm(ks[21], (N_ODD, C_WIDTH, D_MODEL), DEEPNORM_BETA * C_WIDTH ** -0.5),
        'w_router': nrm(ks[22], (DEPTH, D_MODEL, N_EXPERTS), D_MODEL ** -0.5),
        'b_router': nrm(ks[23], (DEPTH, N_EXPERTS), 0.01),
        'w_gu': nrm(ks[24], (DEPTH, N_EXPERTS, D_MODEL, 2 * D_FF), D_MODEL ** -0.5),
        'b_gu': nrm(ks[25], (DEPTH, N_EXPERTS, 2 * D_FF), 0.02),
        'w_dn': nrm(ks[26], (DEPTH, N_EXPERTS, D_FF, D_MODEL), DEEPNORM_BETA * D_FF ** -0.5),
        'b_dn': nrm(ks[27], (DEPTH, N_EXPERTS, D_MODEL), 0.02),
    }


def reference(x_prompt, x_sample, cache_k_ab, cache_v_ab, c, c_ctx, w_ada, b_ada, ln_mix_g, ln_mix_b,
              ln_ffn_g, ln_ffn_b, w_in_ab, w_out_ab, sink_ab, w_in_c, b_in_c, ln_v_g, ln_v_b, w_sp, b_sp,
              w_out_c, w_router, b_router, w_gu, b_gu, w_dn, b_dn):
    p = {'w_ada': w_ada, 'b_ada': b_ada, 'ln_mix_g': ln_mix_g, 'ln_mix_b': ln_mix_b,
         'ln_ffn_g': ln_ffn_g, 'ln_ffn_b': ln_ffn_b, 'w_in_ab': w_in_ab, 'w_out_ab': w_out_ab,
         'sink_ab': sink_ab, 'w_in_c': w_in_c, 'b_in_c': b_in_c, 'ln_v_g': ln_v_g, 'ln_v_b': ln_v_b,
         'w_sp': w_sp, 'b_sp': b_sp, 'w_out_c': w_out_c, 'w_router': w_router, 'b_router': b_router,
         'w_gu': w_gu, 'b_gu': b_gu, 'w_dn': w_dn, 'b_dn': b_dn}
    y_prompt, ks, vs = run_stream(x_prompt, c_ctx[None, :], None, None, p)
    state_k_ab = jnp.stack(ks, axis=1)
    state_v_ab = jnp.stack(vs, axis=1)
    y_sample, _, _ = run_stream(x_sample, c, cache_k_ab, cache_v_ab, p)
    return (y_prompt, y_sample, state_k_ab, state_v_ab)
```

```python
import functools
import math

import numpy as np
import jax
import jax.numpy as jnp
from jax import lax
from jax.experimental import pallas as pl
from jax.experimental.pallas import tpu as pltpu

GRID_W = 64
BLK = 128
HEAD_DIM = 64
A_GROUPS = 4
N_KV_HEADS = 4
C_GROUPS = 8
CHUNK = 128
N_EXPERTS = 32
TOP_K = 4
SWIGLU_LIMIT = 7.0
SWIGLU_ALPHA = 1.702
ROPE_THETA = 10000.0
LN_EPS = 1e-6
NEG_INF = -1e30

LANES = 128
TM = 256
TM_MOE = 256
VMEM_LIMIT = 52 * 1024 * 1024

F32 = jnp.float32
BF16 = jnp.bfloat16


def _cparams(n_axes):
    return pltpu.CompilerParams(dimension_semantics=("arbitrary",) * n_axes,
                                vmem_limit_bytes=VMEM_LIMIT)


def _ln(x):
    mu = jnp.mean(x, axis=-1, keepdims=True)
    xc = x - mu
    var = jnp.mean(xc * xc, axis=-1, keepdims=True)
    return xc * lax.rsqrt(var + LN_EPS)


def _dot(a, b):
    return jnp.dot(a, b, preferred_element_type=F32)


def _dot_nt(a, b):
    return lax.dot_general(a, b, (((1,), (1,)), ((), ())), preferred_element_type=F32)


def _split(a):
    hi = a.astype(BF16)
    lo = (a - hi.astype(F32)).astype(BF16)
    return hi, lo


def _dot_3pass(a, b):
    a_hi, a_lo = _split(a)
    b_hi, b_lo = _split(b)
    return _dot(a_hi, b_hi) + (_dot(a_hi, b_lo) + _dot(a_lo, b_hi))


def _lane_select(cols, width):
    m = cols[0].shape[0]
    lane = lax.broadcasted_iota(jnp.int32, (m, width), 1)
    out = jnp.zeros((m, width), cols[0].dtype)
    for j, c in enumerate(cols):
        out = jnp.where(lane == j, c, out)
    return out


def _token_tail(x, y, g_m, ln_g, ln_b, sc_f, sh_f, w_r, b_r, alpha, x1_ref, hf_ref, idx_ref, gate_ref):
    x1 = _ln(alpha * x + g_m * y) * ln_g + ln_b
    x1_ref[...] = x1
    hf = _ln(x1) * (1.0 + sc_f) + sh_f
    hf_ref[...] = hf.astype(BF16)
    logits = _dot_3pass(hf, w_r) + b_r
    lane = lax.broadcasted_iota(jnp.int32, logits.shape, 1)
    vals = logits
    top_v, top_i = [], []
    for _ in range(TOP_K):
        m = jnp.max(vals, axis=-1, keepdims=True)
        am = jnp.min(jnp.where(vals == m, lane, N_EXPERTS), axis=-1, keepdims=True)
        top_v.append(m)
        top_i.append(am)
        vals = jnp.where(lane == am, -jnp.inf, vals)
    e = [jnp.exp(v - top_v[0]) for v in top_v]
    denom = e[0] + e[1] + e[2] + e[3]
    idx_ref[...] = _lane_select(top_i, LANES)
    gate_ref[...] = _lane_select([ek / denom for ek in e], LANES)


def _ada_kernel(cond_ref, w_ref, b_ref, o_ref):
    c = cond_ref[...]
    s = (c * jax.nn.sigmoid(c)).astype(BF16)
    o_ref[...] = _dot(s, w_ref[...].astype(BF16)) + b_ref[...]


def _ada(cond16, w_ada, b_ada):
    depth, d, n = w_ada.shape
    tn = 1536
    return pl.pallas_call(
        _ada_kernel,
        grid=(depth, n // tn),
        in_specs=[
            pl.BlockSpec((16, d), lambda l, j: (0, 0)),
            pl.BlockSpec((None, d, tn), lambda l, j: (l, 0, j)),
            pl.BlockSpec((None, 1, tn), lambda l, j: (l, 0, j)),
        ],
        out_specs=pl.BlockSpec((None, 16, tn), lambda l, j: (l, 0, j)),
        out_shape=jax.ShapeDtypeStruct((depth, 16, n), F32),
        compiler_params=_cparams(2),
        name="ada_mod",
    )(cond16, w_ada, b_ada.reshape(depth, 1, n))


def _inproj_kernel(rows_ref, rblk_ref, x_ref, sc_ref, sh_ref, w_ref, cos_ref, sa_ref, sb_ref, bdc_ref, bds_ref,
                   ac_ref, as_ref, q_ref, k_ref, v_ref, *, a_w, q_w, kv_w):
    del rows_ref, rblk_ref
    h = (_ln(x_ref[...]) * (1.0 + sc_ref[...]) + sh_ref[...]).astype(BF16)
    p = _dot(h, w_ref[...])
    a = p[:, :a_w].astype(BF16)
    ac_ref[...] = _dot(a, bdc_ref[...]).astype(BF16)
    as_ref[...] = _dot(a, bds_ref[...]).astype(BF16)

    cos, sa, sb = cos_ref[...], sa_ref[...], sb_ref[...]

    def rope(t):
        w = t.shape[1]
        reps = w // LANES
        c, a_, b_ = (jnp.tile(z, (1, reps)) for z in (cos, sa, sb))
        nxt = pltpu.roll(t, w - HEAD_DIM // 4, 1)
        prv = pltpu.roll(t, HEAD_DIM // 4, 1)
        return t * c + nxt * a_ + prv * b_

    q = rope(p[:, a_w:a_w + q_w])
    q_ref[...] = (q * (HEAD_DIM ** -0.5)).astype(BF16)
    k_ref[...] = rope(p[:, a_w + q_w:a_w + q_w + kv_w])
    v_ref[...] = p[:, a_w + q_w + kv_w:]


def _inproj(x, mod3, mod_base, rows, rblk, w_in, cos, sa, sb, bdc, bds, a_w, q_w, kv_w):
    t, d = x.shape
    n = w_in.shape[1]
    nt = t // TM
    mod_spec = lambda j: pl.BlockSpec((None, 1, d), lambda i, r, rb: (mod_base + j * 16 + r[i], 0, 0))
    whole = lambda shp: pl.BlockSpec(shp, lambda i, r, rb: (0,) * len(shp))
    rope_spec = pl.BlockSpec((TM, LANES), lambda i, r, rb: (rb[i], 0))
    tok = lambda w: pl.BlockSpec((TM, w), lambda i, r, rb: (i, 0))
    grid_spec = pltpu.PrefetchScalarGridSpec(
        num_scalar_prefetch=2, grid=(nt,),
        in_specs=[tok(d), mod_spec(1), mod_spec(0), whole((d, n)), rope_spec, rope_spec, rope_spec,
                  whole((a_w, a_w)), whole((a_w, a_w))],
        out_specs=[tok(a_w), tok(a_w), tok(q_w), tok(kv_w), tok(kv_w)],
    )
    return pl.pallas_call(
        functools.partial(_inproj_kernel, a_w=a_w, q_w=q_w, kv_w=kv_w),
        grid_spec=grid_spec,
        out_shape=[jax.ShapeDtypeStruct((t, a_w), BF16), jax.ShapeDtypeStruct((t, a_w), BF16),
                   jax.ShapeDtypeStruct((t, q_w), BF16), jax.ShapeDtypeStruct((t, kv_w), F32),
                   jax.ShapeDtypeStruct((t, kv_w), F32)],
        compiler_params=_cparams(1),
        name="inproj_ab",
    )(rows, rblk, x, mod3, mod3, w_in, cos, sa, sb, bdc, bds)


def _fourier_kernel(c_ref, s_ref, ac_ref, as_ref, o_ref, *, scale):
    y = _dot(c_ref[...], ac_ref[...]) - _dot(s_ref[...], as_ref[...])
    o_ref[...] = (y * scale).astype(BF16)


def _fourier(ac, as_, cs, ss, n_batch, seq, blk_off, tm):
    a_w = ac.shape[1]
    nr = seq // tm
    a_spec = pl.BlockSpec((seq, a_w), lambda r, b: (blk_off + b, 0))
    t_spec = pl.BlockSpec((tm, seq), lambda r, b: (r, 0))
    return pl.pallas_call(
        functools.partial(_fourier_kernel, scale=(seq * (a_w // A_GROUPS)) ** -0.5),
        grid=(nr, n_batch),
        in_specs=[t_spec, t_spec, a_spec, a_spec],
        out_specs=pl.BlockSpec((tm, a_w), lambda r, b: (b * nr + r, 0)),
        out_shape=jax.ShapeDtypeStruct((n_batch * seq, a_w), BF16),
        compiler_params=_cparams(2),
        name="fourier_%d" % seq,
    )(cs, ss, ac, as_)


def _softmax_pv(scores, sink_col, values):
    m = sink_col
    for s in scores:
        m = jnp.maximum(m, jnp.max(s, axis=-1, keepdims=True))
    denom = jnp.exp(sink_col - m)
    acc = None
    for s, v in zip(scores, values):
        e = jnp.exp(s - m)
        denom = denom + jnp.sum(e, axis=-1, keepdims=True)
        pv = _dot(e.astype(BF16), v)
        acc = pv if acc is None else acc + pv
    return acc / denom


def _sink_col(sink_ref, head0, g, rows):
    return jnp.concatenate([jnp.full((rows, 1), sink_ref[head0 + j], F32) for j in range(g)], axis=0)


def _attn_ctx_kernel(sink_ref, q_ref, k_ref, v_ref, o_ref, *, g):
    s_len = q_ref.shape[0]
    for h in range(N_KV_HEADS):
        sl = slice(h * HEAD_DIM, (h + 1) * HEAD_DIM)
        kh = k_ref[:, sl].astype(BF16)
        vh = v_ref[:, sl].astype(BF16)
        q3 = jnp.concatenate([q_ref[:, (h * g + j) * HEAD_DIM:(h * g + j + 1) * HEAD_DIM] for j in range(g)], axis=0)
        o = _softmax_pv([_dot_nt(q3, kh)], _sink_col(sink_ref, h * g, g, s_len), [vh])
        for j in range(g):
            o_ref[:, (h * g + j) * HEAD_DIM:(h * g + j + 1) * HEAD_DIM] = o[j * s_len:(j + 1) * s_len].astype(BF16)


def _attn_ctx(sink, q, k, v, n_batch, seq):
    q_w, kv_w = q.shape[1], k.shape[1]
    g = q_w // kv_w
    return pl.pallas_call(
        functools.partial(_attn_ctx_kernel, g=g),
        grid=(n_batch,),
        in_specs=[pl.BlockSpec(memory_space=pltpu.SMEM),
                  pl.BlockSpec((seq, q_w), lambda b: (b, 0)),
                  pl.BlockSpec((seq, kv_w), lambda b: (b, 0)),
                  pl.BlockSpec((seq, kv_w), lambda b: (b, 0))],
        out_specs=pl.BlockSpec((seq, q_w), lambda b: (b, 0)),
        out_shape=jax.ShapeDtypeStruct((n_batch * seq, q_w), BF16),
        compiler_params=_cparams(1),
        name="attn_ctx",
    )(sink, q, k, v)


def _attn_lat_kernel(sink_ref, q_ref, k_ref, v_ref, ck_ref, cv_ref, o_ref, *, g, nb):
    i = pl.program_id(1)
    row = lax.broadcasted_iota(jnp.int32, (BLK, BLK), 0)
    col = lax.broadcasted_iota(jnp.int32, (BLK, BLK), 1)
    masks = [jnp.logical_and(col >= row, i >= 1), None, jnp.logical_and(col <= row, i <= nb - 2)]
    starts = [pl.multiple_of(jnp.clip(i + d, 0, nb - 1) * BLK, BLK) for d in (-1, 0, 1)]
    for h in range(N_KV_HEADS):
        sl = slice(h * HEAD_DIM, (h + 1) * HEAD_DIM)
        q3 = jnp.concatenate([q_ref[:, (h * g + j) * HEAD_DIM:(h * g + j + 1) * HEAD_DIM] for j in range(g)], axis=0)
        scores, values = [], []
        for st, mk in zip(starts, masks):
            kb = k_ref[pl.ds(st, BLK), sl].astype(BF16)
            s = _dot_nt(q3, kb)
            if mk is not None:
                s = jnp.where(jnp.concatenate([mk] * g, axis=0), s, NEG_INF)
            scores.append(s)
            values.append(v_ref[pl.ds(st, BLK), sl].astype(BF16))
        scores.append(_dot_nt(q3, ck_ref[:, sl].astype(BF16)))
        values.append(cv_ref[:, sl].astype(BF16))
        o = _softmax_pv(scores, _sink_col(sink_ref, h * g, g, BLK), values)
        for j in range(g):
            o_ref[:, (h * g + j) * HEAD_DIM:(h * g + j + 1) * HEAD_DIM] = o[j * BLK:(j + 1) * BLK].astype(BF16)


def _attn_lat(sink, q, k, v, cache_k, cache_v, layer_slot, n_batch, seq, tok_off):
    q_w, kv_w = q.shape[1], k.shape[1]
    g = q_w // kv_w
    nb = seq // BLK
    past = cache_k.shape[2]
    kv_spec = pl.BlockSpec((seq, kv_w), lambda b, i: (tok_off // seq + b, 0))
    c_spec = pl.BlockSpec((None, None, past, kv_w), lambda b, i: (b, layer_slot, 0, 0))
    return pl.pallas_call(
        functools.partial(_attn_lat_kernel, g=g, nb=nb),
        grid=(n_batch, nb),
        in_specs=[pl.BlockSpec(memory_space=pltpu.SMEM),
                  pl.BlockSpec((BLK, q_w), lambda b, i: (tok_off // BLK + b * nb + i, 0)),
                  kv_spec, kv_spec, c_spec, c_spec],
        out_specs=pl.BlockSpec((BLK, q_w), lambda b, i: (b * nb + i, 0)),
        out_shape=jax.ShapeDtypeStruct((n_batch * seq, q_w), BF16),
        compiler_params=_cparams(2),
        name="attn_lat",
    )(sink, q, k, v, cache_k, cache_v)


def _post_ab_kernel(rows_ref, x_ref, yap_ref, ybp_ref, yas_ref, ybs_ref, w_ref, gm_ref, lg_ref, lb_ref,
                    scf_ref, shf_ref, wr_ref, br_ref, x1_ref, hf_ref, idx_ref, gate_ref, *, n_ctx_tiles, alpha):
    del rows_ref
    is_ctx = pl.program_id(0) < n_ctx_tiles
    ya = jnp.where(is_ctx, yap_ref[...], yas_ref[...])
    yb = jnp.where(is_ctx, ybp_ref[...], ybs_ref[...])
    a_w = ya.shape[1]
    y = _dot(ya, w_ref[:a_w, :]) + _dot(yb, w_ref[a_w:, :])
    _token_tail(x_ref[...], y, gm_ref[...], lg_ref[...], lb_ref[...], scf_ref[...], shf_ref[...],
                wr_ref[...], br_ref[...], alpha, x1_ref, hf_ref, idx_ref, gate_ref)


def _tail_out(t, d):
    shapes = [jax.ShapeDtypeStruct((t, d), F32), jax.ShapeDtypeStruct((t, d), BF16),
              jax.ShapeDtypeStruct((t, LANES), jnp.int32), jax.ShapeDtypeStruct((t, LANES), F32)]
    specs = [pl.BlockSpec((TM, d), lambda i, r: (i, 0)), pl.BlockSpec((TM, d), lambda i, r: (i, 0)),
             pl.BlockSpec((TM, LANES), lambda i, r: (i, 0)), pl.BlockSpec((TM, LANES), lambda i, r: (i, 0))]
    return shapes, specs


def _post_ab(x, ya_p, yb_p, ya_s, yb_s, w_out, mod3, mod_base, rows, ln_g, ln_b, w_r, b_r, alpha):
    t, d = x.shape
    nt = t // TM
    n_ctx_tiles = ya_p.shape[0] // TM
    a_w, q_w = ya_p.shape[1], yb_p.shape[1]
    mod_spec = lambda j: pl.BlockSpec((None, 1, d), lambda i, r: (mod_base + j * 16 + r[i], 0, 0))
    whole = lambda shp: pl.BlockSpec(shp, lambda i, r: (0,) * len(shp))
    ctx = lambda w: pl.BlockSpec((TM, w), lambda i, r: (jnp.minimum(i, n_ctx_tiles - 1), 0))
    lat = lambda w: pl.BlockSpec((TM, w), lambda i, r: (jnp.maximum(i - n_ctx_tiles, 0), 0))
    shapes, specs = _tail_out(t, d)
    grid_spec = pltpu.PrefetchScalarGridSpec(
        num_scalar_prefetch=1, grid=(nt,),
        in_specs=[pl.BlockSpec((TM, d), lambda i, r: (i, 0)), ctx(a_w), ctx(q_w), lat(a_w), lat(q_w),
                  whole(w_out.shape), mod_spec(2), whole((1, d)), whole((1, d)), mod_spec(4), mod_spec(3),
                  whole(w_r.shape), whole(b_r.shape)],
        out_specs=specs,
    )
    return pl.pallas_call(
        functools.partial(_post_ab_kernel, n_ctx_tiles=n_ctx_tiles, alpha=alpha),
        grid_spec=grid_spec, out_shape=shapes, compiler_params=_cparams(1), name="post_ab",
    )(rows, x, ya_p, yb_p, ya_s, yb_s, w_out, mod3, ln_g, ln_b, mod3, mod3, w_r, b_r)


def _gmlp_kernel(rows_ref, x_ref, scm_ref, shm_ref, win_ref, bin_ref, gv_ref, bv_ref, wsp_ref, bsp_ref, wout_ref,
                 gm_ref, lg_ref, lb_ref, scf_ref, shf_ref, wr_ref, br_ref, x1_ref, hf_ref, idx_ref, gate_ref,
                 *, alpha):
    del rows_ref
    x = x_ref[...]
    h = (_ln(x) * (1.0 + scm_ref[...]) + shm_ref[...]).astype(BF16)
    z = _dot(h, win_ref[...]) + bin_ref[...]
    z = 0.5 * z * (1.0 + lax.erf(z * (2.0 ** -0.5)))
    c_w = z.shape[1] // 2
    u = z[:, :c_w]
    v = (_ln(z[:, c_w:]) * gv_ref[...] + bv_ref[...]).astype(BF16)
    gd = c_w // C_GROUPS
    chunks = []
    for n in range(x.shape[0] // CHUNK):
        groups = []
        for g in range(C_GROUPS):
            vg = v[n * CHUNK:(n + 1) * CHUNK, g * gd:(g + 1) * gd]
            groups.append(_dot(wsp_ref[g], vg) + bsp_ref[g])
        chunks.append(jnp.concatenate(groups, axis=1))
    mixed = jnp.concatenate(chunks, axis=0)
    y = _dot((u * mixed).astype(BF16), wout_ref[...])
    _token_tail(x, y, gm_ref[...], lg_ref[...], lb_ref[...], scf_ref[...], shf_ref[...],
                wr_ref[...], br_ref[...], alpha, x1_ref, hf_ref, idx_ref, gate_ref)


def _gmlp(x, mod3, mod_base, rows, w_in, b_in, g_v, b_v, w_sp, b_sp, w_out, ln_g, ln_b, w_r, b_r, alpha):
    t, d = x.shape
    nt = t // TM
    mod_spec = lambda j: pl.BlockSpec((None, 1, d), lambda i, r: (mod_base + j * 16 + r[i], 0, 0))
    whole = lambda shp: pl.BlockSpec(shp, lambda i, r: (0,) * len(shp))
    shapes, specs = _tail_out(t, d)
    grid_spec = pltpu.PrefetchScalarGridSpec(
        num_scalar_prefetch=1, grid=(nt,),
        in_specs=[pl.BlockSpec((TM, d), lambda i, r: (i, 0)), mod_spec(1), mod_spec(0),
                  whole(w_in.shape), whole(b_in.shape), whole(g_v.shape), whole(b_v.shape),
                  whole(w_sp.shape), whole(b_sp.shape), whole(w_out.shape),
                  mod_spec(2), whole((1, d)), whole((1, d)), mod_spec(4), mod_spec(3),
                  whole(w_r.shape), whole(b_r.shape)],
        out_specs=specs,
    )
    return pl.pallas_call(
        functools.partial(_gmlp_kernel, alpha=alpha),
        grid_spec=grid_spec, out_shape=shapes, compiler_params=_cparams(1), name="gmlp",
    )(rows, x, mod3, mod3, w_in, b_in, g_v, b_v, w_sp, b_sp, w_out, mod3, ln_g, ln_b, mod3, mod3, w_r, b_r)


def _moe_kernel(be_ref, bf_ref, nu_ref, x_ref, wgu_ref, bgu_ref, wdn_ref, bdn_ref, o_ref, wgu_s, wdn_s):
    i = pl.program_id(0)
    d_ff = wdn_ref.shape[0]

    @pl.when(i < nu_ref[0])
    def _():
        @pl.when(bf_ref[i] == 1)
        def _():
            wgu_s[...] = wgu_ref[...].astype(BF16)
            wdn_s[...] = wdn_ref[...].astype(BF16)

        gu = _dot(x_ref[...], wgu_s[...]) + bgu_ref[...]
        gate = jnp.minimum(gu[:, :d_ff], SWIGLU_LIMIT)
        lin = jnp.clip(gu[:, d_ff:], -SWIGLU_LIMIT, SWIGLU_LIMIT)
        glu = gate * jax.nn.sigmoid(SWIGLU_ALPHA * gate)
        hid = ((lin + 1.0) * glu).astype(BF16)
        o_ref[...] = _dot(hid, wdn_s[...]) + bdn_ref[...]


def _moe(xs, blk_e, blk_first, n_used, w_gu, b_gu, w_dn, b_dn, layer):
    n_rows, d = xs.shape
    n_e, _, ff2 = w_gu.shape[1:]
    d_ff = ff2 // 2
    nt = n_rows // TM_MOE
    row_spec = pl.BlockSpec((TM_MOE, d), lambda i, be, bf, nu: (jnp.minimum(i, nu[0] - 1), 0))
    grid_spec = pltpu.PrefetchScalarGridSpec(
        num_scalar_prefetch=3, grid=(nt,),
        in_specs=[row_spec,
                  pl.BlockSpec((None, None, d, ff2), lambda i, be, bf, nu: (layer, be[i], 0, 0)),
                  pl.BlockSpec((None, None, 1, ff2), lambda i, be, bf, nu: (layer, be[i], 0, 0)),
                  pl.BlockSpec((None, None, d_ff, d), lambda i, be, bf, nu: (layer, be[i], 0, 0)),
                  pl.BlockSpec((None, None, 1, d), lambda i, be, bf, nu: (layer, be[i], 0, 0))],
        out_specs=row_spec,
        scratch_shapes=[pltpu.VMEM((d, ff2), BF16), pltpu.VMEM((d_ff, d), BF16)],
    )
    depth = w_gu.shape[0]
    return pl.pallas_call(
        _moe_kernel, grid_spec=grid_spec,
        out_shape=jax.ShapeDtypeStruct((n_rows, d), F32),
        compiler_params=_cparams(1), name="moe_experts",
    )(blk_e, blk_first, n_used, xs, w_gu, b_gu.reshape(depth, n_e, 1, ff2), w_dn, b_dn.reshape(depth, n_e, 1, d))


def _combine_kernel(rows_ref, x_ref, y4_ref, gate_ref, gf_ref, lg_ref, lb_ref, o_ref, *, alpha):
    del rows_ref
    d = x_ref.shape[1]
    gate = gate_ref[...]
    y = gate[:, 0:1] * y4_ref[:, 0:d]
    for k in range(1, TOP_K):
        y = y + gate[:, k:k + 1] * y4_ref[:, k * d:(k + 1) * d]
    o_ref[...] = _ln(alpha * x_ref[...] + gf_ref[...] * y) * lg_ref[...] + lb_ref[...]


def _combine(x1, y4, gate, mod3, mod_base, rows, ln_g, ln_b, alpha):
    t, d = x1.shape
    nt = t // TM
    grid_spec = pltpu.PrefetchScalarGridSpec(
        num_scalar_prefetch=1, grid=(nt,),
        in_specs=[pl.BlockSpec((TM, d), lambda i, r: (i, 0)),
                  pl.BlockSpec((TM, TOP_K * d), lambda i, r: (i, 0)),
                  pl.BlockSpec((TM, LANES), lambda i, r: (i, 0)),
                  pl.BlockSpec((None, 1, d), lambda i, r: (mod_base + 5 * 16 + r[i], 0, 0)),
                  pl.BlockSpec((1, d), lambda i, r: (0, 0)), pl.BlockSpec((1, d), lambda i, r: (0, 0))],
        out_specs=pl.BlockSpec((TM, d), lambda i, r: (i, 0)),
    )
    return pl.pallas_call(
        functools.partial(_combine_kernel, alpha=alpha), grid_spec=grid_spec,
        out_shape=jax.ShapeDtypeStruct((t, d), F32), compiler_params=_cparams(1), name="moe_combine",
    )(rows, x1, y4, gate, mod3, ln_g, ln_b)


def _route_plan(idx):
    t = idx.shape[0]
    flat_e = idx.reshape(-1)
    n_assign = flat_e.shape[0]
    onehot = (flat_e[:, None] == jnp.arange(N_EXPERTS, dtype=jnp.int32)[None, :]).astype(jnp.int32)
    csum = jnp.cumsum(onehot, axis=0)
    rank = jnp.sum(jnp.where(onehot > 0, csum - 1, 0), axis=1)
    counts = csum[-1]
    padded = (counts + TM_MOE - 1) // TM_MOE * TM_MOE
    pend = jnp.cumsum(padded)
    pstart = pend - padded
    dest = pstart[flat_e] + rank
    n_rows = n_assign + N_EXPERTS * TM_MOE
    n_tiles = n_rows // TM_MOE
    row_tok = jnp.zeros((n_rows,), jnp.int32).at[dest].set(jnp.arange(n_assign, dtype=jnp.int32) // TOP_K)
    n_used = (pend[-1] // TM_MOE).astype(jnp.int32)
    tile_start = jnp.arange(n_tiles, dtype=jnp.int32) * TM_MOE
    blk_e = jnp.minimum(jnp.searchsorted(pend, tile_start, side='right'), N_EXPERTS - 1).astype(jnp.int32)
    last_e = blk_e[jnp.maximum(n_used - 1, 0)]
    blk_e = jnp.where(jnp.arange(n_tiles) < n_used, blk_e, last_e)
    blk_first = jnp.concatenate([jnp.ones((1,), jnp.int32), (blk_e[1:] != blk_e[:-1]).astype(jnp.int32)])
    return dest.reshape(t, TOP_K), row_tok, blk_e, blk_first, n_used.reshape(1)


def _dft_tables(n):
    j = jnp.arange(n, dtype=jnp.int32)
    ang = ((j[:, None] * j[None, :]) % n).astype(F32) * (2.0 * math.pi / n)
    return jnp.cos(ang), jnp.sin(ang)


def _rope_tables(n_lat, n_ctx_rows):
    half, quarter = HEAD_DIM // 2, HEAD_DIM // 4
    tpos = jnp.arange(n_lat, dtype=jnp.int32)
    lane = jnp.arange(LANES, dtype=jnp.int32) % HEAD_DIM
    pos = jnp.where(lane[None, :] < half, (tpos // GRID_W)[:, None], (tpos % GRID_W)[:, None]).astype(F32)
    fidx = (lane % quarter).astype(F32)
    freqs = ROPE_THETA ** (-fidx / quarter)
    ang = pos * freqs[None, :]
    cos, sin = jnp.cos(ang), jnp.sin(ang)
    first = (lane % half) < quarter
    sa = jnp.where(first[None, :], -sin, 0.0)
    sb = jnp.where(first[None, :], 0.0, sin)
    ident = lambda v: jnp.full((n_ctx_rows, LANES), v, F32)
    return (jnp.concatenate([ident(1.0), cos]), jnp.concatenate([ident(0.0), sa]), jnp.concatenate([ident(0.0), sb]))


def kernel(x_prompt, x_sample, cache_k_ab, cache_v_ab, c, c_ctx, w_ada, b_ada, ln_mix_g, ln_mix_b, ln_ffn_g, ln_ffn_b, w_in_ab, w_out_ab, sink_ab, w_in_c, b_in_c, ln_v_g, ln_v_b, w_sp, b_sp, w_out_c, w_router, b_router, w_gu, b_gu, w_dn, b_dn):
    n_ctx_b, ctx_seq, d = x_prompt.shape
    n_lat_b, lat_seq, _ = x_sample.shape
    depth = w_ada.shape[0]
    t_ctx, t_lat = n_ctx_b * ctx_seq, n_lat_b * lat_seq
    t = t_ctx + t_lat
    alpha = (2 * depth) ** 0.25
    kv_w = N_KV_HEADS * HEAD_DIM
    a_w = d // 4
    q_w = d - a_w
    assert ctx_seq % TM == 0 and lat_seq % TM == 0 and n_lat_b + 1 <= 16

    rows_np = np.concatenate([np.zeros(t_ctx // TM, np.int32), 1 + np.arange(t_lat // TM, dtype=np.int32) // (lat_seq // TM)])
    rblk_np = np.concatenate([np.zeros(t_ctx // TM, np.int32), 1 + np.arange(t_lat // TM, dtype=np.int32) % (lat_seq // TM)])
    rows, rblk = jnp.asarray(rows_np), jnp.asarray(rblk_np)

    cond16 = jnp.zeros((16, d), F32).at[0].set(c_ctx).at[1:1 + n_lat_b].set(c)
    mod = _ada(cond16, w_ada, b_ada)
    mod3 = mod.reshape(depth, 16, 6, d).transpose(0, 2, 1, 3).reshape(depth * 6 * 16, 1, d)

    cos, sa, sb = _rope_tables(lat_seq, TM)
    cs_ctx, ss_ctx = (z.astype(BF16) for z in _dft_tables(ctx_seq))
    cs_lat, ss_lat = (z.astype(BF16) for z in _dft_tables(lat_seq))
    gd = a_w // A_GROUPS
    cd, sd = _dft_tables(gd)
    eye = jnp.eye(A_GROUPS, dtype=F32)
    bdc, bds = jnp.kron(eye, cd).astype(BF16), jnp.kron(eye, sd).astype(BF16)

    past = cache_k_ab.shape[2]
    cache_k = cache_k_ab.reshape(n_lat_b, -1, past, kv_w)
    cache_v = cache_v_ab.reshape(n_lat_b, -1, past, kv_w)

    x = jnp.concatenate([x_prompt.reshape(t_ctx, d), x_sample.reshape(t_lat, d)], axis=0)
    row2 = lambda v: v.reshape(1, -1)
    ks, vs = [], []
    for l in range(depth):
        j = l // 2
        mod_base = l * 6 * 16
        w_r, b_r = w_router[l], row2(b_router[l])
        if l % 2 == 0:
            ac, as_, q, k, v = _inproj(x, mod3, mod_base, rows, rblk, w_in_ab[j].astype(BF16), cos, sa, sb,
                                       bdc, bds, a_w, q_w, kv_w)
            ks.append(k[:t_ctx].reshape(n_ctx_b, ctx_seq, N_KV_HEADS, HEAD_DIM))
            vs.append(v[:t_ctx].reshape(n_ctx_b, ctx_seq, N_KV_HEADS, HEAD_DIM))
            ya_p = _fourier(ac, as_, cs_ctx, ss_ctx, n_ctx_b, ctx_seq, 0, min(ctx_seq, 512))
            ya_s = _fourier(ac, as_, cs_lat, ss_lat, n_lat_b, lat_seq, t_ctx // lat_seq, min(lat_seq, 512))
            yb_p = _attn_ctx(sink_ab[j], q, k, v, n_ctx_b, ctx_seq)
            yb_s = _attn_lat(sink_ab[j], q, k, v, cache_k, cache_v, j, n_lat_b, lat_seq, t_ctx)
            x1, hf, idx, gate = _post_ab(x, ya_p, yb_p, ya_s, yb_s, w_out_ab[j].astype(BF16), mod3, mod_base, rows,
                                         row2(ln_mix_g[l]), row2(ln_mix_b[l]), w_r, b_r, alpha)
        else:
            x1, hf, idx, gate = _gmlp(x, mod3, mod_base, rows, w_in_c[j].astype(BF16), row2(b_in_c[j]),
                                      row2(ln_v_g[j]), row2(ln_v_b[j]), w_sp[j].astype(BF16), b_sp[j][:, :, None],
                                      w_out_c[j].astype(BF16), row2(ln_mix_g[l]), row2(ln_mix_b[l]), w_r, b_r, alpha)
        dest, row_tok, blk_e, blk_first, n_used = _route_plan(idx[:, :TOP_K])
        xs = jnp.take(hf, row_tok, axis=0)
        out_sorted = _moe(xs, blk_e, blk_first, n_used, w_gu, b_gu, w_dn, b_dn, l)
        y4 = jnp.take(out_sorted, dest.reshape(-1), axis=0).reshape(t, TOP_K * d)
        x = _combine(x1, y4, gate, mod3, mod_base, rows, row2(ln_ffn_g[l]), row2(ln_ffn_b[l]), alpha)

    y_prompt = x[:t_ctx].reshape(n_ctx_b, ctx_seq, d)
    y_sample = x[t_ctx:].reshape(n_lat_b, lat_seq, d)
    return (y_prompt, y_sample, jnp.stack(ks, axis=1), jnp.stack(vs, axis=1))
```

```python
import functools
import math

import numpy as np
import jax
import jax.numpy as jnp
from jax import lax
from jax.experimental import pallas as pl
from jax.experimental.pallas import tpu as pltpu
from jax.experimental.pallas import tpu_sc as plsc

GRID_W = 64
BLK = 128
HEAD_DIM = 64
A_GROUPS = 4
N_KV_HEADS = 4
C_GROUPS = 8
CHUNK = 128
N_EXPERTS = 32
TOP_K = 4
SWIGLU_LIMIT = 7.0
SWIGLU_ALPHA = 1.702
ROPE_THETA = 10000.0
LN_EPS = 1e-6
NEG_INF = -1e30

LANES = 128
TM = 256
TM_MOE = 256
VMEM_LIMIT = 52 * 1024 * 1024

F32 = jnp.float32
BF16 = jnp.bfloat16


def _cparams(n_axes):
    return pltpu.CompilerParams(dimension_semantics=("arbitrary",) * n_axes,
                                vmem_limit_bytes=VMEM_LIMIT)


def _ln(x):
    mu = jnp.mean(x, axis=-1, keepdims=True)
    xc = x - mu
    var = jnp.mean(xc * xc, axis=-1, keepdims=True)
    return xc * lax.rsqrt(var + LN_EPS)


def _dot(a, b):
    return jnp.dot(a, b, preferred_element_type=F32)


def _dot_nt(a, b):
    return lax.dot_general(a, b, (((1,), (1,)), ((), ())), preferred_element_type=F32)


def _split(a):
    hi = a.astype(BF16)
    lo = (a - hi.astype(F32)).astype(BF16)
    return hi, lo


def _dot_3pass(a, b):
    a_hi, a_lo = _split(a)
    b_hi, b_lo = _split(b)
    return _dot(a_hi, b_hi) + (_dot(a_hi, b_lo) + _dot(a_lo, b_hi))


def _lane_select(cols, width):
    m = cols[0].shape[0]
    lane = lax.broadcasted_iota(jnp.int32, (m, width), 1)
    out = jnp.zeros((m, width), cols[0].dtype)
    for j, c in enumerate(cols):
        out = jnp.where(lane == j, c, out)
    return out


def _pack_bf16_pairs(v):
    n = v.shape[1] // 2
    lo = pltpu.bitcast(v[:, :n].astype(BF16).astype(F32), jnp.uint32) >> 16
    hi = pltpu.bitcast(v[:, n:].astype(BF16).astype(F32), jnp.uint32) & jnp.uint32(0xFFFF0000)
    return pltpu.bitcast(lo | hi, jnp.int32)


def _unpack_bf16_pairs(p):
    u = pltpu.bitcast(p, jnp.uint32)
    lo = pltpu.bitcast(u << 16, F32)
    hi = pltpu.bitcast(u & jnp.uint32(0xFFFF0000), F32)
    return jnp.concatenate([lo, hi], axis=1).astype(BF16)


def _token_tail(x, y, g_m, ln_g, ln_b, sc_f, sh_f, w_r, b_r, alpha, x1_ref, hf_ref, route_ref, gate_ref,
                cnt_ref, carry_ref):
    @pl.when(pl.program_id(0) == 0)
    def _():
        carry_ref[...] = jnp.zeros_like(carry_ref)

    x1 = _ln(alpha * x + g_m * y) * ln_g + ln_b
    x1_ref[...] = x1
    hf = _ln(x1) * (1.0 + sc_f) + sh_f
    hf_ref[...] = _pack_bf16_pairs(hf)
    logits = _dot_3pass(hf, w_r) + b_r
    tm = logits.shape[0]
    lane = lax.broadcasted_iota(jnp.int32, logits.shape, 1)
    vals = logits
    top_v, top_i = [], []
    for _ in range(TOP_K):
        m = jnp.max(vals, axis=-1, keepdims=True)
        am = jnp.min(jnp.where(vals == m, lane, N_EXPERTS), axis=-1, keepdims=True)
        top_v.append(m)
        top_i.append(am)
        vals = jnp.where(lane == am, -jnp.inf, vals)
    e = [jnp.exp(v - top_v[0]) for v in top_v]
    denom = e[0] + e[1] + e[2] + e[3]
    gate_ref[...] = _lane_select([ek / denom for ek in e], LANES)

    member = jnp.zeros(logits.shape, F32)
    for am in top_i:
        member = jnp.where(lane == am, 1.0, member)
    r_i = lax.broadcasted_iota(jnp.int32, (tm, tm), 0)
    c_i = lax.broadcasted_iota(jnp.int32, (tm, tm), 1)
    earlier = jnp.where(c_i < r_i, 1.0, 0.0).astype(BF16)
    before = _dot(earlier, member.astype(BF16)) + carry_ref[...]
    ranks = [jnp.sum(jnp.where(lane == am, before, 0.0), axis=-1, keepdims=True).astype(jnp.int32) for am in top_i]
    route_ref[...] = _lane_select(top_i + ranks, LANES)
    carry = carry_ref[...] + jnp.sum(member, axis=0, keepdims=True)
    carry_ref[...] = carry
    cnt_ref[...] = jnp.broadcast_to(carry, cnt_ref.shape)


def _ada_kernel(cond_ref, w_ref, b_ref, o_ref):
    c = cond_ref[...]
    s = (c * jax.nn.sigmoid(c)).astype(BF16)
    o_ref[...] = _dot(s, w_ref[...].astype(BF16)) + b_ref[...]


def _ada(cond16, w_ada, b_ada):
    depth, d, n = w_ada.shape
    tn = 1536
    return pl.pallas_call(
        _ada_kernel,
        grid=(depth, n // tn),
        in_specs=[
            pl.BlockSpec((16, d), lambda l, j: (0, 0)),
            pl.BlockSpec((None, d, tn), lambda l, j: (l, 0, j)),
            pl.BlockSpec((None, 1, tn), lambda l, j: (l, 0, j)),
        ],
        out_specs=pl.BlockSpec((None, 16, tn), lambda l, j: (l, 0, j)),
        out_shape=jax.ShapeDtypeStruct((depth, 16, n), F32),
        compiler_params=_cparams(2),
        name="ada_mod",
    )(cond16, w_ada, b_ada.reshape(depth, 1, n))


def _inproj_kernel(rows_ref, rblk_ref, x_ref, sc_ref, sh_ref, w_ref, cos_ref, sa_ref, sb_ref, bdc_ref, bds_ref,
                   ac_ref, as_ref, q_ref, k_ref, v_ref, *, a_w, q_w, kv_w):
    del rows_ref, rblk_ref
    h = (_ln(x_ref[...]) * (1.0 + sc_ref[...]) + sh_ref[...]).astype(BF16)
    p = _dot(h, w_ref[...])
    a = p[:, :a_w].astype(BF16)
    ac_ref[...] = _dot(a, bdc_ref[...]).astype(BF16)
    as_ref[...] = _dot(a, bds_ref[...]).astype(BF16)

    cos, sa, sb = cos_ref[...], sa_ref[...], sb_ref[...]

    def rope(t):
        w = t.shape[1]
        reps = w // LANES
        c, a_, b_ = (jnp.tile(z, (1, reps)) for z in (cos, sa, sb))
        nxt = pltpu.roll(t, w - HEAD_DIM // 4, 1)
        prv = pltpu.roll(t, HEAD_DIM // 4, 1)
        return t * c + nxt * a_ + prv * b_

    q = rope(p[:, a_w:a_w + q_w])
    q_ref[...] = (q * (HEAD_DIM ** -0.5)).astype(BF16)
    k_ref[...] = rope(p[:, a_w + q_w:a_w + q_w + kv_w])
    v_ref[...] = p[:, a_w + q_w + kv_w:]


def _inproj(x, mod3, mod_base, rows, rblk, w_in, cos, sa, sb, bdc, bds, a_w, q_w, kv_w):
    t, d = x.shape
    n = w_in.shape[1]
    nt = t // TM
    mod_spec = lambda j: pl.BlockSpec((None, 1, d), lambda i, r, rb: (mod_base + j * 16 + r[i], 0, 0))
    whole = lambda shp: pl.BlockSpec(shp, lambda i, r, rb: (0,) * len(shp))
    rope_spec = pl.BlockSpec((TM, LANES), lambda i, r, rb: (rb[i], 0))
    tok = lambda w: pl.BlockSpec((TM, w), lambda i, r, rb: (i, 0))
    grid_spec = pltpu.PrefetchScalarGridSpec(
        num_scalar_prefetch=2, grid=(nt,),
        in_specs=[tok(d), mod_spec(1), mod_spec(0), whole((d, n)), rope_spec, rope_spec, rope_spec,
                  whole((a_w, a_w)), whole((a_w, a_w))],
        out_specs=[tok(a_w), tok(a_w), tok(q_w), tok(kv_w), tok(kv_w)],
    )
    return pl.pallas_call(
        functools.partial(_inproj_kernel, a_w=a_w, q_w=q_w, kv_w=kv_w),
        grid_spec=grid_spec,
        out_shape=[jax.ShapeDtypeStruct((t, a_w), BF16), jax.ShapeDtypeStruct((t, a_w), BF16),
                   jax.ShapeDtypeStruct((t, q_w), BF16), jax.ShapeDtypeStruct((t, kv_w), F32),
                   jax.ShapeDtypeStruct((t, kv_w), F32)],
        compiler_params=_cparams(1),
        name="inproj_ab",
    )(rows, rblk, x, mod3, mod3, w_in, cos, sa, sb, bdc, bds)


def _fourier_kernel(c_ref, s_ref, ac_ref, as_ref, o_ref, *, scale):
    y = _dot(c_ref[...], ac_ref[...]) - _dot(s_ref[...], as_ref[...])
    o_ref[...] = (y * scale).astype(BF16)


def _fourier(ac, as_, cs, ss, n_batch, seq, blk_off, tm):
    a_w = ac.shape[1]
    nr = seq // tm
    a_spec = pl.BlockSpec((seq, a_w), lambda r, b: (blk_off + b, 0))
    t_spec = pl.BlockSpec((tm, seq), lambda r, b: (r, 0))
    return pl.pallas_call(
        functools.partial(_fourier_kernel, scale=(seq * (a_w // A_GROUPS)) ** -0.5),
        grid=(nr, n_batch),
        in_specs=[t_spec, t_spec, a_spec, a_spec],
        out_specs=pl.BlockSpec((tm, a_w), lambda r, b: (b * nr + r, 0)),
        out_shape=jax.ShapeDtypeStruct((n_batch * seq, a_w), BF16),
        compiler_params=_cparams(2),
        name="fourier_%d" % seq,
    )(cs, ss, ac, as_)


def _softmax_pv(scores, sink_col, values):
    m = sink_col
    for s in scores:
        m = jnp.maximum(m, jnp.max(s, axis=-1, keepdims=True))
    denom = jnp.exp(sink_col - m)
    acc = None
    for s, v in zip(scores, values):
        e = jnp.exp(s - m)
        denom = denom + jnp.sum(e, axis=-1, keepdims=True)
        pv = _dot(e.astype(BF16), v)
        acc = pv if acc is None else acc + pv
    return acc / denom


def _sink_col(sink_ref, head0, g, rows):
    return jnp.concatenate([jnp.full((rows, 1), sink_ref[head0 + j], F32) for j in range(g)], axis=0)


def _attn_ctx_kernel(sink_ref, q_ref, k_ref, v_ref, o_ref, *, g):
    s_len = q_ref.shape[0]
    for h in range(N_KV_HEADS):
        sl = slice(h * HEAD_DIM, (h + 1) * HEAD_DIM)
        kh = k_ref[:, sl].astype(BF16)
        vh = v_ref[:, sl].astype(BF16)
        q3 = jnp.concatenate([q_ref[:, (h * g + j) * HEAD_DIM:(h * g + j + 1) * HEAD_DIM] for j in range(g)], axis=0)
        o = _softmax_pv([_dot_nt(q3, kh)], _sink_col(sink_ref, h * g, g, s_len), [vh])
        for j in range(g):
            o_ref[:, (h * g + j) * HEAD_DIM:(h * g + j + 1) * HEAD_DIM] = o[j * s_len:(j + 1) * s_len].astype(BF16)


def _attn_ctx(sink, q, k, v, n_batch, seq):
    q_w, kv_w = q.shape[1], k.shape[1]
    g = q_w // kv_w
    return pl.pallas_call(
        functools.partial(_attn_ctx_kernel, g=g),
        grid=(n_batch,),
        in_specs=[pl.BlockSpec(memory_space=pltpu.SMEM),
                  pl.BlockSpec((seq, q_w), lambda b: (b, 0)),
                  pl.BlockSpec((seq, kv_w), lambda b: (b, 0)),
                  pl.BlockSpec((seq, kv_w), lambda b: (b, 0))],
        out_specs=pl.BlockSpec((seq, q_w), lambda b: (b, 0)),
        out_shape=jax.ShapeDtypeStruct((n_batch * seq, q_w), BF16),
        compiler_params=_cparams(1),
        name="attn_ctx",
    )(sink, q, k, v)


def _attn_lat_kernel(sink_ref, q_ref, k_ref, v_ref, ck_ref, cv_ref, o_ref, *, g, nb):
    i = pl.program_id(1)
    row = lax.broadcasted_iota(jnp.int32, (BLK, BLK), 0)
    col = lax.broadcasted_iota(jnp.int32, (BLK, BLK), 1)
    masks = [jnp.logical_and(col >= row, i >= 1), None, jnp.logical_and(col <= row, i <= nb - 2)]
    starts = [pl.multiple_of(jnp.clip(i + d, 0, nb - 1) * BLK, BLK) for d in (-1, 0, 1)]
    for h in range(N_KV_HEADS):
        sl = slice(h * HEAD_DIM, (h + 1) * HEAD_DIM)
        q3 = jnp.concatenate([q_ref[:, (h * g + j) * HEAD_DIM:(h * g + j + 1) * HEAD_DIM] for j in range(g)], axis=0)
        scores, values = [], []
        for st, mk in zip(starts, masks):
            kb = k_ref[pl.ds(st, BLK), sl].astype(BF16)
            s = _dot_nt(q3, kb)
            if mk is not None:
                s = jnp.where(jnp.concatenate([mk] * g, axis=0), s, NEG_INF)
            scores.append(s)
            values.append(v_ref[pl.ds(st, BLK), sl].astype(BF16))
        scores.append(_dot_nt(q3, ck_ref[:, sl].astype(BF16)))
        values.append(cv_ref[:, sl].astype(BF16))
        o = _softmax_pv(scores, _sink_col(sink_ref, h * g, g, BLK), values)
        for j in range(g):
            o_ref[:, (h * g + j) * HEAD_DIM:(h * g + j + 1) * HEAD_DIM] = o[j * BLK:(j + 1) * BLK].astype(BF16)


def _attn_lat(sink, q, k, v, cache_k, cache_v, layer_slot, n_batch, seq, tok_off):
    q_w, kv_w = q.shape[1], k.shape[1]
    g = q_w // kv_w
    nb = seq // BLK
    past = cache_k.shape[2]
    kv_spec = pl.BlockSpec((seq, kv_w), lambda b, i: (tok_off // seq + b, 0))
    c_spec = pl.BlockSpec((None, None, past, kv_w), lambda b, i: (b, layer_slot, 0, 0))
    return pl.pallas_call(
        functools.partial(_attn_lat_kernel, g=g, nb=nb),
        grid=(n_batch, nb),
        in_specs=[pl.BlockSpec(memory_space=pltpu.SMEM),
                  pl.BlockSpec((BLK, q_w), lambda b, i: (tok_off // BLK + b * nb + i, 0)),
                  kv_spec, kv_spec, c_spec, c_spec],
        out_specs=pl.BlockSpec((BLK, q_w), lambda b, i: (b * nb + i, 0)),
        out_shape=jax.ShapeDtypeStruct((n_batch * seq, q_w), BF16),
        compiler_params=_cparams(2),
        name="attn_lat",
    )(sink, q, k, v, cache_k, cache_v)


def _post_ab_kernel(rows_ref, x_ref, yap_ref, ybp_ref, yas_ref, ybs_ref, w_ref, gm_ref, lg_ref, lb_ref,
                    scf_ref, shf_ref, wr_ref, br_ref, x1_ref, hf_ref, route_ref, gate_ref, cnt_ref, carry_ref,
                    *, n_ctx_tiles, alpha):
    del rows_ref
    is_ctx = pl.program_id(0) < n_ctx_tiles
    ya = jnp.where(is_ctx, yap_ref[...], yas_ref[...])
    yb = jnp.where(is_ctx, ybp_ref[...], ybs_ref[...])
    a_w = ya.shape[1]
    y = _dot(ya, w_ref[:a_w, :]) + _dot(yb, w_ref[a_w:, :])
    _token_tail(x_ref[...], y, gm_ref[...], lg_ref[...], lb_ref[...], scf_ref[...], shf_ref[...],
                wr_ref[...], br_ref[...], alpha, x1_ref, hf_ref, route_ref, gate_ref, cnt_ref, carry_ref)


def _tail_out(t, d):
    shapes = [jax.ShapeDtypeStruct((t, d), F32), jax.ShapeDtypeStruct((t, d // 2), jnp.int32),
              jax.ShapeDtypeStruct((t, LANES), jnp.int32), jax.ShapeDtypeStruct((t, LANES), F32),
              jax.ShapeDtypeStruct((8, N_EXPERTS), F32)]
    specs = [pl.BlockSpec((TM, d), lambda i, r: (i, 0)), pl.BlockSpec((TM, d // 2), lambda i, r: (i, 0)),
             pl.BlockSpec((TM, LANES), lambda i, r: (i, 0)), pl.BlockSpec((TM, LANES), lambda i, r: (i, 0)),
             pl.BlockSpec((8, N_EXPERTS), lambda i, r: (0, 0))]
    return shapes, specs


def _post_ab(x, ya_p, yb_p, ya_s, yb_s, w_out, mod3, mod_base, rows, ln_g, ln_b, w_r, b_r, alpha):
    t, d = x.shape
    nt = t // TM
    n_ctx_tiles = ya_p.shape[0] // TM
    a_w, q_w = ya_p.shape[1], yb_p.shape[1]
    mod_spec = lambda j: pl.BlockSpec((None, 1, d), lambda i, r: (mod_base + j * 16 + r[i], 0, 0))
    whole = lambda shp: pl.BlockSpec(shp, lambda i, r: (0,) * len(shp))
    ctx = lambda w: pl.BlockSpec((TM, w), lambda i, r: (jnp.minimum(i, n_ctx_tiles - 1), 0))
    lat = lambda w: pl.BlockSpec((TM, w), lambda i, r: (jnp.maximum(i - n_ctx_tiles, 0), 0))
    shapes, specs = _tail_out(t, d)
    grid_spec = pltpu.PrefetchScalarGridSpec(
        num_scalar_prefetch=1, grid=(nt,),
        in_specs=[pl.BlockSpec((TM, d), lambda i, r: (i, 0)), ctx(a_w), ctx(q_w), lat(a_w), lat(q_w),
                  whole(w_out.shape), mod_spec(2), whole((1, d)), whole((1, d)), mod_spec(4), mod_spec(3),
                  whole(w_r.shape), whole(b_r.shape)],
        out_specs=specs,
        scratch_shapes=[pltpu.VMEM((1, N_EXPERTS), F32)],
    )
    return pl.pallas_call(
        functools.partial(_post_ab_kernel, n_ctx_tiles=n_ctx_tiles, alpha=alpha),
        grid_spec=grid_spec, out_shape=shapes, compiler_params=_cparams(1), name="post_ab",
    )(rows, x, ya_p, yb_p, ya_s, yb_s, w_out, mod3, ln_g, ln_b, mod3, mod3, w_r, b_r)


def _gmlp_kernel(rows_ref, x_ref, scm_ref, shm_ref, win_ref, bin_ref, gv_ref, bv_ref, wsp_ref, bsp_ref, wout_ref,
                 gm_ref, lg_ref, lb_ref, scf_ref, shf_ref, wr_ref, br_ref, x1_ref, hf_ref, route_ref, gate_ref,
                 cnt_ref, carry_ref, *, alpha):
    del rows_ref
    x = x_ref[...]
    h = (_ln(x) * (1.0 + scm_ref[...]) + shm_ref[...]).astype(BF16)
    z = _dot(h, win_ref[...]) + bin_ref[...]
    z = 0.5 * z * (1.0 + lax.erf(z * (2.0 ** -0.5)))
    c_w = z.shape[1] // 2
    u = z[:, :c_w]
    v = (_ln(z[:, c_w:]) * gv_ref[...] + bv_ref[...]).astype(BF16)
    gd = c_w // C_GROUPS
    chunks = []
    for n in range(x.shape[0] // CHUNK):
        groups = []
        for g in range(C_GROUPS):
            vg = v[n * CHUNK:(n + 1) * CHUNK, g * gd:(g + 1) * gd]
            groups.append(_dot(wsp_ref[g], vg) + bsp_ref[g])
        chunks.append(jnp.concatenate(groups, axis=1))
    mixed = jnp.concatenate(chunks, axis=0)
    y = _dot((u * mixed).astype(BF16), wout_ref[...])
    _token_tail(x, y, gm_ref[...], lg_ref[...], lb_ref[...], scf_ref[...], shf_ref[...],
                wr_ref[...], br_ref[...], alpha, x1_ref, hf_ref, route_ref, gate_ref, cnt_ref, carry_ref)


def _gmlp(x, mod3, mod_base, rows, w_in, b_in, g_v, b_v, w_sp, b_sp, w_out, ln_g, ln_b, w_r, b_r, alpha):
    t, d = x.shape
    nt = t // TM
    mod_spec = lambda j: pl.BlockSpec((None, 1, d), lambda i, r: (mod_base + j * 16 + r[i], 0, 0))
    whole = lambda shp: pl.BlockSpec(shp, lambda i, r: (0,) * len(shp))
    shapes, specs = _tail_out(t, d)
    grid_spec = pltpu.PrefetchScalarGridSpec(
        num_scalar_prefetch=1, grid=(nt,),
        in_specs=[pl.BlockSpec((TM, d), lambda i, r: (i, 0)), mod_spec(1), mod_spec(0),
                  whole(w_in.shape), whole(b_in.shape), whole(g_v.shape), whole(b_v.shape),
                  whole(w_sp.shape), whole(b_sp.shape), whole(w_out.shape),
                  mod_spec(2), whole((1, d)), whole((1, d)), mod_spec(4), mod_spec(3),
                  whole(w_r.shape), whole(b_r.shape)],
        out_specs=specs,
        scratch_shapes=[pltpu.VMEM((1, N_EXPERTS), F32)],
    )
    return pl.pallas_call(
        functools.partial(_gmlp_kernel, alpha=alpha),
        grid_spec=grid_spec, out_shape=shapes, compiler_params=_cparams(1), name="gmlp",
    )(rows, x, mod3, mod3, w_in, b_in, g_v, b_v, w_sp, b_sp, w_out, mod3, ln_g, ln_b, mod3, mod3, w_r, b_r)


def _moe_kernel(be_ref, bf_ref, nu_ref, x_ref, wgu_ref, bgu_ref, wdn_ref, bdn_ref, o_ref, wgu_s, wdn_s):
    i = pl.program_id(0)
    d_ff = wdn_ref.shape[0]

    @pl.when(i < nu_ref[0])
    def _():
        @pl.when(bf_ref[i] == 1)
        def _():
            wgu_s[...] = wgu_ref[...].astype(BF16)
            wdn_s[...] = wdn_ref[...].astype(BF16)

        gu = _dot(_unpack_bf16_pairs(x_ref[...]), wgu_s[...]) + bgu_ref[...]
        gate = jnp.minimum(gu[:, :d_ff], SWIGLU_LIMIT)
        lin = jnp.clip(gu[:, d_ff:], -SWIGLU_LIMIT, SWIGLU_LIMIT)
        glu = gate * jax.nn.sigmoid(SWIGLU_ALPHA * gate)
        hid = ((lin + 1.0) * glu).astype(BF16)
        o_ref[...] = _dot(hid, wdn_s[...]) + bdn_ref[...]


def _moe(xs, blk_e, blk_first, n_used, w_gu, b_gu, w_dn, b_dn, layer):
    n_rows = xs.shape[0]
    n_e, d, ff2 = w_gu.shape[1:]
    d_ff = ff2 // 2
    nt = n_rows // TM_MOE
    row_spec = pl.BlockSpec((TM_MOE, d), lambda i, be, bf, nu: (jnp.minimum(i, nu[0] - 1), 0))
    grid_spec = pltpu.PrefetchScalarGridSpec(
        num_scalar_prefetch=3, grid=(nt,),
        in_specs=[pl.BlockSpec((TM_MOE, d // 2), lambda i, be, bf, nu: (jnp.minimum(i, nu[0] - 1), 0)),
                  pl.BlockSpec((None, None, d, ff2), lambda i, be, bf, nu: (layer, be[i], 0, 0)),
                  pl.BlockSpec((None, None, 1, ff2), lambda i, be, bf, nu: (layer, be[i], 0, 0)),
                  pl.BlockSpec((None, None, d_ff, d), lambda i, be, bf, nu: (layer, be[i], 0, 0)),
                  pl.BlockSpec((None, None, 1, d), lambda i, be, bf, nu: (layer, be[i], 0, 0))],
        out_specs=row_spec,
        scratch_shapes=[pltpu.VMEM((d, ff2), BF16), pltpu.VMEM((d_ff, d), BF16)],
    )
    depth = w_gu.shape[0]
    return pl.pallas_call(
        _moe_kernel, grid_spec=grid_spec,
        out_shape=jax.ShapeDtypeStruct((n_rows, d), F32),
        compiler_params=_cparams(1), name="moe_experts",
    )(blk_e, blk_first, n_used, xs, w_gu, b_gu.reshape(depth, n_e, 1, ff2), w_dn, b_dn.reshape(depth, n_e, 1, d))


def _combine_kernel(rows_ref, x_ref, y0_ref, y1_ref, y2_ref, y3_ref, gate_ref, gf_ref, lg_ref, lb_ref, o_ref, *, alpha):
    del rows_ref
    gate = gate_ref[...]
    y = gate[:, 0:1] * y0_ref[...]
    for k, y_ref in ((1, y1_ref), (2, y2_ref), (3, y3_ref)):
        y = y + gate[:, k:k + 1] * y_ref[...]
    o_ref[...] = _ln(alpha * x_ref[...] + gf_ref[...] * y) * lg_ref[...] + lb_ref[...]


def _combine(x1, y4, gate, mod3, mod_base, rows, ln_g, ln_b, alpha):
    t, d = x1.shape
    nt = t // TM
    y_spec = lambda k: pl.BlockSpec((None, TM, d), lambda i, r: (k, i, 0))
    grid_spec = pltpu.PrefetchScalarGridSpec(
        num_scalar_prefetch=1, grid=(nt,),
        in_specs=[pl.BlockSpec((TM, d), lambda i, r: (i, 0)), y_spec(0), y_spec(1), y_spec(2), y_spec(3),
                  pl.BlockSpec((TM, LANES), lambda i, r: (i, 0)),
                  pl.BlockSpec((None, 1, d), lambda i, r: (mod_base + 5 * 16 + r[i], 0, 0)),
                  pl.BlockSpec((1, d), lambda i, r: (0, 0)), pl.BlockSpec((1, d), lambda i, r: (0, 0))],
        out_specs=pl.BlockSpec((TM, d), lambda i, r: (i, 0)),
    )
    return pl.pallas_call(
        functools.partial(_combine_kernel, alpha=alpha), grid_spec=grid_spec,
        out_shape=jax.ShapeDtypeStruct((t, d), F32), compiler_params=_cparams(1), name="moe_combine",
    )(rows, x1, y4, y4, y4, y4, gate, mod3, ln_g, ln_b)


def _route_plan(route, counts, n_tiles):
    idx, rank = route[:, :TOP_K], route[:, TOP_K:2 * TOP_K]
    padded = (counts + TM_MOE - 1) // TM_MOE * TM_MOE
    pend = jnp.cumsum(padded)
    pstart = pend - padded
    onehot = idx[:, :, None] == jnp.arange(N_EXPERTS, dtype=jnp.int32)[None, None, :]
    dest = jnp.sum(jnp.where(onehot, pstart[None, None, :], 0), axis=-1) + rank
    n_used = (pend[-1] // TM_MOE).astype(jnp.int32)
    tile_start = jnp.arange(n_tiles, dtype=jnp.int32) * TM_MOE
    blk_e = jnp.sum((tile_start[:, None] >= pend[None, :]).astype(jnp.int32), axis=1)
    last_e = jnp.sum((jnp.maximum(n_used - 1, 0) * TM_MOE >= pend).astype(jnp.int32))
    blk_e = jnp.minimum(jnp.where(jnp.arange(n_tiles) < n_used, blk_e, last_e), N_EXPERTS - 1).astype(jnp.int32)
    blk_first = jnp.concatenate([jnp.ones((1,), jnp.int32), (blk_e[1:] != blk_e[:-1]).astype(jnp.int32)])
    return dest.T, blk_e, blk_first, n_used.reshape(1)


SC_CORES = 2
SC_SUBCORES = 16


def _sc_mesh():
    return plsc.VectorSubcoreMesh(core_axis_name="core", subcore_axis_name="subcore")


def _sc_scatter_rows(src, dest, n_rows, window):
    n_slots, t = dest.shape
    w = src.shape[1]
    per_worker = t // (SC_CORES * SC_SUBCORES)
    assert per_worker * SC_CORES * SC_SUBCORES == t and per_worker % window == 0

    @functools.partial(pl.kernel, out_type=jax.ShapeDtypeStruct((n_rows, w), src.dtype), mesh=_sc_mesh(),
                       scratch_types=[pltpu.VMEM((n_slots, window), jnp.int32), pltpu.VMEM((window, w), src.dtype)],
                       name="sc_dispatch")
    def scatter(src_hbm, dest_hbm, out_hbm, idx_v, rows_v):
        worker = lax.axis_index("subcore") * SC_CORES + lax.axis_index("core")

        @pl.loop(0, per_worker // window)
        def _(c):
            base = pl.multiple_of(worker * per_worker + c * window, window)
            pltpu.sync_copy(src_hbm.at[pl.ds(base, window)], rows_v)
            pltpu.sync_copy(dest_hbm.at[:, pl.ds(base, window)], idx_v)
            for k in range(n_slots):
                pltpu.sync_copy(rows_v, out_hbm.at[idx_v.at[k]])

    return scatter(src, dest)


def _sc_gather_rows(table, idx, window):
    n = idx.shape[0]
    w = table.shape[1]
    per_worker = n // (SC_CORES * SC_SUBCORES)
    assert per_worker * SC_CORES * SC_SUBCORES == n and per_worker % window == 0

    @functools.partial(pl.kernel, out_type=jax.ShapeDtypeStruct((n, w), table.dtype), mesh=_sc_mesh(),
                       scratch_types=[pltpu.VMEM((window,), jnp.int32), pltpu.VMEM((window, w), table.dtype)],
                       name="sc_collect")
    def gather(table_hbm, idx_hbm, out_hbm, idx_v, rows_v):
        worker = lax.axis_index("subcore") * SC_CORES + lax.axis_index("core")

        @pl.loop(0, per_worker // window)
        def _(c):
            base = pl.multiple_of(worker * per_worker + c * window, window)
            pltpu.sync_copy(idx_hbm.at[pl.ds(base, window)], idx_v)
            pltpu.sync_copy(table_hbm.at[idx_v], rows_v)
            pltpu.sync_copy(rows_v, out_hbm.at[pl.ds(base, window)])

    return gather(table, idx)


def _dft_tables(n):
    j = jnp.arange(n, dtype=jnp.int32)
    ang = ((j[:, None] * j[None, :]) % n).astype(F32) * (2.0 * math.pi / n)
    return jnp.cos(ang), jnp.sin(ang)


def _rope_tables(n_lat, n_ctx_rows):
    half, quarter = HEAD_DIM // 2, HEAD_DIM // 4
    tpos = jnp.arange(n_lat, dtype=jnp.int32)
    lane = jnp.arange(LANES, dtype=jnp.int32) % HEAD_DIM
    pos = jnp.where(lane[None, :] < half, (tpos // GRID_W)[:, None], (tpos % GRID_W)[:, None]).astype(F32)
    fidx = (lane % quarter).astype(F32)
    freqs = ROPE_THETA ** (-fidx / quarter)
    ang = pos * freqs[None, :]
    cos, sin = jnp.cos(ang), jnp.sin(ang)
    first = (lane % half) < quarter
    sa = jnp.where(first[None, :], -sin, 0.0)
    sb = jnp.where(first[None, :], 0.0, sin)
    ident = lambda v: jnp.full((n_ctx_rows, LANES), v, F32)
    return (jnp.concatenate([ident(1.0), cos]), jnp.concatenate([ident(0.0), sa]), jnp.concatenate([ident(0.0), sb]))


def kernel(x_prompt, x_sample, cache_k_ab, cache_v_ab, c, c_ctx, w_ada, b_ada, ln_mix_g, ln_mix_b, ln_ffn_g, ln_ffn_b, w_in_ab, w_out_ab, sink_ab, w_in_c, b_in_c, ln_v_g, ln_v_b, w_sp, b_sp, w_out_c, w_router, b_router, w_gu, b_gu, w_dn, b_dn):
    n_ctx_b, ctx_seq, d = x_prompt.shape
    n_lat_b, lat_seq, _ = x_sample.shape
    depth = w_ada.shape[0]
    t_ctx, t_lat = n_ctx_b * ctx_seq, n_lat_b * lat_seq
    t = t_ctx + t_lat
    alpha = (2 * depth) ** 0.25
    kv_w = N_KV_HEADS * HEAD_DIM
    a_w = d // 4
    q_w = d - a_w
    assert ctx_seq % TM == 0 and lat_seq % TM == 0 and n_lat_b + 1 <= 16

    rows_np = np.concatenate([np.zeros(t_ctx // TM, np.int32), 1 + np.arange(t_lat // TM, dtype=np.int32) // (lat_seq // TM)])
    rblk_np = np.concatenate([np.zeros(t_ctx // TM, np.int32), 1 + np.arange(t_lat // TM, dtype=np.int32) % (lat_seq // TM)])
    rows, rblk = jnp.asarray(rows_np), jnp.asarray(rblk_np)

    cond16 = jnp.zeros((16, d), F32).at[0].set(c_ctx).at[1:1 + n_lat_b].set(c)
    mod = _ada(cond16, w_ada, b_ada)
    mod3 = mod.reshape(depth, 16, 6, d).transpose(0, 2, 1, 3).reshape(depth * 6 * 16, 1, d)

    cos, sa, sb = _rope_tables(lat_seq, TM)
    cs_ctx, ss_ctx = (z.astype(BF16) for z in _dft_tables(ctx_seq))
    cs_lat, ss_lat = (z.astype(BF16) for z in _dft_tables(lat_seq))
    gd = a_w // A_GROUPS
    cd, sd = _dft_tables(gd)
    eye = jnp.eye(A_GROUPS, dtype=F32)
    bdc, bds = jnp.kron(eye, cd).astype(BF16), jnp.kron(eye, sd).astype(BF16)

    past = cache_k_ab.shape[2]
    cache_k = cache_k_ab.reshape(n_lat_b, -1, past, kv_w)
    cache_v = cache_v_ab.reshape(n_lat_b, -1, past, kv_w)

    x = jnp.concatenate([x_prompt.reshape(t_ctx, d), x_sample.reshape(t_lat, d)], axis=0)
    row2 = lambda v: v.reshape(1, -1)
    ks, vs = [], []
    for l in range(depth):
        j = l // 2
        mod_base = l * 6 * 16
        w_r, b_r = w_router[l], row2(b_router[l])
        if l % 2 == 0:
            ac, as_, q, k, v = _inproj(x, mod3, mod_base, rows, rblk, w_in_ab[j].astype(BF16), cos, sa, sb,
                                       bdc, bds, a_w, q_w, kv_w)
            ks.append(k[:t_ctx].reshape(n_ctx_b, ctx_seq, N_KV_HEADS, HEAD_DIM))
            vs.append(v[:t_ctx].reshape(n_ctx_b, ctx_seq, N_KV_HEADS, HEAD_DIM))
            ya_p = _fourier(ac, as_, cs_ctx, ss_ctx, n_ctx_b, ctx_seq, 0, min(ctx_seq, 512))
            ya_s = _fourier(ac, as_, cs_lat, ss_lat, n_lat_b, lat_seq, t_ctx // lat_seq, min(lat_seq, 512))
            yb_p = _attn_ctx(sink_ab[j], q, k, v, n_ctx_b, ctx_seq)
            yb_s = _attn_lat(sink_ab[j], q, k, v, cache_k, cache_v, j, n_lat_b, lat_seq, t_ctx)
            x1, hf, route, gate, cnt = _post_ab(x, ya_p, yb_p, ya_s, yb_s, w_out_ab[j].astype(BF16), mod3, mod_base,
                                                rows, row2(ln_mix_g[l]), row2(ln_mix_b[l]), w_r, b_r, alpha)
        else:
            x1, hf, route, gate, cnt = _gmlp(x, mod3, mod_base, rows, w_in_c[j].astype(BF16), row2(b_in_c[j]),
                                             row2(ln_v_g[j]), row2(ln_v_b[j]), w_sp[j].astype(BF16),
                                             b_sp[j][:, :, None], w_out_c[j].astype(BF16), row2(ln_mix_g[l]),
                                             row2(ln_mix_b[l]), w_r, b_r, alpha)
        n_rows = t * TOP_K + N_EXPERTS * TM_MOE
        dest, blk_e, blk_first, n_used = _route_plan(route, cnt[0].astype(jnp.int32), n_rows // TM_MOE)
        xs = _sc_scatter_rows(hf, dest, n_rows, 128)
        out_sorted = _moe(xs, blk_e, blk_first, n_used, w_gu, b_gu, w_dn, b_dn, l)
        y4 = _sc_gather_rows(out_sorted, dest.reshape(-1), 64).reshape(TOP_K, t, d)
        x = _combine(x1, y4, gate, mod3, mod_base, rows, row2(ln_ffn_g[l]), row2(ln_ffn_b[l]), alpha)

    y_prompt = x[:t_ctx].reshape(n_ctx_b, ctx_seq, d)
    y_sample = x[t_ctx:].reshape(n_lat_b, lat_seq, d)
    return (y_prompt, y_sample, jnp.stack(ks, axis=1), jnp.stack(vs, axis=1))
```

```python
import functools
import math

import numpy as np
import jax
import jax.numpy as jnp
from jax import lax
from jax.experimental import pallas as pl
from jax.experimental.pallas import tpu as pltpu
from jax.experimental.pallas import tpu_sc as plsc

GRID_W = 64
BLK = 128
WINDOW = 128
HEAD_DIM = 64
A_GROUPS = 4
N_KV_HEADS = 4
C_GROUPS = 8
CHUNK = 128
N_EXPERTS = 32
TOP_K = 4
SWIGLU_LIMIT = 7.0
SWIGLU_ALPHA = 1.702
ROPE_THETA = 10000.0
LN_EPS = 1e-6
NEG_INF = -1e30
LOG2_E = math.log2(math.e)

LANES = 128
TM = 512
TM_MOE = 256
VMEM_LIMIT = 52 * 1024 * 1024

F32 = jnp.float32
BF16 = jnp.bfloat16


def _cparams(n_axes):
    return pltpu.CompilerParams(dimension_semantics=("arbitrary",) * n_axes,
                                vmem_limit_bytes=VMEM_LIMIT)


def _ln(x):
    mu = jnp.mean(x, axis=-1, keepdims=True)
    xc = x - mu
    var = jnp.mean(xc * xc, axis=-1, keepdims=True)
    return xc * lax.rsqrt(var + LN_EPS)


def _dot(a, b):
    return jnp.dot(a, b, preferred_element_type=F32)


def _dot_nt(a, b):
    return lax.dot_general(a, b, (((1,), (1,)), ((), ())), preferred_element_type=F32)


def _split(a):
    hi = a.astype(BF16)
    lo = (a - hi.astype(F32)).astype(BF16)
    return hi, lo


def _dot_3pass(a, b):
    a_hi, a_lo = _split(a)
    b_hi, b_lo = _split(b)
    return _dot(a_hi, b_hi) + (_dot(a_hi, b_lo) + _dot(a_lo, b_hi))


def _lane_select(cols, width):
    m = cols[0].shape[0]
    lane = lax.broadcasted_iota(jnp.int32, (m, width), 1)
    out = jnp.zeros((m, width), cols[0].dtype)
    for j, c in enumerate(cols):
        out = jnp.where(lane == j, c, out)
    return out


def _pack_bf16_pairs(v):
    n = v.shape[1] // 2
    lo = pltpu.bitcast(v[:, :n].astype(BF16).astype(F32), jnp.uint32) >> 16
    hi = pltpu.bitcast(v[:, n:].astype(BF16).astype(F32), jnp.uint32) & jnp.uint32(0xFFFF0000)
    return pltpu.bitcast(lo | hi, jnp.int32)


def _unpack_bf16_pairs(p):
    u = pltpu.bitcast(p, jnp.uint32)
    lo = pltpu.bitcast(u << 16, F32)
    hi = pltpu.bitcast(u & jnp.uint32(0xFFFF0000), F32)
    return jnp.concatenate([lo, hi], axis=1).astype(BF16)


def _row_chains(n_rows, n_chains=2):
    step = n_rows // n_chains
    return [slice(c * step, (c + 1) * step) for c in range(n_chains)]


def _run_skewed(stages, states):
    states = list(states)
    for step in range(len(stages) + len(states) - 1):
        for c in range(len(states)):
            if 0 <= step - c < len(stages):
                states[c] = stages[step - c](states[c])
    return states


def _tail_rows(x, y, g_m, ln_g, ln_b, sc_f, sh_f, w_r, b_r, alpha):
    x1 = _ln(alpha * x + g_m * y) * ln_g + ln_b
    hf = _ln(x1) * (1.0 + sc_f) + sh_f
    hf_hi, hf_lo = _split(hf)
    w_hi, w_lo = _split(w_r)
    logits = _dot_nt(w_hi, hf_hi) + (_dot_nt(w_hi, hf_lo) + _dot_nt(w_lo, hf_hi)) + b_r
    return x1, _pack_bf16_pairs(hf), logits


def _tail_route(logits, route_ref, gate_ref, cnt_ref, carry_ref):
    @pl.when(pl.program_id(0) == 0)
    def _():
        carry_ref[...] = jnp.zeros_like(carry_ref)

    tm = logits.shape[1]
    sub = lax.broadcasted_iota(jnp.int32, logits.shape, 0)
    vals = logits
    top_v, top_i = [], []
    for _ in range(TOP_K):
        m = jnp.max(vals, axis=0, keepdims=True)
        am = jnp.min(jnp.where(vals == m, sub, N_EXPERTS), axis=0, keepdims=True)
        top_v.append(m)
        top_i.append(am)
        vals = jnp.where(sub == am, -jnp.inf, vals)
    e = [jnp.exp(v - top_v[0]) for v in top_v]
    denom = e[0] + e[1] + e[2] + e[3]
    gates_t = jnp.concatenate([ek / denom for ek in e] + [jnp.zeros((LANES - TOP_K, tm), F32)], axis=0)
    gate_ref[...] = gates_t.T

    member = jnp.zeros(logits.shape, F32)
    for am in top_i:
        member = jnp.where(sub == am, 1.0, member)
    r_i = lax.broadcasted_iota(jnp.int32, (tm, tm), 0)
    c_i = lax.broadcasted_iota(jnp.int32, (tm, tm), 1)
    earlier = jnp.where(r_i < c_i, 1.0, 0.0).astype(BF16)
    before = _dot(member.astype(BF16), earlier) + carry_ref[...]
    ranks = [jnp.sum(jnp.where(sub == am, before, 0.0), axis=0, keepdims=True).astype(jnp.int32) for am in top_i]
    route_ref[...] = jnp.concatenate(top_i + ranks, axis=0)
    carry = carry_ref[...] + jnp.sum(member, axis=1, keepdims=True)
    carry_ref[...] = carry
    cnt_ref[...] = jnp.broadcast_to(carry, cnt_ref.shape)


def _ada_kernel(cond_ref, w_ref, b_ref, o_ref):
    c = cond_ref[...]
    s = (c * jax.nn.sigmoid(c)).astype(BF16)
    o_ref[...] = _dot(s, w_ref[...].astype(BF16)) + b_ref[...]


def _ada(cond16, w_ada, b_ada):
    depth, d, n = w_ada.shape
    tn = 1536
    return pl.pallas_call(
        _ada_kernel,
        grid=(depth, n // tn),
        in_specs=[
            pl.BlockSpec((16, d), lambda l, j: (0, 0)),
            pl.BlockSpec((None, d, tn), lambda l, j: (l, 0, j)),
            pl.BlockSpec((None, 1, tn), lambda l, j: (l, 0, j)),
        ],
        out_specs=pl.BlockSpec((None, 16, tn), lambda l, j: (l, 0, j)),
        out_shape=jax.ShapeDtypeStruct((depth, 16, n), F32),
        compiler_params=_cparams(2),
        name="ada_mod",
    )(cond16, w_ada, b_ada.reshape(depth, 1, n))


def _inproj_kernel(rows_ref, rblk_ref, x_ref, sc_ref, sh_ref, w_ref, cos_ref, sa_ref, sb_ref, bdc_ref, bds_ref,
                   ac_ref, as_ref, q_ref, k_ref, v_ref, kb_ref, vb_ref, *, a_w, q_w, kv_w):
    del rows_ref, rblk_ref
    g = q_w // kv_w

    def norm_in(s):
        return dict(s, h=(_ln(x_ref[s["rows"], :]) * (1.0 + sc_ref[...]) + sh_ref[...]).astype(BF16))

    def proj(s):
        return dict(s, p=_dot(s["h"], w_ref[...]))

    def finish(s):
        rs, p = s["rows"], s["p"]
        a = p[:, :a_w].astype(BF16)
        ac_ref[rs, :] = _dot(a, bdc_ref[...]).astype(BF16)
        as_ref[rs, :] = _dot(a, bds_ref[...]).astype(BF16)
        cos, sa, sb = cos_ref[rs, :], sa_ref[rs, :], sb_ref[rs, :]

        def rope(t):
            w = t.shape[1]
            reps = w // LANES
            c, a_, b_ = (jnp.tile(z, (1, reps)) for z in (cos, sa, sb))
            nxt = pltpu.roll(t, w - HEAD_DIM // 4, 1)
            prv = pltpu.roll(t, HEAD_DIM // 4, 1)
            return t * c + nxt * a_ + prv * b_

        q = rope(p[:, a_w:a_w + q_w]) * (HEAD_DIM ** -0.5 * LOG2_E)
        lane = lax.broadcasted_iota(jnp.int32, (q.shape[0], LANES), 1)
        for j in range(q_w // HEAD_DIM):
            tile = q[:, (j // 2) * LANES:(j // 2 + 1) * LANES]
            dst_low = (j // g) % 2 == 0
            if (j % 2 == 0) != dst_low:
                tile = pltpu.roll(tile, HEAD_DIM, 1)
            keep = (lane < HEAD_DIM) if dst_low else (lane >= HEAD_DIM)
            q_ref[rs, j * LANES:(j + 1) * LANES] = jnp.where(keep, tile, 0.0).astype(BF16)
        k = rope(p[:, a_w + q_w:a_w + q_w + kv_w])
        v = p[:, a_w + q_w + kv_w:]
        k_ref[rs, :] = k
        v_ref[rs, :] = v
        kb_ref[rs, :] = k.astype(BF16)
        vb_ref[rs, :] = v.astype(BF16)
        return s

    _run_skewed([norm_in, proj, finish], [dict(rows=rs) for rs in _row_chains(x_ref.shape[0])])


def _inproj(x, mod3, mod_base, rows, rblk, w_in, cos, sa, sb, bdc, bds, a_w, q_w, kv_w):
    t, d = x.shape
    n = w_in.shape[1]
    nt = t // TM
    mod_spec = lambda j: pl.BlockSpec((None, 1, d), lambda i, r, rb: (mod_base + j * 16 + r[i], 0, 0))
    whole = lambda shp: pl.BlockSpec(shp, lambda i, r, rb: (0,) * len(shp))
    rope_spec = pl.BlockSpec((TM, LANES), lambda i, r, rb: (rb[i], 0))
    tok = lambda w: pl.BlockSpec((TM, w), lambda i, r, rb: (i, 0))
    grid_spec = pltpu.PrefetchScalarGridSpec(
        num_scalar_prefetch=2, grid=(nt,),
        in_specs=[tok(d), mod_spec(1), mod_spec(0), whole((d, n)), rope_spec, rope_spec, rope_spec,
                  whole((a_w, a_w)), whole((a_w, a_w))],
        out_specs=[tok(a_w), tok(a_w), tok(2 * q_w), tok(kv_w), tok(kv_w), tok(kv_w), tok(kv_w)],
    )
    return pl.pallas_call(
        functools.partial(_inproj_kernel, a_w=a_w, q_w=q_w, kv_w=kv_w),
        grid_spec=grid_spec,
        out_shape=[jax.ShapeDtypeStruct((t, a_w), BF16), jax.ShapeDtypeStruct((t, a_w), BF16),
                   jax.ShapeDtypeStruct((t, 2 * q_w), BF16), jax.ShapeDtypeStruct((t, kv_w), F32),
                   jax.ShapeDtypeStruct((t, kv_w), F32), jax.ShapeDtypeStruct((t, kv_w), BF16),
                   jax.ShapeDtypeStruct((t, kv_w), BF16)],
        compiler_params=_cparams(1),
        name="inproj_ab",
    )(rows, rblk, x, mod3, mod3, w_in, cos, sa, sb, bdc, bds)


def _fourier_kernel(c_ref, s_ref, ac_ref, as_ref, o_ref, *, scale):
    y = _dot(c_ref[...], ac_ref[...]) - _dot(s_ref[...], as_ref[...])
    o_ref[...] = (y * scale).astype(BF16)


def _fourier(ac, as_, cs, ss, n_batch, seq, blk_off, tm):
    a_w = ac.shape[1]
    nr = seq // tm
    a_spec = pl.BlockSpec((seq, a_w), lambda r, b: (blk_off + b, 0))
    t_spec = pl.BlockSpec((tm, seq), lambda r, b: (r, 0))
    return pl.pallas_call(
        functools.partial(_fourier_kernel, scale=(seq * (a_w // A_GROUPS)) ** -0.5),
        grid=(nr, n_batch),
        in_specs=[t_spec, t_spec, a_spec, a_spec],
        out_specs=pl.BlockSpec((tm, a_w), lambda r, b: (b * nr + r, 0)),
        out_shape=jax.ShapeDtypeStruct((n_batch * seq, a_w), BF16),
        compiler_params=_cparams(2),
        name="fourier_%d" % seq,
    )(cs, ss, ac, as_)


def _softmax_pv(scores, sink_col, values):
    m = sink_col
    for s in scores:
        m = jnp.maximum(m, jnp.max(s, axis=-1, keepdims=True))
    acc = None
    for s, v in zip(scores, values):
        e = jnp.exp2(s - m).astype(BF16)
        pv = _dot(e, jnp.concatenate([v, jnp.ones_like(v)], axis=1))
        acc = pv if acc is None else acc + pv
    d = values[0].shape[1]
    return acc[:, :d] / (acc[:, d:d + 1] + jnp.exp2(sink_col - m))


def _attend_pairs(sink_ref, q_ref, keys, values, mask, o_ref, *, g):
    rows = q_ref.shape[0]
    per_pair = 2 * g
    lane = lax.broadcasted_iota(jnp.int32, (rows, LANES), 1)
    full_mask = None if mask is None else jnp.concatenate([mask] * per_pair, axis=0)
    for p in range(N_KV_HEADS // 2):
        h0 = p * per_pair
        qp = jnp.concatenate([q_ref[:, (h0 + j) * LANES:(h0 + j + 1) * LANES] for j in range(per_pair)], axis=0)
        sink_col = jnp.concatenate([jnp.full((rows, 1), sink_ref[h0 + j] * LOG2_E, F32) for j in range(per_pair)],
                                   axis=0)
        scores = [_dot_nt(qp, kf(p)) for kf in keys]
        if full_mask is not None:
            scores[0] = jnp.where(full_mask, scores[0], NEG_INF)
        o = _softmax_pv(scores, sink_col, [vf(p) for vf in values])
        for t in range(per_pair // 2):
            halves = []
            for j in (2 * t, 2 * t + 1):
                blk = o[j * rows:(j + 1) * rows]
                valid_low = ((h0 + j) // g) % 2 == 0
                if valid_low != (j % 2 == 0):
                    blk = pltpu.roll(blk, HEAD_DIM, 1)
                halves.append(blk)
            tile = jnp.where(lane < HEAD_DIM, halves[0], halves[1])
            o_ref[:, (h0 // 2 + t) * LANES:(h0 // 2 + t + 1) * LANES] = tile.astype(BF16)


def _attn_ctx_kernel(sink_ref, q_ref, k_ref, v_ref, o_ref, *, g):
    pair = lambda ref: (lambda p: ref[:, p * LANES:(p + 1) * LANES])
    _attend_pairs(sink_ref, q_ref, [pair(k_ref)], [pair(v_ref)], None, o_ref, g=g)


def _attn_ctx(sink, q, k, v, n_batch, seq):
    qp_w, kv_w = q.shape[1], k.shape[1]
    q_w = qp_w // 2
    g = q_w // kv_w
    return pl.pallas_call(
        functools.partial(_attn_ctx_kernel, g=g),
        grid=(n_batch,),
        in_specs=[pl.BlockSpec(memory_space=pltpu.SMEM),
                  pl.BlockSpec((seq, qp_w), lambda b: (b, 0)),
                  pl.BlockSpec((seq, kv_w), lambda b: (b, 0)),
                  pl.BlockSpec((seq, kv_w), lambda b: (b, 0))],
        out_specs=pl.BlockSpec((seq, q_w), lambda b: (b, 0)),
        out_shape=jax.ShapeDtypeStruct((n_batch * seq, q_w), BF16),
        compiler_params=_cparams(1),
        name="attn_ctx",
    )(sink, q, k, v)


def _attn_lat_kernel(sink_ref, q_ref, k_ref, v_ref, ck_ref, cv_ref, o_ref, *, g, seq):
    i = pl.program_id(1)
    n_loc = 3 * BLK
    start = pl.multiple_of(jnp.clip((i - 1) * BLK, 0, seq - n_loc), BLK)
    row = lax.broadcasted_iota(jnp.int32, (BLK, n_loc), 0)
    col = lax.broadcasted_iota(jnp.int32, (BLK, n_loc), 1)
    band = jnp.abs(row + (i * BLK - start) - col) <= WINDOW
    loc = lambda ref: (lambda p: ref[pl.ds(start, n_loc), p * LANES:(p + 1) * LANES])
    ctx = lambda ref: (lambda p: ref[:, p * LANES:(p + 1) * LANES])
    _attend_pairs(sink_ref, q_ref, [loc(k_ref), ctx(ck_ref)], [loc(v_ref), ctx(cv_ref)], band, o_ref, g=g)


def _attn_lat(sink, q, k, v, cache_k, cache_v, layer_slot, n_batch, seq, tok_off):
    qp_w, kv_w = q.shape[1], k.shape[1]
    q_w = qp_w // 2
    g = q_w // kv_w
    nb = seq // BLK
    past = cache_k.shape[2]
    assert seq >= 3 * BLK
    kv_spec = pl.BlockSpec((seq, kv_w), lambda b, i: (tok_off // seq + b, 0))
    c_spec = pl.BlockSpec((None, None, past, kv_w), lambda b, i: (b, layer_slot, 0, 0))
    return pl.pallas_call(
        functools.partial(_attn_lat_kernel, g=g, seq=seq),
        grid=(n_batch, nb),
        in_specs=[pl.BlockSpec(memory_space=pltpu.SMEM),
                  pl.BlockSpec((BLK, qp_w), lambda b, i: (tok_off // BLK + b * nb + i, 0)),
                  kv_spec, kv_spec, c_spec, c_spec],
        out_specs=pl.BlockSpec((BLK, q_w), lambda b, i: (b * nb + i, 0)),
        out_shape=jax.ShapeDtypeStruct((n_batch * seq, q_w), BF16),
        compiler_params=_cparams(2),
        name="attn_lat",
    )(sink, q, k, v, cache_k, cache_v)


def _post_ab_kernel(rows_ref, x_ref, yap_ref, ybp_ref, yas_ref, ybs_ref, w_ref, gm_ref, lg_ref, lb_ref,
                    scf_ref, shf_ref, wr_ref, br_ref, x1_ref, hf_ref, route_ref, gate_ref, cnt_ref, carry_ref,
                    *, n_ctx_tiles, alpha):
    del rows_ref
    is_ctx = pl.program_id(0) < n_ctx_tiles
    a_w = yap_ref.shape[1]

    def proj_out(s):
        rs = s["rows"]
        ya = jnp.where(is_ctx, yap_ref[rs, :], yas_ref[rs, :])
        yb = jnp.where(is_ctx, ybp_ref[rs, :], ybs_ref[rs, :])
        return dict(s, y=_dot(ya, w_ref[:a_w, :]) + _dot(yb, w_ref[a_w:, :]))

    def tail(s):
        rs = s["rows"]
        x1, hf, lg = _tail_rows(x_ref[rs, :], s["y"], gm_ref[...], lg_ref[...], lb_ref[...], scf_ref[...],
                                shf_ref[...], wr_ref[...], br_ref[...], alpha)
        x1_ref[rs, :] = x1
        hf_ref[rs, :] = hf
        return dict(s, logits=lg)

    states = _run_skewed([proj_out, tail], [dict(rows=rs) for rs in _row_chains(x_ref.shape[0], 4)])
    _tail_route(jnp.concatenate([s["logits"] for s in states], axis=1), route_ref, gate_ref, cnt_ref, carry_ref)


def _tail_out(t, d):
    shapes = [jax.ShapeDtypeStruct((t, d), F32), jax.ShapeDtypeStruct((t, d // 2), jnp.int32),
              jax.ShapeDtypeStruct((2 * TOP_K, t), jnp.int32), jax.ShapeDtypeStruct((t, LANES), F32),
              jax.ShapeDtypeStruct((N_EXPERTS, LANES), F32)]
    specs = [pl.BlockSpec((TM, d), lambda i, r: (i, 0)), pl.BlockSpec((TM, d // 2), lambda i, r: (i, 0)),
             pl.BlockSpec((2 * TOP_K, TM), lambda i, r: (0, i)), pl.BlockSpec((TM, LANES), lambda i, r: (i, 0)),
             pl.BlockSpec((N_EXPERTS, LANES), lambda i, r: (0, 0))]
    return shapes, specs


_TAIL_SCRATCH = [pltpu.VMEM((N_EXPERTS, 1), F32)]


def _post_ab(x, ya_p, yb_p, ya_s, yb_s, w_out, mod3, mod_base, rows, ln_g, ln_b, w_r, b_r, alpha):
    t, d = x.shape
    nt = t // TM
    n_ctx_tiles = ya_p.shape[0] // TM
    a_w, q_w = ya_p.shape[1], yb_p.shape[1]
    mod_spec = lambda j: pl.BlockSpec((None, 1, d), lambda i, r: (mod_base + j * 16 + r[i], 0, 0))
    whole = lambda shp: pl.BlockSpec(shp, lambda i, r: (0,) * len(shp))
    ctx = lambda w: pl.BlockSpec((TM, w), lambda i, r: (jnp.minimum(i, n_ctx_tiles - 1), 0))
    lat = lambda w: pl.BlockSpec((TM, w), lambda i, r: (jnp.maximum(i - n_ctx_tiles, 0), 0))
    shapes, specs = _tail_out(t, d)
    grid_spec = pltpu.PrefetchScalarGridSpec(
        num_scalar_prefetch=1, grid=(nt,),
        in_specs=[pl.BlockSpec((TM, d), lambda i, r: (i, 0)), ctx(a_w), ctx(q_w), lat(a_w), lat(q_w),
                  whole(w_out.shape), mod_spec(2), whole((1, d)), whole((1, d)), mod_spec(4), mod_spec(3),
                  whole(w_r.shape), whole(b_r.shape)],
        out_specs=specs,
        scratch_shapes=_TAIL_SCRATCH,
    )
    return pl.pallas_call(
        functools.partial(_post_ab_kernel, n_ctx_tiles=n_ctx_tiles, alpha=alpha),
        grid_spec=grid_spec, out_shape=shapes, compiler_params=_cparams(1), name="post_ab",
    )(rows, x, ya_p, yb_p, ya_s, yb_s, w_out, mod3, ln_g, ln_b, mod3, mod3, w_r, b_r)


def _gmlp_kernel(rows_ref, x_ref, scm_ref, shm_ref, win_ref, bin_ref, gv_ref, bv_ref, wsp_ref, bsp_ref, wout_ref,
                 gm_ref, lg_ref, lb_ref, scf_ref, shf_ref, wr_ref, br_ref, x1_ref, hf_ref, route_ref, gate_ref,
                 cnt_ref, carry_ref, *, alpha):
    del rows_ref
    assert (x_ref.shape[0] // 2) % CHUNK == 0
    c_w = win_ref.shape[1] // 2
    gd = c_w // C_GROUPS

    def norm_in(s):
        x = x_ref[s["rows"], :]
        return dict(s, x=x, h=(_ln(x) * (1.0 + scm_ref[...]) + shm_ref[...]).astype(BF16))

    def proj_in(s):
        return dict(s, z=_dot(s["h"], win_ref[...]) + bin_ref[...])

    def gate_split(s):
        z = s["z"]
        z = 0.5 * z * (1.0 + lax.erf(z * (2.0 ** -0.5)))
        return dict(s, u=z[:, :c_w], v=(_ln(z[:, c_w:]) * gv_ref[...] + bv_ref[...]).astype(BF16))

    def spatial(s):
        v = s["v"]
        chunks = []
        for n in range(v.shape[0] // CHUNK):
            groups = [_dot(wsp_ref[g], v[n * CHUNK:(n + 1) * CHUNK, g * gd:(g + 1) * gd]) + bsp_ref[g]
                      for g in range(C_GROUPS)]
            chunks.append(jnp.concatenate(groups, axis=1))
        return dict(s, t=(s["u"] * jnp.concatenate(chunks, axis=0)).astype(BF16))

    def proj_out(s):
        return dict(s, y=_dot(s["t"], wout_ref[...]))

    def tail(s):
        x1, hf, lg = _tail_rows(s["x"], s["y"], gm_ref[...], lg_ref[...], lb_ref[...], scf_ref[...], shf_ref[...],
                                wr_ref[...], br_ref[...], alpha)
        x1_ref[s["rows"], :] = x1
        hf_ref[s["rows"], :] = hf
        return dict(s, logits=lg)

    states = _run_skewed([norm_in, proj_in, gate_split, spatial, proj_out, tail],
                         [dict(rows=rs) for rs in _row_chains(x_ref.shape[0])])
    _tail_route(jnp.concatenate([s["logits"] for s in states], axis=1), route_ref, gate_ref, cnt_ref, carry_ref)


def _gmlp(x, mod3, mod_base, rows, w_in, b_in, g_v, b_v, w_sp, b_sp, w_out, ln_g, ln_b, w_r, b_r, alpha):
    t, d = x.shape
    nt = t // TM
    mod_spec = lambda j: pl.BlockSpec((None, 1, d), lambda i, r: (mod_base + j * 16 + r[i], 0, 0))
    whole = lambda shp: pl.BlockSpec(shp, lambda i, r: (0,) * len(shp))
    shapes, specs = _tail_out(t, d)
    grid_spec = pltpu.PrefetchScalarGridSpec(
        num_scalar_prefetch=1, grid=(nt,),
        in_specs=[pl.BlockSpec((TM, d), lambda i, r: (i, 0)), mod_spec(1), mod_spec(0),
                  whole(w_in.shape), whole(b_in.shape), whole(g_v.shape), whole(b_v.shape),
                  whole(w_sp.shape), whole(b_sp.shape), whole(w_out.shape),
                  mod_spec(2), whole((1, d)), whole((1, d)), mod_spec(4), mod_spec(3),
                  whole(w_r.shape), whole(b_r.shape)],
        out_specs=specs,
        scratch_shapes=_TAIL_SCRATCH,
    )
    return pl.pallas_call(
        functools.partial(_gmlp_kernel, alpha=alpha),
        grid_spec=grid_spec, out_shape=shapes, compiler_params=_cparams(1), name="gmlp",
    )(rows, x, mod3, mod3, w_in, b_in, g_v, b_v, w_sp, b_sp, w_out, mod3, ln_g, ln_b, mod3, mod3, w_r, b_r)


def _moe_kernel(be_ref, bf_ref, nu_ref, x_ref, wgu_ref, bgu_ref, wdn_ref, bdn_ref, o_ref, wgu_s, wdn_s):
    i = pl.program_id(0)
    d_ff = wdn_ref.shape[0]

    @pl.when(i < nu_ref[0])
    def _():
        @pl.when(bf_ref[i] == 1)
        def _():
            wgu_s[...] = wgu_ref[...].astype(BF16)
            wdn_s[...] = wdn_ref[...].astype(BF16)

        gu = _dot(_unpack_bf16_pairs(x_ref[...]), wgu_s[...]) + bgu_ref[...]
        gate = jnp.minimum(gu[:, :d_ff], SWIGLU_LIMIT)
        lin = jnp.clip(gu[:, d_ff:], -SWIGLU_LIMIT, SWIGLU_LIMIT)
        glu = gate * jax.nn.sigmoid(SWIGLU_ALPHA * gate)
        hid = ((lin + 1.0) * glu).astype(BF16)
        o_ref[...] = _dot(hid, wdn_s[...]) + bdn_ref[...]


def _moe(xs, blk_e, blk_first, n_used, w_gu, b_gu, w_dn, b_dn, layer):
    n_rows = xs.shape[0]
    n_e, d, ff2 = w_gu.shape[1:]
    d_ff = ff2 // 2
    nt = n_rows // TM_MOE
    row_spec = pl.BlockSpec((TM_MOE, d), lambda i, be, bf, nu: (jnp.minimum(i, nu[0] - 1), 0))
    grid_spec = pltpu.PrefetchScalarGridSpec(
        num_scalar_prefetch=3, grid=(nt,),
        in_specs=[pl.BlockSpec((TM_MOE, d // 2), lambda i, be, bf, nu: (jnp.minimum(i, nu[0] - 1), 0)),
                  pl.BlockSpec((None, None, d, ff2), lambda i, be, bf, nu: (layer, be[i], 0, 0)),
                  pl.BlockSpec((None, None, 1, ff2), lambda i, be, bf, nu: (layer, be[i], 0, 0)),
                  pl.BlockSpec((None, None, d_ff, d), lambda i, be, bf, nu: (layer, be[i], 0, 0)),
                  pl.BlockSpec((None, None, 1, d), lambda i, be, bf, nu: (layer, be[i], 0, 0))],
        out_specs=row_spec,
        scratch_shapes=[pltpu.VMEM((d, ff2), BF16), pltpu.VMEM((d_ff, d), BF16)],
    )
    depth = w_gu.shape[0]
    return pl.pallas_call(
        _moe_kernel, grid_spec=grid_spec,
        out_shape=jax.ShapeDtypeStruct((n_rows, d), F32),
        compiler_params=_cparams(1), name="moe_experts",
    )(blk_e, blk_first, n_used, xs, w_gu, b_gu.reshape(depth, n_e, 1, ff2), w_dn, b_dn.reshape(depth, n_e, 1, d))


def _combine_kernel(rows_ref, x_ref, y0_ref, y1_ref, y2_ref, y3_ref, gate_ref, gf_ref, lg_ref, lb_ref, o_ref, *, alpha):
    del rows_ref
    gate = gate_ref[...]
    y = gate[:, 0:1] * y0_ref[...]
    for k, y_ref in ((1, y1_ref), (2, y2_ref), (3, y3_ref)):
        y = y + gate[:, k:k + 1] * y_ref[...]
    o_ref[...] = _ln(alpha * x_ref[...] + gf_ref[...] * y) * lg_ref[...] + lb_ref[...]


def _combine(x1, y4, gate, mod3, mod_base, rows, ln_g, ln_b, alpha):
    t, d = x1.shape
    nt = t // TM
    y_spec = lambda k: pl.BlockSpec((None, TM, d), lambda i, r: (k, i, 0))
    grid_spec = pltpu.PrefetchScalarGridSpec(
        num_scalar_prefetch=1, grid=(nt,),
        in_specs=[pl.BlockSpec((TM, d), lambda i, r: (i, 0)), y_spec(0), y_spec(1), y_spec(2), y_spec(3),
                  pl.BlockSpec((TM, LANES), lambda i, r: (i, 0)),
                  pl.BlockSpec((None, 1, d), lambda i, r: (mod_base + 5 * 16 + r[i], 0, 0)),
                  pl.BlockSpec((1, d), lambda i, r: (0, 0)), pl.BlockSpec((1, d), lambda i, r: (0, 0))],
        out_specs=pl.BlockSpec((TM, d), lambda i, r: (i, 0)),
    )
    return pl.pallas_call(
        functools.partial(_combine_kernel, alpha=alpha), grid_spec=grid_spec,
        out_shape=jax.ShapeDtypeStruct((t, d), F32), compiler_params=_cparams(1), name="moe_combine",
    )(rows, x1, y4, y4, y4, y4, gate, mod3, ln_g, ln_b)


def _route_plan(route, counts, n_tiles):
    idx, rank = route[:TOP_K], route[TOP_K:]
    padded = (counts + TM_MOE - 1) // TM_MOE * TM_MOE
    pend = jnp.cumsum(padded)
    pstart = pend - padded
    onehot = idx[:, :, None] == jnp.arange(N_EXPERTS, dtype=jnp.int32)[None, None, :]
    dest = jnp.sum(jnp.where(onehot, pstart[None, None, :], 0), axis=-1) + rank
    n_used = (pend[-1] // TM_MOE).astype(jnp.int32)
    tile_start = jnp.arange(n_tiles, dtype=jnp.int32) * TM_MOE
    blk_e = jnp.sum((tile_start[:, None] >= pend[None, :]).astype(jnp.int32), axis=1)
    last_e = jnp.sum((jnp.maximum(n_used - 1, 0) * TM_MOE >= pend).astype(jnp.int32))
    blk_e = jnp.minimum(jnp.where(jnp.arange(n_tiles) < n_used, blk_e, last_e), N_EXPERTS - 1).astype(jnp.int32)
    blk_first = jnp.concatenate([jnp.ones((1,), jnp.int32), (blk_e[1:] != blk_e[:-1]).astype(jnp.int32)])
    return dest, blk_e, blk_first, n_used.reshape(1)


SC_CORES = 2
SC_SUBCORES = 16


def _sc_mesh():
    return plsc.VectorSubcoreMesh(core_axis_name="core", subcore_axis_name="subcore")


def _sc_scatter_rows(src, dest, n_rows, window):
    n_slots, t = dest.shape
    w = src.shape[1]
    per_worker = t // (SC_CORES * SC_SUBCORES)
    assert per_worker * SC_CORES * SC_SUBCORES == t and per_worker % window == 0

    @functools.partial(pl.kernel, out_type=jax.ShapeDtypeStruct((n_rows, w), src.dtype), mesh=_sc_mesh(),
                       scratch_types=[pltpu.VMEM((n_slots, window), jnp.int32), pltpu.VMEM((window, w), src.dtype)],
                       name="sc_dispatch")
    def scatter(src_hbm, dest_hbm, out_hbm, idx_v, rows_v):
        worker = lax.axis_index("subcore") * SC_CORES + lax.axis_index("core")

        @pl.loop(0, per_worker // window)
        def _(c):
            base = pl.multiple_of(worker * per_worker + c * window, window)
            pltpu.sync_copy(src_hbm.at[pl.ds(base, window)], rows_v)
            pltpu.sync_copy(dest_hbm.at[:, pl.ds(base, window)], idx_v)
            for k in range(n_slots):
                pltpu.sync_copy(rows_v, out_hbm.at[idx_v.at[k]])

    return scatter(src, dest)


def _sc_gather_rows(table, idx, window):
    n = idx.shape[0]
    w = table.shape[1]
    per_worker = n // (SC_CORES * SC_SUBCORES)
    assert per_worker * SC_CORES * SC_SUBCORES == n and per_worker % window == 0

    @functools.partial(pl.kernel, out_type=jax.ShapeDtypeStruct((n, w), table.dtype), mesh=_sc_mesh(),
                       scratch_types=[pltpu.VMEM((window,), jnp.int32), pltpu.VMEM((window, w), table.dtype)],
                       name="sc_collect")
    def gather(table_hbm, idx_hbm, out_hbm, idx_v, rows_v):
        worker = lax.axis_index("subcore") * SC_CORES + lax.axis_index("core")

        @pl.loop(0, per_worker // window)
        def _(c):
            base = pl.multiple_of(worker * per_worker + c * window, window)
            pltpu.sync_copy(idx_hbm.at[pl.ds(base, window)], idx_v)
            pltpu.sync_copy(table_hbm.at[idx_v], rows_v)
            pltpu.sync_copy(rows_v, out_hbm.at[pl.ds(base, window)])

    return gather(table, idx)


def _dft_tables(n):
    j = jnp.arange(n, dtype=jnp.int32)
    ang = ((j[:, None] * j[None, :]) % n).astype(F32) * (2.0 * math.pi / n)
    return jnp.cos(ang), jnp.sin(ang)


def _rope_tables(n_lat, n_ctx_rows):
    half, quarter = HEAD_DIM // 2, HEAD_DIM // 4
    tpos = jnp.arange(n_lat, dtype=jnp.int32)
    lane = jnp.arange(LANES, dtype=jnp.int32) % HEAD_DIM
    pos = jnp.where(lane[None, :] < half, (tpos // GRID_W)[:, None], (tpos % GRID_W)[:, None]).astype(F32)
    fidx = (lane % quarter).astype(F32)
    freqs = ROPE_THETA ** (-fidx / quarter)
    ang = pos * freqs[None, :]
    cos, sin = jnp.cos(ang), jnp.sin(ang)
    first = (lane % half) < quarter
    sa = jnp.where(first[None, :], -sin, 0.0)
    sb = jnp.where(first[None, :], 0.0, sin)
    ident = lambda v: jnp.full((n_ctx_rows, LANES), v, F32)
    return (jnp.concatenate([ident(1.0), cos]), jnp.concatenate([ident(0.0), sa]), jnp.concatenate([ident(0.0), sb]))


def kernel(x_prompt, x_sample, cache_k_ab, cache_v_ab, c, c_ctx, w_ada, b_ada, ln_mix_g, ln_mix_b, ln_ffn_g, ln_ffn_b, w_in_ab, w_out_ab, sink_ab, w_in_c, b_in_c, ln_v_g, ln_v_b, w_sp, b_sp, w_out_c, w_router, b_router, w_gu, b_gu, w_dn, b_dn):
    n_ctx_b, ctx_seq, d = x_prompt.shape
    n_lat_b, lat_seq, _ = x_sample.shape
    depth = w_ada.shape[0]
    t_ctx, t_lat = n_ctx_b * ctx_seq, n_lat_b * lat_seq
    t = t_ctx + t_lat
    alpha = (2 * depth) ** 0.25
    kv_w = N_KV_HEADS * HEAD_DIM
    a_w = d // 4
    q_w = d - a_w
    assert t_ctx % TM == 0 and lat_seq % TM == 0 and t_ctx % lat_seq == 0 and n_lat_b + 1 <= 16

    rows_np = np.concatenate([np.zeros(t_ctx // TM, np.int32), 1 + np.arange(t_lat // TM, dtype=np.int32) // (lat_seq // TM)])
    rblk_np = np.concatenate([np.zeros(t_ctx // TM, np.int32), 1 + np.arange(t_lat // TM, dtype=np.int32) % (lat_seq // TM)])
    rows, rblk = jnp.asarray(rows_np), jnp.asarray(rblk_np)

    cond16 = jnp.zeros((16, d), F32).at[0].set(c_ctx).at[1:1 + n_lat_b].set(c)
    mod = _ada(cond16, w_ada, b_ada)
    mod3 = mod.reshape(depth, 16, 6, d).transpose(0, 2, 1, 3).reshape(depth * 6 * 16, 1, d)

    cos, sa, sb = _rope_tables(lat_seq, TM)
    cs_ctx, ss_ctx = (z.astype(BF16) for z in _dft_tables(ctx_seq))
    cs_lat, ss_lat = (z.astype(BF16) for z in _dft_tables(lat_seq))
    gd = a_w // A_GROUPS
    cd, sd = _dft_tables(gd)
    eye = jnp.eye(A_GROUPS, dtype=F32)
    bdc, bds = jnp.kron(eye, cd).astype(BF16), jnp.kron(eye, sd).astype(BF16)

    past = cache_k_ab.shape[2]
    cache_k = cache_k_ab.reshape(n_lat_b, -1, past, kv_w).astype(BF16)
    cache_v = cache_v_ab.reshape(n_lat_b, -1, past, kv_w).astype(BF16)

    x = jnp.concatenate([x_prompt.reshape(t_ctx, d), x_sample.reshape(t_lat, d)], axis=0)
    row2 = lambda v: v.reshape(1, -1)
    ks, vs = [], []
    for l in range(depth):
        j = l // 2
        mod_base = l * 6 * 16
        w_r, b_r = w_router[l].T, b_router[l].reshape(-1, 1)
        if l % 2 == 0:
            ac, as_, q, k32, v32, k, v = _inproj(x, mod3, mod_base, rows, rblk, w_in_ab[j].astype(BF16), cos, sa, sb,
                                                 bdc, bds, a_w, q_w, kv_w)
            ks.append(k32[:t_ctx].reshape(n_ctx_b, ctx_seq, N_KV_HEADS, HEAD_DIM))
            vs.append(v32[:t_ctx].reshape(n_ctx_b, ctx_seq, N_KV_HEADS, HEAD_DIM))
            ya_p = _fourier(ac, as_, cs_ctx, ss_ctx, n_ctx_b, ctx_seq, 0, min(ctx_seq, 512))
            ya_s = _fourier(ac, as_, cs_lat, ss_lat, n_lat_b, lat_seq, t_ctx // lat_seq, min(lat_seq, 512))
            yb_p = _attn_ctx(sink_ab[j], q, k, v, n_ctx_b, ctx_seq)
            yb_s = _attn_lat(sink_ab[j], q, k, v, cache_k, cache_v, j, n_lat_b, lat_seq, t_ctx)
            x1, hf, route, gate, cnt = _post_ab(x, ya_p, yb_p, ya_s, yb_s, w_out_ab[j].astype(BF16), mod3, mod_base,
                                                rows, row2(ln_mix_g[l]), row2(ln_mix_b[l]), w_r, b_r, alpha)
        else:
            x1, hf, route, gate, cnt = _gmlp(x, mod3, mod_base, rows, w_in_c[j].astype(BF16), row2(b_in_c[j]),
                                             row2(ln_v_g[j]), row2(ln_v_b[j]), w_sp[j].astype(BF16),
                                             b_sp[j][:, :, None], w_out_c[j].astype(BF16), row2(ln_mix_g[l]),
                                             row2(ln_mix_b[l]), w_r, b_r, alpha)
        n_rows = t * TOP_K + N_EXPERTS * TM_MOE
        dest, blk_e, blk_first, n_used = _route_plan(route, cnt[:, 0].astype(jnp.int32), n_rows // TM_MOE)
        xs = _sc_scatter_rows(hf, dest, n_rows, 128)
        out_sorted = _moe(xs, blk_e, blk_first, n_used, w_gu, b_gu, w_dn, b_dn, l)
        y4 = _sc_gather_rows(out_sorted, dest.reshape(-1), 64).reshape(TOP_K, t, d)
        x = _combine(x1, y4, gate, mod3, mod_base, rows, row2(ln_ffn_g[l]), row2(ln_ffn_b[l]), alpha)

    y_prompt = x[:t_ctx].reshape(n_ctx_b, ctx_seq, d)
    y_sample = x[t_ctx:].reshape(n_lat_b, lat_seq, d)
    return (y_prompt, y_sample, jnp.stack(ks, axis=1), jnp.stack(vs, axis=1))
```

```python
import functools
import math

import numpy as np
import jax
import jax.numpy as jnp
from jax import lax
from jax.experimental import pallas as pl
from jax.experimental.pallas import tpu as pltpu
from jax.experimental.pallas import tpu_sc as plsc

GRID_W = 64
BLK = 128
WINDOW = 128
HEAD_DIM = 64
A_GROUPS = 4
N_KV_HEADS = 4
C_GROUPS = 8
CHUNK = 128
N_EXPERTS = 32
TOP_K = 4
SWIGLU_LIMIT = 7.0
SWIGLU_ALPHA = 1.702
ROPE_THETA = 10000.0
LN_EPS = 1e-6
NEG_INF = -1e30
LOG2_E = math.log2(math.e)

LANES = 128
TM = 512
TM_MOE = 256
VMEM_LIMIT = 52 * 1024 * 1024

F32 = jnp.float32
BF16 = jnp.bfloat16


def _cparams(n_axes):
    return pltpu.CompilerParams(dimension_semantics=("arbitrary",) * n_axes,
                                vmem_limit_bytes=VMEM_LIMIT)


def _ln(x):
    mu = jnp.mean(x, axis=-1, keepdims=True)
    xc = x - mu
    var = jnp.mean(xc * xc, axis=-1, keepdims=True)
    return xc * lax.rsqrt(var + LN_EPS)


def _dot(a, b):
    return jnp.dot(a, b, preferred_element_type=F32)


def _dot_nt(a, b):
    return lax.dot_general(a, b, (((1,), (1,)), ((), ())), preferred_element_type=F32)


def _split(a):
    hi = a.astype(BF16)
    lo = (a - hi.astype(F32)).astype(BF16)
    return hi, lo


def _dot_3pass(a, b):
    a_hi, a_lo = _split(a)
    b_hi, b_lo = _split(b)
    return _dot(a_hi, b_hi) + (_dot(a_hi, b_lo) + _dot(a_lo, b_hi))


def _lane_select(cols, width):
    m = cols[0].shape[0]
    lane = lax.broadcasted_iota(jnp.int32, (m, width), 1)
    out = jnp.zeros((m, width), cols[0].dtype)
    for j, c in enumerate(cols):
        out = jnp.where(lane == j, c, out)
    return out


def _pack_bf16_pairs(v):
    n = v.shape[1] // 2
    lo = pltpu.bitcast(v[:, :n].astype(BF16).astype(F32), jnp.uint32) >> 16
    hi = pltpu.bitcast(v[:, n:].astype(BF16).astype(F32), jnp.uint32) & jnp.uint32(0xFFFF0000)
    return pltpu.bitcast(lo | hi, jnp.int32)


def _unpack_bf16_pairs(p):
    u = pltpu.bitcast(p, jnp.uint32)
    lo = pltpu.bitcast(u << 16, F32)
    hi = pltpu.bitcast(u & jnp.uint32(0xFFFF0000), F32)
    return jnp.concatenate([lo, hi], axis=1).astype(BF16)


def _row_chains(n_rows, n_chains=2):
    step = n_rows // n_chains
    return [slice(c * step, (c + 1) * step) for c in range(n_chains)]


def _run_skewed(stages, states):
    states = list(states)
    for step in range(len(stages) + len(states) - 1):
        for c in range(len(states)):
            if 0 <= step - c < len(stages):
                states[c] = stages[step - c](states[c])
    return states


def _tail_rows(x, y, g_m, ln_g, ln_b, sc_f, sh_f, w_r, b_r, alpha):
    x1 = _ln(alpha * x + g_m * y) * ln_g + ln_b
    hf = _ln(x1) * (1.0 + sc_f) + sh_f
    hf_hi, hf_lo = _split(hf)
    w_hi, w_lo = _split(w_r)
    logits = _dot_nt(w_hi, hf_hi) + (_dot_nt(w_hi, hf_lo) + _dot_nt(w_lo, hf_hi)) + b_r
    return x1, _pack_bf16_pairs(hf), logits


def _tail_route(logits, route_ref, gate_ref, cnt_ref, carry_ref):
    @pl.when(pl.program_id(0) == 0)
    def _():
        carry_ref[...] = jnp.zeros_like(carry_ref)

    tm = logits.shape[1]
    sub = lax.broadcasted_iota(jnp.int32, logits.shape, 0)
    vals = logits
    top_v, top_i = [], []
    for _ in range(TOP_K):
        m = jnp.max(vals, axis=0, keepdims=True)
        am = jnp.min(jnp.where(vals == m, sub, N_EXPERTS), axis=0, keepdims=True)
        top_v.append(m)
        top_i.append(am)
        vals = jnp.where(sub == am, -jnp.inf, vals)
    e = [jnp.exp(v - top_v[0]) for v in top_v]
    denom = e[0] + e[1] + e[2] + e[3]
    gates_t = jnp.concatenate([ek / denom for ek in e] + [jnp.zeros((LANES - TOP_K, tm), F32)], axis=0)
    gate_ref[...] = gates_t.T

    member = jnp.zeros(logits.shape, F32)
    for am in top_i:
        member = jnp.where(sub == am, 1.0, member)
    r_i = lax.broadcasted_iota(jnp.int32, (tm, tm), 0)
    c_i = lax.broadcasted_iota(jnp.int32, (tm, tm), 1)
    earlier = jnp.where(r_i < c_i, 1.0, 0.0).astype(BF16)
    before = _dot(member.astype(BF16), earlier) + carry_ref[...]
    ranks = [jnp.sum(jnp.where(sub == am, before, 0.0), axis=0, keepdims=True).astype(jnp.int32) for am in top_i]
    route_ref[...] = jnp.concatenate(top_i + ranks, axis=0)
    carry = carry_ref[...] + jnp.sum(member, axis=1, keepdims=True)
    carry_ref[...] = carry
    cnt_ref[...] = jnp.broadcast_to(carry, cnt_ref.shape)


def _ada_kernel(cond_ref, w_ref, b_ref, o_ref):
    c = cond_ref[...]
    s = (c * jax.nn.sigmoid(c)).astype(BF16)
    o_ref[...] = _dot(s, w_ref[...].astype(BF16)) + b_ref[...]


def _ada(cond16, w_ada, b_ada):
    depth, d, n = w_ada.shape
    tn = 1536
    return pl.pallas_call(
        _ada_kernel,
        grid=(depth, n // tn),
        in_specs=[
            pl.BlockSpec((16, d), lambda l, j: (0, 0)),
            pl.BlockSpec((None, d, tn), lambda l, j: (l, 0, j)),
            pl.BlockSpec((None, 1, tn), lambda l, j: (l, 0, j)),
        ],
        out_specs=pl.BlockSpec((None, 16, tn), lambda l, j: (l, 0, j)),
        out_shape=jax.ShapeDtypeStruct((depth, 16, n), F32),
        compiler_params=_cparams(2),
        name="ada_mod",
    )(cond16, w_ada, b_ada.reshape(depth, 1, n))


def _inproj_kernel(rows_ref, rblk_ref, x_ref, sc_ref, sh_ref, w_ref, cos_ref, sa_ref, sb_ref, bdc_ref, bds_ref,
                   ac_ref, as_ref, q_ref, k_ref, v_ref, kb_ref, vb_ref, *, a_w, q_w, kv_w):
    del rows_ref, rblk_ref
    g = q_w // kv_w

    def norm_in(s):
        return dict(s, h=(_ln(x_ref[s["rows"], :]) * (1.0 + sc_ref[...]) + sh_ref[...]).astype(BF16))

    def proj(s):
        return dict(s, p=_dot(s["h"], w_ref[...]))

    def finish(s):
        rs, p = s["rows"], s["p"]
        a = p[:, :a_w].astype(BF16)
        ac_ref[rs, :] = _dot(a, bdc_ref[...]).astype(BF16)
        as_ref[rs, :] = _dot(a, bds_ref[...]).astype(BF16)
        cos, sa, sb = cos_ref[rs, :], sa_ref[rs, :], sb_ref[rs, :]

        def rope(t):
            w = t.shape[1]
            reps = w // LANES
            c, a_, b_ = (jnp.tile(z, (1, reps)) for z in (cos, sa, sb))
            nxt = pltpu.roll(t, w - HEAD_DIM // 4, 1)
            prv = pltpu.roll(t, HEAD_DIM // 4, 1)
            return t * c + nxt * a_ + prv * b_

        q = rope(p[:, a_w:a_w + q_w]) * (HEAD_DIM ** -0.5 * LOG2_E)
        lane = lax.broadcasted_iota(jnp.int32, (q.shape[0], LANES), 1)
        for j in range(q_w // HEAD_DIM):
            tile = q[:, (j // 2) * LANES:(j // 2 + 1) * LANES]
            dst_low = (j // g) % 2 == 0
            if (j % 2 == 0) != dst_low:
                tile = pltpu.roll(tile, HEAD_DIM, 1)
            keep = (lane < HEAD_DIM) if dst_low else (lane >= HEAD_DIM)
            q_ref[rs, j * LANES:(j + 1) * LANES] = jnp.where(keep, tile, 0.0).astype(BF16)
        k = rope(p[:, a_w + q_w:a_w + q_w + kv_w])
        v = p[:, a_w + q_w + kv_w:]
        k_ref[rs, :] = k
        v_ref[rs, :] = v
        kb_ref[rs, :] = k.astype(BF16)
        vb_ref[rs, :] = v.astype(BF16)
        return s

    _run_skewed([norm_in, proj, finish], [dict(rows=rs) for rs in _row_chains(x_ref.shape[0])])


def _inproj(x, mod3, mod_base, rows, rblk, w_in, cos, sa, sb, bdc, bds, a_w, q_w, kv_w):
    t, d = x.shape
    n = w_in.shape[1]
    nt = t // TM
    mod_spec = lambda j: pl.BlockSpec((None, 1, d), lambda i, r, rb: (mod_base + j * 16 + r[i], 0, 0))
    whole = lambda shp: pl.BlockSpec(shp, lambda i, r, rb: (0,) * len(shp))
    rope_spec = pl.BlockSpec((TM, LANES), lambda i, r, rb: (rb[i], 0))
    tok = lambda w: pl.BlockSpec((TM, w), lambda i, r, rb: (i, 0))
    grid_spec = pltpu.PrefetchScalarGridSpec(
        num_scalar_prefetch=2, grid=(nt,),
        in_specs=[tok(d), mod_spec(1), mod_spec(0), whole((d, n)), rope_spec, rope_spec, rope_spec,
                  whole((a_w, a_w)), whole((a_w, a_w))],
        out_specs=[tok(a_w), tok(a_w), tok(2 * q_w), tok(kv_w), tok(kv_w), tok(kv_w), tok(kv_w)],
    )
    return pl.pallas_call(
        functools.partial(_inproj_kernel, a_w=a_w, q_w=q_w, kv_w=kv_w),
        grid_spec=grid_spec,
        out_shape=[jax.ShapeDtypeStruct((t, a_w), BF16), jax.ShapeDtypeStruct((t, a_w), BF16),
                   jax.ShapeDtypeStruct((t, 2 * q_w), BF16), jax.ShapeDtypeStruct((t, kv_w), F32),
                   jax.ShapeDtypeStruct((t, kv_w), F32), jax.ShapeDtypeStruct((t, kv_w), BF16),
                   jax.ShapeDtypeStruct((t, kv_w), BF16)],
        compiler_params=_cparams(1),
        name="inproj_ab",
    )(rows, rblk, x, mod3, mod3, w_in, cos, sa, sb, bdc, bds)


def _fourier_kernel(c_ref, s_ref, ac_ref, as_ref, o_ref, *, scale):
    y = _dot(c_ref[...], ac_ref[...]) - _dot(s_ref[...], as_ref[...])
    o_ref[...] = (y * scale).astype(BF16)


def _fourier(ac, as_, cs, ss, n_batch, seq, blk_off, tm):
    a_w = ac.shape[1]
    nr = seq // tm
    a_spec = pl.BlockSpec((seq, a_w), lambda r, b: (blk_off + b, 0))
    t_spec = pl.BlockSpec((tm, seq), lambda r, b: (r, 0))
    return pl.pallas_call(
        functools.partial(_fourier_kernel, scale=(seq * (a_w // A_GROUPS)) ** -0.5),
        grid=(nr, n_batch),
        in_specs=[t_spec, t_spec, a_spec, a_spec],
        out_specs=pl.BlockSpec((tm, a_w), lambda r, b: (b * nr + r, 0)),
        out_shape=jax.ShapeDtypeStruct((n_batch * seq, a_w), BF16),
        compiler_params=_cparams(2),
        name="fourier_%d" % seq,
    )(cs, ss, ac, as_)


def _softmax_pv(scores, sink_col, values):
    m = sink_col
    for s in scores:
        m = jnp.maximum(m, jnp.max(s, axis=-1, keepdims=True))
    acc = None
    for s, v in zip(scores, values):
        e = jnp.exp2(s - m).astype(BF16)
        pv = _dot(e, jnp.concatenate([v, jnp.ones_like(v)], axis=1))
        acc = pv if acc is None else acc + pv
    d = values[0].shape[1]
    return acc[:, :d] / (acc[:, d:d + 1] + jnp.exp2(sink_col - m))


def _attend_pairs(sink_ref, q_ref, keys, values, mask, o_ref, *, g):
    rows = q_ref.shape[0]
    per_pair = 2 * g
    lane = lax.broadcasted_iota(jnp.int32, (rows, LANES), 1)
    full_mask = None if mask is None else jnp.concatenate([mask] * per_pair, axis=0)
    for p in range(N_KV_HEADS // 2):
        h0 = p * per_pair
        qp = jnp.concatenate([q_ref[:, (h0 + j) * LANES:(h0 + j + 1) * LANES] for j in range(per_pair)], axis=0)
        sink_col = jnp.concatenate([jnp.full((rows, 1), sink_ref[h0 + j] * LOG2_E, F32) for j in range(per_pair)],
                                   axis=0)
        scores = [_dot_nt(qp, kf(p)) for kf in keys]
        if full_mask is not None:
            scores[0] = jnp.where(full_mask, scores[0], NEG_INF)
        o = _softmax_pv(scores, sink_col, [vf(p) for vf in values])
        for t in range(per_pair // 2):
            halves = []
            for j in (2 * t, 2 * t + 1):
                blk = o[j * rows:(j + 1) * rows]
                valid_low = ((h0 + j) // g) % 2 == 0
                if valid_low != (j % 2 == 0):
                    blk = pltpu.roll(blk, HEAD_DIM, 1)
                halves.append(blk)
            tile = jnp.where(lane < HEAD_DIM, halves[0], halves[1])
            o_ref[:, (h0 // 2 + t) * LANES:(h0 // 2 + t + 1) * LANES] = tile.astype(BF16)


def _attn_ctx_kernel(sink_ref, q_ref, k_ref, v_ref, o_ref, *, g):
    pair = lambda ref: (lambda p: ref[:, p * LANES:(p + 1) * LANES])
    _attend_pairs(sink_ref, q_ref, [pair(k_ref)], [pair(v_ref)], None, o_ref, g=g)


def _attn_ctx(sink, q, k, v, n_batch, seq):
    qp_w, kv_w = q.shape[1], k.shape[1]
    q_w = qp_w // 2
    g = q_w // kv_w
    return pl.pallas_call(
        functools.partial(_attn_ctx_kernel, g=g),
        grid=(n_batch,),
        in_specs=[pl.BlockSpec(memory_space=pltpu.SMEM),
                  pl.BlockSpec((seq, qp_w), lambda b: (b, 0)),
                  pl.BlockSpec((seq, kv_w), lambda b: (b, 0)),
                  pl.BlockSpec((seq, kv_w), lambda b: (b, 0))],
        out_specs=pl.BlockSpec((seq, q_w), lambda b: (b, 0)),
        out_shape=jax.ShapeDtypeStruct((n_batch * seq, q_w), BF16),
        compiler_params=_cparams(1),
        name="attn_ctx",
    )(sink, q, k, v)


def _attn_lat_kernel(sink_ref, q_ref, k_ref, v_ref, ck_ref, cv_ref, o_ref, *, g, seq):
    i = pl.program_id(1)
    n_loc = 3 * BLK
    start = pl.multiple_of(jnp.clip((i - 1) * BLK, 0, seq - n_loc), BLK)
    row = lax.broadcasted_iota(jnp.int32, (BLK, n_loc), 0)
    col = lax.broadcasted_iota(jnp.int32, (BLK, n_loc), 1)
    band = jnp.abs(row + (i * BLK - start) - col) <= WINDOW
    loc = lambda ref: (lambda p: ref[pl.ds(start, n_loc), p * LANES:(p + 1) * LANES])
    ctx = lambda ref: (lambda p: ref[:, p * LANES:(p + 1) * LANES])
    _attend_pairs(sink_ref, q_ref, [loc(k_ref), ctx(ck_ref)], [loc(v_ref), ctx(cv_ref)], band, o_ref, g=g)


def _attn_lat(sink, q, k, v, cache_k, cache_v, layer_slot, n_batch, seq, tok_off):
    qp_w, kv_w = q.shape[1], k.shape[1]
    q_w = qp_w // 2
    g = q_w // kv_w
    nb = seq // BLK
    past = cache_k.shape[2]
    assert seq >= 3 * BLK
    kv_spec = pl.BlockSpec((seq, kv_w), lambda b, i: (tok_off // seq + b, 0))
    c_spec = pl.BlockSpec((None, None, past, kv_w), lambda b, i: (b, layer_slot, 0, 0))
    return pl.pallas_call(
        functools.partial(_attn_lat_kernel, g=g, seq=seq),
        grid=(n_batch, nb),
        in_specs=[pl.BlockSpec(memory_space=pltpu.SMEM),
                  pl.BlockSpec((BLK, qp_w), lambda b, i: (tok_off // BLK + b * nb + i, 0)),
                  kv_spec, kv_spec, c_spec, c_spec],
        out_specs=pl.BlockSpec((BLK, q_w), lambda b, i: (b * nb + i, 0)),
        out_shape=jax.ShapeDtypeStruct((n_batch * seq, q_w), BF16),
        compiler_params=_cparams(2),
        name="attn_lat",
    )(sink, q, k, v, cache_k, cache_v)


def _post_ab_kernel(rows_ref, x_ref, yap_ref, ybp_ref, yas_ref, ybs_ref, w_ref, gm_ref, lg_ref, lb_ref,
                    scf_ref, shf_ref, wr_ref, br_ref, x1_ref, hf_ref, route_ref, gate_ref, cnt_ref, carry_ref,
                    *, n_ctx_tiles, alpha):
    del rows_ref
    is_ctx = pl.program_id(0) < n_ctx_tiles
    a_w = yap_ref.shape[1]

    def proj_out(s):
        rs = s["rows"]
        ya = jnp.where(is_ctx, yap_ref[rs, :], yas_ref[rs, :])
        yb = jnp.where(is_ctx, ybp_ref[rs, :], ybs_ref[rs, :])
        return dict(s, y=_dot(ya, w_ref[:a_w, :]) + _dot(yb, w_ref[a_w:, :]))

    def tail(s):
        rs = s["rows"]
        x1, hf, lg = _tail_rows(x_ref[rs, :], s["y"], gm_ref[...], lg_ref[...], lb_ref[...], scf_ref[...],
                                shf_ref[...], wr_ref[...], br_ref[...], alpha)
        x1_ref[rs, :] = x1
        hf_ref[rs, :] = hf
        return dict(s, logits=lg)

    states = _run_skewed([proj_out, tail], [dict(rows=rs) for rs in _row_chains(x_ref.shape[0], 4)])
    _tail_route(jnp.concatenate([s["logits"] for s in states], axis=1), route_ref, gate_ref, cnt_ref, carry_ref)


def _tail_out(t, d):
    shapes = [jax.ShapeDtypeStruct((t, d), F32), jax.ShapeDtypeStruct((t, d // 2), jnp.int32),
              jax.ShapeDtypeStruct((2 * TOP_K, t), jnp.int32), jax.ShapeDtypeStruct((t, LANES), F32),
              jax.ShapeDtypeStruct((N_EXPERTS, LANES), F32)]
    specs = [pl.BlockSpec((TM, d), lambda i, r: (i, 0)), pl.BlockSpec((TM, d // 2), lambda i, r: (i, 0)),
             pl.BlockSpec((2 * TOP_K, TM), lambda i, r: (0, i)), pl.BlockSpec((TM, LANES), lambda i, r: (i, 0)),
             pl.BlockSpec((N_EXPERTS, LANES), lambda i, r: (0, 0))]
    return shapes, specs


_TAIL_SCRATCH = [pltpu.VMEM((N_EXPERTS, 1), F32)]


def _post_ab(x, ya_p, yb_p, ya_s, yb_s, w_out, mod3, mod_base, rows, ln_g, ln_b, w_r, b_r, alpha):
    t, d = x.shape
    nt = t // TM
    n_ctx_tiles = ya_p.shape[0] // TM
    a_w, q_w = ya_p.shape[1], yb_p.shape[1]
    mod_spec = lambda j: pl.BlockSpec((None, 1, d), lambda i, r: (mod_base + j * 16 + r[i], 0, 0))
    whole = lambda shp: pl.BlockSpec(shp, lambda i, r: (0,) * len(shp))
    ctx = lambda w: pl.BlockSpec((TM, w), lambda i, r: (jnp.minimum(i, n_ctx_tiles - 1), 0))
    lat = lambda w: pl.BlockSpec((TM, w), lambda i, r: (jnp.maximum(i - n_ctx_tiles, 0), 0))
    shapes, specs = _tail_out(t, d)
    grid_spec = pltpu.PrefetchScalarGridSpec(
        num_scalar_prefetch=1, grid=(nt,),
        in_specs=[pl.BlockSpec((TM, d), lambda i, r: (i, 0)), ctx(a_w), ctx(q_w), lat(a_w), lat(q_w),
                  whole(w_out.shape), mod_spec(2), whole((1, d)), whole((1, d)), mod_spec(4), mod_spec(3),
                  whole(w_r.shape), whole(b_r.shape)],
        out_specs=specs,
        scratch_shapes=_TAIL_SCRATCH,
    )
    return pl.pallas_call(
        functools.partial(_post_ab_kernel, n_ctx_tiles=n_ctx_tiles, alpha=alpha),
        grid_spec=grid_spec, out_shape=shapes, compiler_params=_cparams(1), name="post_ab",
    )(rows, x, ya_p, yb_p, ya_s, yb_s, w_out, mod3, ln_g, ln_b, mod3, mod3, w_r, b_r)


def _gmlp_kernel(rows_ref, x_ref, scm_ref, shm_ref, win_ref, bin_ref, gv_ref, bv_ref, wsp_ref, bsp_ref, wout_ref,
                 gm_ref, lg_ref, lb_ref, scf_ref, shf_ref, wr_ref, br_ref, x1_ref, hf_ref, route_ref, gate_ref,
                 cnt_ref, carry_ref, *, alpha):
    del rows_ref
    assert (x_ref.shape[0] // 2) % CHUNK == 0
    c_w = win_ref.shape[1] // 2
    gd = c_w // C_GROUPS

    def norm_in(s):
        x = x_ref[s["rows"], :]
        return dict(s, x=x, h=(_ln(x) * (1.0 + scm_ref[...]) + shm_ref[...]).astype(BF16))

    def proj_in(s):
        return dict(s, z=_dot(s["h"], win_ref[...]) + bin_ref[...])

    def gate_split(s):
        z = s["z"]
        z = 0.5 * z * (1.0 + lax.erf(z * (2.0 ** -0.5)))
        return dict(s, u=z[:, :c_w], v=(_ln(z[:, c_w:]) * gv_ref[...] + bv_ref[...]).astype(BF16))

    def spatial(s):
        v = s["v"]
        chunks = []
        for n in range(v.shape[0] // CHUNK):
            groups = [_dot(wsp_ref[g], v[n * CHUNK:(n + 1) * CHUNK, g * gd:(g + 1) * gd]) + bsp_ref[g]
                      for g in range(C_GROUPS)]
            chunks.append(jnp.concatenate(groups, axis=1))
        return dict(s, t=(s["u"] * jnp.concatenate(chunks, axis=0)).astype(BF16))

    def proj_out(s):
        return dict(s, y=_dot(s["t"], wout_ref[...]))

    def tail(s):
        x1, hf, lg = _tail_rows(s["x"], s["y"], gm_ref[...], lg_ref[...], lb_ref[...], scf_ref[...], shf_ref[...],
                                wr_ref[...], br_ref[...], alpha)
        x1_ref[s["rows"], :] = x1
        hf_ref[s["rows"], :] = hf
        return dict(s, logits=lg)

    states = _run_skewed([norm_in, proj_in, gate_split, spatial, proj_out, tail],
                         [dict(rows=rs) for rs in _row_chains(x_ref.shape[0])])
    _tail_route(jnp.concatenate([s["logits"] for s in states], axis=1), route_ref, gate_ref, cnt_ref, carry_ref)


def _gmlp(x, mod3, mod_base, rows, w_in, b_in, g_v, b_v, w_sp, b_sp, w_out, ln_g, ln_b, w_r, b_r, alpha):
    t, d = x.shape
    nt = t // TM
    mod_spec = lambda j: pl.BlockSpec((None, 1, d), lambda i, r: (mod_base + j * 16 + r[i], 0, 0))
    whole = lambda shp: pl.BlockSpec(shp, lambda i, r: (0,) * len(shp))
    shapes, specs = _tail_out(t, d)
    grid_spec = pltpu.PrefetchScalarGridSpec(
        num_scalar_prefetch=1, grid=(nt,),
        in_specs=[pl.BlockSpec((TM, d), lambda i, r: (i, 0)), mod_spec(1), mod_spec(0),
                  whole(w_in.shape), whole(b_in.shape), whole(g_v.shape), whole(b_v.shape),
                  whole(w_sp.shape), whole(b_sp.shape), whole(w_out.shape),
                  mod_spec(2), whole((1, d)), whole((1, d)), mod_spec(4), mod_spec(3),
                  whole(w_r.shape), whole(b_r.shape)],
        out_specs=specs,
        scratch_shapes=_TAIL_SCRATCH,
    )
    return pl.pallas_call(
        functools.partial(_gmlp_kernel, alpha=alpha),
        grid_spec=grid_spec, out_shape=shapes, compiler_params=_cparams(1), name="gmlp",
    )(rows, x, mod3, mod3, w_in, b_in, g_v, b_v, w_sp, b_sp, w_out, mod3, ln_g, ln_b, mod3, mod3, w_r, b_r)


def _moe_kernel(be_ref, bf_ref, nx_ref, sl_ref, nu_ref, x_ref, wgu_hbm, bgu_ref, wdn_hbm, bdn_ref, o_ref,
                wgu_f, wdn_f, wgu_s, wdn_s, sem, *, layer):
    i = pl.program_id(0)
    d_ff = wdn_s.shape[0]

    def weight_copies(e, slot):
        return (pltpu.make_async_copy(wgu_hbm.at[layer, e], wgu_f.at[slot], sem.at[0, slot]),
                pltpu.make_async_copy(wdn_hbm.at[layer, e], wdn_f.at[slot], sem.at[1, slot]))

    @pl.when(i < nu_ref[0])
    def _():
        @pl.when(bf_ref[i] == 1)
        def _():
            slot = sl_ref[i]

            @pl.when(i == 0)
            def _():
                for c in weight_copies(be_ref[i], slot):
                    c.start()

            for c in weight_copies(be_ref[i], slot):
                c.wait()
            wgu_s[...] = wgu_f[slot].astype(BF16)
            wdn_s[...] = wdn_f[slot].astype(BF16)

            @pl.when(nx_ref[i] >= 0)
            def _():
                for c in weight_copies(nx_ref[i], 1 - slot):
                    c.start()

        gu = _dot(_unpack_bf16_pairs(x_ref[...]), wgu_s[...]) + bgu_ref[...]
        gate = jnp.minimum(gu[:, :d_ff], SWIGLU_LIMIT)
        lin = jnp.clip(gu[:, d_ff:], -SWIGLU_LIMIT, SWIGLU_LIMIT)
        glu = gate * jax.nn.sigmoid(SWIGLU_ALPHA * gate)
        hid = ((lin + 1.0) * glu).astype(BF16)
        o_ref[...] = _pack_bf16_pairs(_dot(hid, wdn_s[...]) + bdn_ref[...])


def _moe(xs, plan, w_gu, b_gu, w_dn, b_dn, layer):
    n_rows = xs.shape[0]
    depth, n_e, d, ff2 = w_gu.shape
    d_ff = ff2 // 2
    nt = n_rows // TM_MOE
    n_plan = len(plan)
    row_spec = pl.BlockSpec((TM_MOE, d // 2), lambda i, *p: (jnp.minimum(i, p[-1][0] - 1), 0))
    bias_spec = lambda w: pl.BlockSpec((None, None, 1, w), lambda i, *p: (layer, p[0][i], 0, 0))
    grid_spec = pltpu.PrefetchScalarGridSpec(
        num_scalar_prefetch=n_plan, grid=(nt,),
        in_specs=[row_spec, pl.BlockSpec(memory_space=pl.ANY), bias_spec(ff2),
                  pl.BlockSpec(memory_space=pl.ANY), bias_spec(d)],
        out_specs=row_spec,
        scratch_shapes=[pltpu.VMEM((2, d, ff2), F32), pltpu.VMEM((2, d_ff, d), F32),
                        pltpu.VMEM((d, ff2), BF16), pltpu.VMEM((d_ff, d), BF16),
                        pltpu.SemaphoreType.DMA((2, 2))],
    )
    return pl.pallas_call(
        functools.partial(_moe_kernel, layer=layer), grid_spec=grid_spec,
        out_shape=jax.ShapeDtypeStruct((n_rows, d // 2), jnp.int32),
        compiler_params=_cparams(1), name="moe_experts",
    )(*plan, xs, w_gu, b_gu.reshape(depth, n_e, 1, ff2), w_dn, b_dn.reshape(depth, n_e, 1, d))


def _unpack_f32_pairs(p):
    u = pltpu.bitcast(p, jnp.uint32)
    return jnp.concatenate([pltpu.bitcast(u << 16, F32), pltpu.bitcast(u & jnp.uint32(0xFFFF0000), F32)], axis=1)


def _combine_kernel(rows_ref, x_ref, y0_ref, y1_ref, y2_ref, y3_ref, gate_ref, gf_ref, lg_ref, lb_ref, o_ref, *, alpha):
    del rows_ref
    gate = gate_ref[...]
    y = gate[:, 0:1] * _unpack_f32_pairs(y0_ref[...])
    for k, y_ref in ((1, y1_ref), (2, y2_ref), (3, y3_ref)):
        y = y + gate[:, k:k + 1] * _unpack_f32_pairs(y_ref[...])
    o_ref[...] = _ln(alpha * x_ref[...] + gf_ref[...] * y) * lg_ref[...] + lb_ref[...]


def _combine(x1, y4, gate, mod3, mod_base, rows, ln_g, ln_b, alpha):
    t, d = x1.shape
    nt = t // TM
    y_spec = lambda k: pl.BlockSpec((None, TM, d // 2), lambda i, r: (k, i, 0))
    grid_spec = pltpu.PrefetchScalarGridSpec(
        num_scalar_prefetch=1, grid=(nt,),
        in_specs=[pl.BlockSpec((TM, d), lambda i, r: (i, 0)), y_spec(0), y_spec(1), y_spec(2), y_spec(3),
                  pl.BlockSpec((TM, LANES), lambda i, r: (i, 0)),
                  pl.BlockSpec((None, 1, d), lambda i, r: (mod_base + 5 * 16 + r[i], 0, 0)),
                  pl.BlockSpec((1, d), lambda i, r: (0, 0)), pl.BlockSpec((1, d), lambda i, r: (0, 0))],
        out_specs=pl.BlockSpec((TM, d), lambda i, r: (i, 0)),
    )
    return pl.pallas_call(
        functools.partial(_combine_kernel, alpha=alpha), grid_spec=grid_spec,
        out_shape=jax.ShapeDtypeStruct((t, d), F32), compiler_params=_cparams(1), name="moe_combine",
    )(rows, x1, y4, y4, y4, y4, gate, mod3, ln_g, ln_b)


def _route_plan(route, counts, n_tiles):
    idx, rank = route[:TOP_K], route[TOP_K:]
    padded = (counts + TM_MOE - 1) // TM_MOE * TM_MOE
    pend = jnp.cumsum(padded)
    pstart = pend - padded
    onehot = idx[:, :, None] == jnp.arange(N_EXPERTS, dtype=jnp.int32)[None, None, :]
    dest = jnp.sum(jnp.where(onehot, pstart[None, None, :], 0), axis=-1) + rank
    n_used = (pend[-1] // TM_MOE).astype(jnp.int32)
    tile_start = jnp.arange(n_tiles, dtype=jnp.int32) * TM_MOE
    blk_e = jnp.sum((tile_start[:, None] >= pend[None, :]).astype(jnp.int32), axis=1)
    last_e = jnp.sum((jnp.maximum(n_used - 1, 0) * TM_MOE >= pend).astype(jnp.int32))
    blk_e = jnp.minimum(jnp.where(jnp.arange(n_tiles) < n_used, blk_e, last_e), N_EXPERTS - 1).astype(jnp.int32)
    blk_first = jnp.concatenate([jnp.ones((1,), jnp.int32), (blk_e[1:] != blk_e[:-1]).astype(jnp.int32)])
    ar = jnp.arange(N_EXPERTS, dtype=jnp.int32)
    later = jnp.logical_and((counts > 0)[None, :], ar[None, :] > ar[:, None])
    next_e = jnp.min(jnp.where(later, ar[None, :], N_EXPERTS), axis=1)
    next_e = jnp.where(next_e == N_EXPERTS, -1, next_e)
    blk_next = jnp.sum(jnp.where(blk_e[:, None] == ar[None, :], next_e[None, :], 0), axis=1).astype(jnp.int32)
    blk_slot = ((jnp.cumsum(blk_first) - 1) % 2).astype(jnp.int32)
    return dest, (blk_e, blk_first, blk_next, blk_slot, n_used.reshape(1))


SC_CORES = 2
SC_SUBCORES = 16


def _sc_mesh():
    return plsc.VectorSubcoreMesh(core_axis_name="core", subcore_axis_name="subcore")


def _sc_scatter_rows(src, dest, n_rows, window):
    n_slots, t = dest.shape
    w = src.shape[1]
    per_worker = t // (SC_CORES * SC_SUBCORES)
    assert per_worker * SC_CORES * SC_SUBCORES == t and per_worker % window == 0

    @functools.partial(pl.kernel, out_type=jax.ShapeDtypeStruct((n_rows, w), src.dtype), mesh=_sc_mesh(),
                       scratch_types=[pltpu.VMEM((n_slots, window), jnp.int32), pltpu.VMEM((window, w), src.dtype)],
                       name="sc_dispatch")
    def scatter(src_hbm, dest_hbm, out_hbm, idx_v, rows_v):
        worker = lax.axis_index("subcore") * SC_CORES + lax.axis_index("core")

        @pl.loop(0, per_worker // window)
        def _(c):
            base = pl.multiple_of(worker * per_worker + c * window, window)
            pltpu.sync_copy(src_hbm.at[pl.ds(base, window)], rows_v)
            pltpu.sync_copy(dest_hbm.at[:, pl.ds(base, window)], idx_v)
            for k in range(n_slots):
                pltpu.sync_copy(rows_v, out_hbm.at[idx_v.at[k]])

    return scatter(src, dest)


def _sc_gather_rows(table, idx, window):
    n = idx.shape[0]
    w = table.shape[1]
    per_worker = n // (SC_CORES * SC_SUBCORES)
    assert per_worker * SC_CORES * SC_SUBCORES == n and per_worker % window == 0

    @functools.partial(pl.kernel, out_type=jax.ShapeDtypeStruct((n, w), table.dtype), mesh=_sc_mesh(),
                       scratch_types=[pltpu.VMEM((window,), jnp.int32), pltpu.VMEM((window, w), table.dtype)],
                       name="sc_collect")
    def gather(table_hbm, idx_hbm, out_hbm, idx_v, rows_v):
        worker = lax.axis_index("subcore") * SC_CORES + lax.axis_index("core")

        @pl.loop(0, per_worker // window)
        def _(c):
            base = pl.multiple_of(worker * per_worker + c * window, window)
            pltpu.sync_copy(idx_hbm.at[pl.ds(base, window)], idx_v)
            pltpu.sync_copy(table_hbm.at[idx_v], rows_v)
            pltpu.sync_copy(rows_v, out_hbm.at[pl.ds(base, window)])

    return gather(table, idx)


def _dft_tables(n):
    j = jnp.arange(n, dtype=jnp.int32)
    ang = ((j[:, None] * j[None, :]) % n).astype(F32) * (2.0 * math.pi / n)
    return jnp.cos(ang), jnp.sin(ang)


def _rope_tables(n_lat, n_ctx_rows):
    half, quarter = HEAD_DIM // 2, HEAD_DIM // 4
    tpos = jnp.arange(n_lat, dtype=jnp.int32)
    lane = jnp.arange(LANES, dtype=jnp.int32) % HEAD_DIM
    pos = jnp.where(lane[None, :] < half, (tpos // GRID_W)[:, None], (tpos % GRID_W)[:, None]).astype(F32)
    fidx = (lane % quarter).astype(F32)
    freqs = ROPE_THETA ** (-fidx / quarter)
    ang = pos * freqs[None, :]
    cos, sin = jnp.cos(ang), jnp.sin(ang)
    first = (lane % half) < quarter
    sa = jnp.where(first[None, :], -sin, 0.0)
    sb = jnp.where(first[None, :], 0.0, sin)
    ident = lambda v: jnp.full((n_ctx_rows, LANES), v, F32)
    return (jnp.concatenate([ident(1.0), cos]), jnp.concatenate([ident(0.0), sa]), jnp.concatenate([ident(0.0), sb]))


def kernel(x_prompt, x_sample, cache_k_ab, cache_v_ab, c, c_ctx, w_ada, b_ada, ln_mix_g, ln_mix_b, ln_ffn_g, ln_ffn_b, w_in_ab, w_out_ab, sink_ab, w_in_c, b_in_c, ln_v_g, ln_v_b, w_sp, b_sp, w_out_c, w_router, b_router, w_gu, b_gu, w_dn, b_dn):
    n_ctx_b, ctx_seq, d = x_prompt.shape
    n_lat_b, lat_seq, _ = x_sample.shape
    depth = w_ada.shape[0]
    t_ctx, t_lat = n_ctx_b * ctx_seq, n_lat_b * lat_seq
    t = t_ctx + t_lat
    alpha = (2 * depth) ** 0.25
    kv_w = N_KV_HEADS * HEAD_DIM
    a_w = d // 4
    q_w = d - a_w
    assert t_ctx % TM == 0 and lat_seq % TM == 0 and t_ctx % lat_seq == 0 and n_lat_b + 1 <= 16

    rows_np = np.concatenate([np.zeros(t_ctx // TM, np.int32), 1 + np.arange(t_lat // TM, dtype=np.int32) // (lat_seq // TM)])
    rblk_np = np.concatenate([np.zeros(t_ctx // TM, np.int32), 1 + np.arange(t_lat // TM, dtype=np.int32) % (lat_seq // TM)])
    rows, rblk = jnp.asarray(rows_np), jnp.asarray(rblk_np)

    cond16 = jnp.zeros((16, d), F32).at[0].set(c_ctx).at[1:1 + n_lat_b].set(c)
    mod = _ada(cond16, w_ada, b_ada)
    mod3 = mod.reshape(depth, 16, 6, d).transpose(0, 2, 1, 3).reshape(depth * 6 * 16, 1, d)

    cos, sa, sb = _rope_tables(lat_seq, TM)
    cs_ctx, ss_ctx = (z.astype(BF16) for z in _dft_tables(ctx_seq))
    cs_lat, ss_lat = (z.astype(BF16) for z in _dft_tables(lat_seq))
    gd = a_w // A_GROUPS
    cd, sd = _dft_tables(gd)
    eye = jnp.eye(A_GROUPS, dtype=F32)
    bdc, bds = jnp.kron(eye, cd).astype(BF16), jnp.kron(eye, sd).astype(BF16)

    past = cache_k_ab.shape[2]
    cache_k = cache_k_ab.reshape(n_lat_b, -1, past, kv_w).astype(BF16)
    cache_v = cache_v_ab.reshape(n_lat_b, -1, past, kv_w).astype(BF16)

    x = jnp.concatenate([x_prompt.reshape(t_ctx, d), x_sample.reshape(t_lat, d)], axis=0)
    row2 = lambda v: v.reshape(1, -1)
    ks, vs = [], []
    for l in range(depth):
        j = l // 2
        mod_base = l * 6 * 16
        w_r, b_r = w_router[l].T, b_router[l].reshape(-1, 1)
        if l % 2 == 0:
            ac, as_, q, k32, v32, k, v = _inproj(x, mod3, mod_base, rows, rblk, w_in_ab[j].astype(BF16), cos, sa, sb,
                                                 bdc, bds, a_w, q_w, kv_w)
            ks.append(k32[:t_ctx].reshape(n_ctx_b, ctx_seq, N_KV_HEADS, HEAD_DIM))
            vs.append(v32[:t_ctx].reshape(n_ctx_b, ctx_seq, N_KV_HEADS, HEAD_DIM))
            ya_p = _fourier(ac, as_, cs_ctx, ss_ctx, n_ctx_b, ctx_seq, 0, min(ctx_seq, 512))
            ya_s = _fourier(ac, as_, cs_lat, ss_lat, n_lat_b, lat_seq, t_ctx // lat_seq, min(lat_seq, 512))
            yb_p = _attn_ctx(sink_ab[j], q, k, v, n_ctx_b, ctx_seq)
            yb_s = _attn_lat(sink_ab[j], q, k, v, cache_k, cache_v, j, n_lat_b, lat_seq, t_ctx)
            x1, hf, route, gate, cnt = _post_ab(x, ya_p, yb_p, ya_s, yb_s, w_out_ab[j].astype(BF16), mod3, mod_base,
                                                rows, row2(ln_mix_g[l]), row2(ln_mix_b[l]), w_r, b_r, alpha)
        else:
            x1, hf, route, gate, cnt = _gmlp(x, mod3, mod_base, rows, w_in_c[j].astype(BF16), row2(b_in_c[j]),
                                             row2(ln_v_g[j]), row2(ln_v_b[j]), w_sp[j].astype(BF16),
                                             b_sp[j][:, :, None], w_out_c[j].astype(BF16), row2(ln_mix_g[l]),
                                             row2(ln_mix_b[l]), w_r, b_r, alpha)
        n_rows = t * TOP_K + N_EXPERTS * TM_MOE
        dest, plan = _route_plan(route, cnt[:, 0].astype(jnp.int32), n_rows // TM_MOE)
        xs = _sc_scatter_rows(hf, dest, n_rows, 128)
        out_sorted = _moe(xs, plan, w_gu, b_gu, w_dn, b_dn, l)
        y4 = _sc_gather_rows(out_sorted, dest.reshape(-1), 128).reshape(TOP_K, t, d // 2)
        x = _combine(x1, y4, gate, mod3, mod_base, rows, row2(ln_ffn_g[l]), row2(ln_ffn_b[l]), alpha)

    y_prompt = x[:t_ctx].reshape(n_ctx_b, ctx_seq, d)
    y_sample = x[t_ctx:].reshape(n_lat_b, lat_seq, d)
    return (y_prompt, y_sample, jnp.stack(ks, axis=1), jnp.stack(vs, axis=1))
```

```python
import functools
import math

import numpy as np
import jax
import jax.numpy as jnp
from jax import lax
from jax.experimental import pallas as pl
from jax.experimental.pallas import tpu as pltpu
from jax.experimental.pallas import tpu_sc as plsc

GRID_W = 64
BLK = 128
WINDOW = 128
HEAD_DIM = 64
A_GROUPS = 4
N_KV_HEADS = 4
C_GROUPS = 8
CHUNK = 128
N_EXPERTS = 32
TOP_K = 4
SWIGLU_LIMIT = 7.0
SWIGLU_ALPHA = 1.702
ROPE_THETA = 10000.0
LN_EPS = 1e-6
NEG_INF = -1e30
LOG2_E = math.log2(math.e)

LANES = 128
TM = 512
TM_MOE = 512
VMEM_LIMIT = 52 * 1024 * 1024

F32 = jnp.float32
BF16 = jnp.bfloat16


def _cparams(n_axes):
    return pltpu.CompilerParams(dimension_semantics=("arbitrary",) * n_axes,
                                vmem_limit_bytes=VMEM_LIMIT)


def _ln(x):
    mu = jnp.mean(x, axis=-1, keepdims=True)
    xc = x - mu
    var = jnp.mean(xc * xc, axis=-1, keepdims=True)
    return xc * lax.rsqrt(var + LN_EPS)


def _dot(a, b):
    return jnp.dot(a, b, preferred_element_type=F32)


def _dot_nt(a, b):
    return lax.dot_general(a, b, (((1,), (1,)), ((), ())), preferred_element_type=F32)


def _split(a):
    hi = a.astype(BF16)
    lo = (a - hi.astype(F32)).astype(BF16)
    return hi, lo


def _dot_3pass(a, b):
    a_hi, a_lo = _split(a)
    b_hi, b_lo = _split(b)
    return _dot(a_hi, b_hi) + (_dot(a_hi, b_lo) + _dot(a_lo, b_hi))


def _lane_select(cols, width):
    m = cols[0].shape[0]
    lane = lax.broadcasted_iota(jnp.int32, (m, width), 1)
    out = jnp.zeros((m, width), cols[0].dtype)
    for j, c in enumerate(cols):
        out = jnp.where(lane == j, c, out)
    return out


def _pack_bf16_pairs(v):
    n = v.shape[1] // 2
    lo = pltpu.bitcast(v[:, :n].astype(BF16).astype(F32), jnp.uint32) >> 16
    hi = pltpu.bitcast(v[:, n:].astype(BF16).astype(F32), jnp.uint32) & jnp.uint32(0xFFFF0000)
    return pltpu.bitcast(lo | hi, jnp.int32)


def _unpack_bf16_pairs(p):
    u = pltpu.bitcast(p, jnp.uint32)
    lo = pltpu.bitcast(u << 16, F32)
    hi = pltpu.bitcast(u & jnp.uint32(0xFFFF0000), F32)
    return jnp.concatenate([lo, hi], axis=1).astype(BF16)


def _row_chains(n_rows, n_chains=2):
    step = n_rows // n_chains
    return [slice(c * step, (c + 1) * step) for c in range(n_chains)]


def _run_skewed(stages, states):
    states = list(states)
    for step in range(len(stages) + len(states) - 1):
        for c in range(len(states)):
            if 0 <= step - c < len(stages):
                states[c] = stages[step - c](states[c])
    return states


def _tail_rows(x, y, g_m, ln_g, ln_b, sc_f, sh_f, w_r, b_r, alpha):
    x1 = _ln(alpha * x + g_m * y) * ln_g + ln_b
    hf = _ln(x1) * (1.0 + sc_f) + sh_f
    hf_hi, hf_lo = _split(hf)
    w_hi, w_lo = _split(w_r)
    logits = _dot_nt(w_hi, hf_hi) + (_dot_nt(w_hi, hf_lo) + _dot_nt(w_lo, hf_hi)) + b_r
    return x1, _pack_bf16_pairs(hf), logits


def _tail_route(logits, route_ref, gate_ref, cnt_ref, carry_ref):
    @pl.when(pl.program_id(0) == 0)
    def _():
        carry_ref[...] = jnp.zeros_like(carry_ref)

    tm = logits.shape[1]
    sub = lax.broadcasted_iota(jnp.int32, logits.shape, 0)
    vals = logits
    top_v, top_i = [], []
    for _ in range(TOP_K):
        m = jnp.max(vals, axis=0, keepdims=True)
        am = jnp.min(jnp.where(vals == m, sub, N_EXPERTS), axis=0, keepdims=True)
        top_v.append(m)
        top_i.append(am)
        vals = jnp.where(sub == am, -jnp.inf, vals)
    e = [jnp.exp(v - top_v[0]) for v in top_v]
    denom = e[0] + e[1] + e[2] + e[3]
    gates_t = jnp.concatenate([ek / denom for ek in e] + [jnp.zeros((LANES - TOP_K, tm), F32)], axis=0)
    gate_ref[...] = gates_t.T

    member = jnp.zeros(logits.shape, F32)
    for am in top_i:
        member = jnp.where(sub == am, 1.0, member)
    r_i = lax.broadcasted_iota(jnp.int32, (tm, tm), 0)
    c_i = lax.broadcasted_iota(jnp.int32, (tm, tm), 1)
    earlier = jnp.where(r_i < c_i, 1.0, 0.0).astype(BF16)
    before = _dot(member.astype(BF16), earlier) + carry_ref[...]
    ranks = [jnp.sum(jnp.where(sub == am, before, 0.0), axis=0, keepdims=True).astype(jnp.int32) for am in top_i]
    route_ref[...] = jnp.concatenate(top_i + ranks, axis=0)
    carry = carry_ref[...] + jnp.sum(member, axis=1, keepdims=True)
    carry_ref[...] = carry
    cnt_ref[...] = jnp.broadcast_to(carry, cnt_ref.shape)


def _ada_kernel(cond_ref, w_ref, b_ref, o_ref):
    c = cond_ref[...]
    s = (c * jax.nn.sigmoid(c)).astype(BF16)
    o_ref[...] = _dot(s, w_ref[...].astype(BF16)) + b_ref[...]


def _ada(cond16, w_ada, b_ada):
    depth, d, n = w_ada.shape
    tn = 1536
    return pl.pallas_call(
        _ada_kernel,
        grid=(depth, n // tn),
        in_specs=[
            pl.BlockSpec((16, d), lambda l, j: (0, 0)),
            pl.BlockSpec((None, d, tn), lambda l, j: (l, 0, j)),
            pl.BlockSpec((None, 1, tn), lambda l, j: (l, 0, j)),
        ],
        out_specs=pl.BlockSpec((None, 16, tn), lambda l, j: (l, 0, j)),
        out_shape=jax.ShapeDtypeStruct((depth, 16, n), F32),
        compiler_params=_cparams(2),
        name="ada_mod",
    )(cond16, w_ada, b_ada.reshape(depth, 1, n))


def _inproj_kernel(rows_ref, rblk_ref, x_ref, sc_ref, sh_ref, w_ref, cos_ref, sa_ref, sb_ref, bdc_ref, bds_ref,
                   ac_ref, as_ref, q_ref, k_ref, v_ref, kb_ref, vb_ref, *, a_w, q_w, kv_w):
    del rows_ref, rblk_ref
    g = q_w // kv_w

    def norm_in(s):
        return dict(s, h=(_ln(x_ref[s["rows"], :]) * (1.0 + sc_ref[...]) + sh_ref[...]).astype(BF16))

    def proj(s):
        return dict(s, p=_dot(s["h"], w_ref[...]))

    def finish(s):
        rs, p = s["rows"], s["p"]
        a = p[:, :a_w].astype(BF16)
        ac_ref[rs, :] = _dot(a, bdc_ref[...]).astype(BF16)
        as_ref[rs, :] = _dot(a, bds_ref[...]).astype(BF16)
        cos, sa, sb = cos_ref[rs, :], sa_ref[rs, :], sb_ref[rs, :]

        def rope(t):
            w = t.shape[1]
            reps = w // LANES
            c, a_, b_ = (jnp.tile(z, (1, reps)) for z in (cos, sa, sb))
            nxt = pltpu.roll(t, w - HEAD_DIM // 4, 1)
            prv = pltpu.roll(t, HEAD_DIM // 4, 1)
            return t * c + nxt * a_ + prv * b_

        q = rope(p[:, a_w:a_w + q_w]) * (HEAD_DIM ** -0.5 * LOG2_E)
        lane = lax.broadcasted_iota(jnp.int32, (q.shape[0], LANES), 1)
        for j in range(q_w // HEAD_DIM):
            tile = q[:, (j // 2) * LANES:(j // 2 + 1) * LANES]
            dst_low = (j // g) % 2 == 0
            if (j % 2 == 0) != dst_low:
                tile = pltpu.roll(tile, HEAD_DIM, 1)
            keep = (lane < HEAD_DIM) if dst_low else (lane >= HEAD_DIM)
            q_ref[rs, j * LANES:(j + 1) * LANES] = jnp.where(keep, tile, 0.0).astype(BF16)
        k = rope(p[:, a_w + q_w:a_w + q_w + kv_w])
        v = p[:, a_w + q_w + kv_w:]
        k_ref[rs, :] = k
        v_ref[rs, :] = v
        kb_ref[rs, :] = k.astype(BF16)
        vb_ref[rs, :] = v.astype(BF16)
        return s

    _run_skewed([norm_in, proj, finish], [dict(rows=rs) for rs in _row_chains(x_ref.shape[0])])


def _inproj(x, mod3, mod_base, rows, rblk, w_in, cos, sa, sb, bdc, bds, a_w, q_w, kv_w):
    t, d = x.shape
    n = w_in.shape[1]
    nt = t // TM
    mod_spec = lambda j: pl.BlockSpec((None, 1, d), lambda i, r, rb: (mod_base + j * 16 + r[i], 0, 0))
    whole = lambda shp: pl.BlockSpec(shp, lambda i, r, rb: (0,) * len(shp))
    rope_spec = pl.BlockSpec((TM, LANES), lambda i, r, rb: (rb[i], 0))
    tok = lambda w: pl.BlockSpec((TM, w), lambda i, r, rb: (i, 0))
    grid_spec = pltpu.PrefetchScalarGridSpec(
        num_scalar_prefetch=2, grid=(nt,),
        in_specs=[tok(d), mod_spec(1), mod_spec(0), whole((d, n)), rope_spec, rope_spec, rope_spec,
                  whole((a_w, a_w)), whole((a_w, a_w))],
        out_specs=[tok(a_w), tok(a_w), tok(2 * q_w), tok(kv_w), tok(kv_w), tok(kv_w), tok(kv_w)],
    )
    return pl.pallas_call(
        functools.partial(_inproj_kernel, a_w=a_w, q_w=q_w, kv_w=kv_w),
        grid_spec=grid_spec,
        out_shape=[jax.ShapeDtypeStruct((t, a_w), BF16), jax.ShapeDtypeStruct((t, a_w), BF16),
                   jax.ShapeDtypeStruct((t, 2 * q_w), BF16), jax.ShapeDtypeStruct((t, kv_w), F32),
                   jax.ShapeDtypeStruct((t, kv_w), F32), jax.ShapeDtypeStruct((t, kv_w), BF16),
                   jax.ShapeDtypeStruct((t, kv_w), BF16)],
        compiler_params=_cparams(1),
        name="inproj_ab",
    )(rows, rblk, x, mod3, mod3, w_in, cos, sa, sb, bdc, bds)


def _fourier_kernel(c_ref, s_ref, ac_ref, as_ref, o_ref, *, scale):
    y = _dot(c_ref[...], ac_ref[...]) - _dot(s_ref[...], as_ref[...])
    o_ref[...] = (y * scale).astype(BF16)


def _fourier(ac, as_, cs, ss, n_batch, seq, blk_off, tm):
    a_w = ac.shape[1]
    nr = seq // tm
    a_spec = pl.BlockSpec((seq, a_w), lambda r, b: (blk_off + b, 0))
    t_spec = pl.BlockSpec((tm, seq), lambda r, b: (r, 0))
    return pl.pallas_call(
        functools.partial(_fourier_kernel, scale=(seq * (a_w // A_GROUPS)) ** -0.5),
        grid=(nr, n_batch),
        in_specs=[t_spec, t_spec, a_spec, a_spec],
        out_specs=pl.BlockSpec((tm, a_w), lambda r, b: (b * nr + r, 0)),
        out_shape=jax.ShapeDtypeStruct((n_batch * seq, a_w), BF16),
        compiler_params=_cparams(2),
        name="fourier_%d" % seq,
    )(cs, ss, ac, as_)


def _attend_pairs(sink_ref, q_ref, keys, values, mask, o_ref, *, g):
    rows = q_ref.shape[0]
    lane = lax.broadcasted_iota(jnp.int32, (rows, LANES), 1)
    full_mask = None if mask is None else jnp.concatenate([mask] * g, axis=0)

    def scores(s):
        h0, p = s["h0"], s["h0"] // (2 * g)
        qp = jnp.concatenate([q_ref[:, (h0 + j) * LANES:(h0 + j + 1) * LANES] for j in range(g)], axis=0)
        sc = [_dot_nt(qp, kf(p)) for kf in keys]
        if full_mask is not None:
            sc[0] = jnp.where(full_mask, sc[0], NEG_INF)
        return dict(s, sc=sc)

    def row_max(s):
        m = jnp.concatenate([jnp.full((rows, 1), sink_ref[s["h0"] + j] * LOG2_E, F32) for j in range(g)], axis=0)
        sink_col = m
        for sc in s["sc"]:
            m = jnp.maximum(m, jnp.max(sc, axis=-1, keepdims=True))
        return dict(s, m=m, sink=jnp.exp2(sink_col - m))

    def weights(s):
        return dict(s, e=[jnp.exp2(sc - s["m"]).astype(BF16) for sc in s["sc"]], sc=None)

    def weighted_values(s):
        p = s["h0"] // (2 * g)
        acc = None
        for e, vf in zip(s["e"], values):
            v = vf(p)
            pv = _dot(e, jnp.concatenate([v, jnp.ones_like(v)], axis=1))
            acc = pv if acc is None else acc + pv
        return dict(s, o=acc[:, :LANES] / (acc[:, LANES:LANES + 1] + s["sink"]), e=None)

    states = _run_skewed([scores, row_max, weights, weighted_values],
                         [dict(h0=h * g) for h in range(N_KV_HEADS)])
    heads = {}
    for s in states:
        for j in range(g):
            blk = s["o"][j * rows:(j + 1) * rows]
            head = s["h0"] + j
            if (((head // g) % 2 == 0) != (head % 2 == 0)):
                blk = pltpu.roll(blk, HEAD_DIM, 1)
            heads[head] = blk
    for t in range(len(heads) // 2):
        tile = jnp.where(lane < HEAD_DIM, heads[2 * t], heads[2 * t + 1])
        o_ref[:, t * LANES:(t + 1) * LANES] = tile.astype(BF16)


def _attn_ctx_kernel(sink_ref, q_ref, k_ref, v_ref, o_ref, *, g):
    pair = lambda ref: (lambda p: ref[:, p * LANES:(p + 1) * LANES])
    _attend_pairs(sink_ref, q_ref, [pair(k_ref)], [pair(v_ref)], None, o_ref, g=g)


def _attn_ctx(sink, q, k, v, n_batch, seq):
    qp_w, kv_w = q.shape[1], k.shape[1]
    q_w = qp_w // 2
    g = q_w // kv_w
    return pl.pallas_call(
        functools.partial(_attn_ctx_kernel, g=g),
        grid=(n_batch,),
        in_specs=[pl.BlockSpec(memory_space=pltpu.SMEM),
                  pl.BlockSpec((seq, qp_w), lambda b: (b, 0)),
                  pl.BlockSpec((seq, kv_w), lambda b: (b, 0)),
                  pl.BlockSpec((seq, kv_w), lambda b: (b, 0))],
        out_specs=pl.BlockSpec((seq, q_w), lambda b: (b, 0)),
        out_shape=jax.ShapeDtypeStruct((n_batch * seq, q_w), BF16),
        compiler_params=_cparams(1),
        name="attn_ctx",
    )(sink, q, k, v)


def _attn_lat_kernel(sink_ref, q_ref, k_ref, v_ref, ck_ref, cv_ref, o_ref, *, g, seq):
    i = pl.program_id(1)
    n_loc = 3 * BLK
    start = pl.multiple_of(jnp.clip((i - 1) * BLK, 0, seq - n_loc), BLK)
    row = lax.broadcasted_iota(jnp.int32, (BLK, n_loc), 0)
    col = lax.broadcasted_iota(jnp.int32, (BLK, n_loc), 1)
    band = jnp.abs(row + (i * BLK - start) - col) <= WINDOW
    loc = lambda ref: (lambda p: ref[pl.ds(start, n_loc), p * LANES:(p + 1) * LANES])
    ctx = lambda ref: (lambda p: ref[:, p * LANES:(p + 1) * LANES])
    _attend_pairs(sink_ref, q_ref, [loc(k_ref), ctx(ck_ref)], [loc(v_ref), ctx(cv_ref)], band, o_ref, g=g)


def _attn_lat(sink, q, k, v, cache_k, cache_v, layer_slot, n_batch, seq, tok_off):
    qp_w, kv_w = q.shape[1], k.shape[1]
    q_w = qp_w // 2
    g = q_w // kv_w
    nb = seq // BLK
    past = cache_k.shape[2]
    assert seq >= 3 * BLK
    kv_spec = pl.BlockSpec((seq, kv_w), lambda b, i: (tok_off // seq + b, 0))
    c_spec = pl.BlockSpec((None, None, past, kv_w), lambda b, i: (b, layer_slot, 0, 0))
    return pl.pallas_call(
        functools.partial(_attn_lat_kernel, g=g, seq=seq),
        grid=(n_batch, nb),
        in_specs=[pl.BlockSpec(memory_space=pltpu.SMEM),
                  pl.BlockSpec((BLK, qp_w), lambda b, i: (tok_off // BLK + b * nb + i, 0)),
                  kv_spec, kv_spec, c_spec, c_spec],
        out_specs=pl.BlockSpec((BLK, q_w), lambda b, i: (b * nb + i, 0)),
        out_shape=jax.ShapeDtypeStruct((n_batch * seq, q_w), BF16),
        compiler_params=_cparams(2),
        name="attn_lat",
    )(sink, q, k, v, cache_k, cache_v)


def _post_ab_kernel(rows_ref, x_ref, yap_ref, ybp_ref, yas_ref, ybs_ref, w_ref, gm_ref, lg_ref, lb_ref,
                    scf_ref, shf_ref, wr_ref, br_ref, x1_ref, hf_ref, route_ref, gate_ref, cnt_ref, carry_ref,
                    *, n_ctx_tiles, alpha):
    del rows_ref
    is_ctx = pl.program_id(0) < n_ctx_tiles
    a_w = yap_ref.shape[1]

    def proj_out(s):
        rs = s["rows"]
        ya = jnp.where(is_ctx, yap_ref[rs, :], yas_ref[rs, :])
        yb = jnp.where(is_ctx, ybp_ref[rs, :], ybs_ref[rs, :])
        return dict(s, y=_dot(ya, w_ref[:a_w, :]) + _dot(yb, w_ref[a_w:, :]))

    def tail(s):
        rs = s["rows"]
        x1, hf, lg = _tail_rows(x_ref[rs, :], s["y"], gm_ref[...], lg_ref[...], lb_ref[...], scf_ref[...],
                                shf_ref[...], wr_ref[...], br_ref[...], alpha)
        x1_ref[rs, :] = x1
        hf_ref[rs, :] = hf
        return dict(s, logits=lg)

    states = _run_skewed([proj_out, tail], [dict(rows=rs) for rs in _row_chains(x_ref.shape[0], 4)])
    _tail_route(jnp.concatenate([s["logits"] for s in states], axis=1), route_ref, gate_ref, cnt_ref, carry_ref)


def _tail_out(t, d):
    shapes = [jax.ShapeDtypeStruct((t, d), F32), jax.ShapeDtypeStruct((t, d // 2), jnp.int32),
              jax.ShapeDtypeStruct((2 * TOP_K, t), jnp.int32), jax.ShapeDtypeStruct((t, LANES), F32),
              jax.ShapeDtypeStruct((N_EXPERTS, LANES), F32)]
    specs = [pl.BlockSpec((TM, d), lambda i, r: (i, 0)), pl.BlockSpec((TM, d // 2), lambda i, r: (i, 0)),
             pl.BlockSpec((2 * TOP_K, TM), lambda i, r: (0, i)), pl.BlockSpec((TM, LANES), lambda i, r: (i, 0)),
             pl.BlockSpec((N_EXPERTS, LANES), lambda i, r: (0, 0))]
    return shapes, specs


_TAIL_SCRATCH = [pltpu.VMEM((N_EXPERTS, 1), F32)]


def _post_ab(x, ya_p, yb_p, ya_s, yb_s, w_out, mod3, mod_base, rows, ln_g, ln_b, w_r, b_r, alpha):
    t, d = x.shape
    nt = t // TM
    n_ctx_tiles = ya_p.shape[0] // TM
    a_w, q_w = ya_p.shape[1], yb_p.shape[1]
    mod_spec = lambda j: pl.BlockSpec((None, 1, d), lambda i, r: (mod_base + j * 16 + r[i], 0, 0))
    whole = lambda shp: pl.BlockSpec(shp, lambda i, r: (0,) * len(shp))
    ctx = lambda w: pl.BlockSpec((TM, w), lambda i, r: (jnp.minimum(i, n_ctx_tiles - 1), 0))
    lat = lambda w: pl.BlockSpec((TM, w), lambda i, r: (jnp.maximum(i - n_ctx_tiles, 0), 0))
    shapes, specs = _tail_out(t, d)
    grid_spec = pltpu.PrefetchScalarGridSpec(
        num_scalar_prefetch=1, grid=(nt,),
        in_specs=[pl.BlockSpec((TM, d), lambda i, r: (i, 0)), ctx(a_w), ctx(q_w), lat(a_w), lat(q_w),
                  whole(w_out.shape), mod_spec(2), whole((1, d)), whole((1, d)), mod_spec(4), mod_spec(3),
                  whole(w_r.shape), whole(b_r.shape)],
        out_specs=specs,
        scratch_shapes=_TAIL_SCRATCH,
    )
    return pl.pallas_call(
        functools.partial(_post_ab_kernel, n_ctx_tiles=n_ctx_tiles, alpha=alpha),
        grid_spec=grid_spec, out_shape=shapes, compiler_params=_cparams(1), name="post_ab",
    )(rows, x, ya_p, yb_p, ya_s, yb_s, w_out, mod3, ln_g, ln_b, mod3, mod3, w_r, b_r)


def _gmlp_kernel(rows_ref, x_ref, scm_ref, shm_ref, win_ref, bin_ref, gv_ref, bv_ref, wsp_ref, bsp_ref, wout_ref,
                 gm_ref, lg_ref, lb_ref, scf_ref, shf_ref, wr_ref, br_ref, x1_ref, hf_ref, route_ref, gate_ref,
                 cnt_ref, carry_ref, *, alpha):
    del rows_ref
    assert (x_ref.shape[0] // 4) % CHUNK == 0
    c_w = win_ref.shape[1] // 2
    gd = c_w // C_GROUPS

    def norm_in(s):
        x = x_ref[s["rows"], :]
        return dict(s, x=x, h=(_ln(x) * (1.0 + scm_ref[...]) + shm_ref[...]).astype(BF16))

    def proj_in(s):
        return dict(s, z=_dot(s["h"], win_ref[...]) + bin_ref[...])

    def gate_split(s):
        z = s["z"]
        z = 0.5 * z * (1.0 + lax.erf(z * (2.0 ** -0.5)))
        return dict(s, u=z[:, :c_w], v=(_ln(z[:, c_w:]) * gv_ref[...] + bv_ref[...]).astype(BF16))

    def spatial(s):
        v = s["v"]
        chunks = []
        for n in range(v.shape[0] // CHUNK):
            groups = [_dot(wsp_ref[g], v[n * CHUNK:(n + 1) * CHUNK, g * gd:(g + 1) * gd]) + bsp_ref[g]
                      for g in range(C_GROUPS)]
            chunks.append(jnp.concatenate(groups, axis=1))
        return dict(s, t=(s["u"] * jnp.concatenate(chunks, axis=0)).astype(BF16))

    def proj_out(s):
        return dict(s, y=_dot(s["t"], wout_ref[...]))

    def tail(s):
        x1, hf, lg = _tail_rows(s["x"], s["y"], gm_ref[...], lg_ref[...], lb_ref[...], scf_ref[...], shf_ref[...],
                                wr_ref[...], br_ref[...], alpha)
        x1_ref[s["rows"], :] = x1
        hf_ref[s["rows"], :] = hf
        return dict(s, logits=lg)

    states = _run_skewed([norm_in, proj_in, gate_split, spatial, proj_out, tail],
                         [dict(rows=rs) for rs in _row_chains(x_ref.shape[0], 4)])
    _tail_route(jnp.concatenate([s["logits"] for s in states], axis=1), route_ref, gate_ref, cnt_ref, carry_ref)


def _gmlp(x, mod3, mod_base, rows, w_in, b_in, g_v, b_v, w_sp, b_sp, w_out, ln_g, ln_b, w_r, b_r, alpha):
    t, d = x.shape
    nt = t // TM
    mod_spec = lambda j: pl.BlockSpec((None, 1, d), lambda i, r: (mod_base + j * 16 + r[i], 0, 0))
    whole = lambda shp: pl.BlockSpec(shp, lambda i, r: (0,) * len(shp))
    shapes, specs = _tail_out(t, d)
    grid_spec = pltpu.PrefetchScalarGridSpec(
        num_scalar_prefetch=1, grid=(nt,),
        in_specs=[pl.BlockSpec((TM, d), lambda i, r: (i, 0)), mod_spec(1), mod_spec(0),
                  whole(w_in.shape), whole(b_in.shape), whole(g_v.shape), whole(b_v.shape),
                  whole(w_sp.shape), whole(b_sp.shape), whole(w_out.shape),
                  mod_spec(2), whole((1, d)), whole((1, d)), mod_spec(4), mod_spec(3),
                  whole(w_r.shape), whole(b_r.shape)],
        out_specs=specs,
        scratch_shapes=_TAIL_SCRATCH,
    )
    return pl.pallas_call(
        functools.partial(_gmlp_kernel, alpha=alpha),
        grid_spec=grid_spec, out_shape=shapes, compiler_params=_cparams(1), name="gmlp",
    )(rows, x, mod3, mod3, w_in, b_in, g_v, b_v, w_sp, b_sp, w_out, mod3, ln_g, ln_b, mod3, mod3, w_r, b_r)


def _moe_kernel(be_ref, bf_ref, nx_ref, sl_ref, hf_ref, nu_ref, x_ref, wgu_hbm, bgu_ref, wdn_hbm, bdn_ref, o_ref,
                wgu_f, wdn_f, wgu_s, wdn_s, sem, *, layer):
    i = pl.program_id(0)
    d_ff = wdn_s.shape[0]

    def weight_copies(e, slot):
        return (pltpu.make_async_copy(wgu_hbm.at[layer, e], wgu_f.at[slot], sem.at[0, slot]),
                pltpu.make_async_copy(wdn_hbm.at[layer, e], wdn_f.at[slot], sem.at[1, slot]))

    @pl.when(i < nu_ref[0])
    def _():
        @pl.when(bf_ref[i] == 1)
        def _():
            slot = sl_ref[i]

            @pl.when(i == 0)
            def _():
                for c in weight_copies(be_ref[i], slot):
                    c.start()

            for c in weight_copies(be_ref[i], slot):
                c.wait()
            wgu_s[...] = wgu_f[slot].astype(BF16)
            wdn_s[...] = wdn_f[slot].astype(BF16)

            @pl.when(nx_ref[i] >= 0)
            def _():
                for c in weight_copies(nx_ref[i], 1 - slot):
                    c.start()

        def up(s):
            return dict(s, gu=_dot(_unpack_bf16_pairs(x_ref[s["rows"], :]), wgu_s[...]) + bgu_ref[...])

        def act(s):
            gu = s["gu"]
            gate = jnp.minimum(gu[:, :d_ff], SWIGLU_LIMIT)
            lin = jnp.clip(gu[:, d_ff:], -SWIGLU_LIMIT, SWIGLU_LIMIT)
            glu = gate * jax.nn.sigmoid(SWIGLU_ALPHA * gate)
            return dict(s, gu=None, hid=((lin + 1.0) * glu).astype(BF16))

        def down(s):
            o_ref[s["rows"], :] = _pack_bf16_pairs(_dot(s["hid"], wdn_s[...]) + bdn_ref[...])
            return dict(s, hid=None)

        def run(n_chains, n_rows):
            _run_skewed([up, act, down], [dict(rows=rs) for rs in _row_chains(n_rows, n_chains)])

        @pl.when(hf_ref[i] == 0)
        def _():
            run(2, TM_MOE)

        @pl.when(hf_ref[i] == 1)
        def _():
            run(1, TM_MOE // 2)


def _moe(xs, plan, w_gu, b_gu, w_dn, b_dn, layer):
    n_rows = xs.shape[0]
    depth, n_e, d, ff2 = w_gu.shape
    d_ff = ff2 // 2
    nt = n_rows // TM_MOE
    n_plan = len(plan)
    row_spec = pl.BlockSpec((TM_MOE, d // 2), lambda i, *p: (jnp.minimum(i, p[-1][0] - 1), 0))
    bias_spec = lambda w: pl.BlockSpec((None, None, 1, w), lambda i, *p: (layer, p[0][i], 0, 0))
    grid_spec = pltpu.PrefetchScalarGridSpec(
        num_scalar_prefetch=n_plan, grid=(nt,),
        in_specs=[row_spec, pl.BlockSpec(memory_space=pl.ANY), bias_spec(ff2),
                  pl.BlockSpec(memory_space=pl.ANY), bias_spec(d)],
        out_specs=row_spec,
        scratch_shapes=[pltpu.VMEM((2, d, ff2), F32), pltpu.VMEM((2, d_ff, d), F32),
                        pltpu.VMEM((d, ff2), BF16), pltpu.VMEM((d_ff, d), BF16),
                        pltpu.SemaphoreType.DMA((2, 2))],
    )
    return pl.pallas_call(
        functools.partial(_moe_kernel, layer=layer), grid_spec=grid_spec,
        out_shape=jax.ShapeDtypeStruct((n_rows, d // 2), jnp.int32),
        compiler_params=_cparams(1), name="moe_experts",
    )(*plan, xs, w_gu, b_gu.reshape(depth, n_e, 1, ff2), w_dn, b_dn.reshape(depth, n_e, 1, d))


def _unpack_f32_pairs(p):
    u = pltpu.bitcast(p, jnp.uint32)
    return jnp.concatenate([pltpu.bitcast(u << 16, F32), pltpu.bitcast(u & jnp.uint32(0xFFFF0000), F32)], axis=1)


def _combine_kernel(rows_ref, x_ref, y0_ref, y1_ref, y2_ref, y3_ref, gate_ref, gf_ref, lg_ref, lb_ref, *o_refs,
                    alpha, n_ctx_tiles):
    del rows_ref
    gate = gate_ref[...]
    y = gate[:, 0:1] * _unpack_f32_pairs(y0_ref[...])
    for k, y_ref in ((1, y1_ref), (2, y2_ref), (3, y3_ref)):
        y = y + gate[:, k:k + 1] * _unpack_f32_pairs(y_ref[...])
    out = _ln(alpha * x_ref[...] + gf_ref[...] * y) * lg_ref[...] + lb_ref[...]
    if n_ctx_tiles is None:
        o_refs[0][...] = out
    else:
        @pl.when(pl.program_id(0) < n_ctx_tiles)
        def _():
            o_refs[0][...] = out

        @pl.when(pl.program_id(0) >= n_ctx_tiles)
        def _():
            o_refs[1][...] = out


def _combine(x1, y4, gate, mod3, mod_base, rows, ln_g, ln_b, alpha, t_ctx=None):
    t, d = x1.shape
    nt = t // TM
    y_spec = lambda k: pl.BlockSpec((None, TM, d // 2), lambda i, r: (k, i, 0))
    if t_ctx is None:
        n_ctx_tiles = None
        out_specs = pl.BlockSpec((TM, d), lambda i, r: (i, 0))
        out_shape = jax.ShapeDtypeStruct((t, d), F32)
    else:
        n_ctx_tiles = t_ctx // TM
        out_specs = [pl.BlockSpec((TM, d), lambda i, r: (jnp.minimum(i, n_ctx_tiles - 1), 0)),
                     pl.BlockSpec((TM, d), lambda i, r: (jnp.maximum(i - n_ctx_tiles, 0), 0))]
        out_shape = [jax.ShapeDtypeStruct((t_ctx, d), F32), jax.ShapeDtypeStruct((t - t_ctx, d), F32)]
    grid_spec = pltpu.PrefetchScalarGridSpec(
        num_scalar_prefetch=1, grid=(nt,),
        in_specs=[pl.BlockSpec((TM, d), lambda i, r: (i, 0)), y_spec(0), y_spec(1), y_spec(2), y_spec(3),
                  pl.BlockSpec((TM, LANES), lambda i, r: (i, 0)),
                  pl.BlockSpec((None, 1, d), lambda i, r: (mod_base + 5 * 16 + r[i], 0, 0)),
                  pl.BlockSpec((1, d), lambda i, r: (0, 0)), pl.BlockSpec((1, d), lambda i, r: (0, 0))],
        out_specs=out_specs,
    )
    return pl.pallas_call(
        functools.partial(_combine_kernel, alpha=alpha, n_ctx_tiles=n_ctx_tiles), grid_spec=grid_spec,
        out_shape=out_shape, compiler_params=_cparams(1), name="moe_combine",
    )(rows, x1, y4, y4, y4, y4, gate, mod3, ln_g, ln_b)


def _route_plan(route, counts, n_tiles):
    idx, rank = route[:TOP_K], route[TOP_K:]
    padded = (counts + TM_MOE - 1) // TM_MOE * TM_MOE
    pend = jnp.cumsum(padded)
    pstart = pend - padded
    onehot = idx[:, :, None] == jnp.arange(N_EXPERTS, dtype=jnp.int32)[None, None, :]
    dest = jnp.sum(jnp.where(onehot, pstart[None, None, :], 0), axis=-1) + rank
    n_used = (pend[-1] // TM_MOE).astype(jnp.int32)
    tile_start = jnp.arange(n_tiles, dtype=jnp.int32) * TM_MOE
    blk_e = jnp.sum((tile_start[:, None] >= pend[None, :]).astype(jnp.int32), axis=1)
    last_e = jnp.sum((jnp.maximum(n_used - 1, 0) * TM_MOE >= pend).astype(jnp.int32))
    blk_e = jnp.minimum(jnp.where(jnp.arange(n_tiles) < n_used, blk_e, last_e), N_EXPERTS - 1).astype(jnp.int32)
    blk_first = jnp.concatenate([jnp.ones((1,), jnp.int32), (blk_e[1:] != blk_e[:-1]).astype(jnp.int32)])
    ar = jnp.arange(N_EXPERTS, dtype=jnp.int32)
    later = jnp.logical_and((counts > 0)[None, :], ar[None, :] > ar[:, None])
    next_e = jnp.min(jnp.where(later, ar[None, :], N_EXPERTS), axis=1)
    next_e = jnp.where(next_e == N_EXPERTS, -1, next_e)
    blk_next = jnp.sum(jnp.where(blk_e[:, None] == ar[None, :], next_e[None, :], 0), axis=1).astype(jnp.int32)
    blk_slot = ((jnp.cumsum(blk_first) - 1) % 2).astype(jnp.int32)
    left = jnp.sum(jnp.where(blk_e[:, None] == ar[None, :], (pstart + counts)[None, :], 0), axis=1) - tile_start
    blk_half = (left <= TM_MOE // 2).astype(jnp.int32)
    return dest, (blk_e, blk_first, blk_next, blk_slot, blk_half, n_used.reshape(1))


SC_CORES = 2
SC_SUBCORES = 16


def _sc_mesh():
    return plsc.VectorSubcoreMesh(core_axis_name="core", subcore_axis_name="subcore")


def _sc_scatter_rows(src, dest, n_rows, window):
    n_slots, t = dest.shape
    w = src.shape[1]
    per_worker = t // (SC_CORES * SC_SUBCORES)
    assert per_worker * SC_CORES * SC_SUBCORES == t and per_worker % window == 0

    @functools.partial(pl.kernel, out_type=jax.ShapeDtypeStruct((n_rows, w), src.dtype), mesh=_sc_mesh(),
                       scratch_types=[pltpu.VMEM((n_slots, window), jnp.int32), pltpu.VMEM((window, w), src.dtype)],
                       name="sc_dispatch")
    def scatter(src_hbm, dest_hbm, out_hbm, idx_v, rows_v):
        worker = lax.axis_index("subcore") * SC_CORES + lax.axis_index("core")

        @pl.loop(0, per_worker // window)
        def _(c):
            base = pl.multiple_of(worker * per_worker + c * window, window)
            pltpu.sync_copy(src_hbm.at[pl.ds(base, window)], rows_v)
            pltpu.sync_copy(dest_hbm.at[:, pl.ds(base, window)], idx_v)
            for k in range(n_slots):
                pltpu.sync_copy(rows_v, out_hbm.at[idx_v.at[k]])

    return scatter(src, dest)


def _sc_gather_rows(table, idx, window):
    n = idx.shape[0]
    w = table.shape[1]
    per_worker = n // (SC_CORES * SC_SUBCORES)
    assert per_worker * SC_CORES * SC_SUBCORES == n and per_worker % window == 0

    @functools.partial(pl.kernel, out_type=jax.ShapeDtypeStruct((n, w), table.dtype), mesh=_sc_mesh(),
                       scratch_types=[pltpu.VMEM((window,), jnp.int32), pltpu.VMEM((window, w), table.dtype)],
                       name="sc_collect")
    def gather(table_hbm, idx_hbm, out_hbm, idx_v, rows_v):
        worker = lax.axis_index("subcore") * SC_CORES + lax.axis_index("core")

        @pl.loop(0, per_worker // window)
        def _(c):
            base = pl.multiple_of(worker * per_worker + c * window, window)
            pltpu.sync_copy(idx_hbm.at[pl.ds(base, window)], idx_v)
            pltpu.sync_copy(table_hbm.at[idx_v], rows_v)
            pltpu.sync_copy(rows_v, out_hbm.at[pl.ds(base, window)])

    return gather(table, idx)


def _dft_tables(n):
    j = jnp.arange(n, dtype=jnp.int32)
    ang = ((j[:, None] * j[None, :]) % n).astype(F32) * (2.0 * math.pi / n)
    return jnp.cos(ang), jnp.sin(ang)


def _rope_tables(n_lat, n_ctx_rows):
    half, quarter = HEAD_DIM // 2, HEAD_DIM // 4
    tpos = jnp.arange(n_lat, dtype=jnp.int32)
    lane = jnp.arange(LANES, dtype=jnp.int32) % HEAD_DIM
    pos = jnp.where(lane[None, :] < half, (tpos // GRID_W)[:, None], (tpos % GRID_W)[:, None]).astype(F32)
    fidx = (lane % quarter).astype(F32)
    freqs = ROPE_THETA ** (-fidx / quarter)
    ang = pos * freqs[None, :]
    cos, sin = jnp.cos(ang), jnp.sin(ang)
    first = (lane % half) < quarter
    sa = jnp.where(first[None, :], -sin, 0.0)
    sb = jnp.where(first[None, :], 0.0, sin)
    ident = lambda v: jnp.full((n_ctx_rows, LANES), v, F32)
    return (jnp.concatenate([ident(1.0), cos]), jnp.concatenate([ident(0.0), sa]), jnp.concatenate([ident(0.0), sb]))


def kernel(x_prompt, x_sample, cache_k_ab, cache_v_ab, c, c_ctx, w_ada, b_ada, ln_mix_g, ln_mix_b, ln_ffn_g, ln_ffn_b, w_in_ab, w_out_ab, sink_ab, w_in_c, b_in_c, ln_v_g, ln_v_b, w_sp, b_sp, w_out_c, w_router, b_router, w_gu, b_gu, w_dn, b_dn):
    n_ctx_b, ctx_seq, d = x_prompt.shape
    n_lat_b, lat_seq, _ = x_sample.shape
    depth = w_ada.shape[0]
    t_ctx, t_lat = n_ctx_b * ctx_seq, n_lat_b * lat_seq
    t = t_ctx + t_lat
    alpha = (2 * depth) ** 0.25
    kv_w = N_KV_HEADS * HEAD_DIM
    a_w = d // 4
    q_w = d - a_w
    assert t_ctx % TM == 0 and lat_seq % TM == 0 and t_ctx % lat_seq == 0 and n_lat_b + 1 <= 16

    rows_np = np.concatenate([np.zeros(t_ctx // TM, np.int32), 1 + np.arange(t_lat // TM, dtype=np.int32) // (lat_seq // TM)])
    rblk_np = np.concatenate([np.zeros(t_ctx // TM, np.int32), 1 + np.arange(t_lat // TM, dtype=np.int32) % (lat_seq // TM)])
    rows, rblk = jnp.asarray(rows_np), jnp.asarray(rblk_np)

    cond16 = jnp.zeros((16, d), F32).at[0].set(c_ctx).at[1:1 + n_lat_b].set(c)
    mod = _ada(cond16, w_ada, b_ada)
    mod3 = mod.reshape(depth, 16, 6, d).transpose(0, 2, 1, 3).reshape(depth * 6 * 16, 1, d)

    cos, sa, sb = _rope_tables(lat_seq, TM)
    cs_ctx, ss_ctx = (z.astype(BF16) for z in _dft_tables(ctx_seq))
    cs_lat, ss_lat = (z.astype(BF16) for z in _dft_tables(lat_seq))
    gd = a_w // A_GROUPS
    cd, sd = _dft_tables(gd)
    eye = jnp.eye(A_GROUPS, dtype=F32)
    bdc, bds = jnp.kron(eye, cd).astype(BF16), jnp.kron(eye, sd).astype(BF16)

    past = cache_k_ab.shape[2]
    cache_k = cache_k_ab.reshape(n_lat_b, -1, past, kv_w).astype(BF16)
    cache_v = cache_v_ab.reshape(n_lat_b, -1, past, kv_w).astype(BF16)

    x = jnp.concatenate([x_prompt.reshape(t_ctx, d), x_sample.reshape(t_lat, d)], axis=0)
    row2 = lambda v: v.reshape(1, -1)
    ks, vs = [], []
    for l in range(depth):
        j = l // 2
        mod_base = l * 6 * 16
        w_r, b_r = w_router[l].T, b_router[l].reshape(-1, 1)
        if l % 2 == 0:
            ac, as_, q, k32, v32, k, v = _inproj(x, mod3, mod_base, rows, rblk, w_in_ab[j].astype(BF16), cos, sa, sb,
                                                 bdc, bds, a_w, q_w, kv_w)
            ks.append(k32[:t_ctx].reshape(n_ctx_b, ctx_seq, N_KV_HEADS, HEAD_DIM))
            vs.append(v32[:t_ctx].reshape(n_ctx_b, ctx_seq, N_KV_HEADS, HEAD_DIM))
            ya_p = _fourier(ac, as_, cs_ctx, ss_ctx, n_ctx_b, ctx_seq, 0, min(ctx_seq, 512))
            ya_s = _fourier(ac, as_, cs_lat, ss_lat, n_lat_b, lat_seq, t_ctx // lat_seq, min(lat_seq, 512))
            yb_p = _attn_ctx(sink_ab[j], q, k, v, n_ctx_b, ctx_seq)
            yb_s = _attn_lat(sink_ab[j], q, k, v, cache_k, cache_v, j, n_lat_b, lat_seq, t_ctx)
            x1, hf, route, gate, cnt = _post_ab(x, ya_p, yb_p, ya_s, yb_s, w_out_ab[j].astype(BF16), mod3, mod_base,
                                                rows, row2(ln_mix_g[l]), row2(ln_mix_b[l]), w_r, b_r, alpha)
        else:
            x1, hf, route, gate, cnt = _gmlp(x, mod3, mod_base, rows, w_in_c[j].astype(BF16), row2(b_in_c[j]),
                                             row2(ln_v_g[j]), row2(ln_v_b[j]), w_sp[j].astype(BF16),
                                             b_sp[j][:, :, None], w_out_c[j].astype(BF16), row2(ln_mix_g[l]),
                                             row2(ln_mix_b[l]), w_r, b_r, alpha)
        n_rows = t * TOP_K + N_EXPERTS * TM_MOE
        dest, plan = _route_plan(route, cnt[:, 0].astype(jnp.int32), n_rows // TM_MOE)
        xs = _sc_scatter_rows(hf, dest, n_rows, 128)
        out_sorted = _moe(xs, plan, w_gu, b_gu, w_dn, b_dn, l)
        y4 = _sc_gather_rows(out_sorted, dest.reshape(-1), 128).reshape(TOP_K, t, d // 2)
        x = _combine(x1, y4, gate, mod3, mod_base, rows, row2(ln_ffn_g[l]), row2(ln_ffn_b[l]), alpha,
                     t_ctx if l == depth - 1 else None)

    y_prompt = x[0].reshape(n_ctx_b, ctx_seq, d)
    y_sample = x[1].reshape(n_lat_b, lat_seq, d)
    return (y_prompt, y_sample, jnp.stack(ks, axis=1), jnp.stack(vs, axis=1))
```

```python
import functools
import math

import numpy as np
import jax
import jax.numpy as jnp
from jax import lax
from jax.experimental import pallas as pl
from jax.experimental.pallas import tpu as pltpu
from jax.experimental.pallas import tpu_sc as plsc

GRID_W = 64
BLK = 128
WINDOW = 128
HEAD_DIM = 64
A_GROUPS = 4
N_KV_HEADS = 4
C_GROUPS = 8
CHUNK = 128
N_EXPERTS = 32
TOP_K = 4
SWIGLU_LIMIT = 7.0
SWIGLU_ALPHA = 1.702
ROPE_THETA = 10000.0
LN_EPS = 1e-6
NEG_INF = -1e30
LOG2_E = math.log2(math.e)

LANES = 128
TM = 512
TM_MOE = 512
VMEM_LIMIT = 52 * 1024 * 1024

F32 = jnp.float32
BF16 = jnp.bfloat16


def _cparams(n_axes):
    return pltpu.CompilerParams(dimension_semantics=("arbitrary",) * n_axes,
                                vmem_limit_bytes=VMEM_LIMIT)


def _ln(x):
    mu = jnp.mean(x, axis=-1, keepdims=True)
    xc = x - mu
    var = jnp.mean(xc * xc, axis=-1, keepdims=True)
    return xc * lax.rsqrt(var + LN_EPS)


def _dot(a, b):
    return jnp.dot(a, b, preferred_element_type=F32)


def _dot_nt(a, b):
    return lax.dot_general(a, b, (((1,), (1,)), ((), ())), preferred_element_type=F32)


def _split(a):
    hi = a.astype(BF16)
    lo = (a - hi.astype(F32)).astype(BF16)
    return hi, lo


def _dot_3pass(a, b):
    a_hi, a_lo = _split(a)
    b_hi, b_lo = _split(b)
    return _dot(a_hi, b_hi) + (_dot(a_hi, b_lo) + _dot(a_lo, b_hi))


def _lane_select(cols, width):
    m = cols[0].shape[0]
    lane = lax.broadcasted_iota(jnp.int32, (m, width), 1)
    out = jnp.zeros((m, width), cols[0].dtype)
    for j, c in enumerate(cols):
        out = jnp.where(lane == j, c, out)
    return out


def _pack_bf16_pairs(v):
    n = v.shape[1] // 2
    lo = pltpu.bitcast(v[:, :n].astype(BF16).astype(F32), jnp.uint32) >> 16
    hi = pltpu.bitcast(v[:, n:].astype(BF16).astype(F32), jnp.uint32) & jnp.uint32(0xFFFF0000)
    return pltpu.bitcast(lo | hi, jnp.int32)


def _unpack_bf16_pairs(p):
    u = pltpu.bitcast(p, jnp.uint32)
    lo = pltpu.bitcast(u << 16, F32)
    hi = pltpu.bitcast(u & jnp.uint32(0xFFFF0000), F32)
    return jnp.concatenate([lo, hi], axis=1).astype(BF16)


def _row_chains(n_rows, n_chains=2):
    step = n_rows // n_chains
    return [slice(c * step, (c + 1) * step) for c in range(n_chains)]


def _run_skewed(stages, states):
    states = list(states)
    for step in range(len(stages) + len(states) - 1):
        for c in range(len(states)):
            if 0 <= step - c < len(stages):
                states[c] = stages[step - c](states[c])
    return states


def _tail_rows(x, y, g_m, ln_g, ln_b, sc_f, sh_f, w_r, b_r, alpha):
    x1 = _ln(alpha * x + g_m * y) * ln_g + ln_b
    hf = _ln(x1) * (1.0 + sc_f) + sh_f
    hf_hi, hf_lo = _split(hf)
    w_hi, w_lo = _split(w_r)
    logits = _dot_nt(w_hi, hf_hi) + (_dot_nt(w_hi, hf_lo) + _dot_nt(w_lo, hf_hi)) + b_r
    return x1, _pack_bf16_pairs(hf), logits


def _tail_route(logits, route_ref, gate_ref, cnt_ref, carry_ref):
    @pl.when(pl.program_id(0) == 0)
    def _():
        carry_ref[...] = jnp.zeros_like(carry_ref)

    tm = logits.shape[1]
    sub = lax.broadcasted_iota(jnp.int32, logits.shape, 0)
    vals = logits
    top_v, top_i = [], []
    for _ in range(TOP_K):
        m = jnp.max(vals, axis=0, keepdims=True)
        am = jnp.min(jnp.where(vals == m, sub, N_EXPERTS), axis=0, keepdims=True)
        top_v.append(m)
        top_i.append(am)
        vals = jnp.where(sub == am, -jnp.inf, vals)
    e = [jnp.exp(v - top_v[0]) for v in top_v]
    denom = e[0] + e[1] + e[2] + e[3]
    gates_t = jnp.concatenate([ek / denom for ek in e] + [jnp.zeros((LANES - TOP_K, tm), F32)], axis=0)
    gate_ref[...] = gates_t.T

    member = jnp.zeros(logits.shape, F32)
    for am in top_i:
        member = jnp.where(sub == am, 1.0, member)
    r_i = lax.broadcasted_iota(jnp.int32, (tm, tm), 0)
    c_i = lax.broadcasted_iota(jnp.int32, (tm, tm), 1)
    earlier = jnp.where(r_i < c_i, 1.0, 0.0).astype(BF16)
    before = _dot(member.astype(BF16), earlier) + carry_ref[...]
    ranks = [jnp.sum(jnp.where(sub == am, before, 0.0), axis=0, keepdims=True).astype(jnp.int32) for am in top_i]
    route_ref[...] = jnp.concatenate(top_i + ranks, axis=0)
    carry = carry_ref[...] + jnp.sum(member, axis=1, keepdims=True)
    carry_ref[...] = carry
    cnt_ref[...] = jnp.broadcast_to(carry, cnt_ref.shape)


def _ada_kernel(cond_ref, w_ref, b_ref, o_ref):
    c = cond_ref[...]
    s = (c * jax.nn.sigmoid(c)).astype(BF16)
    o_ref[...] = _dot(s, w_ref[...].astype(BF16)) + b_ref[...]


def _ada(cond16, w_ada, b_ada):
    depth, d, n = w_ada.shape
    tn = 1536
    return pl.pallas_call(
        _ada_kernel,
        grid=(depth, n // tn),
        in_specs=[
            pl.BlockSpec((16, d), lambda l, j: (0, 0)),
            pl.BlockSpec((None, d, tn), lambda l, j: (l, 0, j)),
            pl.BlockSpec((None, 1, tn), lambda l, j: (l, 0, j)),
        ],
        out_specs=pl.BlockSpec((None, 16, tn), lambda l, j: (l, 0, j)),
        out_shape=jax.ShapeDtypeStruct((depth, 16, n), F32),
        compiler_params=_cparams(2),
        name="ada_mod",
    )(cond16, w_ada, b_ada.reshape(depth, 1, n))


def _group_rows(xc_ref, xl_ref, rows, n_ctx_tiles):
    return jnp.where(pl.program_id(0) < n_ctx_tiles, xc_ref[rows, :], xl_ref[rows, :])


def _group_specs(x, n_ctx_tiles):
    if isinstance(x, tuple):
        arrays, lat = x, (lambda i, *_: (jnp.maximum(i - n_ctx_tiles, 0), 0))
    else:
        arrays, lat = (x, x), (lambda i, *_: (jnp.maximum(i, n_ctx_tiles), 0))
    d = arrays[0].shape[1]
    ctx = lambda i, *_: (jnp.minimum(i, n_ctx_tiles - 1), 0)
    return arrays, [pl.BlockSpec((TM, d), ctx), pl.BlockSpec((TM, d), lat)]


def _inproj_kernel(rows_ref, rblk_ref, xc_ref, xl_ref, sc_ref, sh_ref, w_ref, cos_ref, sa_ref, sb_ref, bdc_ref,
                   bds_ref, ac_ref, as_ref, q_ref, k_ref, v_ref, kb_ref, vb_ref, *, a_w, q_w, kv_w, n_ctx_tiles):
    del rows_ref, rblk_ref
    g = q_w // kv_w

    def norm_in(s):
        x = _group_rows(xc_ref, xl_ref, s["rows"], n_ctx_tiles)
        return dict(s, h=(_ln(x) * (1.0 + sc_ref[...]) + sh_ref[...]).astype(BF16))

    def proj(s):
        return dict(s, p=_dot(s["h"], w_ref[...]))

    def finish(s):
        rs, p = s["rows"], s["p"]
        a = p[:, :a_w].astype(BF16)
        ac_ref[rs, :] = _dot(a, bdc_ref[...]).astype(BF16)
        as_ref[rs, :] = _dot(a, bds_ref[...]).astype(BF16)
        cos, sa, sb = cos_ref[rs, :], sa_ref[rs, :], sb_ref[rs, :]

        def rope(t):
            w = t.shape[1]
            reps = w // LANES
            c, a_, b_ = (jnp.tile(z, (1, reps)) for z in (cos, sa, sb))
            nxt = pltpu.roll(t, w - HEAD_DIM // 4, 1)
            prv = pltpu.roll(t, HEAD_DIM // 4, 1)
            return t * c + nxt * a_ + prv * b_

        q = rope(p[:, a_w:a_w + q_w]) * (HEAD_DIM ** -0.5 * LOG2_E)
        lane = lax.broadcasted_iota(jnp.int32, (q.shape[0], LANES), 1)
        for j in range(q_w // HEAD_DIM):
            tile = q[:, (j // 2) * LANES:(j // 2 + 1) * LANES]
            dst_low = (j // g) % 2 == 0
            if (j % 2 == 0) != dst_low:
                tile = pltpu.roll(tile, HEAD_DIM, 1)
            keep = (lane < HEAD_DIM) if dst_low else (lane >= HEAD_DIM)
            q_ref[rs, j * LANES:(j + 1) * LANES] = jnp.where(keep, tile, 0.0).astype(BF16)
        k = rope(p[:, a_w + q_w:a_w + q_w + kv_w])
        v = p[:, a_w + q_w + kv_w:]
        k_ref[rs, :] = k
        v_ref[rs, :] = v
        kb_ref[rs, :] = k.astype(BF16)
        vb_ref[rs, :] = v.astype(BF16)
        return s

    _run_skewed([norm_in, proj, finish], [dict(rows=rs) for rs in _row_chains(xc_ref.shape[0])])


def _inproj(x, n_ctx_tiles, mod3, mod_base, rows, rblk, w_in, cos, sa, sb, bdc, bds, a_w, q_w, kv_w):
    (xc, xl), x_specs = _group_specs(x, n_ctx_tiles)
    d = xc.shape[1]
    t = rows.shape[0] * TM
    n = w_in.shape[1]
    nt = t // TM
    mod_spec = lambda j: pl.BlockSpec((None, 1, d), lambda i, r, rb: (mod_base + j * 16 + r[i], 0, 0))
    whole = lambda shp: pl.BlockSpec(shp, lambda i, r, rb: (0,) * len(shp))
    rope_spec = pl.BlockSpec((TM, LANES), lambda i, r, rb: (rb[i], 0))
    tok = lambda w: pl.BlockSpec((TM, w), lambda i, r, rb: (i, 0))
    grid_spec = pltpu.PrefetchScalarGridSpec(
        num_scalar_prefetch=2, grid=(nt,),
        in_specs=x_specs + [mod_spec(1), mod_spec(0), whole((d, n)), rope_spec, rope_spec, rope_spec,
                            whole((a_w, a_w)), whole((a_w, a_w))],
        out_specs=[tok(a_w), tok(a_w), tok(2 * q_w), tok(kv_w), tok(kv_w), tok(kv_w), tok(kv_w)],
    )
    return pl.pallas_call(
        functools.partial(_inproj_kernel, a_w=a_w, q_w=q_w, kv_w=kv_w, n_ctx_tiles=n_ctx_tiles),
        grid_spec=grid_spec,
        out_shape=[jax.ShapeDtypeStruct((t, a_w), BF16), jax.ShapeDtypeStruct((t, a_w), BF16),
                   jax.ShapeDtypeStruct((t, 2 * q_w), BF16), jax.ShapeDtypeStruct((t, kv_w), F32),
                   jax.ShapeDtypeStruct((t, kv_w), F32), jax.ShapeDtypeStruct((t, kv_w), BF16),
                   jax.ShapeDtypeStruct((t, kv_w), BF16)],
        compiler_params=_cparams(1),
        name="inproj_ab",
    )(rows, rblk, xc, xl, mod3, mod3, w_in, cos, sa, sb, bdc, bds)


def _fourier_kernel(c_ref, s_ref, ac_ref, as_ref, o_ref, *, scale):
    y = _dot(c_ref[...], ac_ref[...]) - _dot(s_ref[...], as_ref[...])
    o_ref[...] = (y * scale).astype(BF16)


def _fourier(ac, as_, cs, ss, n_batch, seq, blk_off, tm):
    a_w = ac.shape[1]
    nr = seq // tm
    a_spec = pl.BlockSpec((seq, a_w), lambda r, b: (blk_off + b, 0))
    t_spec = pl.BlockSpec((tm, seq), lambda r, b: (r, 0))
    return pl.pallas_call(
        functools.partial(_fourier_kernel, scale=(seq * (a_w // A_GROUPS)) ** -0.5),
        grid=(nr, n_batch),
        in_specs=[t_spec, t_spec, a_spec, a_spec],
        out_specs=pl.BlockSpec((tm, a_w), lambda r, b: (b * nr + r, 0)),
        out_shape=jax.ShapeDtypeStruct((n_batch * seq, a_w), BF16),
        compiler_params=_cparams(2),
        name="fourier_%d" % seq,
    )(cs, ss, ac, as_)


def _attend_pairs(sink_ref, q_ref, keys, values, mask, o_ref, *, g):
    rows = q_ref.shape[0]
    lane = lax.broadcasted_iota(jnp.int32, (rows, LANES), 1)
    full_mask = None if mask is None else jnp.concatenate([mask] * g, axis=0)

    def scores(s):
        h0, p = s["h0"], s["h0"] // (2 * g)
        qp = jnp.concatenate([q_ref[:, (h0 + j) * LANES:(h0 + j + 1) * LANES] for j in range(g)], axis=0)
        sc = [_dot_nt(qp, kf(p)) for kf in keys]
        if full_mask is not None:
            sc[0] = jnp.where(full_mask, sc[0], NEG_INF)
        return dict(s, sc=sc)

    def row_max(s):
        m = jnp.concatenate([jnp.full((rows, 1), sink_ref[s["h0"] + j] * LOG2_E, F32) for j in range(g)], axis=0)
        sink_col = m
        for sc in s["sc"]:
            m = jnp.maximum(m, jnp.max(sc, axis=-1, keepdims=True))
        return dict(s, m=m, sink=jnp.exp2(sink_col - m))

    def weights(s):
        return dict(s, e=[jnp.exp2(sc - s["m"]).astype(BF16) for sc in s["sc"]], sc=None)

    def weighted_values(s):
        p = s["h0"] // (2 * g)
        acc = None
        for e, vf in zip(s["e"], values):
            v = vf(p)
            pv = _dot(e, jnp.concatenate([v, jnp.ones_like(v)], axis=1))
            acc = pv if acc is None else acc + pv
        return dict(s, o=acc[:, :LANES] / (acc[:, LANES:LANES + 1] + s["sink"]), e=None)

    states = _run_skewed([scores, row_max, weights, weighted_values],
                         [dict(h0=h * g) for h in range(N_KV_HEADS)])
    heads = {}
    for s in states:
        for j in range(g):
            blk = s["o"][j * rows:(j + 1) * rows]
            head = s["h0"] + j
            if (((head // g) % 2 == 0) != (head % 2 == 0)):
                blk = pltpu.roll(blk, HEAD_DIM, 1)
            heads[head] = blk
    for t in range(len(heads) // 2):
        tile = jnp.where(lane < HEAD_DIM, heads[2 * t], heads[2 * t + 1])
        o_ref[:, t * LANES:(t + 1) * LANES] = tile.astype(BF16)


def _attn_ctx_kernel(sink_ref, q_ref, k_ref, v_ref, o_ref, *, g):
    pair = lambda ref: (lambda p: ref[:, p * LANES:(p + 1) * LANES])
    _attend_pairs(sink_ref, q_ref, [pair(k_ref)], [pair(v_ref)], None, o_ref, g=g)


def _attn_ctx(sink, q, k, v, n_batch, seq):
    qp_w, kv_w = q.shape[1], k.shape[1]
    q_w = qp_w // 2
    g = q_w // kv_w
    return pl.pallas_call(
        functools.partial(_attn_ctx_kernel, g=g),
        grid=(n_batch,),
        in_specs=[pl.BlockSpec(memory_space=pltpu.SMEM),
                  pl.BlockSpec((seq, qp_w), lambda b: (b, 0)),
                  pl.BlockSpec((seq, kv_w), lambda b: (b, 0)),
                  pl.BlockSpec((seq, kv_w), lambda b: (b, 0))],
        out_specs=pl.BlockSpec((seq, q_w), lambda b: (b, 0)),
        out_shape=jax.ShapeDtypeStruct((n_batch * seq, q_w), BF16),
        compiler_params=_cparams(1),
        name="attn_ctx",
    )(sink, q, k, v)


def _attn_lat_kernel(sink_ref, q_ref, k_ref, v_ref, ck_ref, cv_ref, o_ref, *, g, seq):
    i = pl.program_id(1)
    n_loc = 3 * BLK
    start = pl.multiple_of(jnp.clip((i - 1) * BLK, 0, seq - n_loc), BLK)
    row = lax.broadcasted_iota(jnp.int32, (BLK, n_loc), 0)
    col = lax.broadcasted_iota(jnp.int32, (BLK, n_loc), 1)
    band = jnp.abs(row + (i * BLK - start) - col) <= WINDOW
    loc = lambda ref: (lambda p: ref[pl.ds(start, n_loc), p * LANES:(p + 1) * LANES])
    ctx = lambda ref: (lambda p: ref[:, p * LANES:(p + 1) * LANES])
    _attend_pairs(sink_ref, q_ref, [loc(k_ref), ctx(ck_ref)], [loc(v_ref), ctx(cv_ref)], band, o_ref, g=g)


def _attn_lat(sink, q, k, v, cache_k, cache_v, layer_slot, n_batch, seq, tok_off):
    qp_w, kv_w = q.shape[1], k.shape[1]
    q_w = qp_w // 2
    g = q_w // kv_w
    nb = seq // BLK
    past = cache_k.shape[2]
    assert seq >= 3 * BLK
    kv_spec = pl.BlockSpec((seq, kv_w), lambda b, i: (tok_off // seq + b, 0))
    c_spec = pl.BlockSpec((None, None, past, kv_w), lambda b, i: (b, layer_slot, 0, 0))
    return pl.pallas_call(
        functools.partial(_attn_lat_kernel, g=g, seq=seq),
        grid=(n_batch, nb),
        in_specs=[pl.BlockSpec(memory_space=pltpu.SMEM),
                  pl.BlockSpec((BLK, qp_w), lambda b, i: (tok_off // BLK + b * nb + i, 0)),
                  kv_spec, kv_spec, c_spec, c_spec],
        out_specs=pl.BlockSpec((BLK, q_w), lambda b, i: (b * nb + i, 0)),
        out_shape=jax.ShapeDtypeStruct((n_batch * seq, q_w), BF16),
        compiler_params=_cparams(2),
        name="attn_lat",
    )(sink, q, k, v, cache_k, cache_v)


def _post_ab_kernel(rows_ref, xc_ref, xl_ref, yap_ref, ybp_ref, yas_ref, ybs_ref, w_ref, gm_ref, lg_ref, lb_ref,
                    scf_ref, shf_ref, wr_ref, br_ref, x1_ref, hf_ref, route_ref, gate_ref, cnt_ref, carry_ref,
                    *, n_ctx_tiles, alpha):
    del rows_ref
    is_ctx = pl.program_id(0) < n_ctx_tiles
    a_w = yap_ref.shape[1]

    def proj_out(s):
        rs = s["rows"]
        pick = lambda p_ref, s_ref: pltpu.bitcast(
            jnp.where(is_ctx, pltpu.bitcast(p_ref[rs, :], jnp.uint32), pltpu.bitcast(s_ref[rs, :], jnp.uint32)), BF16)
        ya = pick(yap_ref, yas_ref)
        yb = pick(ybp_ref, ybs_ref)
        return dict(s, y=_dot(ya, w_ref[:a_w, :]) + _dot(yb, w_ref[a_w:, :]))

    def tail(s):
        rs = s["rows"]
        x1, hf, lg = _tail_rows(_group_rows(xc_ref, xl_ref, rs, n_ctx_tiles), s["y"], gm_ref[...], lg_ref[...],
                                lb_ref[...], scf_ref[...], shf_ref[...], wr_ref[...], br_ref[...], alpha)
        x1_ref[rs, :] = x1
        hf_ref[rs, :] = hf
        return dict(s, logits=lg)

    states = _run_skewed([proj_out, tail], [dict(rows=rs) for rs in _row_chains(xc_ref.shape[0], 4)])
    _tail_route(jnp.concatenate([s["logits"] for s in states], axis=1), route_ref, gate_ref, cnt_ref, carry_ref)


def _tail_out(t, d):
    shapes = [jax.ShapeDtypeStruct((t, d), F32), jax.ShapeDtypeStruct((t, d // 2), jnp.int32),
              jax.ShapeDtypeStruct((2 * TOP_K, t), jnp.int32), jax.ShapeDtypeStruct((t, LANES), F32),
              jax.ShapeDtypeStruct((N_EXPERTS, LANES), F32)]
    specs = [pl.BlockSpec((TM, d), lambda i, r: (i, 0)), pl.BlockSpec((TM, d // 2), lambda i, r: (i, 0)),
             pl.BlockSpec((2 * TOP_K, TM), lambda i, r: (0, i)), pl.BlockSpec((TM, LANES), lambda i, r: (i, 0)),
             pl.BlockSpec((N_EXPERTS, LANES), lambda i, r: (0, 0))]
    return shapes, specs


_TAIL_SCRATCH = [pltpu.VMEM((N_EXPERTS, 1), F32)]


def _post_ab(x, ya_p, yb_p, ya_s, yb_s, w_out, mod3, mod_base, rows, ln_g, ln_b, w_r, b_r, alpha):
    n_ctx_tiles = ya_p.shape[0] // TM
    (xc, xl), x_specs = _group_specs(x, n_ctx_tiles)
    d = xc.shape[1]
    nt = rows.shape[0]
    t = nt * TM
    a_w, q_w = ya_p.shape[1], yb_p.shape[1]
    mod_spec = lambda j: pl.BlockSpec((None, 1, d), lambda i, r: (mod_base + j * 16 + r[i], 0, 0))
    whole = lambda shp: pl.BlockSpec(shp, lambda i, r: (0,) * len(shp))
    ctx = lambda w: pl.BlockSpec((TM, w), lambda i, r: (jnp.minimum(i, n_ctx_tiles - 1), 0))
    lat = lambda w: pl.BlockSpec((TM, w), lambda i, r: (jnp.maximum(i - n_ctx_tiles, 0), 0))
    shapes, specs = _tail_out(t, d)
    grid_spec = pltpu.PrefetchScalarGridSpec(
        num_scalar_prefetch=1, grid=(nt,),
        in_specs=x_specs + [ctx(a_w), ctx(q_w), lat(a_w), lat(q_w),
                            whole(w_out.shape), mod_spec(2), whole((1, d)), whole((1, d)), mod_spec(4), mod_spec(3),
                            whole(w_r.shape), whole(b_r.shape)],
        out_specs=specs,
        scratch_shapes=_TAIL_SCRATCH,
    )
    return pl.pallas_call(
        functools.partial(_post_ab_kernel, n_ctx_tiles=n_ctx_tiles, alpha=alpha),
        grid_spec=grid_spec, out_shape=shapes, compiler_params=_cparams(1), name="post_ab",
    )(rows, xc, xl, ya_p, yb_p, ya_s, yb_s, w_out, mod3, ln_g, ln_b, mod3, mod3, w_r, b_r)


def _gmlp_kernel(rows_ref, x_ref, scm_ref, shm_ref, win_ref, bin_ref, gv_ref, bv_ref, wsp_ref, bsp_ref, wout_ref,
                 gm_ref, lg_ref, lb_ref, scf_ref, shf_ref, wr_ref, br_ref, x1_ref, hf_ref, route_ref, gate_ref,
                 cnt_ref, carry_ref, *, alpha):
    del rows_ref
    assert (x_ref.shape[0] // 2) % CHUNK == 0
    c_w = win_ref.shape[1] // 2
    gd = c_w // C_GROUPS

    def norm_in(s):
        x = x_ref[s["rows"], :]
        return dict(s, x=x, h=(_ln(x) * (1.0 + scm_ref[...]) + shm_ref[...]).astype(BF16))

    def proj_in(s):
        return dict(s, z=_dot(s["h"], win_ref[...]) + bin_ref[...])

    def gate_split(s):
        z = s["z"]
        z = 0.5 * z * (1.0 + lax.erf(z * (2.0 ** -0.5)))
        return dict(s, u=z[:, :c_w], v=(_ln(z[:, c_w:]) * gv_ref[...] + bv_ref[...]).astype(BF16))

    def spatial(s):
        v = s["v"]
        chunks = []
        for n in range(v.shape[0] // CHUNK):
            groups = [_dot(wsp_ref[g], v[n * CHUNK:(n + 1) * CHUNK, g * gd:(g + 1) * gd]) + bsp_ref[g]
                      for g in range(C_GROUPS)]
            chunks.append(jnp.concatenate(groups, axis=1))
        return dict(s, t=(s["u"] * jnp.concatenate(chunks, axis=0)).astype(BF16))

    def proj_out(s):
        return dict(s, y=_dot(s["t"], wout_ref[...]))

    def tail(s):
        x1, hf, lg = _tail_rows(s["x"], s["y"], gm_ref[...], lg_ref[...], lb_ref[...], scf_ref[...], shf_ref[...],
                                wr_ref[...], br_ref[...], alpha)
        x1_ref[s["rows"], :] = x1
        hf_ref[s["rows"], :] = hf
        return dict(s, logits=lg)

    states = _run_skewed([norm_in, proj_in, gate_split, spatial, proj_out, tail],
                         [dict(rows=rs) for rs in _row_chains(x_ref.shape[0])])
    _tail_route(jnp.concatenate([s["logits"] for s in states], axis=1), route_ref, gate_ref, cnt_ref, carry_ref)


def _gmlp(x, mod3, mod_base, rows, w_in, b_in, g_v, b_v, w_sp, b_sp, w_out, ln_g, ln_b, w_r, b_r, alpha):
    t, d = x.shape
    nt = t // TM
    mod_spec = lambda j: pl.BlockSpec((None, 1, d), lambda i, r: (mod_base + j * 16 + r[i], 0, 0))
    whole = lambda shp: pl.BlockSpec(shp, lambda i, r: (0,) * len(shp))
    shapes, specs = _tail_out(t, d)
    grid_spec = pltpu.PrefetchScalarGridSpec(
        num_scalar_prefetch=1, grid=(nt,),
        in_specs=[pl.BlockSpec((TM, d), lambda i, r: (i, 0)), mod_spec(1), mod_spec(0),
                  whole(w_in.shape), whole(b_in.shape), whole(g_v.shape), whole(b_v.shape),
                  whole(w_sp.shape), whole(b_sp.shape), whole(w_out.shape),
                  mod_spec(2), whole((1, d)), whole((1, d)), mod_spec(4), mod_spec(3),
                  whole(w_r.shape), whole(b_r.shape)],
        out_specs=specs,
        scratch_shapes=_TAIL_SCRATCH,
    )
    return pl.pallas_call(
        functools.partial(_gmlp_kernel, alpha=alpha),
        grid_spec=grid_spec, out_shape=shapes, compiler_params=_cparams(1), name="gmlp",
    )(rows, x, mod3, mod3, w_in, b_in, g_v, b_v, w_sp, b_sp, w_out, mod3, ln_g, ln_b, mod3, mod3, w_r, b_r)


def _moe_kernel(be_ref, bf_ref, nx_ref, sl_ref, hf_ref, nu_ref, x_ref, wgu_hbm, bgu_ref, wdn_hbm, bdn_ref, o_ref,
                wgu_f, wdn_f, wgu_s, wdn_s, sem, *, layer):
    i = pl.program_id(0)
    d_ff = wdn_s.shape[0]

    def weight_copies(e, slot):
        return (pltpu.make_async_copy(wgu_hbm.at[layer, e], wgu_f.at[slot], sem.at[0, slot]),
                pltpu.make_async_copy(wdn_hbm.at[layer, e], wdn_f.at[slot], sem.at[1, slot]))

    @pl.when(i < nu_ref[0])
    def _():
        @pl.when(bf_ref[i] == 1)
        def _():
            slot = sl_ref[i]

            @pl.when(i == 0)
            def _():
                for c in weight_copies(be_ref[i], slot):
                    c.start()

            for c in weight_copies(be_ref[i], slot):
                c.wait()
            wgu_s[...] = wgu_f[slot].astype(BF16)
            wdn_s[...] = wdn_f[slot].astype(BF16)

            @pl.when(nx_ref[i] >= 0)
            def _():
                for c in weight_copies(nx_ref[i], 1 - slot):
                    c.start()

        def up(s):
            return dict(s, gu=_dot(_unpack_bf16_pairs(x_ref[s["rows"], :]), wgu_s[...]) + bgu_ref[...])

        def act(s):
            gu = s["gu"]
            gate = jnp.minimum(gu[:, :d_ff], SWIGLU_LIMIT)
            lin = jnp.clip(gu[:, d_ff:], -SWIGLU_LIMIT, SWIGLU_LIMIT)
            glu = gate * jax.nn.sigmoid(SWIGLU_ALPHA * gate)
            return dict(s, gu=None, hid=((lin + 1.0) * glu).astype(BF16))

        def down(s):
            o_ref[s["rows"], :] = _pack_bf16_pairs(_dot(s["hid"], wdn_s[...]) + bdn_ref[...])
            return dict(s, hid=None)

        def run(n_chains, n_rows):
            _run_skewed([up, act, down], [dict(rows=rs) for rs in _row_chains(n_rows, n_chains)])

        @pl.when(hf_ref[i] == 0)
        def _():
            run(2, TM_MOE)

        @pl.when(hf_ref[i] == 1)
        def _():
            run(1, TM_MOE // 2)


def _moe(xs, plan, w_gu, b_gu, w_dn, b_dn, layer):
    n_rows = xs.shape[0]
    depth, n_e, d, ff2 = w_gu.shape
    d_ff = ff2 // 2
    nt = n_rows // TM_MOE
    n_plan = len(plan)
    row_spec = pl.BlockSpec((TM_MOE, d // 2), lambda i, *p: (jnp.minimum(i, p[-1][0] - 1), 0))
    bias_spec = lambda w: pl.BlockSpec((None, None, 1, w), lambda i, *p: (layer, p[0][i], 0, 0))
    grid_spec = pltpu.PrefetchScalarGridSpec(
        num_scalar_prefetch=n_plan, grid=(nt,),
        in_specs=[row_spec, pl.BlockSpec(memory_space=pl.ANY), bias_spec(ff2),
                  pl.BlockSpec(memory_space=pl.ANY), bias_spec(d)],
        out_specs=row_spec,
        scratch_shapes=[pltpu.VMEM((2, d, ff2), F32), pltpu.VMEM((2, d_ff, d), F32),
                        pltpu.VMEM((d, ff2), BF16), pltpu.VMEM((d_ff, d), BF16),
                        pltpu.SemaphoreType.DMA((2, 2))],
    )
    return pl.pallas_call(
        functools.partial(_moe_kernel, layer=layer), grid_spec=grid_spec,
        out_shape=jax.ShapeDtypeStruct((n_rows, d // 2), jnp.int32),
        compiler_params=_cparams(1), name="moe_experts",
    )(*plan, xs, w_gu, b_gu.reshape(depth, n_e, 1, ff2), w_dn, b_dn.reshape(depth, n_e, 1, d))


def _unpack_f32_pairs(p):
    u = pltpu.bitcast(p, jnp.uint32)
    return jnp.concatenate([pltpu.bitcast(u << 16, F32), pltpu.bitcast(u & jnp.uint32(0xFFFF0000), F32)], axis=1)


def _combine_kernel(rows_ref, x_ref, y0_ref, y1_ref, y2_ref, y3_ref, gate_ref, gf_ref, lg_ref, lb_ref, *o_refs,
                    alpha, n_ctx_tiles):
    del rows_ref
    gate = gate_ref[...]
    y = gate[:, 0:1] * _unpack_f32_pairs(y0_ref[...])
    for k, y_ref in ((1, y1_ref), (2, y2_ref), (3, y3_ref)):
        y = y + gate[:, k:k + 1] * _unpack_f32_pairs(y_ref[...])
    out = _ln(alpha * x_ref[...] + gf_ref[...] * y) * lg_ref[...] + lb_ref[...]
    if n_ctx_tiles is None:
        o_refs[0][...] = out
    else:
        @pl.when(pl.program_id(0) < n_ctx_tiles)
        def _():
            o_refs[0][...] = out

        @pl.when(pl.program_id(0) >= n_ctx_tiles)
        def _():
            o_refs[1][...] = out


def _combine(x1, y4, gate, mod3, mod_base, rows, ln_g, ln_b, alpha, t_ctx=None):
    t, d = x1.shape
    nt = t // TM
    y_spec = lambda k: pl.BlockSpec((None, TM, d // 2), lambda i, r: (k, i, 0))
    if t_ctx is None:
        n_ctx_tiles = None
        out_specs = pl.BlockSpec((TM, d), lambda i, r: (i, 0))
        out_shape = jax.ShapeDtypeStruct((t, d), F32)
    else:
        n_ctx_tiles = t_ctx // TM
        out_specs = [pl.BlockSpec((TM, d), lambda i, r: (jnp.minimum(i, n_ctx_tiles - 1), 0)),
                     pl.BlockSpec((TM, d), lambda i, r: (jnp.maximum(i - n_ctx_tiles, 0), 0))]
        out_shape = [jax.ShapeDtypeStruct((t_ctx, d), F32), jax.ShapeDtypeStruct((t - t_ctx, d), F32)]
    grid_spec = pltpu.PrefetchScalarGridSpec(
        num_scalar_prefetch=1, grid=(nt,),
        in_specs=[pl.BlockSpec((TM, d), lambda i, r: (i, 0)), y_spec(0), y_spec(1), y_spec(2), y_spec(3),
                  pl.BlockSpec((TM, LANES), lambda i, r: (i, 0)),
                  pl.BlockSpec((None, 1, d), lambda i, r: (mod_base + 5 * 16 + r[i], 0, 0)),
                  pl.BlockSpec((1, d), lambda i, r: (0, 0)), pl.BlockSpec((1, d), lambda i, r: (0, 0))],
        out_specs=out_specs,
    )
    return pl.pallas_call(
        functools.partial(_combine_kernel, alpha=alpha, n_ctx_tiles=n_ctx_tiles), grid_spec=grid_spec,
        out_shape=out_shape, compiler_params=_cparams(1), name="moe_combine",
    )(rows, x1, y4, y4, y4, y4, gate, mod3, ln_g, ln_b)


def _route_plan(route, counts, n_tiles):
    idx, rank = route[:TOP_K], route[TOP_K:]
    padded = (counts + TM_MOE - 1) // TM_MOE * TM_MOE
    pend = jnp.cumsum(padded)
    pstart = pend - padded
    onehot = idx[:, :, None] == jnp.arange(N_EXPERTS, dtype=jnp.int32)[None, None, :]
    dest = jnp.sum(jnp.where(onehot, pstart[None, None, :], 0), axis=-1) + rank
    n_used = (pend[-1] // TM_MOE).astype(jnp.int32)
    tile_start = jnp.arange(n_tiles, dtype=jnp.int32) * TM_MOE
    blk_e = jnp.sum((tile_start[:, None] >= pend[None, :]).astype(jnp.int32), axis=1)
    last_e = jnp.sum((jnp.maximum(n_used - 1, 0) * TM_MOE >= pend).astype(jnp.int32))
    blk_e = jnp.minimum(jnp.where(jnp.arange(n_tiles) < n_used, blk_e, last_e), N_EXPERTS - 1).astype(jnp.int32)
    blk_first = jnp.concatenate([jnp.ones((1,), jnp.int32), (blk_e[1:] != blk_e[:-1]).astype(jnp.int32)])
    ar = jnp.arange(N_EXPERTS, dtype=jnp.int32)
    later = jnp.logical_and((counts > 0)[None, :], ar[None, :] > ar[:, None])
    next_e = jnp.min(jnp.where(later, ar[None, :], N_EXPERTS), axis=1)
    next_e = jnp.where(next_e == N_EXPERTS, -1, next_e)
    blk_next = jnp.sum(jnp.where(blk_e[:, None] == ar[None, :], next_e[None, :], 0), axis=1).astype(jnp.int32)
    blk_slot = ((jnp.cumsum(blk_first) - 1) % 2).astype(jnp.int32)
    left = jnp.sum(jnp.where(blk_e[:, None] == ar[None, :], (pstart + counts)[None, :], 0), axis=1) - tile_start
    blk_half = (left <= TM_MOE // 2).astype(jnp.int32)
    return dest, (blk_e, blk_first, blk_next, blk_slot, blk_half, n_used.reshape(1))


SC_CORES = 2
SC_SUBCORES = 16


def _sc_mesh():
    return plsc.VectorSubcoreMesh(core_axis_name="core", subcore_axis_name="subcore")


def _sc_scatter_rows(src, dest, n_rows, window):
    n_slots, t = dest.shape
    w = src.shape[1]
    per_worker = t // (SC_CORES * SC_SUBCORES)
    assert per_worker * SC_CORES * SC_SUBCORES == t and per_worker % window == 0

    @functools.partial(pl.kernel, out_type=jax.ShapeDtypeStruct((n_rows, w), src.dtype), mesh=_sc_mesh(),
                       scratch_types=[pltpu.VMEM((n_slots, window), jnp.int32), pltpu.VMEM((window, w), src.dtype)],
                       name="sc_dispatch")
    def scatter(src_hbm, dest_hbm, out_hbm, idx_v, rows_v):
        worker = lax.axis_index("subcore") * SC_CORES + lax.axis_index("core")

        @pl.loop(0, per_worker // window)
        def _(c):
            base = pl.multiple_of(worker * per_worker + c * window, window)
            pltpu.sync_copy(src_hbm.at[pl.ds(base, window)], rows_v)
            pltpu.sync_copy(dest_hbm.at[:, pl.ds(base, window)], idx_v)
            for k in range(n_slots):
                pltpu.sync_copy(rows_v, out_hbm.at[idx_v.at[k]])

    return scatter(src, dest)


def _sc_gather_rows(table, idx, window):
    n = idx.shape[0]
    w = table.shape[1]
    per_worker = n // (SC_CORES * SC_SUBCORES)
    assert per_worker * SC_CORES * SC_SUBCORES == n and per_worker % window == 0

    @functools.partial(pl.kernel, out_type=jax.ShapeDtypeStruct((n, w), table.dtype), mesh=_sc_mesh(),
                       scratch_types=[pltpu.VMEM((window,), jnp.int32), pltpu.VMEM((window, w), table.dtype)],
                       name="sc_collect")
    def gather(table_hbm, idx_hbm, out_hbm, idx_v, rows_v):
        worker = lax.axis_index("subcore") * SC_CORES + lax.axis_index("core")

        @pl.loop(0, per_worker // window)
        def _(c):
            base = pl.multiple_of(worker * per_worker + c * window, window)
            pltpu.sync_copy(idx_hbm.at[pl.ds(base, window)], idx_v)
            pltpu.sync_copy(table_hbm.at[idx_v], rows_v)
            pltpu.sync_copy(rows_v, out_hbm.at[pl.ds(base, window)])

    return gather(table, idx)


def _dft_tables(n):
    k = jnp.arange(n, dtype=jnp.int32)

    def trig(rows):
        ang = ((rows[:, None] * k[None, :]) % n).astype(F32) * (2.0 * math.pi / n)
        return jnp.cos(ang), jnp.sin(ang)

    n2 = 64
    if n < 4 * n2:
        return trig(k)
    ca, sa = (z[:, None, :] for z in trig(jnp.arange(n // n2, dtype=jnp.int32) * n2))
    cb, sb = (z[None, :, :] for z in trig(jnp.arange(n2, dtype=jnp.int32)))
    return (ca * cb - sa * sb).reshape(n, n), (sa * cb + ca * sb).reshape(n, n)


def _rope_tables(n_lat, n_ctx_rows):
    half, quarter = HEAD_DIM // 2, HEAD_DIM // 4
    tpos = jnp.arange(n_lat, dtype=jnp.int32)
    lane = jnp.arange(LANES, dtype=jnp.int32) % HEAD_DIM
    pos = jnp.where(lane[None, :] < half, (tpos // GRID_W)[:, None], (tpos % GRID_W)[:, None]).astype(F32)
    fidx = (lane % quarter).astype(F32)
    freqs = ROPE_THETA ** (-fidx / quarter)
    ang = pos * freqs[None, :]
    cos, sin = jnp.cos(ang), jnp.sin(ang)
    first = (lane % half) < quarter
    sa = jnp.where(first[None, :], -sin, 0.0)
    sb = jnp.where(first[None, :], 0.0, sin)
    ident = lambda v: jnp.full((n_ctx_rows, LANES), v, F32)
    return (jnp.concatenate([ident(1.0), cos]), jnp.concatenate([ident(0.0), sa]), jnp.concatenate([ident(0.0), sb]))


def kernel(x_prompt, x_sample, cache_k_ab, cache_v_ab, c, c_ctx, w_ada, b_ada, ln_mix_g, ln_mix_b, ln_ffn_g, ln_ffn_b, w_in_ab, w_out_ab, sink_ab, w_in_c, b_in_c, ln_v_g, ln_v_b, w_sp, b_sp, w_out_c, w_router, b_router, w_gu, b_gu, w_dn, b_dn):
    n_ctx_b, ctx_seq, d = x_prompt.shape
    n_lat_b, lat_seq, _ = x_sample.shape
    depth = w_ada.shape[0]
    t_ctx, t_lat = n_ctx_b * ctx_seq, n_lat_b * lat_seq
    t = t_ctx + t_lat
    alpha = (2 * depth) ** 0.25
    kv_w = N_KV_HEADS * HEAD_DIM
    a_w = d // 4
    q_w = d - a_w
    assert t_ctx % TM == 0 and lat_seq % TM == 0 and t_ctx % lat_seq == 0 and n_lat_b + 1 <= 16

    rows_np = np.concatenate([np.zeros(t_ctx // TM, np.int32), 1 + np.arange(t_lat // TM, dtype=np.int32) // (lat_seq // TM)])
    rblk_np = np.concatenate([np.zeros(t_ctx // TM, np.int32), 1 + np.arange(t_lat // TM, dtype=np.int32) % (lat_seq // TM)])
    rows, rblk = jnp.asarray(rows_np), jnp.asarray(rblk_np)

    cond16 = jnp.zeros((16, d), F32).at[0].set(c_ctx).at[1:1 + n_lat_b].set(c)
    mod = _ada(cond16, w_ada, b_ada)
    mod3 = mod.reshape(depth, 16, 6, d).transpose(0, 2, 1, 3).reshape(depth * 6 * 16, 1, d)

    cos, sa, sb = _rope_tables(lat_seq, TM)
    cs_ctx, ss_ctx = (z.astype(BF16) for z in _dft_tables(ctx_seq))
    cs_lat, ss_lat = (z.astype(BF16) for z in _dft_tables(lat_seq))
    gd = a_w // A_GROUPS
    cd, sd = _dft_tables(gd)
    eye = jnp.eye(A_GROUPS, dtype=F32)
    bdc, bds = jnp.kron(eye, cd).astype(BF16), jnp.kron(eye, sd).astype(BF16)

    past = cache_k_ab.shape[2]
    cache_k = cache_k_ab.reshape(n_lat_b, -1, past, kv_w).astype(BF16)
    cache_v = cache_v_ab.reshape(n_lat_b, -1, past, kv_w).astype(BF16)

    x = (x_prompt.reshape(t_ctx, d), x_sample.reshape(t_lat, d))
    row2 = lambda v: v.reshape(1, -1)
    ks, vs = [], []
    for l in range(depth):
        j = l // 2
        mod_base = l * 6 * 16
        w_r, b_r = w_router[l].T, b_router[l].reshape(-1, 1)
        if l % 2 == 0:
            ac, as_, q, k32, v32, k, v = _inproj(x, t_ctx // TM, mod3, mod_base, rows, rblk, w_in_ab[j].astype(BF16),
                                                 cos, sa, sb, bdc, bds, a_w, q_w, kv_w)
            ks.append(k32[:t_ctx].reshape(n_ctx_b, ctx_seq, N_KV_HEADS, HEAD_DIM))
            vs.append(v32[:t_ctx].reshape(n_ctx_b, ctx_seq, N_KV_HEADS, HEAD_DIM))
            ya_p = _fourier(ac, as_, cs_ctx, ss_ctx, n_ctx_b, ctx_seq, 0, min(ctx_seq, 512))
            ya_s = _fourier(ac, as_, cs_lat, ss_lat, n_lat_b, lat_seq, t_ctx // lat_seq, min(lat_seq, 512))
            yb_p = _attn_ctx(sink_ab[j], q, k, v, n_ctx_b, ctx_seq)
            yb_s = _attn_lat(sink_ab[j], q, k, v, cache_k, cache_v, j, n_lat_b, lat_seq, t_ctx)
            x1, hf, route, gate, cnt = _post_ab(x, ya_p, yb_p, ya_s, yb_s, w_out_ab[j].astype(BF16), mod3, mod_base,
                                                rows, row2(ln_mix_g[l]), row2(ln_mix_b[l]), w_r, b_r, alpha)
        else:
            x1, hf, route, gate, cnt = _gmlp(x, mod3, mod_base, rows, w_in_c[j].astype(BF16), row2(b_in_c[j]),
                                             row2(ln_v_g[j]), row2(ln_v_b[j]), w_sp[j].astype(BF16),
                                             b_sp[j][:, :, None], w_out_c[j].astype(BF16), row2(ln_mix_g[l]),
                                             row2(ln_mix_b[l]), w_r, b_r, alpha)
        n_rows = t * TOP_K + N_EXPERTS * TM_MOE
        dest, plan = _route_plan(route, cnt[:, 0].astype(jnp.int32), n_rows // TM_MOE)
        xs = _sc_scatter_rows(hf, dest, n_rows, 128)
        out_sorted = _moe(xs, plan, w_gu, b_gu, w_dn, b_dn, l)
        y4 = _sc_gather_rows(out_sorted, dest.reshape(-1), 128).reshape(TOP_K, t, d // 2)
        x = _combine(x1, y4, gate, mod3, mod_base, rows, row2(ln_ffn_g[l]), row2(ln_ffn_b[l]), alpha,
                     t_ctx if l == depth - 1 else None)

    y_prompt = x[0].reshape(n_ctx_b, ctx_seq, d)
    y_sample = x[1].reshape(n_lat_b, lat_seq, d)
    return (y_prompt, y_sample, jnp.stack(ks, axis=1), jnp.stack(vs, axis=1))
```

```python
import functools
import math

import numpy as np
import jax
import jax.numpy as jnp
from jax import lax
from jax.experimental import pallas as pl
from jax.experimental.pallas import tpu as pltpu
from jax.experimental.pallas import tpu_sc as plsc

GRID_W = 64
BLK = 128
WINDOW = 128
HEAD_DIM = 64
A_GROUPS = 4
N_KV_HEADS = 4
C_GROUPS = 8
CHUNK = 128
N_EXPERTS = 32
TOP_K = 4
SWIGLU_LIMIT = 7.0
SWIGLU_ALPHA = 1.702
ROPE_THETA = 10000.0
LN_EPS = 1e-6
NEG_INF = -1e30
LOG2_E = math.log2(math.e)

LANES = 128
TM = 1024
TM_COMBINE = 512
TM_MOE = 512
VMEM_LIMIT = 52 * 1024 * 1024

F32 = jnp.float32
BF16 = jnp.bfloat16


def _cparams(n_axes):
    return pltpu.CompilerParams(dimension_semantics=("arbitrary",) * n_axes,
                                vmem_limit_bytes=VMEM_LIMIT)


def _ln(x):
    mu = jnp.mean(x, axis=-1, keepdims=True)
    xc = x - mu
    var = jnp.mean(xc * xc, axis=-1, keepdims=True)
    return xc * lax.rsqrt(var + LN_EPS)


def _dot(a, b):
    return jnp.dot(a, b, preferred_element_type=F32)


def _dot_nt(a, b):
    return lax.dot_general(a, b, (((1,), (1,)), ((), ())), preferred_element_type=F32)


def _split(a):
    hi = a.astype(BF16)
    lo = (a - hi.astype(F32)).astype(BF16)
    return hi, lo


def _dot_3pass(a, b):
    a_hi, a_lo = _split(a)
    b_hi, b_lo = _split(b)
    return _dot(a_hi, b_hi) + (_dot(a_hi, b_lo) + _dot(a_lo, b_hi))


def _lane_select(cols, width):
    m = cols[0].shape[0]
    lane = lax.broadcasted_iota(jnp.int32, (m, width), 1)
    out = jnp.zeros((m, width), cols[0].dtype)
    for j, c in enumerate(cols):
        out = jnp.where(lane == j, c, out)
    return out


def _pack_bf16_pairs(v):
    n = v.shape[1] // 2
    lo = pltpu.bitcast(v[:, :n].astype(BF16).astype(F32), jnp.uint32) >> 16
    hi = pltpu.bitcast(v[:, n:].astype(BF16).astype(F32), jnp.uint32) & jnp.uint32(0xFFFF0000)
    return pltpu.bitcast(lo | hi, jnp.int32)


def _unpack_bf16_pairs(p):
    u = pltpu.bitcast(p, jnp.uint32)
    lo = pltpu.bitcast(u << 16, F32)
    hi = pltpu.bitcast(u & jnp.uint32(0xFFFF0000), F32)
    return jnp.concatenate([lo, hi], axis=1).astype(BF16)


def _row_chains(n_rows, n_chains=2):
    step = n_rows // n_chains
    return [slice(c * step, (c + 1) * step) for c in range(n_chains)]


def _run_skewed(stages, states):
    states = list(states)
    for step in range(len(stages) + len(states) - 1):
        for c in range(len(states)):
            if 0 <= step - c < len(stages):
                states[c] = stages[step - c](states[c])
    return states


def _tail_rows(x, y, g_m, ln_g, ln_b, sc_f, sh_f, w_r, b_r, alpha):
    x1 = _ln(alpha * x + g_m * y) * ln_g + ln_b
    hf = _ln(x1) * (1.0 + sc_f) + sh_f
    hf_hi, hf_lo = _split(hf)
    w_hi, w_lo = _split(w_r)
    logits = _dot_nt(w_hi, hf_hi) + (_dot_nt(w_hi, hf_lo) + _dot_nt(w_lo, hf_hi)) + b_r
    return x1, _pack_bf16_pairs(hf), logits


def _tail_route(logits, route_ref, gate_ref, cnt_ref, carry_ref):
    @pl.when(pl.program_id(0) == 0)
    def _():
        carry_ref[...] = jnp.zeros_like(carry_ref)

    tm = logits.shape[1]
    sub = lax.broadcasted_iota(jnp.int32, logits.shape, 0)
    vals = logits
    top_v, top_i = [], []
    for _ in range(TOP_K):
        m = jnp.max(vals, axis=0, keepdims=True)
        am = jnp.min(jnp.where(vals == m, sub, N_EXPERTS), axis=0, keepdims=True)
        top_v.append(m)
        top_i.append(am)
        vals = jnp.where(sub == am, -jnp.inf, vals)
    e = [jnp.exp(v - top_v[0]) for v in top_v]
    denom = e[0] + e[1] + e[2] + e[3]
    gates_t = jnp.concatenate([ek / denom for ek in e] + [jnp.zeros((LANES - TOP_K, tm), F32)], axis=0)
    gate_ref[...] = gates_t.T

    member = jnp.zeros(logits.shape, F32)
    for am in top_i:
        member = jnp.where(sub == am, 1.0, member)
    r_i = lax.broadcasted_iota(jnp.int32, (tm, tm), 0)
    c_i = lax.broadcasted_iota(jnp.int32, (tm, tm), 1)
    earlier = jnp.where(r_i < c_i, 1.0, 0.0).astype(BF16)
    before = _dot(member.astype(BF16), earlier) + carry_ref[...]
    ranks = [jnp.sum(jnp.where(sub == am, before, 0.0), axis=0, keepdims=True).astype(jnp.int32) for am in top_i]
    route_ref[...] = jnp.concatenate(top_i + ranks, axis=0)
    carry = carry_ref[...] + jnp.sum(member, axis=1, keepdims=True)
    carry_ref[...] = carry
    cnt_ref[...] = jnp.broadcast_to(carry, cnt_ref.shape)


def _ada_kernel(cond_ref, w_ref, b_ref, o_ref):
    c = cond_ref[...]
    s = (c * jax.nn.sigmoid(c)).astype(BF16)
    o_ref[...] = _dot(s, w_ref[...].astype(BF16)) + b_ref[...]


def _ada(cond16, w_ada, b_ada):
    depth, d, n = w_ada.shape
    tn = 1536
    return pl.pallas_call(
        _ada_kernel,
        grid=(depth, n // tn),
        in_specs=[
            pl.BlockSpec((16, d), lambda l, j: (0, 0)),
            pl.BlockSpec((None, d, tn), lambda l, j: (l, 0, j)),
            pl.BlockSpec((None, 1, tn), lambda l, j: (l, 0, j)),
        ],
        out_specs=pl.BlockSpec((None, 16, tn), lambda l, j: (l, 0, j)),
        out_shape=jax.ShapeDtypeStruct((depth, 16, n), F32),
        compiler_params=_cparams(2),
        name="ada_mod",
    )(cond16, w_ada, b_ada.reshape(depth, 1, n))


def _group_rows(xc_ref, xl_ref, rows, n_ctx_tiles):
    return jnp.where(pl.program_id(0) < n_ctx_tiles, xc_ref[rows, :], xl_ref[rows, :])


def _group_specs(x, n_ctx_tiles):
    if isinstance(x, tuple):
        arrays, lat = x, (lambda i, *_: (jnp.maximum(i - n_ctx_tiles, 0), 0))
    else:
        arrays, lat = (x, x), (lambda i, *_: (jnp.maximum(i, n_ctx_tiles), 0))
    d = arrays[0].shape[1]
    ctx = lambda i, *_: (jnp.minimum(i, n_ctx_tiles - 1), 0)
    return arrays, [pl.BlockSpec((TM, d), ctx), pl.BlockSpec((TM, d), lat)]


def _inproj_kernel(rows_ref, rblk_ref, xc_ref, xl_ref, sc_ref, sh_ref, w_ref, cos_ref, sa_ref, sb_ref, bdc_ref,
                   bds_ref, ac_ref, as_ref, q_ref, k_ref, v_ref, kb_ref, vb_ref, *, a_w, q_w, kv_w, n_ctx_tiles):
    del rows_ref, rblk_ref
    g = q_w // kv_w

    def norm_in(s):
        x = _group_rows(xc_ref, xl_ref, s["rows"], n_ctx_tiles)
        return dict(s, h=(_ln(x) * (1.0 + sc_ref[...]) + sh_ref[...]).astype(BF16))

    def proj(s):
        return dict(s, p=_dot(s["h"], w_ref[...]))

    def finish(s):
        rs, p = s["rows"], s["p"]
        a = p[:, :a_w].astype(BF16)
        ac_ref[rs, :] = _dot(a, bdc_ref[...]).astype(BF16)
        as_ref[rs, :] = _dot(a, bds_ref[...]).astype(BF16)
        cos, sa, sb = cos_ref[rs, :], sa_ref[rs, :], sb_ref[rs, :]

        def rope(t):
            w = t.shape[1]
            reps = w // LANES
            c, a_, b_ = (jnp.tile(z, (1, reps)) for z in (cos, sa, sb))
            nxt = pltpu.roll(t, w - HEAD_DIM // 4, 1)
            prv = pltpu.roll(t, HEAD_DIM // 4, 1)
            return t * c + nxt * a_ + prv * b_

        q = rope(p[:, a_w:a_w + q_w]) * (HEAD_DIM ** -0.5 * LOG2_E)
        lane = lax.broadcasted_iota(jnp.int32, (q.shape[0], LANES), 1)
        for j in range(q_w // HEAD_DIM):
            tile = q[:, (j // 2) * LANES:(j // 2 + 1) * LANES]
            dst_low = (j // g) % 2 == 0
            if (j % 2 == 0) != dst_low:
                tile = pltpu.roll(tile, HEAD_DIM, 1)
            keep = (lane < HEAD_DIM) if dst_low else (lane >= HEAD_DIM)
            q_ref[rs, j * LANES:(j + 1) * LANES] = jnp.where(keep, tile, 0.0).astype(BF16)
        k = rope(p[:, a_w + q_w:a_w + q_w + kv_w])
        v = p[:, a_w + q_w + kv_w:]
        k_ref[rs, :] = k
        v_ref[rs, :] = v
        kb_ref[rs, :] = k.astype(BF16)
        vb_ref[rs, :] = v.astype(BF16)
        return s

    _run_skewed([norm_in, proj, finish], [dict(rows=rs) for rs in _row_chains(xc_ref.shape[0])])


def _inproj(x, n_ctx_tiles, mod3, mod_base, rows, rblk, w_in, cos, sa, sb, bdc, bds, a_w, q_w, kv_w):
    (xc, xl), x_specs = _group_specs(x, n_ctx_tiles)
    d = xc.shape[1]
    t = rows.shape[0] * TM
    n = w_in.shape[1]
    nt = t // TM
    mod_spec = lambda j: pl.BlockSpec((None, 1, d), lambda i, r, rb: (mod_base + j * 16 + r[i], 0, 0))
    whole = lambda shp: pl.BlockSpec(shp, lambda i, r, rb: (0,) * len(shp))
    rope_spec = pl.BlockSpec((TM, LANES), lambda i, r, rb: (rb[i], 0))
    tok = lambda w: pl.BlockSpec((TM, w), lambda i, r, rb: (i, 0))
    grid_spec = pltpu.PrefetchScalarGridSpec(
        num_scalar_prefetch=2, grid=(nt,),
        in_specs=x_specs + [mod_spec(1), mod_spec(0), whole((d, n)), rope_spec, rope_spec, rope_spec,
                            whole((a_w, a_w)), whole((a_w, a_w))],
        out_specs=[tok(a_w), tok(a_w), tok(2 * q_w), tok(kv_w), tok(kv_w), tok(kv_w), tok(kv_w)],
    )
    return pl.pallas_call(
        functools.partial(_inproj_kernel, a_w=a_w, q_w=q_w, kv_w=kv_w, n_ctx_tiles=n_ctx_tiles),
        grid_spec=grid_spec,
        out_shape=[jax.ShapeDtypeStruct((t, a_w), BF16), jax.ShapeDtypeStruct((t, a_w), BF16),
                   jax.ShapeDtypeStruct((t, 2 * q_w), BF16), jax.ShapeDtypeStruct((t, kv_w), F32),
                   jax.ShapeDtypeStruct((t, kv_w), F32), jax.ShapeDtypeStruct((t, kv_w), BF16),
                   jax.ShapeDtypeStruct((t, kv_w), BF16)],
        compiler_params=_cparams(1),
        name="inproj_ab",
    )(rows, rblk, xc, xl, mod3, mod3, w_in, cos, sa, sb, bdc, bds)


def _fourier_kernel(c_ref, s_ref, ac_ref, as_ref, o_ref, *, scale):
    y = _dot(c_ref[...], ac_ref[...]) - _dot(s_ref[...], as_ref[...])
    o_ref[...] = (y * scale).astype(BF16)


def _fourier(ac, as_, cs, ss, n_batch, seq, blk_off, tm):
    a_w = ac.shape[1]
    nr = seq // tm
    a_spec = pl.BlockSpec((seq, a_w), lambda r, b: (blk_off + b, 0))
    t_spec = pl.BlockSpec((tm, seq), lambda r, b: (r, 0))
    return pl.pallas_call(
        functools.partial(_fourier_kernel, scale=(seq * (a_w // A_GROUPS)) ** -0.5),
        grid=(nr, n_batch),
        in_specs=[t_spec, t_spec, a_spec, a_spec],
        out_specs=pl.BlockSpec((tm, a_w), lambda r, b: (b * nr + r, 0)),
        out_shape=jax.ShapeDtypeStruct((n_batch * seq, a_w), BF16),
        compiler_params=_cparams(2),
        name="fourier_%d" % seq,
    )(cs, ss, ac, as_)


def _attend_pairs(sink_ref, q_ref, keys, values, mask, o_ref, *, g):
    rows = q_ref.shape[0]
    lane = lax.broadcasted_iota(jnp.int32, (rows, LANES), 1)
    full_mask = None if mask is None else jnp.concatenate([mask] * g, axis=0)

    def scores(s):
        h0, p = s["h0"], s["h0"] // (2 * g)
        qp = jnp.concatenate([q_ref[:, (h0 + j) * LANES:(h0 + j + 1) * LANES] for j in range(g)], axis=0)
        sc = [_dot_nt(qp, kf(p)) for kf in keys]
        if full_mask is not None:
            sc[0] = jnp.where(full_mask, sc[0], NEG_INF)
        return dict(s, sc=sc)

    def row_max(s):
        m = jnp.concatenate([jnp.full((rows, 1), sink_ref[s["h0"] + j] * LOG2_E, F32) for j in range(g)], axis=0)
        sink_col = m
        for sc in s["sc"]:
            m = jnp.maximum(m, jnp.max(sc, axis=-1, keepdims=True))
        return dict(s, m=m, sink=jnp.exp2(sink_col - m))

    def weights(s):
        return dict(s, e=[jnp.exp2(sc - s["m"]).astype(BF16) for sc in s["sc"]], sc=None)

    def weighted_values(s):
        p = s["h0"] // (2 * g)
        acc = None
        for e, vf in zip(s["e"], values):
            v = vf(p)
            pv = _dot(e, jnp.concatenate([v, jnp.ones_like(v)], axis=1))
            acc = pv if acc is None else acc + pv
        return dict(s, o=acc[:, :LANES] / (acc[:, LANES:LANES + 1] + s["sink"]), e=None)

    states = _run_skewed([scores, row_max, weights, weighted_values],
                         [dict(h0=h * g) for h in range(N_KV_HEADS)])
    heads = {}
    for s in states:
        for j in range(g):
            blk = s["o"][j * rows:(j + 1) * rows]
            head = s["h0"] + j
            if (((head // g) % 2 == 0) != (head % 2 == 0)):
                blk = pltpu.roll(blk, HEAD_DIM, 1)
            heads[head] = blk
    for t in range(len(heads) // 2):
        tile = jnp.where(lane < HEAD_DIM, heads[2 * t], heads[2 * t + 1])
        o_ref[:, t * LANES:(t + 1) * LANES] = tile.astype(BF16)


def _attn_ctx_kernel(sink_ref, q_ref, k_ref, v_ref, o_ref, *, g):
    pair = lambda ref: (lambda p: ref[:, p * LANES:(p + 1) * LANES])
    _attend_pairs(sink_ref, q_ref, [pair(k_ref)], [pair(v_ref)], None, o_ref, g=g)


def _attn_ctx(sink, q, k, v, n_batch, seq):
    qp_w, kv_w = q.shape[1], k.shape[1]
    q_w = qp_w // 2
    g = q_w // kv_w
    return pl.pallas_call(
        functools.partial(_attn_ctx_kernel, g=g),
        grid=(n_batch,),
        in_specs=[pl.BlockSpec(memory_space=pltpu.SMEM),
                  pl.BlockSpec((seq, qp_w), lambda b: (b, 0)),
                  pl.BlockSpec((seq, kv_w), lambda b: (b, 0)),
                  pl.BlockSpec((seq, kv_w), lambda b: (b, 0))],
        out_specs=pl.BlockSpec((seq, q_w), lambda b: (b, 0)),
        out_shape=jax.ShapeDtypeStruct((n_batch * seq, q_w), BF16),
        compiler_params=_cparams(1),
        name="attn_ctx",
    )(sink, q, k, v)


def _attn_lat_kernel(sink_ref, q_ref, k_ref, v_ref, ck_ref, cv_ref, o_ref, *, g, seq):
    i = pl.program_id(1)
    n_loc = 3 * BLK
    start = pl.multiple_of(jnp.clip((i - 1) * BLK, 0, seq - n_loc), BLK)
    row = lax.broadcasted_iota(jnp.int32, (BLK, n_loc), 0)
    col = lax.broadcasted_iota(jnp.int32, (BLK, n_loc), 1)
    band = jnp.abs(row + (i * BLK - start) - col) <= WINDOW
    loc = lambda ref: (lambda p: ref[pl.ds(start, n_loc), p * LANES:(p + 1) * LANES])
    ctx = lambda ref: (lambda p: ref[:, p * LANES:(p + 1) * LANES])
    _attend_pairs(sink_ref, q_ref, [loc(k_ref), ctx(ck_ref)], [loc(v_ref), ctx(cv_ref)], band, o_ref, g=g)


def _attn_lat(sink, q, k, v, cache_k, cache_v, layer_slot, n_batch, seq, tok_off):
    qp_w, kv_w = q.shape[1], k.shape[1]
    q_w = qp_w // 2
    g = q_w // kv_w
    nb = seq // BLK
    past = cache_k.shape[2]
    assert seq >= 3 * BLK
    kv_spec = pl.BlockSpec((seq, kv_w), lambda b, i: (tok_off // seq + b, 0))
    c_spec = pl.BlockSpec((None, None, past, kv_w), lambda b, i: (b, layer_slot, 0, 0))
    return pl.pallas_call(
        functools.partial(_attn_lat_kernel, g=g, seq=seq),
        grid=(n_batch, nb),
        in_specs=[pl.BlockSpec(memory_space=pltpu.SMEM),
                  pl.BlockSpec((BLK, qp_w), lambda b, i: (tok_off // BLK + b * nb + i, 0)),
                  kv_spec, kv_spec, c_spec, c_spec],
        out_specs=pl.BlockSpec((BLK, q_w), lambda b, i: (b * nb + i, 0)),
        out_shape=jax.ShapeDtypeStruct((n_batch * seq, q_w), BF16),
        compiler_params=_cparams(2),
        name="attn_lat",
    )(sink, q, k, v, cache_k, cache_v)


def _post_ab_kernel(rows_ref, xc_ref, xl_ref, yap_ref, ybp_ref, yas_ref, ybs_ref, w_ref, gm_ref, lg_ref, lb_ref,
                    scf_ref, shf_ref, wr_ref, br_ref, x1_ref, hf_ref, route_ref, gate_ref, cnt_ref, carry_ref,
                    *, n_ctx_tiles, alpha):
    del rows_ref
    is_ctx = pl.program_id(0) < n_ctx_tiles
    a_w = yap_ref.shape[1]

    def proj_out(s):
        rs = s["rows"]
        pick = lambda p_ref, s_ref: pltpu.bitcast(
            jnp.where(is_ctx, pltpu.bitcast(p_ref[rs, :], jnp.uint32), pltpu.bitcast(s_ref[rs, :], jnp.uint32)), BF16)
        ya = pick(yap_ref, yas_ref)
        yb = pick(ybp_ref, ybs_ref)
        return dict(s, y=_dot(ya, w_ref[:a_w, :]) + _dot(yb, w_ref[a_w:, :]))

    def tail(s):
        rs = s["rows"]
        x1, hf, lg = _tail_rows(_group_rows(xc_ref, xl_ref, rs, n_ctx_tiles), s["y"], gm_ref[...], lg_ref[...],
                                lb_ref[...], scf_ref[...], shf_ref[...], wr_ref[...], br_ref[...], alpha)
        x1_ref[rs, :] = x1
        hf_ref[rs, :] = hf
        return dict(s, logits=lg)

    states = _run_skewed([proj_out, tail], [dict(rows=rs) for rs in _row_chains(xc_ref.shape[0], 4)])
    _tail_route(jnp.concatenate([s["logits"] for s in states], axis=1), route_ref, gate_ref, cnt_ref, carry_ref)


def _tail_out(t, d):
    shapes = [jax.ShapeDtypeStruct((t, d), F32), jax.ShapeDtypeStruct((t, d // 2), jnp.int32),
              jax.ShapeDtypeStruct((2 * TOP_K, t), jnp.int32), jax.ShapeDtypeStruct((t, LANES), F32),
              jax.ShapeDtypeStruct((N_EXPERTS, LANES), F32)]
    specs = [pl.BlockSpec((TM, d), lambda i, r: (i, 0)), pl.BlockSpec((TM, d // 2), lambda i, r: (i, 0)),
             pl.BlockSpec((2 * TOP_K, TM), lambda i, r: (0, i)), pl.BlockSpec((TM, LANES), lambda i, r: (i, 0)),
             pl.BlockSpec((N_EXPERTS, LANES), lambda i, r: (0, 0))]
    return shapes, specs


_TAIL_SCRATCH = [pltpu.VMEM((N_EXPERTS, 1), F32)]


def _post_ab(x, ya_p, yb_p, ya_s, yb_s, w_out, mod3, mod_base, rows, ln_g, ln_b, w_r, b_r, alpha):
    n_ctx_tiles = ya_p.shape[0] // TM
    (xc, xl), x_specs = _group_specs(x, n_ctx_tiles)
    d = xc.shape[1]
    nt = rows.shape[0]
    t = nt * TM
    a_w, q_w = ya_p.shape[1], yb_p.shape[1]
    mod_spec = lambda j: pl.BlockSpec((None, 1, d), lambda i, r: (mod_base + j * 16 + r[i], 0, 0))
    whole = lambda shp: pl.BlockSpec(shp, lambda i, r: (0,) * len(shp))
    ctx = lambda w: pl.BlockSpec((TM, w), lambda i, r: (jnp.minimum(i, n_ctx_tiles - 1), 0))
    lat = lambda w: pl.BlockSpec((TM, w), lambda i, r: (jnp.maximum(i - n_ctx_tiles, 0), 0))
    shapes, specs = _tail_out(t, d)
    grid_spec = pltpu.PrefetchScalarGridSpec(
        num_scalar_prefetch=1, grid=(nt,),
        in_specs=x_specs + [ctx(a_w), ctx(q_w), lat(a_w), lat(q_w),
                            whole(w_out.shape), mod_spec(2), whole((1, d)), whole((1, d)), mod_spec(4), mod_spec(3),
                            whole(w_r.shape), whole(b_r.shape)],
        out_specs=specs,
        scratch_shapes=_TAIL_SCRATCH,
    )
    return pl.pallas_call(
        functools.partial(_post_ab_kernel, n_ctx_tiles=n_ctx_tiles, alpha=alpha),
        grid_spec=grid_spec, out_shape=shapes, compiler_params=_cparams(1), name="post_ab",
    )(rows, xc, xl, ya_p, yb_p, ya_s, yb_s, w_out, mod3, ln_g, ln_b, mod3, mod3, w_r, b_r)


def _gmlp_kernel(rows_ref, x_ref, scm_ref, shm_ref, win_ref, bin_ref, gv_ref, bv_ref, wsp_ref, bsp_ref, wout_ref,
                 gm_ref, lg_ref, lb_ref, scf_ref, shf_ref, wr_ref, br_ref, x1_ref, hf_ref, route_ref, gate_ref,
                 cnt_ref, carry_ref, *, alpha):
    del rows_ref
    assert (x_ref.shape[0] // 2) % CHUNK == 0
    c_w = win_ref.shape[1] // 2
    gd = c_w // C_GROUPS

    def norm_in(s):
        x = x_ref[s["rows"], :]
        return dict(s, x=x, h=(_ln(x) * (1.0 + scm_ref[...]) + shm_ref[...]).astype(BF16))

    def proj_in(s):
        return dict(s, z=_dot(s["h"], win_ref[...]) + bin_ref[...])

    def gate_split(s):
        z = s["z"]
        z = 0.5 * z * (1.0 + lax.erf(z * (2.0 ** -0.5)))
        return dict(s, u=z[:, :c_w], v=(_ln(z[:, c_w:]) * gv_ref[...] + bv_ref[...]).astype(BF16))

    def spatial(s):
        v = s["v"]
        chunks = []
        for n in range(v.shape[0] // CHUNK):
            groups = [_dot(wsp_ref[g], v[n * CHUNK:(n + 1) * CHUNK, g * gd:(g + 1) * gd]) + bsp_ref[g]
                      for g in range(C_GROUPS)]
            chunks.append(jnp.concatenate(groups, axis=1))
        return dict(s, t=(s["u"] * jnp.concatenate(chunks, axis=0)).astype(BF16))

    def proj_out(s):
        return dict(s, y=_dot(s["t"], wout_ref[...]))

    def tail(s):
        x1, hf, lg = _tail_rows(s["x"], s["y"], gm_ref[...], lg_ref[...], lb_ref[...], scf_ref[...], shf_ref[...],
                                wr_ref[...], br_ref[...], alpha)
        x1_ref[s["rows"], :] = x1
        hf_ref[s["rows"], :] = hf
        return dict(s, logits=lg)

    states = _run_skewed([norm_in, proj_in, gate_split, spatial, proj_out, tail],
                         [dict(rows=rs) for rs in _row_chains(x_ref.shape[0])])
    _tail_route(jnp.concatenate([s["logits"] for s in states], axis=1), route_ref, gate_ref, cnt_ref, carry_ref)


def _gmlp(x, mod3, mod_base, rows, w_in, b_in, g_v, b_v, w_sp, b_sp, w_out, ln_g, ln_b, w_r, b_r, alpha):
    t, d = x.shape
    nt = t // TM
    mod_spec = lambda j: pl.BlockSpec((None, 1, d), lambda i, r: (mod_base + j * 16 + r[i], 0, 0))
    whole = lambda shp: pl.BlockSpec(shp, lambda i, r: (0,) * len(shp))
    shapes, specs = _tail_out(t, d)
    grid_spec = pltpu.PrefetchScalarGridSpec(
        num_scalar_prefetch=1, grid=(nt,),
        in_specs=[pl.BlockSpec((TM, d), lambda i, r: (i, 0)), mod_spec(1), mod_spec(0),
                  whole(w_in.shape), whole(b_in.shape), whole(g_v.shape), whole(b_v.shape),
                  whole(w_sp.shape), whole(b_sp.shape), whole(w_out.shape),
                  mod_spec(2), whole((1, d)), whole((1, d)), mod_spec(4), mod_spec(3),
                  whole(w_r.shape), whole(b_r.shape)],
        out_specs=specs,
        scratch_shapes=_TAIL_SCRATCH,
    )
    return pl.pallas_call(
        functools.partial(_gmlp_kernel, alpha=alpha),
        grid_spec=grid_spec, out_shape=shapes, compiler_params=_cparams(1), name="gmlp",
    )(rows, x, mod3, mod3, w_in, b_in, g_v, b_v, w_sp, b_sp, w_out, mod3, ln_g, ln_b, mod3, mod3, w_r, b_r)


def _moe_kernel(be_ref, bf_ref, nx_ref, sl_ref, hf_ref, nu_ref, x_ref, wgu_hbm, bgu_ref, wdn_hbm, bdn_ref, o_ref,
                wgu_f, wdn_f, wgu_s, wdn_s, sem, *, layer):
    i = pl.program_id(0)
    d_ff = wdn_s.shape[0]

    def weight_copies(e, slot):
        return (pltpu.make_async_copy(wgu_hbm.at[layer, e], wgu_f.at[slot], sem.at[0, slot]),
                pltpu.make_async_copy(wdn_hbm.at[layer, e], wdn_f.at[slot], sem.at[1, slot]))

    @pl.when(i < nu_ref[0])
    def _():
        @pl.when(bf_ref[i] == 1)
        def _():
            slot = sl_ref[i]

            @pl.when(i == 0)
            def _():
                for c in weight_copies(be_ref[i], slot):
                    c.start()

            for c in weight_copies(be_ref[i], slot):
                c.wait()
            wgu_s[...] = wgu_f[slot].astype(BF16)
            wdn_s[...] = wdn_f[slot].astype(BF16)

            @pl.when(nx_ref[i] >= 0)
            def _():
                for c in weight_copies(nx_ref[i], 1 - slot):
                    c.start()

        def up(s):
            return dict(s, gu=_dot(_unpack_bf16_pairs(x_ref[s["rows"], :]), wgu_s[...]) + bgu_ref[...])

        def act(s):
            gu = s["gu"]
            gate = jnp.minimum(gu[:, :d_ff], SWIGLU_LIMIT)
            lin = jnp.clip(gu[:, d_ff:], -SWIGLU_LIMIT, SWIGLU_LIMIT)
            glu = gate * jax.nn.sigmoid(SWIGLU_ALPHA * gate)
            return dict(s, gu=None, hid=((lin + 1.0) * glu).astype(BF16))

        def down(s):
            o_ref[s["rows"], :] = _pack_bf16_pairs(_dot(s["hid"], wdn_s[...]) + bdn_ref[...])
            return dict(s, hid=None)

        def run(n_chains, n_rows):
            _run_skewed([up, act, down], [dict(rows=rs) for rs in _row_chains(n_rows, n_chains)])

        @pl.when(hf_ref[i] == 0)
        def _():
            run(2, TM_MOE)

        @pl.when(hf_ref[i] == 1)
        def _():
            run(1, TM_MOE // 2)


def _moe(xs, plan, w_gu, b_gu, w_dn, b_dn, layer):
    n_rows = xs.shape[0]
    depth, n_e, d, ff2 = w_gu.shape
    d_ff = ff2 // 2
    nt = n_rows // TM_MOE
    n_plan = len(plan)
    row_spec = pl.BlockSpec((TM_MOE, d // 2), lambda i, *p: (jnp.minimum(i, p[-1][0] - 1), 0))
    bias_spec = lambda w: pl.BlockSpec((None, None, 1, w), lambda i, *p: (layer, p[0][i], 0, 0))
    grid_spec = pltpu.PrefetchScalarGridSpec(
        num_scalar_prefetch=n_plan, grid=(nt,),
        in_specs=[row_spec, pl.BlockSpec(memory_space=pl.ANY), bias_spec(ff2),
                  pl.BlockSpec(memory_space=pl.ANY), bias_spec(d)],
        out_specs=row_spec,
        scratch_shapes=[pltpu.VMEM((2, d, ff2), F32), pltpu.VMEM((2, d_ff, d), F32),
                        pltpu.VMEM((d, ff2), BF16), pltpu.VMEM((d_ff, d), BF16),
                        pltpu.SemaphoreType.DMA((2, 2))],
    )
    return pl.pallas_call(
        functools.partial(_moe_kernel, layer=layer), grid_spec=grid_spec,
        out_shape=jax.ShapeDtypeStruct((n_rows, d // 2), jnp.int32),
        compiler_params=_cparams(1), name="moe_experts",
    )(*plan, xs, w_gu, b_gu.reshape(depth, n_e, 1, ff2), w_dn, b_dn.reshape(depth, n_e, 1, d))


def _unpack_f32_pairs(p):
    u = pltpu.bitcast(p, jnp.uint32)
    return jnp.concatenate([pltpu.bitcast(u << 16, F32), pltpu.bitcast(u & jnp.uint32(0xFFFF0000), F32)], axis=1)


def _combine_kernel(rows_ref, x_ref, y0_ref, y1_ref, y2_ref, y3_ref, gate_ref, gf_ref, lg_ref, lb_ref, *o_refs,
                    alpha, n_ctx_tiles):
    del rows_ref
    gate = gate_ref[...]
    y = gate[:, 0:1] * _unpack_f32_pairs(y0_ref[...])
    for k, y_ref in ((1, y1_ref), (2, y2_ref), (3, y3_ref)):
        y = y + gate[:, k:k + 1] * _unpack_f32_pairs(y_ref[...])
    out = _ln(alpha * x_ref[...] + gf_ref[...] * y) * lg_ref[...] + lb_ref[...]
    if n_ctx_tiles is None:
        o_refs[0][...] = out
    else:
        @pl.when(pl.program_id(0) < n_ctx_tiles)
        def _():
            o_refs[0][...] = out

        @pl.when(pl.program_id(0) >= n_ctx_tiles)
        def _():
            o_refs[1][...] = out


def _combine(x1, y4, gate, mod3, mod_base, rows, ln_g, ln_b, alpha, t_ctx=None):
    t, d = x1.shape
    nt = rows.shape[0]
    tm = t // nt
    y_spec = lambda k: pl.BlockSpec((None, tm, d // 2), lambda i, r: (k, i, 0))
    if t_ctx is None:
        n_ctx_tiles = None
        out_specs = pl.BlockSpec((tm, d), lambda i, r: (i, 0))
        out_shape = jax.ShapeDtypeStruct((t, d), F32)
    else:
        n_ctx_tiles = t_ctx // tm
        out_specs = [pl.BlockSpec((tm, d), lambda i, r: (jnp.minimum(i, n_ctx_tiles - 1), 0)),
                     pl.BlockSpec((tm, d), lambda i, r: (jnp.maximum(i - n_ctx_tiles, 0), 0))]
        out_shape = [jax.ShapeDtypeStruct((t_ctx, d), F32), jax.ShapeDtypeStruct((t - t_ctx, d), F32)]
    grid_spec = pltpu.PrefetchScalarGridSpec(
        num_scalar_prefetch=1, grid=(nt,),
        in_specs=[pl.BlockSpec((tm, d), lambda i, r: (i, 0)), y_spec(0), y_spec(1), y_spec(2), y_spec(3),
                  pl.BlockSpec((tm, LANES), lambda i, r: (i, 0)),
                  pl.BlockSpec((None, 1, d), lambda i, r: (mod_base + 5 * 16 + r[i], 0, 0)),
                  pl.BlockSpec((1, d), lambda i, r: (0, 0)), pl.BlockSpec((1, d), lambda i, r: (0, 0))],
        out_specs=out_specs,
    )
    return pl.pallas_call(
        functools.partial(_combine_kernel, alpha=alpha, n_ctx_tiles=n_ctx_tiles), grid_spec=grid_spec,
        out_shape=out_shape, compiler_params=_cparams(1), name="moe_combine",
    )(rows, x1, y4, y4, y4, y4, gate, mod3, ln_g, ln_b)


def _route_plan(route, counts, n_tiles):
    idx, rank = route[:TOP_K], route[TOP_K:]
    padded = (counts + TM_MOE - 1) // TM_MOE * TM_MOE
    pend = jnp.cumsum(padded)
    pstart = pend - padded
    onehot = idx[:, :, None] == jnp.arange(N_EXPERTS, dtype=jnp.int32)[None, None, :]
    dest = jnp.sum(jnp.where(onehot, pstart[None, None, :], 0), axis=-1) + rank
    n_used = (pend[-1] // TM_MOE).astype(jnp.int32)
    tile_start = jnp.arange(n_tiles, dtype=jnp.int32) * TM_MOE
    blk_e = jnp.sum((tile_start[:, None] >= pend[None, :]).astype(jnp.int32), axis=1)
    last_e = jnp.sum((jnp.maximum(n_used - 1, 0) * TM_MOE >= pend).astype(jnp.int32))
    blk_e = jnp.minimum(jnp.where(jnp.arange(n_tiles) < n_used, blk_e, last_e), N_EXPERTS - 1).astype(jnp.int32)
    blk_first = jnp.concatenate([jnp.ones((1,), jnp.int32), (blk_e[1:] != blk_e[:-1]).astype(jnp.int32)])
    ar = jnp.arange(N_EXPERTS, dtype=jnp.int32)
    later = jnp.logical_and((counts > 0)[None, :], ar[None, :] > ar[:, None])
    next_e = jnp.min(jnp.where(later, ar[None, :], N_EXPERTS), axis=1)
    next_e = jnp.where(next_e == N_EXPERTS, -1, next_e)
    blk_next = jnp.sum(jnp.where(blk_e[:, None] == ar[None, :], next_e[None, :], 0), axis=1).astype(jnp.int32)
    blk_slot = ((jnp.cumsum(blk_first) - 1) % 2).astype(jnp.int32)
    left = jnp.sum(jnp.where(blk_e[:, None] == ar[None, :], (pstart + counts)[None, :], 0), axis=1) - tile_start
    blk_half = (left <= TM_MOE // 2).astype(jnp.int32)
    return dest, (blk_e, blk_first, blk_next, blk_slot, blk_half, n_used.reshape(1))


SC_CORES = 2
SC_SUBCORES = 16


def _sc_mesh():
    return plsc.VectorSubcoreMesh(core_axis_name="core", subcore_axis_name="subcore")


def _sc_scatter_rows(src, dest, n_rows, window):
    n_slots, t = dest.shape
    w = src.shape[1]
    per_worker = t // (SC_CORES * SC_SUBCORES)
    assert per_worker * SC_CORES * SC_SUBCORES == t and per_worker % window == 0

    @functools.partial(pl.kernel, out_type=jax.ShapeDtypeStruct((n_rows, w), src.dtype), mesh=_sc_mesh(),
                       scratch_types=[pltpu.VMEM((n_slots, window), jnp.int32), pltpu.VMEM((window, w), src.dtype)],
                       name="sc_dispatch")
    def scatter(src_hbm, dest_hbm, out_hbm, idx_v, rows_v):
        worker = lax.axis_index("subcore") * SC_CORES + lax.axis_index("core")

        @pl.loop(0, per_worker // window)
        def _(c):
            base = pl.multiple_of(worker * per_worker + c * window, window)
            pltpu.sync_copy(src_hbm.at[pl.ds(base, window)], rows_v)
            pltpu.sync_copy(dest_hbm.at[:, pl.ds(base, window)], idx_v)
            for k in range(n_slots):
                pltpu.sync_copy(rows_v, out_hbm.at[idx_v.at[k]])

    return scatter(src, dest)


def _sc_gather_rows(table, idx, window):
    n = idx.shape[0]
    w = table.shape[1]
    per_worker = n // (SC_CORES * SC_SUBCORES)
    assert per_worker * SC_CORES * SC_SUBCORES == n and per_worker % window == 0

    @functools.partial(pl.kernel, out_type=jax.ShapeDtypeStruct((n, w), table.dtype), mesh=_sc_mesh(),
                       scratch_types=[pltpu.VMEM((window,), jnp.int32), pltpu.VMEM((window, w), table.dtype)],
                       name="sc_collect")
    def gather(table_hbm, idx_hbm, out_hbm, idx_v, rows_v):
        worker = lax.axis_index("subcore") * SC_CORES + lax.axis_index("core")

        @pl.loop(0, per_worker // window)
        def _(c):
            base = pl.multiple_of(worker * per_worker + c * window, window)
            pltpu.sync_copy(idx_hbm.at[pl.ds(base, window)], idx_v)
            pltpu.sync_copy(table_hbm.at[idx_v], rows_v)
            pltpu.sync_copy(rows_v, out_hbm.at[pl.ds(base, window)])

    return gather(table, idx)


def _dft_tables(n):
    k = jnp.arange(n, dtype=jnp.int32)

    def trig(rows):
        ang = ((rows[:, None] * k[None, :]) % n).astype(F32) * (2.0 * math.pi / n)
        return jnp.cos(ang), jnp.sin(ang)

    n2 = 64
    if n < 4 * n2:
        return trig(k)
    ca, sa = (z[:, None, :] for z in trig(jnp.arange(n // n2, dtype=jnp.int32) * n2))
    cb, sb = (z[None, :, :] for z in trig(jnp.arange(n2, dtype=jnp.int32)))
    return (ca * cb - sa * sb).reshape(n, n), (sa * cb + ca * sb).reshape(n, n)


def _rope_tables(n_lat, n_ctx_rows):
    half, quarter = HEAD_DIM // 2, HEAD_DIM // 4
    tpos = jnp.arange(n_lat, dtype=jnp.int32)
    lane = jnp.arange(LANES, dtype=jnp.int32) % HEAD_DIM
    pos = jnp.where(lane[None, :] < half, (tpos // GRID_W)[:, None], (tpos % GRID_W)[:, None]).astype(F32)
    fidx = (lane % quarter).astype(F32)
    freqs = ROPE_THETA ** (-fidx / quarter)
    ang = pos * freqs[None, :]
    cos, sin = jnp.cos(ang), jnp.sin(ang)
    first = (lane % half) < quarter
    sa = jnp.where(first[None, :], -sin, 0.0)
    sb = jnp.where(first[None, :], 0.0, sin)
    ident = lambda v: jnp.full((n_ctx_rows, LANES), v, F32)
    return (jnp.concatenate([ident(1.0), cos]), jnp.concatenate([ident(0.0), sa]), jnp.concatenate([ident(0.0), sb]))


def kernel(x_prompt, x_sample, cache_k_ab, cache_v_ab, c, c_ctx, w_ada, b_ada, ln_mix_g, ln_mix_b, ln_ffn_g, ln_ffn_b, w_in_ab, w_out_ab, sink_ab, w_in_c, b_in_c, ln_v_g, ln_v_b, w_sp, b_sp, w_out_c, w_router, b_router, w_gu, b_gu, w_dn, b_dn):
    n_ctx_b, ctx_seq, d = x_prompt.shape
    n_lat_b, lat_seq, _ = x_sample.shape
    depth = w_ada.shape[0]
    t_ctx, t_lat = n_ctx_b * ctx_seq, n_lat_b * lat_seq
    t = t_ctx + t_lat
    alpha = (2 * depth) ** 0.25
    kv_w = N_KV_HEADS * HEAD_DIM
    a_w = d // 4
    q_w = d - a_w
    assert t_ctx % TM == 0 and lat_seq % TM == 0 and t_ctx % lat_seq == 0 and n_lat_b + 1 <= 16

    cond_rows = lambda tm: jnp.asarray(np.concatenate(
        [np.zeros(t_ctx // tm, np.int32), 1 + np.arange(t_lat // tm, dtype=np.int32) // (lat_seq // tm)]))
    rows, rows_c = cond_rows(TM), cond_rows(TM_COMBINE)
    rblk = jnp.asarray(np.concatenate(
        [np.zeros(t_ctx // TM, np.int32), 1 + np.arange(t_lat // TM, dtype=np.int32) % (lat_seq // TM)]))

    cond16 = jnp.zeros((16, d), F32).at[0].set(c_ctx).at[1:1 + n_lat_b].set(c)
    mod = _ada(cond16, w_ada, b_ada)
    mod3 = mod.reshape(depth, 16, 6, d).transpose(0, 2, 1, 3).reshape(depth * 6 * 16, 1, d)

    cos, sa, sb = _rope_tables(lat_seq, TM)
    cs_ctx, ss_ctx = (z.astype(BF16) for z in _dft_tables(ctx_seq))
    cs_lat, ss_lat = (z.astype(BF16) for z in _dft_tables(lat_seq))
    gd = a_w // A_GROUPS
    cd, sd = _dft_tables(gd)
    eye = jnp.eye(A_GROUPS, dtype=F32)
    bdc, bds = jnp.kron(eye, cd).astype(BF16), jnp.kron(eye, sd).astype(BF16)

    past = cache_k_ab.shape[2]
    cache_k = cache_k_ab.reshape(n_lat_b, -1, past, kv_w).astype(BF16)
    cache_v = cache_v_ab.reshape(n_lat_b, -1, past, kv_w).astype(BF16)

    x = (x_prompt.reshape(t_ctx, d), x_sample.reshape(t_lat, d))
    row2 = lambda v: v.reshape(1, -1)
    ks, vs = [], []
    for l in range(depth):
        j = l // 2
        mod_base = l * 6 * 16
        w_r, b_r = w_router[l].T, b_router[l].reshape(-1, 1)
        if l % 2 == 0:
            ac, as_, q, k32, v32, k, v = _inproj(x, t_ctx // TM, mod3, mod_base, rows, rblk, w_in_ab[j].astype(BF16),
                                                 cos, sa, sb, bdc, bds, a_w, q_w, kv_w)
            ks.append(k32[:t_ctx].reshape(n_ctx_b, ctx_seq, N_KV_HEADS, HEAD_DIM))
            vs.append(v32[:t_ctx].reshape(n_ctx_b, ctx_seq, N_KV_HEADS, HEAD_DIM))
            ya_p = _fourier(ac, as_, cs_ctx, ss_ctx, n_ctx_b, ctx_seq, 0, min(ctx_seq, 512))
            ya_s = _fourier(ac, as_, cs_lat, ss_lat, n_lat_b, lat_seq, t_ctx // lat_seq, min(lat_seq, 512))
            yb_p = _attn_ctx(sink_ab[j], q, k, v, n_ctx_b, ctx_seq)
            yb_s = _attn_lat(sink_ab[j], q, k, v, cache_k, cache_v, j, n_lat_b, lat_seq, t_ctx)
            x1, hf, route, gate, cnt = _post_ab(x, ya_p, yb_p, ya_s, yb_s, w_out_ab[j].astype(BF16), mod3, mod_base,
                                                rows, row2(ln_mix_g[l]), row2(ln_mix_b[l]), w_r, b_r, alpha)
        else:
            x1, hf, route, gate, cnt = _gmlp(x, mod3, mod_base, rows, w_in_c[j].astype(BF16), row2(b_in_c[j]),
                                             row2(ln_v_g[j]), row2(ln_v_b[j]), w_sp[j].astype(BF16),
                                             b_sp[j][:, :, None], w_out_c[j].astype(BF16), row2(ln_mix_g[l]),
                                             row2(ln_mix_b[l]), w_r, b_r, alpha)
        n_rows = t * TOP_K + N_EXPERTS * TM_MOE
        dest, plan = _route_plan(route, cnt[:, 0].astype(jnp.int32), n_rows // TM_MOE)
        xs = _sc_scatter_rows(hf, dest, n_rows, 128)
        out_sorted = _moe(xs, plan, w_gu, b_gu, w_dn, b_dn, l)
        y4 = _sc_gather_rows(out_sorted, dest.reshape(-1), 128).reshape(TOP_K, t, d // 2)
        x = _combine(x1, y4, gate, mod3, mod_base, rows_c, row2(ln_ffn_g[l]), row2(ln_ffn_b[l]), alpha,
                     t_ctx if l == depth - 1 else None)

    y_prompt = x[0].reshape(n_ctx_b, ctx_seq, d)
    y_sample = x[1].reshape(n_lat_b, lat_seq, d)
    return (y_prompt, y_sample, jnp.stack(ks, axis=1), jnp.stack(vs, axis=1))
```

```python
import functools
import math

import numpy as np
import jax
import jax.numpy as jnp
from jax import lax
from jax.experimental import pallas as pl
from jax.experimental.pallas import tpu as pltpu
from jax.experimental.pallas import tpu_sc as plsc

GRID_W = 64
BLK = 128
WINDOW = 128
HEAD_DIM = 64
A_GROUPS = 4
N_KV_HEADS = 4
C_GROUPS = 8
CHUNK = 128
N_EXPERTS = 32
TOP_K = 4
SWIGLU_LIMIT = 7.0
SWIGLU_ALPHA = 1.702
ROPE_THETA = 10000.0
LN_EPS = 1e-6
NEG_INF = -1e30
LOG2_E = math.log2(math.e)

LANES = 128
TM = 1024
TM_COMBINE = 512
TM_MOE = 512
VMEM_LIMIT = 52 * 1024 * 1024

F32 = jnp.float32
BF16 = jnp.bfloat16


def _cparams(n_axes):
    return pltpu.CompilerParams(dimension_semantics=("arbitrary",) * n_axes,
                                vmem_limit_bytes=VMEM_LIMIT)


def _ln(x):
    mu = jnp.mean(x, axis=-1, keepdims=True)
    xc = x - mu
    var = jnp.mean(xc * xc, axis=-1, keepdims=True)
    return xc * lax.rsqrt(var + LN_EPS)


def _dot(a, b):
    return jnp.dot(a, b, preferred_element_type=F32)


def _dot_nt(a, b):
    return lax.dot_general(a, b, (((1,), (1,)), ((), ())), preferred_element_type=F32)


def _split(a):
    hi = a.astype(BF16)
    lo = (a - hi.astype(F32)).astype(BF16)
    return hi, lo


def _dot_3pass(a, b):
    a_hi, a_lo = _split(a)
    b_hi, b_lo = _split(b)
    return _dot(a_hi, b_hi) + (_dot(a_hi, b_lo) + _dot(a_lo, b_hi))


def _lane_select(cols, width):
    m = cols[0].shape[0]
    lane = lax.broadcasted_iota(jnp.int32, (m, width), 1)
    out = jnp.zeros((m, width), cols[0].dtype)
    for j, c in enumerate(cols):
        out = jnp.where(lane == j, c, out)
    return out


def _pack_bf16_pairs(v):
    n = v.shape[1] // 2
    lo = pltpu.bitcast(v[:, :n].astype(BF16).astype(F32), jnp.uint32) >> 16
    hi = pltpu.bitcast(v[:, n:].astype(BF16).astype(F32), jnp.uint32) & jnp.uint32(0xFFFF0000)
    return pltpu.bitcast(lo | hi, jnp.int32)


def _unpack_bf16_pairs(p):
    u = pltpu.bitcast(p, jnp.uint32)
    lo = pltpu.bitcast(u << 16, F32)
    hi = pltpu.bitcast(u & jnp.uint32(0xFFFF0000), F32)
    return jnp.concatenate([lo, hi], axis=1).astype(BF16)


def _row_chains(n_rows, n_chains=2):
    step = n_rows // n_chains
    return [slice(c * step, (c + 1) * step) for c in range(n_chains)]


def _run_skewed(stages, states):
    states = list(states)
    for step in range(len(stages) + len(states) - 1):
        for c in range(len(states)):
            if 0 <= step - c < len(stages):
                states[c] = stages[step - c](states[c])
    return states


def _tail_rows(x, y, g_m, ln_g, ln_b, sc_f, sh_f, w_r, b_r, alpha):
    x1 = _ln(alpha * x + g_m * y) * ln_g + ln_b
    hf = _ln(x1) * (1.0 + sc_f) + sh_f
    hf_hi, hf_lo = _split(hf)
    w_hi, w_lo = _split(w_r)
    logits = _dot_nt(w_hi, hf_hi) + (_dot_nt(w_hi, hf_lo) + _dot_nt(w_lo, hf_hi)) + b_r
    return x1, _pack_bf16_pairs(hf), logits


def _tail_route(logits, route_ref, gate_ref, cnt_ref, carry_ref):
    @pl.when(pl.program_id(0) == 0)
    def _():
        carry_ref[...] = jnp.zeros_like(carry_ref)

    tm = logits.shape[1]
    sub = lax.broadcasted_iota(jnp.int32, logits.shape, 0)
    vals = logits
    top_v, top_i = [], []
    for _ in range(TOP_K):
        m = jnp.max(vals, axis=0, keepdims=True)
        am = jnp.min(jnp.where(vals == m, sub, N_EXPERTS), axis=0, keepdims=True)
        top_v.append(m)
        top_i.append(am)
        vals = jnp.where(sub == am, -jnp.inf, vals)
    e = [jnp.exp(v - top_v[0]) for v in top_v]
    denom = e[0] + e[1] + e[2] + e[3]
    gates_t = jnp.concatenate([ek / denom for ek in e] + [jnp.zeros((LANES - TOP_K, tm), F32)], axis=0)
    gate_ref[...] = gates_t.T

    member = jnp.zeros(logits.shape, F32)
    for am in top_i:
        member = jnp.where(sub == am, 1.0, member)
    r_i = lax.broadcasted_iota(jnp.int32, (tm, tm), 0)
    c_i = lax.broadcasted_iota(jnp.int32, (tm, tm), 1)
    earlier = jnp.where(r_i < c_i, 1.0, 0.0).astype(BF16)
    before = _dot(member.astype(BF16), earlier) + carry_ref[...]
    ranks = [jnp.sum(jnp.where(sub == am, before, 0.0), axis=0, keepdims=True).astype(jnp.int32) for am in top_i]
    route_ref[...] = jnp.concatenate(top_i + ranks, axis=0)
    carry = carry_ref[...] + jnp.sum(member, axis=1, keepdims=True)
    carry_ref[...] = carry
    cnt_ref[...] = jnp.broadcast_to(carry, cnt_ref.shape)


def _ada_kernel(cond_ref, w_ref, b_ref, o_ref):
    c = cond_ref[...]
    s = (c * jax.nn.sigmoid(c)).astype(BF16)
    o_ref[...] = _dot(s, w_ref[...].astype(BF16)) + b_ref[...]


def _ada(cond16, w_ada, b_ada):
    depth, d, n = w_ada.shape
    tn = 1536
    return pl.pallas_call(
        _ada_kernel,
        grid=(depth, n // tn),
        in_specs=[
            pl.BlockSpec((16, d), lambda l, j: (0, 0)),
            pl.BlockSpec((None, d, tn), lambda l, j: (l, 0, j)),
            pl.BlockSpec((None, 1, tn), lambda l, j: (l, 0, j)),
        ],
        out_specs=pl.BlockSpec((None, 16, tn), lambda l, j: (l, 0, j)),
        out_shape=jax.ShapeDtypeStruct((depth, 16, n), F32),
        compiler_params=_cparams(2),
        name="ada_mod",
    )(cond16, w_ada, b_ada.reshape(depth, 1, n))


def _group_rows(xc_ref, xl_ref, rows, n_ctx_tiles):
    return jnp.where(pl.program_id(0) < n_ctx_tiles, xc_ref[rows, :], xl_ref[rows, :])


def _group_specs(x, n_ctx_tiles):
    if isinstance(x, tuple):
        arrays, lat = x, (lambda i, *_: (jnp.maximum(i - n_ctx_tiles, 0), 0))
    else:
        arrays, lat = (x, x), (lambda i, *_: (jnp.maximum(i, n_ctx_tiles), 0))
    d = arrays[0].shape[1]
    ctx = lambda i, *_: (jnp.minimum(i, n_ctx_tiles - 1), 0)
    return arrays, [pl.BlockSpec((TM, d), ctx), pl.BlockSpec((TM, d), lat)]


def _inproj_kernel(rows_ref, rblk_ref, xc_ref, xl_ref, sc_ref, sh_ref, w_ref, cos_ref, sa_ref, sb_ref, bdc_ref,
                   bds_ref, ac_ref, as_ref, q_ref, k_ref, v_ref, kb_ref, vb_ref, *, a_w, q_w, kv_w, n_ctx_tiles):
    del rows_ref, rblk_ref
    g = q_w // kv_w

    def norm_in(s):
        x = _group_rows(xc_ref, xl_ref, s["rows"], n_ctx_tiles)
        return dict(s, h=(_ln(x) * (1.0 + sc_ref[...]) + sh_ref[...]).astype(BF16))

    def proj(s):
        return dict(s, p=_dot(s["h"], w_ref[...]))

    def finish(s):
        rs, p = s["rows"], s["p"]
        a = p[:, :a_w].astype(BF16)
        ac_ref[rs, :] = _dot(a, bdc_ref[...]).astype(BF16)
        as_ref[rs, :] = _dot(a, bds_ref[...]).astype(BF16)
        cos, sa, sb = cos_ref[rs, :], sa_ref[rs, :], sb_ref[rs, :]

        def rope(t):
            w = t.shape[1]
            reps = w // LANES
            c, a_, b_ = (jnp.tile(z, (1, reps)) for z in (cos, sa, sb))
            nxt = pltpu.roll(t, w - HEAD_DIM // 4, 1)
            prv = pltpu.roll(t, HEAD_DIM // 4, 1)
            return t * c + nxt * a_ + prv * b_

        q = rope(p[:, a_w:a_w + q_w]) * (HEAD_DIM ** -0.5 * LOG2_E)
        lane = lax.broadcasted_iota(jnp.int32, (q.shape[0], LANES), 1)
        for j in range(q_w // HEAD_DIM):
            tile = q[:, (j // 2) * LANES:(j // 2 + 1) * LANES]
            dst_low = (j // g) % 2 == 0
            if (j % 2 == 0) != dst_low:
                tile = pltpu.roll(tile, HEAD_DIM, 1)
            keep = (lane < HEAD_DIM) if dst_low else (lane >= HEAD_DIM)
            q_ref[rs, j * LANES:(j + 1) * LANES] = jnp.where(keep, tile, 0.0).astype(BF16)
        k = rope(p[:, a_w + q_w:a_w + q_w + kv_w])
        v = p[:, a_w + q_w + kv_w:]
        k_ref[rs, :] = k
        v_ref[rs, :] = v
        kb_ref[rs, :] = k.astype(BF16)
        vb_ref[rs, :] = v.astype(BF16)
        return s

    _run_skewed([norm_in, proj, finish], [dict(rows=rs) for rs in _row_chains(xc_ref.shape[0])])


def _inproj(x, n_ctx_tiles, mod3, mod_base, rows, rblk, w_in, cos, sa, sb, bdc, bds, a_w, q_w, kv_w):
    (xc, xl), x_specs = _group_specs(x, n_ctx_tiles)
    d = xc.shape[1]
    t = rows.shape[0] * TM
    n = w_in.shape[1]
    nt = t // TM
    mod_spec = lambda j: pl.BlockSpec((None, 1, d), lambda i, r, rb: (mod_base + j * 16 + r[i], 0, 0))
    whole = lambda shp: pl.BlockSpec(shp, lambda i, r, rb: (0,) * len(shp))
    rope_spec = pl.BlockSpec((TM, LANES), lambda i, r, rb: (rb[i], 0))
    tok = lambda w: pl.BlockSpec((TM, w), lambda i, r, rb: (i, 0))
    grid_spec = pltpu.PrefetchScalarGridSpec(
        num_scalar_prefetch=2, grid=(nt,),
        in_specs=x_specs + [mod_spec(1), mod_spec(0), whole((d, n)), rope_spec, rope_spec, rope_spec,
                            whole((a_w, a_w)), whole((a_w, a_w))],
        out_specs=[tok(a_w), tok(a_w), tok(2 * q_w), tok(kv_w), tok(kv_w), tok(kv_w), tok(kv_w)],
    )
    return pl.pallas_call(
        functools.partial(_inproj_kernel, a_w=a_w, q_w=q_w, kv_w=kv_w, n_ctx_tiles=n_ctx_tiles),
        grid_spec=grid_spec,
        out_shape=[jax.ShapeDtypeStruct((t, a_w), BF16), jax.ShapeDtypeStruct((t, a_w), BF16),
                   jax.ShapeDtypeStruct((t, 2 * q_w), BF16), jax.ShapeDtypeStruct((t, kv_w), F32),
                   jax.ShapeDtypeStruct((t, kv_w), F32), jax.ShapeDtypeStruct((t, kv_w), BF16),
                   jax.ShapeDtypeStruct((t, kv_w), BF16)],
        compiler_params=_cparams(1),
        name="inproj_ab",
    )(rows, rblk, xc, xl, mod3, mod3, w_in, cos, sa, sb, bdc, bds)


def _fourier_kernel(c_ref, s_ref, ac_ref, as_ref, o_ref, *, scale):
    y = _dot(c_ref[...], ac_ref[...]) - _dot(s_ref[...], as_ref[...])
    o_ref[...] = (y * scale).astype(BF16)


def _fourier(ac, as_, cs, ss, n_batch, seq, blk_off, tm):
    a_w = ac.shape[1]
    nr = seq // tm
    a_spec = pl.BlockSpec((seq, a_w), lambda r, b: (blk_off + b, 0))
    t_spec = pl.BlockSpec((tm, seq), lambda r, b: (r, 0))
    return pl.pallas_call(
        functools.partial(_fourier_kernel, scale=(seq * (a_w // A_GROUPS)) ** -0.5),
        grid=(nr, n_batch),
        in_specs=[t_spec, t_spec, a_spec, a_spec],
        out_specs=pl.BlockSpec((tm, a_w), lambda r, b: (b * nr + r, 0)),
        out_shape=jax.ShapeDtypeStruct((n_batch * seq, a_w), BF16),
        compiler_params=_cparams(2),
        name="fourier_%d" % seq,
    )(cs, ss, ac, as_)


def _attend_pairs(sink_ref, q_ref, keys, values, mask, o_ref, *, g):
    rows = q_ref.shape[0]
    lane = lax.broadcasted_iota(jnp.int32, (rows, LANES), 1)
    full_mask = None if mask is None else jnp.concatenate([mask] * g, axis=0)

    def scores(s):
        h0, p = s["h0"], s["h0"] // (2 * g)
        qp = jnp.concatenate([q_ref[:, (h0 + j) * LANES:(h0 + j + 1) * LANES] for j in range(g)], axis=0)
        sc = [_dot_nt(qp, kf(p)) for kf in keys]
        if full_mask is not None:
            sc[0] = jnp.where(full_mask, sc[0], NEG_INF)
        return dict(s, sc=sc)

    def row_max(s):
        m = jnp.concatenate([jnp.full((rows, 1), sink_ref[s["h0"] + j] * LOG2_E, F32) for j in range(g)], axis=0)
        sink_col = m
        for sc in s["sc"]:
            m = jnp.maximum(m, jnp.max(sc, axis=-1, keepdims=True))
        return dict(s, m=m, sink=jnp.exp2(sink_col - m))

    def weights(s):
        return dict(s, e=[jnp.exp2(sc - s["m"]).astype(BF16) for sc in s["sc"]], sc=None)

    def weighted_values(s):
        p = s["h0"] // (2 * g)
        acc = None
        for e, vf in zip(s["e"], values):
            v = vf(p)
            pv = _dot(e, jnp.concatenate([v, jnp.ones_like(v)], axis=1))
            acc = pv if acc is None else acc + pv
        return dict(s, o=acc[:, :LANES] / (acc[:, LANES:LANES + 1] + s["sink"]), e=None)

    states = _run_skewed([scores, row_max, weights, weighted_values],
                         [dict(h0=h * g) for h in range(N_KV_HEADS)])
    heads = {}
    for s in states:
        for j in range(g):
            blk = s["o"][j * rows:(j + 1) * rows]
            head = s["h0"] + j
            if (((head // g) % 2 == 0) != (head % 2 == 0)):
                blk = pltpu.roll(blk, HEAD_DIM, 1)
            heads[head] = blk
    for t in range(len(heads) // 2):
        tile = jnp.where(lane < HEAD_DIM, heads[2 * t], heads[2 * t + 1])
        o_ref[:, t * LANES:(t + 1) * LANES] = tile.astype(BF16)


def _attn_ctx_kernel(sink_ref, q_ref, k_ref, v_ref, o_ref, *, g):
    pair = lambda ref: (lambda p: ref[:, p * LANES:(p + 1) * LANES])
    _attend_pairs(sink_ref, q_ref, [pair(k_ref)], [pair(v_ref)], None, o_ref, g=g)


def _attn_ctx(sink, q, k, v, n_batch, seq):
    qp_w, kv_w = q.shape[1], k.shape[1]
    q_w = qp_w // 2
    g = q_w // kv_w
    return pl.pallas_call(
        functools.partial(_attn_ctx_kernel, g=g),
        grid=(n_batch,),
        in_specs=[pl.BlockSpec(memory_space=pltpu.SMEM),
                  pl.BlockSpec((seq, qp_w), lambda b: (b, 0)),
                  pl.BlockSpec((seq, kv_w), lambda b: (b, 0)),
                  pl.BlockSpec((seq, kv_w), lambda b: (b, 0))],
        out_specs=pl.BlockSpec((seq, q_w), lambda b: (b, 0)),
        out_shape=jax.ShapeDtypeStruct((n_batch * seq, q_w), BF16),
        compiler_params=_cparams(1),
        name="attn_ctx",
    )(sink, q, k, v)


def _attn_lat_kernel(sink_ref, q_ref, k_ref, v_ref, ck_ref, cv_ref, o_ref, *, g, seq):
    i = pl.program_id(1)
    n_loc = 3 * BLK
    start = pl.multiple_of(jnp.clip((i - 1) * BLK, 0, seq - n_loc), BLK)
    row = lax.broadcasted_iota(jnp.int32, (BLK, n_loc), 0)
    col = lax.broadcasted_iota(jnp.int32, (BLK, n_loc), 1)
    band = jnp.abs(row + (i * BLK - start) - col) <= WINDOW
    loc = lambda ref: (lambda p: ref[pl.ds(start, n_loc), p * LANES:(p + 1) * LANES])
    ctx = lambda ref: (lambda p: ref[:, p * LANES:(p + 1) * LANES])
    _attend_pairs(sink_ref, q_ref, [loc(k_ref), ctx(ck_ref)], [loc(v_ref), ctx(cv_ref)], band, o_ref, g=g)


def _attn_lat(sink, q, k, v, cache_k, cache_v, layer_slot, n_batch, seq, tok_off):
    qp_w, kv_w = q.shape[1], k.shape[1]
    q_w = qp_w // 2
    g = q_w // kv_w
    nb = seq // BLK
    past = cache_k.shape[2]
    assert seq >= 3 * BLK
    kv_spec = pl.BlockSpec((seq, kv_w), lambda b, i: (tok_off // seq + b, 0))
    c_spec = pl.BlockSpec((None, None, past, kv_w), lambda b, i: (b, layer_slot, 0, 0))
    return pl.pallas_call(
        functools.partial(_attn_lat_kernel, g=g, seq=seq),
        grid=(n_batch, nb),
        in_specs=[pl.BlockSpec(memory_space=pltpu.SMEM),
                  pl.BlockSpec((BLK, qp_w), lambda b, i: (tok_off // BLK + b * nb + i, 0)),
                  kv_spec, kv_spec, c_spec, c_spec],
        out_specs=pl.BlockSpec((BLK, q_w), lambda b, i: (b * nb + i, 0)),
        out_shape=jax.ShapeDtypeStruct((n_batch * seq, q_w), BF16),
        compiler_params=_cparams(2),
        name="attn_lat",
    )(sink, q, k, v, cache_k, cache_v)


def _post_ab_kernel(rows_ref, xc_ref, xl_ref, yap_ref, ybp_ref, yas_ref, ybs_ref, w_ref, gm_ref, lg_ref, lb_ref,
                    scf_ref, shf_ref, wr_ref, br_ref, x1_ref, hf_ref, route_ref, gate_ref, cnt_ref, carry_ref,
                    *, n_ctx_tiles, alpha):
    del rows_ref
    is_ctx = pl.program_id(0) < n_ctx_tiles
    a_w = yap_ref.shape[1]

    def proj_out(s):
        rs = s["rows"]
        pick = lambda p_ref, s_ref: pltpu.bitcast(
            jnp.where(is_ctx, pltpu.bitcast(p_ref[rs, :], jnp.uint32), pltpu.bitcast(s_ref[rs, :], jnp.uint32)), BF16)
        ya = pick(yap_ref, yas_ref)
        yb = pick(ybp_ref, ybs_ref)
        return dict(s, y=_dot(ya, w_ref[:a_w, :]) + _dot(yb, w_ref[a_w:, :]))

    def tail(s):
        rs = s["rows"]
        x1, hf, lg = _tail_rows(_group_rows(xc_ref, xl_ref, rs, n_ctx_tiles), s["y"], gm_ref[...], lg_ref[...],
                                lb_ref[...], scf_ref[...], shf_ref[...], wr_ref[...], br_ref[...], alpha)
        x1_ref[rs, :] = x1
        hf_ref[rs, :] = hf
        return dict(s, logits=lg)

    states = _run_skewed([proj_out, tail], [dict(rows=rs) for rs in _row_chains(xc_ref.shape[0], 4)])
    _tail_route(jnp.concatenate([s["logits"] for s in states], axis=1), route_ref, gate_ref, cnt_ref, carry_ref)


def _tail_out(t, d, tm):
    shapes = [jax.ShapeDtypeStruct((t, d), F32), jax.ShapeDtypeStruct((t, d // 2), jnp.int32),
              jax.ShapeDtypeStruct((2 * TOP_K, t), jnp.int32), jax.ShapeDtypeStruct((t, LANES), F32),
              jax.ShapeDtypeStruct((N_EXPERTS, LANES), F32)]
    specs = [pl.BlockSpec((tm, d), lambda i, r: (i, 0)), pl.BlockSpec((tm, d // 2), lambda i, r: (i, 0)),
             pl.BlockSpec((2 * TOP_K, tm), lambda i, r: (0, i)), pl.BlockSpec((tm, LANES), lambda i, r: (i, 0)),
             pl.BlockSpec((N_EXPERTS, LANES), lambda i, r: (0, 0))]
    return shapes, specs


_TAIL_SCRATCH = [pltpu.VMEM((N_EXPERTS, 1), F32)]


def _post_ab(x, ya_p, yb_p, ya_s, yb_s, w_out, mod3, mod_base, rows, ln_g, ln_b, w_r, b_r, alpha):
    n_ctx_tiles = ya_p.shape[0] // TM
    (xc, xl), x_specs = _group_specs(x, n_ctx_tiles)
    d = xc.shape[1]
    nt = rows.shape[0]
    t = nt * TM
    a_w, q_w = ya_p.shape[1], yb_p.shape[1]
    mod_spec = lambda j: pl.BlockSpec((None, 1, d), lambda i, r: (mod_base + j * 16 + r[i], 0, 0))
    whole = lambda shp: pl.BlockSpec(shp, lambda i, r: (0,) * len(shp))
    ctx = lambda w: pl.BlockSpec((TM, w), lambda i, r: (jnp.minimum(i, n_ctx_tiles - 1), 0))
    lat = lambda w: pl.BlockSpec((TM, w), lambda i, r: (jnp.maximum(i - n_ctx_tiles, 0), 0))
    shapes, specs = _tail_out(t, d, TM)
    grid_spec = pltpu.PrefetchScalarGridSpec(
        num_scalar_prefetch=1, grid=(nt,),
        in_specs=x_specs + [ctx(a_w), ctx(q_w), lat(a_w), lat(q_w),
                            whole(w_out.shape), mod_spec(2), whole((1, d)), whole((1, d)), mod_spec(4), mod_spec(3),
                            whole(w_r.shape), whole(b_r.shape)],
        out_specs=specs,
        scratch_shapes=_TAIL_SCRATCH,
    )
    return pl.pallas_call(
        functools.partial(_post_ab_kernel, n_ctx_tiles=n_ctx_tiles, alpha=alpha),
        grid_spec=grid_spec, out_shape=shapes, compiler_params=_cparams(1), name="post_ab",
    )(rows, xc, xl, ya_p, yb_p, ya_s, yb_s, w_out, mod3, ln_g, ln_b, mod3, mod3, w_r, b_r)


def _unpack_f32_pairs(p):
    u = pltpu.bitcast(p, jnp.uint32)
    return jnp.concatenate([pltpu.bitcast(u << 16, F32), pltpu.bitcast(u & jnp.uint32(0xFFFF0000), F32)], axis=1)


def _ffn_residual(x1, y_packed, gate, g_f, ln_g, ln_b, alpha):
    y = gate[:, 0:1] * _unpack_f32_pairs(y_packed[0])
    for k in range(1, TOP_K):
        y = y + gate[:, k:k + 1] * _unpack_f32_pairs(y_packed[k])
    return _ln(alpha * x1 + g_f * y) * ln_g + ln_b


def _gmlp_kernel(rows_ref, xp_ref, y0_ref, y1_ref, y2_ref, y3_ref, pgate_ref, pgf_ref, plg_ref, plb_ref,
                 scm_ref, shm_ref, win_ref, bin_ref, gv_ref, bv_ref, wsp_ref, bsp_ref, wout_ref,
                 gm_ref, lg_ref, lb_ref, scf_ref, shf_ref, wr_ref, br_ref, x1_ref, hf_ref, route_ref, gate_ref,
                 cnt_ref, carry_ref, *, alpha):
    del rows_ref
    assert (xp_ref.shape[0] // 2) % CHUNK == 0
    c_w = win_ref.shape[1] // 2
    gd = c_w // C_GROUPS

    def norm_in(s):
        rs = s["rows"]
        x = _ffn_residual(xp_ref[rs, :], [r[rs, :] for r in (y0_ref, y1_ref, y2_ref, y3_ref)], pgate_ref[rs, :],
                          pgf_ref[...], plg_ref[...], plb_ref[...], alpha)
        return dict(s, x=x, h=(_ln(x) * (1.0 + scm_ref[...]) + shm_ref[...]).astype(BF16))

    def proj_in(s):
        return dict(s, z=_dot(s["h"], win_ref[...]) + bin_ref[...])

    def gate_split(s):
        z = s["z"]
        z = 0.5 * z * (1.0 + lax.erf(z * (2.0 ** -0.5)))
        return dict(s, u=z[:, :c_w], v=(_ln(z[:, c_w:]) * gv_ref[...] + bv_ref[...]).astype(BF16))

    def spatial(s):
        v = s["v"]
        chunks = []
        for n in range(v.shape[0] // CHUNK):
            groups = [_dot(wsp_ref[g], v[n * CHUNK:(n + 1) * CHUNK, g * gd:(g + 1) * gd]) + bsp_ref[g]
                      for g in range(C_GROUPS)]
            chunks.append(jnp.concatenate(groups, axis=1))
        return dict(s, t=(s["u"] * jnp.concatenate(chunks, axis=0)).astype(BF16))

    def proj_out(s):
        return dict(s, y=_dot(s["t"], wout_ref[...]))

    def tail(s):
        x1, hf, lg = _tail_rows(s["x"], s["y"], gm_ref[...], lg_ref[...], lb_ref[...], scf_ref[...], shf_ref[...],
                                wr_ref[...], br_ref[...], alpha)
        x1_ref[s["rows"], :] = x1
        hf_ref[s["rows"], :] = hf
        return dict(s, logits=lg)

    states = _run_skewed([norm_in, proj_in, gate_split, spatial, proj_out, tail],
                         [dict(rows=rs) for rs in _row_chains(xp_ref.shape[0])])
    _tail_route(jnp.concatenate([s["logits"] for s in states], axis=1), route_ref, gate_ref, cnt_ref, carry_ref)


def _gmlp(prev, mod3, mod_base, rows, w_in, b_in, g_v, b_v, w_sp, b_sp, w_out, ln_g, ln_b, w_r, b_r, alpha):
    x1p, y4, pgate, plg, plb = prev
    t, d = x1p.shape
    nt = rows.shape[0]
    tm = t // nt
    mod_spec = lambda j, base=mod_base: pl.BlockSpec((None, 1, d), lambda i, r: (base + j * 16 + r[i], 0, 0))
    whole = lambda shp: pl.BlockSpec(shp, lambda i, r: (0,) * len(shp))
    y_spec = lambda k: pl.BlockSpec((None, tm, d // 2), lambda i, r: (k, i, 0))
    shapes, specs = _tail_out(t, d, tm)
    grid_spec = pltpu.PrefetchScalarGridSpec(
        num_scalar_prefetch=1, grid=(nt,),
        in_specs=[pl.BlockSpec((tm, d), lambda i, r: (i, 0)), y_spec(0), y_spec(1), y_spec(2), y_spec(3),
                  pl.BlockSpec((tm, LANES), lambda i, r: (i, 0)), mod_spec(5, mod_base - 6 * 16),
                  whole((1, d)), whole((1, d)), mod_spec(1), mod_spec(0),
                  whole(w_in.shape), whole(b_in.shape), whole(g_v.shape), whole(b_v.shape),
                  whole(w_sp.shape), whole(b_sp.shape), whole(w_out.shape),
                  mod_spec(2), whole((1, d)), whole((1, d)), mod_spec(4), mod_spec(3),
                  whole(w_r.shape), whole(b_r.shape)],
        out_specs=specs,
        scratch_shapes=_TAIL_SCRATCH,
    )
    return pl.pallas_call(
        functools.partial(_gmlp_kernel, alpha=alpha),
        grid_spec=grid_spec, out_shape=shapes, compiler_params=_cparams(1), name="gmlp",
    )(rows, x1p, y4, y4, y4, y4, pgate, mod3, plg, plb, mod3, mod3, w_in, b_in, g_v, b_v, w_sp, b_sp, w_out,
      mod3, ln_g, ln_b, mod3, mod3, w_r, b_r)


def _moe_kernel(be_ref, bf_ref, nx_ref, sl_ref, hf_ref, nu_ref, x_ref, wgu_hbm, bgu_ref, wdn_hbm, bdn_ref, o_ref,
                wgu_f, wdn_f, wgu_s, wdn_s, sem, *, layer):
    i = pl.program_id(0)
    d_ff = wdn_s.shape[0]

    def weight_copies(e, slot):
        return (pltpu.make_async_copy(wgu_hbm.at[layer, e], wgu_f.at[slot], sem.at[0, slot]),
                pltpu.make_async_copy(wdn_hbm.at[layer, e], wdn_f.at[slot], sem.at[1, slot]))

    @pl.when(i < nu_ref[0])
    def _():
        @pl.when(bf_ref[i] == 1)
        def _():
            slot = sl_ref[i]

            @pl.when(i == 0)
            def _():
                for c in weight_copies(be_ref[i], slot):
                    c.start()

            for c in weight_copies(be_ref[i], slot):
                c.wait()
            wgu_s[...] = wgu_f[slot].astype(BF16)
            wdn_s[...] = wdn_f[slot].astype(BF16)

            @pl.when(nx_ref[i] >= 0)
            def _():
                for c in weight_copies(nx_ref[i], 1 - slot):
                    c.start()

        def up(s):
            return dict(s, gu=_dot(_unpack_bf16_pairs(x_ref[s["rows"], :]), wgu_s[...]) + bgu_ref[...])

        def act(s):
            gu = s["gu"]
            gate = jnp.minimum(gu[:, :d_ff], SWIGLU_LIMIT)
            lin = jnp.clip(gu[:, d_ff:], -SWIGLU_LIMIT, SWIGLU_LIMIT)
            glu = gate * jax.nn.sigmoid(SWIGLU_ALPHA * gate)
            return dict(s, gu=None, hid=((lin + 1.0) * glu).astype(BF16))

        def down(s):
            o_ref[s["rows"], :] = _pack_bf16_pairs(_dot(s["hid"], wdn_s[...]) + bdn_ref[...])
            return dict(s, hid=None)

        def run(n_chains, n_rows):
            _run_skewed([up, act, down], [dict(rows=rs) for rs in _row_chains(n_rows, n_chains)])

        @pl.when(hf_ref[i] == 0)
        def _():
            run(2, TM_MOE)

        @pl.when(hf_ref[i] == 1)
        def _():
            run(1, TM_MOE // 2)


def _moe(xs, plan, w_gu, b_gu, w_dn, b_dn, layer):
    n_rows = xs.shape[0]
    depth, n_e, d, ff2 = w_gu.shape
    d_ff = ff2 // 2
    nt = n_rows // TM_MOE
    n_plan = len(plan)
    row_spec = pl.BlockSpec((TM_MOE, d // 2), lambda i, *p: (jnp.minimum(i, p[-1][0] - 1), 0))
    bias_spec = lambda w: pl.BlockSpec((None, None, 1, w), lambda i, *p: (layer, p[0][i], 0, 0))
    grid_spec = pltpu.PrefetchScalarGridSpec(
        num_scalar_prefetch=n_plan, grid=(nt,),
        in_specs=[row_spec, pl.BlockSpec(memory_space=pl.ANY), bias_spec(ff2),
                  pl.BlockSpec(memory_space=pl.ANY), bias_spec(d)],
        out_specs=row_spec,
        scratch_shapes=[pltpu.VMEM((2, d, ff2), F32), pltpu.VMEM((2, d_ff, d), F32),
                        pltpu.VMEM((d, ff2), BF16), pltpu.VMEM((d_ff, d), BF16),
                        pltpu.SemaphoreType.DMA((2, 2))],
    )
    return pl.pallas_call(
        functools.partial(_moe_kernel, layer=layer), grid_spec=grid_spec,
        out_shape=jax.ShapeDtypeStruct((n_rows, d // 2), jnp.int32),
        compiler_params=_cparams(1), name="moe_experts",
    )(*plan, xs, w_gu, b_gu.reshape(depth, n_e, 1, ff2), w_dn, b_dn.reshape(depth, n_e, 1, d))


def _combine_kernel(rows_ref, x_ref, y0_ref, y1_ref, y2_ref, y3_ref, gate_ref, gf_ref, lg_ref, lb_ref, *o_refs,
                    alpha, n_ctx_tiles):
    del rows_ref
    out = _ffn_residual(x_ref[...], [r[...] for r in (y0_ref, y1_ref, y2_ref, y3_ref)], gate_ref[...], gf_ref[...],
                        lg_ref[...], lb_ref[...], alpha)
    if n_ctx_tiles is None:
        o_refs[0][...] = out
    else:
        @pl.when(pl.program_id(0) < n_ctx_tiles)
        def _():
            o_refs[0][...] = out

        @pl.when(pl.program_id(0) >= n_ctx_tiles)
        def _():
            o_refs[1][...] = out


def _combine(x1, y4, gate, mod3, mod_base, rows, ln_g, ln_b, alpha, t_ctx=None):
    t, d = x1.shape
    nt = rows.shape[0]
    tm = t // nt
    y_spec = lambda k: pl.BlockSpec((None, tm, d // 2), lambda i, r: (k, i, 0))
    if t_ctx is None:
        n_ctx_tiles = None
        out_specs = pl.BlockSpec((tm, d), lambda i, r: (i, 0))
        out_shape = jax.ShapeDtypeStruct((t, d), F32)
    else:
        n_ctx_tiles = t_ctx // tm
        out_specs = [pl.BlockSpec((tm, d), lambda i, r: (jnp.minimum(i, n_ctx_tiles - 1), 0)),
                     pl.BlockSpec((tm, d), lambda i, r: (jnp.maximum(i - n_ctx_tiles, 0), 0))]
        out_shape = [jax.ShapeDtypeStruct((t_ctx, d), F32), jax.ShapeDtypeStruct((t - t_ctx, d), F32)]
    grid_spec = pltpu.PrefetchScalarGridSpec(
        num_scalar_prefetch=1, grid=(nt,),
        in_specs=[pl.BlockSpec((tm, d), lambda i, r: (i, 0)), y_spec(0), y_spec(1), y_spec(2), y_spec(3),
                  pl.BlockSpec((tm, LANES), lambda i, r: (i, 0)),
                  pl.BlockSpec((None, 1, d), lambda i, r: (mod_base + 5 * 16 + r[i], 0, 0)),
                  pl.BlockSpec((1, d), lambda i, r: (0, 0)), pl.BlockSpec((1, d), lambda i, r: (0, 0))],
        out_specs=out_specs,
    )
    return pl.pallas_call(
        functools.partial(_combine_kernel, alpha=alpha, n_ctx_tiles=n_ctx_tiles), grid_spec=grid_spec,
        out_shape=out_shape, compiler_params=_cparams(1), name="moe_combine",
    )(rows, x1, y4, y4, y4, y4, gate, mod3, ln_g, ln_b)


def _route_plan(route, counts, n_tiles):
    idx, rank = route[:TOP_K], route[TOP_K:]
    padded = (counts + TM_MOE - 1) // TM_MOE * TM_MOE
    pend = jnp.cumsum(padded)
    pstart = pend - padded
    onehot = idx[:, :, None] == jnp.arange(N_EXPERTS, dtype=jnp.int32)[None, None, :]
    dest = jnp.sum(jnp.where(onehot, pstart[None, None, :], 0), axis=-1) + rank
    n_used = (pend[-1] // TM_MOE).astype(jnp.int32)
    tile_start = jnp.arange(n_tiles, dtype=jnp.int32) * TM_MOE
    blk_e = jnp.sum((tile_start[:, None] >= pend[None, :]).astype(jnp.int32), axis=1)
    last_e = jnp.sum((jnp.maximum(n_used - 1, 0) * TM_MOE >= pend).astype(jnp.int32))
    blk_e = jnp.minimum(jnp.where(jnp.arange(n_tiles) < n_used, blk_e, last_e), N_EXPERTS - 1).astype(jnp.int32)
    blk_first = jnp.concatenate([jnp.ones((1,), jnp.int32), (blk_e[1:] != blk_e[:-1]).astype(jnp.int32)])
    ar = jnp.arange(N_EXPERTS, dtype=jnp.int32)
    later = jnp.logical_and((counts > 0)[None, :], ar[None, :] > ar[:, None])
    next_e = jnp.min(jnp.where(later, ar[None, :], N_EXPERTS), axis=1)
    next_e = jnp.where(next_e == N_EXPERTS, -1, next_e)
    blk_next = jnp.sum(jnp.where(blk_e[:, None] == ar[None, :], next_e[None, :], 0), axis=1).astype(jnp.int32)
    blk_slot = ((jnp.cumsum(blk_first) - 1) % 2).astype(jnp.int32)
    left = jnp.sum(jnp.where(blk_e[:, None] == ar[None, :], (pstart + counts)[None, :], 0), axis=1) - tile_start
    blk_half = (left <= TM_MOE // 2).astype(jnp.int32)
    return dest, (blk_e, blk_first, blk_next, blk_slot, blk_half, n_used.reshape(1))


SC_CORES = 2
SC_SUBCORES = 16


def _sc_mesh():
    return plsc.VectorSubcoreMesh(core_axis_name="core", subcore_axis_name="subcore")


def _sc_scatter_rows(src, dest, n_rows, window):
    n_slots, t = dest.shape
    w = src.shape[1]
    per_worker = t // (SC_CORES * SC_SUBCORES)
    assert per_worker * SC_CORES * SC_SUBCORES == t and per_worker % window == 0

    @functools.partial(pl.kernel, out_type=jax.ShapeDtypeStruct((n_rows, w), src.dtype), mesh=_sc_mesh(),
                       scratch_types=[pltpu.VMEM((n_slots, window), jnp.int32), pltpu.VMEM((window, w), src.dtype)],
                       name="sc_dispatch")
    def scatter(src_hbm, dest_hbm, out_hbm, idx_v, rows_v):
        worker = lax.axis_index("subcore") * SC_CORES + lax.axis_index("core")

        @pl.loop(0, per_worker // window)
        def _(c):
            base = pl.multiple_of(worker * per_worker + c * window, window)
            pltpu.sync_copy(src_hbm.at[pl.ds(base, window)], rows_v)
            pltpu.sync_copy(dest_hbm.at[:, pl.ds(base, window)], idx_v)
            for k in range(n_slots):
                pltpu.sync_copy(rows_v, out_hbm.at[idx_v.at[k]])

    return scatter(src, dest)


def _sc_gather_rows(table, idx, window):
    n = idx.shape[0]
    w = table.shape[1]
    per_worker = n // (SC_CORES * SC_SUBCORES)
    assert per_worker * SC_CORES * SC_SUBCORES == n and per_worker % window == 0

    @functools.partial(pl.kernel, out_type=jax.ShapeDtypeStruct((n, w), table.dtype), mesh=_sc_mesh(),
                       scratch_types=[pltpu.VMEM((window,), jnp.int32), pltpu.VMEM((window, w), table.dtype)],
                       name="sc_collect")
    def gather(table_hbm, idx_hbm, out_hbm, idx_v, rows_v):
        worker = lax.axis_index("subcore") * SC_CORES + lax.axis_index("core")

        @pl.loop(0, per_worker // window)
        def _(c):
            base = pl.multiple_of(worker * per_worker + c * window, window)
            pltpu.sync_copy(idx_hbm.at[pl.ds(base, window)], idx_v)
            pltpu.sync_copy(table_hbm.at[idx_v], rows_v)
            pltpu.sync_copy(rows_v, out_hbm.at[pl.ds(base, window)])

    return gather(table, idx)


def _dft_tables(n):
    k = jnp.arange(n, dtype=jnp.int32)

    def trig(rows):
        ang = ((rows[:, None] * k[None, :]) % n).astype(F32) * (2.0 * math.pi / n)
        return jnp.cos(ang), jnp.sin(ang)

    n2 = 64
    if n < 4 * n2:
        return trig(k)
    ca, sa = (z[:, None, :] for z in trig(jnp.arange(n // n2, dtype=jnp.int32) * n2))
    cb, sb = (z[None, :, :] for z in trig(jnp.arange(n2, dtype=jnp.int32)))
    return (ca * cb - sa * sb).reshape(n, n), (sa * cb + ca * sb).reshape(n, n)


def _rope_tables(n_lat, n_ctx_rows):
    half, quarter = HEAD_DIM // 2, HEAD_DIM // 4
    tpos = jnp.arange(n_lat, dtype=jnp.int32)
    lane = jnp.arange(LANES, dtype=jnp.int32) % HEAD_DIM
    pos = jnp.where(lane[None, :] < half, (tpos // GRID_W)[:, None], (tpos % GRID_W)[:, None]).astype(F32)
    fidx = (lane % quarter).astype(F32)
    freqs = ROPE_THETA ** (-fidx / quarter)
    ang = pos * freqs[None, :]
    cos, sin = jnp.cos(ang), jnp.sin(ang)
    first = (lane % half) < quarter
    sa = jnp.where(first[None, :], -sin, 0.0)
    sb = jnp.where(first[None, :], 0.0, sin)
    ident = lambda v: jnp.full((n_ctx_rows, LANES), v, F32)
    return (jnp.concatenate([ident(1.0), cos]), jnp.concatenate([ident(0.0), sa]), jnp.concatenate([ident(0.0), sb]))


def kernel(x_prompt, x_sample, cache_k_ab, cache_v_ab, c, c_ctx, w_ada, b_ada, ln_mix_g, ln_mix_b, ln_ffn_g, ln_ffn_b, w_in_ab, w_out_ab, sink_ab, w_in_c, b_in_c, ln_v_g, ln_v_b, w_sp, b_sp, w_out_c, w_router, b_router, w_gu, b_gu, w_dn, b_dn):
    n_ctx_b, ctx_seq, d = x_prompt.shape
    n_lat_b, lat_seq, _ = x_sample.shape
    depth = w_ada.shape[0]
    t_ctx, t_lat = n_ctx_b * ctx_seq, n_lat_b * lat_seq
    t = t_ctx + t_lat
    alpha = (2 * depth) ** 0.25
    kv_w = N_KV_HEADS * HEAD_DIM
    a_w = d // 4
    q_w = d - a_w
    assert t_ctx % TM == 0 and lat_seq % TM == 0 and t_ctx % lat_seq == 0 and n_lat_b + 1 <= 16

    cond_rows = lambda tm: jnp.asarray(np.concatenate(
        [np.zeros(t_ctx // tm, np.int32), 1 + np.arange(t_lat // tm, dtype=np.int32) // (lat_seq // tm)]))
    rows, rows_c = cond_rows(TM), cond_rows(TM_COMBINE)
    rblk = jnp.asarray(np.concatenate(
        [np.zeros(t_ctx // TM, np.int32), 1 + np.arange(t_lat // TM, dtype=np.int32) % (lat_seq // TM)]))

    cond16 = jnp.zeros((16, d), F32).at[0].set(c_ctx).at[1:1 + n_lat_b].set(c)
    mod = _ada(cond16, w_ada, b_ada)
    mod3 = mod.reshape(depth, 16, 6, d).transpose(0, 2, 1, 3).reshape(depth * 6 * 16, 1, d)

    cos, sa, sb = _rope_tables(lat_seq, TM)
    cs_ctx, ss_ctx = (z.astype(BF16) for z in _dft_tables(ctx_seq))
    cs_lat, ss_lat = (z.astype(BF16) for z in _dft_tables(lat_seq))
    gd = a_w // A_GROUPS
    cd, sd = _dft_tables(gd)
    eye = jnp.eye(A_GROUPS, dtype=F32)
    bdc, bds = jnp.kron(eye, cd).astype(BF16), jnp.kron(eye, sd).astype(BF16)

    past = cache_k_ab.shape[2]
    cache_k = cache_k_ab.reshape(n_lat_b, -1, past, kv_w).astype(BF16)
    cache_v = cache_v_ab.reshape(n_lat_b, -1, past, kv_w).astype(BF16)

    x = (x_prompt.reshape(t_ctx, d), x_sample.reshape(t_lat, d))
    row2 = lambda v: v.reshape(1, -1)
    ks, vs = [], []
    for l in range(depth):
        j = l // 2
        mod_base = l * 6 * 16
        w_r, b_r = w_router[l].T, b_router[l].reshape(-1, 1)
        if l % 2 == 0:
            ac, as_, q, k32, v32, k, v = _inproj(x, t_ctx // TM, mod3, mod_base, rows, rblk, w_in_ab[j].astype(BF16),
                                                 cos, sa, sb, bdc, bds, a_w, q_w, kv_w)
            ks.append(k32[:t_ctx].reshape(n_ctx_b, ctx_seq, N_KV_HEADS, HEAD_DIM))
            vs.append(v32[:t_ctx].reshape(n_ctx_b, ctx_seq, N_KV_HEADS, HEAD_DIM))
            ya_p = _fourier(ac, as_, cs_ctx, ss_ctx, n_ctx_b, ctx_seq, 0, min(ctx_seq, 512))
            ya_s = _fourier(ac, as_, cs_lat, ss_lat, n_lat_b, lat_seq, t_ctx // lat_seq, min(lat_seq, 512))
            yb_p = _attn_ctx(sink_ab[j], q, k, v, n_ctx_b, ctx_seq)
            yb_s = _attn_lat(sink_ab[j], q, k, v, cache_k, cache_v, j, n_lat_b, lat_seq, t_ctx)
            x1, hf, route, gate, cnt = _post_ab(x, ya_p, yb_p, ya_s, yb_s, w_out_ab[j].astype(BF16), mod3, mod_base,
                                                rows, row2(ln_mix_g[l]), row2(ln_mix_b[l]), w_r, b_r, alpha)
        else:
            x1, hf, route, gate, cnt = _gmlp(prev, mod3, mod_base, rows_c, w_in_c[j].astype(BF16), row2(b_in_c[j]),
                                             row2(ln_v_g[j]), row2(ln_v_b[j]), w_sp[j].astype(BF16),
                                             b_sp[j][:, :, None], w_out_c[j].astype(BF16), row2(ln_mix_g[l]),
                                             row2(ln_mix_b[l]), w_r, b_r, alpha)
        n_rows = t * TOP_K + N_EXPERTS * TM_MOE
        dest, plan = _route_plan(route, cnt[:, 0].astype(jnp.int32), n_rows // TM_MOE)
        xs = _sc_scatter_rows(hf, dest, n_rows, 128)
        out_sorted = _moe(xs, plan, w_gu, b_gu, w_dn, b_dn, l)
        y4 = _sc_gather_rows(out_sorted, dest.reshape(-1), 128).reshape(TOP_K, t, d // 2)
        if l % 2 == 0 and l + 1 < depth:
            prev = (x1, y4, gate, row2(ln_ffn_g[l]), row2(ln_ffn_b[l]))
        else:
            x = _combine(x1, y4, gate, mod3, mod_base, rows_c, row2(ln_ffn_g[l]), row2(ln_ffn_b[l]), alpha,
                         t_ctx if l == depth - 1 else None)

    y_prompt = x[0].reshape(n_ctx_b, ctx_seq, d)
    y_sample = x[1].reshape(n_lat_b, lat_seq, d)
    return (y_prompt, y_sample, jnp.stack(ks, axis=1), jnp.stack(vs, axis=1))
```

```python
import functools
import math

import numpy as np
import jax
import jax.numpy as jnp
from jax import lax
from jax.experimental import pallas as pl
from jax.experimental.pallas import tpu as pltpu
from jax.experimental.pallas import tpu_sc as plsc

GRID_W = 64
BLK = 128
WINDOW = 128
HEAD_DIM = 64
A_GROUPS = 4
N_KV_HEADS = 4
C_GROUPS = 8
CHUNK = 128
N_EXPERTS = 32
TOP_K = 4
SWIGLU_LIMIT = 7.0
SWIGLU_ALPHA = 1.702
ROPE_THETA = 10000.0
LN_EPS = 1e-6
NEG_INF = -1e30
LOG2_E = math.log2(math.e)

LANES = 128
TM = 1024
TM_COMBINE = 512
TM_MOE = 512
VMEM_LIMIT = 52 * 1024 * 1024
VMEM_LIMIT_GMLP = 60 * 1024 * 1024

F32 = jnp.float32
BF16 = jnp.bfloat16


def _cparams(n_axes, vmem_limit=VMEM_LIMIT):
    return pltpu.CompilerParams(dimension_semantics=("arbitrary",) * n_axes, vmem_limit_bytes=vmem_limit)


def _ln(x):
    mu = jnp.mean(x, axis=-1, keepdims=True)
    xc = x - mu
    var = jnp.mean(xc * xc, axis=-1, keepdims=True)
    return xc * lax.rsqrt(var + LN_EPS)


def _dot(a, b):
    return jnp.dot(a, b, preferred_element_type=F32)


def _dot_nt(a, b):
    return lax.dot_general(a, b, (((1,), (1,)), ((), ())), preferred_element_type=F32)


def _split(a):
    hi = a.astype(BF16)
    lo = (a - hi.astype(F32)).astype(BF16)
    return hi, lo


def _dot_3pass(a, b):
    a_hi, a_lo = _split(a)
    b_hi, b_lo = _split(b)
    return _dot(a_hi, b_hi) + (_dot(a_hi, b_lo) + _dot(a_lo, b_hi))


def _lane_select(cols, width):
    m = cols[0].shape[0]
    lane = lax.broadcasted_iota(jnp.int32, (m, width), 1)
    out = jnp.zeros((m, width), cols[0].dtype)
    for j, c in enumerate(cols):
        out = jnp.where(lane == j, c, out)
    return out


def _pack_bf16_pairs(v):
    n = v.shape[1] // 2
    lo = pltpu.bitcast(v[:, :n].astype(BF16).astype(F32), jnp.uint32) >> 16
    hi = pltpu.bitcast(v[:, n:].astype(BF16).astype(F32), jnp.uint32) & jnp.uint32(0xFFFF0000)
    return pltpu.bitcast(lo | hi, jnp.int32)


def _unpack_bf16_pairs(p):
    u = pltpu.bitcast(p, jnp.uint32)
    lo = pltpu.bitcast(u << 16, F32)
    hi = pltpu.bitcast(u & jnp.uint32(0xFFFF0000), F32)
    return jnp.concatenate([lo, hi], axis=1).astype(BF16)


def _row_chains(n_rows, n_chains=2):
    step = n_rows // n_chains
    return [slice(c * step, (c + 1) * step) for c in range(n_chains)]


def _run_skewed(stages, states):
    states = list(states)
    for step in range(len(stages) + len(states) - 1):
        for c in range(len(states)):
            if 0 <= step - c < len(stages):
                states[c] = stages[step - c](states[c])
    return states


def _tail_rows(x, y, g_m, ln_g, ln_b, sc_f, sh_f, w_r, b_r, alpha):
    x1 = _ln(alpha * x + g_m * y) * ln_g + ln_b
    hf = _ln(x1) * (1.0 + sc_f) + sh_f
    hf_hi, hf_lo = _split(hf)
    w_hi, w_lo = _split(w_r)
    logits = _dot_nt(w_hi, hf_hi) + (_dot_nt(w_hi, hf_lo) + _dot_nt(w_lo, hf_hi)) + b_r
    return x1, _pack_bf16_pairs(hf), logits


def _tail_route(logits, route_ref, gate_ref, cnt_ref, carry_ref):
    @pl.when(pl.program_id(0) == 0)
    def _():
        carry_ref[...] = jnp.zeros_like(carry_ref)

    tm = logits.shape[1]
    sub = lax.broadcasted_iota(jnp.int32, logits.shape, 0)
    vals = logits
    top_v, top_i = [], []
    for _ in range(TOP_K):
        m = jnp.max(vals, axis=0, keepdims=True)
        am = jnp.min(jnp.where(vals == m, sub, N_EXPERTS), axis=0, keepdims=True)
        top_v.append(m)
        top_i.append(am)
        vals = jnp.where(sub == am, -jnp.inf, vals)
    e = [jnp.exp(v - top_v[0]) for v in top_v]
    denom = e[0] + e[1] + e[2] + e[3]
    gates_t = jnp.concatenate([ek / denom for ek in e] + [jnp.zeros((LANES - TOP_K, tm), F32)], axis=0)
    gate_ref[...] = gates_t.T

    member = jnp.zeros(logits.shape, F32)
    for am in top_i:
        member = jnp.where(sub == am, 1.0, member)
    r_i = lax.broadcasted_iota(jnp.int32, (tm, tm), 0)
    c_i = lax.broadcasted_iota(jnp.int32, (tm, tm), 1)
    earlier = jnp.where(r_i < c_i, 1.0, 0.0).astype(BF16)
    before = _dot(member.astype(BF16), earlier) + carry_ref[...]
    ranks = [jnp.sum(jnp.where(sub == am, before, 0.0), axis=0, keepdims=True).astype(jnp.int32) for am in top_i]
    route_ref[...] = jnp.concatenate(top_i + ranks, axis=0)
    carry = carry_ref[...] + jnp.sum(member, axis=1, keepdims=True)
    carry_ref[...] = carry
    cnt_ref[...] = jnp.broadcast_to(carry, cnt_ref.shape)


def _ada_kernel(cond_ref, w_ref, b_ref, o_ref):
    c = cond_ref[...]
    s = (c * jax.nn.sigmoid(c)).astype(BF16)
    o_ref[...] = _dot(s, w_ref[...].astype(BF16)) + b_ref[...]


def _ada(cond16, w_ada, b_ada):
    depth, d, n = w_ada.shape
    tn = 1536
    return pl.pallas_call(
        _ada_kernel,
        grid=(depth, n // tn),
        in_specs=[
            pl.BlockSpec((16, d), lambda l, j: (0, 0)),
            pl.BlockSpec((None, d, tn), lambda l, j: (l, 0, j)),
            pl.BlockSpec((None, 1, tn), lambda l, j: (l, 0, j)),
        ],
        out_specs=pl.BlockSpec((None, 16, tn), lambda l, j: (l, 0, j)),
        out_shape=jax.ShapeDtypeStruct((depth, 16, n), F32),
        compiler_params=_cparams(2),
        name="ada_mod",
    )(cond16, w_ada, b_ada.reshape(depth, 1, n))


def _group_rows(xc_ref, xl_ref, rows, n_ctx_tiles):
    return jnp.where(pl.program_id(0) < n_ctx_tiles, xc_ref[rows, :], xl_ref[rows, :])


def _group_specs(x, n_ctx_tiles):
    if isinstance(x, tuple):
        arrays, lat = x, (lambda i, *_: (jnp.maximum(i - n_ctx_tiles, 0), 0))
    else:
        arrays, lat = (x, x), (lambda i, *_: (jnp.maximum(i, n_ctx_tiles), 0))
    d = arrays[0].shape[1]
    ctx = lambda i, *_: (jnp.minimum(i, n_ctx_tiles - 1), 0)
    return arrays, [pl.BlockSpec((TM, d), ctx), pl.BlockSpec((TM, d), lat)]


def _inproj_kernel(rows_ref, rblk_ref, xc_ref, xl_ref, sc_ref, sh_ref, w_ref, cos_ref, sa_ref, sb_ref, bdc_ref,
                   bds_ref, ac_ref, as_ref, q_ref, k_ref, v_ref, kb_ref, vb_ref, *, a_w, q_w, kv_w, n_ctx_tiles):
    del rows_ref, rblk_ref
    g = q_w // kv_w

    def norm_in(s):
        x = _group_rows(xc_ref, xl_ref, s["rows"], n_ctx_tiles)
        return dict(s, h=(_ln(x) * (1.0 + sc_ref[...]) + sh_ref[...]).astype(BF16))

    def proj(s):
        return dict(s, p=_dot(s["h"], w_ref[...]))

    def finish(s):
        rs, p = s["rows"], s["p"]
        a = p[:, :a_w].astype(BF16)
        ac_ref[rs, :] = _dot(a, bdc_ref[...]).astype(BF16)
        as_ref[rs, :] = _dot(a, bds_ref[...]).astype(BF16)
        cos, sa, sb = cos_ref[rs, :], sa_ref[rs, :], sb_ref[rs, :]

        def rope(t):
            w = t.shape[1]
            reps = w // LANES
            c, a_, b_ = (jnp.tile(z, (1, reps)) for z in (cos, sa, sb))
            nxt = pltpu.roll(t, w - HEAD_DIM // 4, 1)
            prv = pltpu.roll(t, HEAD_DIM // 4, 1)
            return t * c + nxt * a_ + prv * b_

        q = rope(p[:, a_w:a_w + q_w]) * (HEAD_DIM ** -0.5 * LOG2_E)
        lane = lax.broadcasted_iota(jnp.int32, (q.shape[0], LANES), 1)
        for j in range(q_w // HEAD_DIM):
            tile = q[:, (j // 2) * LANES:(j // 2 + 1) * LANES]
            dst_low = (j // g) % 2 == 0
            if (j % 2 == 0) != dst_low:
                tile = pltpu.roll(tile, HEAD_DIM, 1)
            keep = (lane < HEAD_DIM) if dst_low else (lane >= HEAD_DIM)
            q_ref[rs, j * LANES:(j + 1) * LANES] = jnp.where(keep, tile, 0.0).astype(BF16)
        k = rope(p[:, a_w + q_w:a_w + q_w + kv_w])
        v = p[:, a_w + q_w + kv_w:]
        k_ref[rs, :] = k
        v_ref[rs, :] = v
        kb_ref[rs, :] = k.astype(BF16)
        vb_ref[rs, :] = v.astype(BF16)
        return s

    _run_skewed([norm_in, proj, finish], [dict(rows=rs) for rs in _row_chains(xc_ref.shape[0])])


def _inproj(x, n_ctx_tiles, mod3, mod_base, rows, rblk, w_in, cos, sa, sb, bdc, bds, a_w, q_w, kv_w):
    (xc, xl), x_specs = _group_specs(x, n_ctx_tiles)
    d = xc.shape[1]
    t = rows.shape[0] * TM
    n = w_in.shape[1]
    nt = t // TM
    mod_spec = lambda j: pl.BlockSpec((None, 1, d), lambda i, r, rb: (mod_base + j * 16 + r[i], 0, 0))
    whole = lambda shp: pl.BlockSpec(shp, lambda i, r, rb: (0,) * len(shp))
    rope_spec = pl.BlockSpec((TM, LANES), lambda i, r, rb: (rb[i], 0))
    tok = lambda w: pl.BlockSpec((TM, w), lambda i, r, rb: (i, 0))
    grid_spec = pltpu.PrefetchScalarGridSpec(
        num_scalar_prefetch=2, grid=(nt,),
        in_specs=x_specs + [mod_spec(1), mod_spec(0), whole((d, n)), rope_spec, rope_spec, rope_spec,
                            whole((a_w, a_w)), whole((a_w, a_w))],
        out_specs=[tok(a_w), tok(a_w), tok(2 * q_w), tok(kv_w), tok(kv_w), tok(kv_w), tok(kv_w)],
    )
    return pl.pallas_call(
        functools.partial(_inproj_kernel, a_w=a_w, q_w=q_w, kv_w=kv_w, n_ctx_tiles=n_ctx_tiles),
        grid_spec=grid_spec,
        out_shape=[jax.ShapeDtypeStruct((t, a_w), BF16), jax.ShapeDtypeStruct((t, a_w), BF16),
                   jax.ShapeDtypeStruct((t, 2 * q_w), BF16), jax.ShapeDtypeStruct((t, kv_w), F32),
                   jax.ShapeDtypeStruct((t, kv_w), F32), jax.ShapeDtypeStruct((t, kv_w), BF16),
                   jax.ShapeDtypeStruct((t, kv_w), BF16)],
        compiler_params=_cparams(1),
        name="inproj_ab",
    )(rows, rblk, xc, xl, mod3, mod3, w_in, cos, sa, sb, bdc, bds)


def _fourier_kernel(c_ref, s_ref, ac_ref, as_ref, o_ref, *, scale):
    y = _dot(c_ref[...], ac_ref[...]) - _dot(s_ref[...], as_ref[...])
    o_ref[...] = (y * scale).astype(BF16)


def _fourier(ac, as_, cs, ss, n_batch, seq, blk_off, tm):
    a_w = ac.shape[1]
    nr = seq // tm
    a_spec = pl.BlockSpec((seq, a_w), lambda r, b: (blk_off + b, 0))
    t_spec = pl.BlockSpec((tm, seq), lambda r, b: (r, 0))
    return pl.pallas_call(
        functools.partial(_fourier_kernel, scale=(seq * (a_w // A_GROUPS)) ** -0.5),
        grid=(nr, n_batch),
        in_specs=[t_spec, t_spec, a_spec, a_spec],
        out_specs=pl.BlockSpec((tm, a_w), lambda r, b: (b * nr + r, 0)),
        out_shape=jax.ShapeDtypeStruct((n_batch * seq, a_w), BF16),
        compiler_params=_cparams(2),
        name="fourier_%d" % seq,
    )(cs, ss, ac, as_)


def _attend_pairs(sink_ref, q_ref, keys, values, mask, o_ref, *, g):
    rows = q_ref.shape[0]
    lane = lax.broadcasted_iota(jnp.int32, (rows, LANES), 1)
    full_mask = None if mask is None else jnp.concatenate([mask] * g, axis=0)

    def scores(s):
        h0, p = s["h0"], s["h0"] // (2 * g)
        qp = jnp.concatenate([q_ref[:, (h0 + j) * LANES:(h0 + j + 1) * LANES] for j in range(g)], axis=0)
        sc = [_dot_nt(qp, kf(p)) for kf in keys]
        if full_mask is not None:
            sc[0] = jnp.where(full_mask, sc[0], NEG_INF)
        return dict(s, sc=sc)

    def row_max(s):
        m = jnp.concatenate([jnp.full((rows, 1), sink_ref[s["h0"] + j] * LOG2_E, F32) for j in range(g)], axis=0)
        sink_col = m
        for sc in s["sc"]:
            m = jnp.maximum(m, jnp.max(sc, axis=-1, keepdims=True))
        return dict(s, m=m, sink=jnp.exp2(sink_col - m))

    def weights(s):
        return dict(s, e=[jnp.exp2(sc - s["m"]).astype(BF16) for sc in s["sc"]], sc=None)

    def weighted_values(s):
        p = s["h0"] // (2 * g)
        acc = None
        for e, vf in zip(s["e"], values):
            v = vf(p)
            pv = _dot(e, jnp.concatenate([v, jnp.ones_like(v)], axis=1))
            acc = pv if acc is None else acc + pv
        return dict(s, o=acc[:, :LANES] / (acc[:, LANES:LANES + 1] + s["sink"]), e=None)

    states = _run_skewed([scores, row_max, weights, weighted_values],
                         [dict(h0=h * g) for h in range(N_KV_HEADS)])
    heads = {}
    for s in states:
        for j in range(g):
            blk = s["o"][j * rows:(j + 1) * rows]
            head = s["h0"] + j
            if (((head // g) % 2 == 0) != (head % 2 == 0)):
                blk = pltpu.roll(blk, HEAD_DIM, 1)
            heads[head] = blk
    for t in range(len(heads) // 2):
        tile = jnp.where(lane < HEAD_DIM, heads[2 * t], heads[2 * t + 1])
        o_ref[:, t * LANES:(t + 1) * LANES] = tile.astype(BF16)


def _attn_ctx_kernel(sink_ref, q_ref, k_ref, v_ref, o_ref, *, g):
    pair = lambda ref: (lambda p: ref[:, p * LANES:(p + 1) * LANES])
    _attend_pairs(sink_ref, q_ref, [pair(k_ref)], [pair(v_ref)], None, o_ref, g=g)


def _attn_ctx(sink, q, k, v, n_batch, seq):
    qp_w, kv_w = q.shape[1], k.shape[1]
    q_w = qp_w // 2
    g = q_w // kv_w
    return pl.pallas_call(
        functools.partial(_attn_ctx_kernel, g=g),
        grid=(n_batch,),
        in_specs=[pl.BlockSpec(memory_space=pltpu.SMEM),
                  pl.BlockSpec((seq, qp_w), lambda b: (b, 0)),
                  pl.BlockSpec((seq, kv_w), lambda b: (b, 0)),
                  pl.BlockSpec((seq, kv_w), lambda b: (b, 0))],
        out_specs=pl.BlockSpec((seq, q_w), lambda b: (b, 0)),
        out_shape=jax.ShapeDtypeStruct((n_batch * seq, q_w), BF16),
        compiler_params=_cparams(1),
        name="attn_ctx",
    )(sink, q, k, v)


def _attn_lat_kernel(sink_ref, q_ref, k_ref, v_ref, ck_ref, cv_ref, o_ref, *, g, seq):
    i = pl.program_id(1)
    n_loc = 3 * BLK
    start = pl.multiple_of(jnp.clip((i - 1) * BLK, 0, seq - n_loc), BLK)
    row = lax.broadcasted_iota(jnp.int32, (BLK, n_loc), 0)
    col = lax.broadcasted_iota(jnp.int32, (BLK, n_loc), 1)
    band = jnp.abs(row + (i * BLK - start) - col) <= WINDOW
    loc = lambda ref: (lambda p: ref[pl.ds(start, n_loc), p * LANES:(p + 1) * LANES])
    ctx = lambda ref: (lambda p: ref[:, p * LANES:(p + 1) * LANES])
    _attend_pairs(sink_ref, q_ref, [loc(k_ref), ctx(ck_ref)], [loc(v_ref), ctx(cv_ref)], band, o_ref, g=g)


def _attn_lat(sink, q, k, v, cache_k, cache_v, layer_slot, n_batch, seq, tok_off):
    qp_w, kv_w = q.shape[1], k.shape[1]
    q_w = qp_w // 2
    g = q_w // kv_w
    nb = seq // BLK
    past = cache_k.shape[2]
    assert seq >= 3 * BLK
    kv_spec = pl.BlockSpec((seq, kv_w), lambda b, i: (tok_off // seq + b, 0))
    c_spec = pl.BlockSpec((None, None, past, kv_w), lambda b, i: (b, layer_slot, 0, 0))
    return pl.pallas_call(
        functools.partial(_attn_lat_kernel, g=g, seq=seq),
        grid=(n_batch, nb),
        in_specs=[pl.BlockSpec(memory_space=pltpu.SMEM),
                  pl.BlockSpec((BLK, qp_w), lambda b, i: (tok_off // BLK + b * nb + i, 0)),
                  kv_spec, kv_spec, c_spec, c_spec],
        out_specs=pl.BlockSpec((BLK, q_w), lambda b, i: (b * nb + i, 0)),
        out_shape=jax.ShapeDtypeStruct((n_batch * seq, q_w), BF16),
        compiler_params=_cparams(2),
        name="attn_lat",
    )(sink, q, k, v, cache_k, cache_v)


def _post_ab_kernel(rows_ref, xc_ref, xl_ref, yap_ref, ybp_ref, yas_ref, ybs_ref, w_ref, gm_ref, lg_ref, lb_ref,
                    scf_ref, shf_ref, wr_ref, br_ref, x1_ref, hf_ref, route_ref, gate_ref, cnt_ref, carry_ref,
                    *, n_ctx_tiles, alpha):
    del rows_ref
    is_ctx = pl.program_id(0) < n_ctx_tiles
    a_w = yap_ref.shape[1]

    def proj_out(s):
        rs = s["rows"]
        pick = lambda p_ref, s_ref: pltpu.bitcast(
            jnp.where(is_ctx, pltpu.bitcast(p_ref[rs, :], jnp.uint32), pltpu.bitcast(s_ref[rs, :], jnp.uint32)), BF16)
        ya = pick(yap_ref, yas_ref)
        yb = pick(ybp_ref, ybs_ref)
        return dict(s, y=_dot(ya, w_ref[:a_w, :]) + _dot(yb, w_ref[a_w:, :]))

    def tail(s):
        rs = s["rows"]
        x1, hf, lg = _tail_rows(_group_rows(xc_ref, xl_ref, rs, n_ctx_tiles), s["y"], gm_ref[...], lg_ref[...],
                                lb_ref[...], scf_ref[...], shf_ref[...], wr_ref[...], br_ref[...], alpha)
        x1_ref[rs, :] = x1
        hf_ref[rs, :] = hf
        return dict(s, logits=lg)

    states = _run_skewed([proj_out, tail], [dict(rows=rs) for rs in _row_chains(xc_ref.shape[0], 4)])
    _tail_route(jnp.concatenate([s["logits"] for s in states], axis=1), route_ref, gate_ref, cnt_ref, carry_ref)


def _tail_out(t, d, tm):
    shapes = [jax.ShapeDtypeStruct((t, d), F32), jax.ShapeDtypeStruct((t, d // 2), jnp.int32),
              jax.ShapeDtypeStruct((2 * TOP_K, t), jnp.int32), jax.ShapeDtypeStruct((t, LANES), F32),
              jax.ShapeDtypeStruct((N_EXPERTS, LANES), F32)]
    specs = [pl.BlockSpec((tm, d), lambda i, r: (i, 0)), pl.BlockSpec((tm, d // 2), lambda i, r: (i, 0)),
             pl.BlockSpec((2 * TOP_K, tm), lambda i, r: (0, i)), pl.BlockSpec((tm, LANES), lambda i, r: (i, 0)),
             pl.BlockSpec((N_EXPERTS, LANES), lambda i, r: (0, 0))]
    return shapes, specs


_TAIL_SCRATCH = [pltpu.VMEM((N_EXPERTS, 1), F32)]


def _post_ab(x, ya_p, yb_p, ya_s, yb_s, w_out, mod3, mod_base, rows, ln_g, ln_b, w_r, b_r, alpha):
    n_ctx_tiles = ya_p.shape[0] // TM
    (xc, xl), x_specs = _group_specs(x, n_ctx_tiles)
    d = xc.shape[1]
    nt = rows.shape[0]
    t = nt * TM
    a_w, q_w = ya_p.shape[1], yb_p.shape[1]
    mod_spec = lambda j: pl.BlockSpec((None, 1, d), lambda i, r: (mod_base + j * 16 + r[i], 0, 0))
    whole = lambda shp: pl.BlockSpec(shp, lambda i, r: (0,) * len(shp))
    ctx = lambda w: pl.BlockSpec((TM, w), lambda i, r: (jnp.minimum(i, n_ctx_tiles - 1), 0))
    lat = lambda w: pl.BlockSpec((TM, w), lambda i, r: (jnp.maximum(i - n_ctx_tiles, 0), 0))
    shapes, specs = _tail_out(t, d, TM)
    grid_spec = pltpu.PrefetchScalarGridSpec(
        num_scalar_prefetch=1, grid=(nt,),
        in_specs=x_specs + [ctx(a_w), ctx(q_w), lat(a_w), lat(q_w),
                            whole(w_out.shape), mod_spec(2), whole((1, d)), whole((1, d)), mod_spec(4), mod_spec(3),
                            whole(w_r.shape), whole(b_r.shape)],
        out_specs=specs,
        scratch_shapes=_TAIL_SCRATCH,
    )
    return pl.pallas_call(
        functools.partial(_post_ab_kernel, n_ctx_tiles=n_ctx_tiles, alpha=alpha),
        grid_spec=grid_spec, out_shape=shapes, compiler_params=_cparams(1), name="post_ab",
    )(rows, xc, xl, ya_p, yb_p, ya_s, yb_s, w_out, mod3, ln_g, ln_b, mod3, mod3, w_r, b_r)


def _unpack_f32_pairs(p):
    u = pltpu.bitcast(p, jnp.uint32)
    return jnp.concatenate([pltpu.bitcast(u << 16, F32), pltpu.bitcast(u & jnp.uint32(0xFFFF0000), F32)], axis=1)


def _ffn_residual(x1, y_packed, gate, g_f, ln_g, ln_b, alpha):
    y = gate[:, 0:1] * _unpack_f32_pairs(y_packed[0])
    for k in range(1, TOP_K):
        y = y + gate[:, k:k + 1] * _unpack_f32_pairs(y_packed[k])
    return _ln(alpha * x1 + g_f * y) * ln_g + ln_b


def _gmlp_kernel(rows_ref, xp_ref, y0_ref, y1_ref, y2_ref, y3_ref, pgate_ref, pgf_ref, plg_ref, plb_ref,
                 scm_ref, shm_ref, win_ref, bin_ref, gv_ref, bv_ref, wsp_ref, bsp_ref, wout_ref,
                 gm_ref, lg_ref, lb_ref, scf_ref, shf_ref, wr_ref, br_ref, x1_ref, hf_ref, route_ref, gate_ref,
                 cnt_ref, carry_ref, *, alpha):
    del rows_ref
    assert (xp_ref.shape[0] // 2) % CHUNK == 0
    c_w = win_ref.shape[1] // 2
    gd = c_w // C_GROUPS

    def norm_in(s):
        rs = s["rows"]
        x = _ffn_residual(xp_ref[rs, :], [r[rs, :] for r in (y0_ref, y1_ref, y2_ref, y3_ref)], pgate_ref[rs, :],
                          pgf_ref[...], plg_ref[...], plb_ref[...], alpha)
        return dict(s, x=x, h=(_ln(x) * (1.0 + scm_ref[...]) + shm_ref[...]).astype(BF16))

    def proj_in(s):
        return dict(s, z=_dot(s["h"], win_ref[...]) + bin_ref[...])

    def gate_split(s):
        z = s["z"]
        z = 0.5 * z * (1.0 + lax.erf(z * (2.0 ** -0.5)))
        return dict(s, u=z[:, :c_w], v=(_ln(z[:, c_w:]) * gv_ref[...] + bv_ref[...]).astype(BF16))

    def spatial(s):
        v = s["v"]
        chunks = []
        for n in range(v.shape[0] // CHUNK):
            groups = [_dot(wsp_ref[g], v[n * CHUNK:(n + 1) * CHUNK, g * gd:(g + 1) * gd]) + bsp_ref[g]
                      for g in range(C_GROUPS)]
            chunks.append(jnp.concatenate(groups, axis=1))
        return dict(s, t=(s["u"] * jnp.concatenate(chunks, axis=0)).astype(BF16))

    def proj_out(s):
        return dict(s, y=_dot(s["t"], wout_ref[...]))

    def tail(s):
        x1, hf, lg = _tail_rows(s["x"], s["y"], gm_ref[...], lg_ref[...], lb_ref[...], scf_ref[...], shf_ref[...],
                                wr_ref[...], br_ref[...], alpha)
        x1_ref[s["rows"], :] = x1
        hf_ref[s["rows"], :] = hf
        return dict(s, logits=lg)

    states = _run_skewed([norm_in, proj_in, gate_split, spatial, proj_out, tail],
                         [dict(rows=rs) for rs in _row_chains(xp_ref.shape[0])])
    _tail_route(jnp.concatenate([s["logits"] for s in states], axis=1), route_ref, gate_ref, cnt_ref, carry_ref)


def _gmlp(prev, mod3, mod_base, rows, w_in, b_in, g_v, b_v, w_sp, b_sp, w_out, ln_g, ln_b, w_r, b_r, alpha):
    x1p, y4, pgate, plg, plb = prev
    t, d = x1p.shape
    nt = rows.shape[0]
    tm = t // nt
    mod_spec = lambda j, base=mod_base: pl.BlockSpec((None, 1, d), lambda i, r: (base + j * 16 + r[i], 0, 0))
    whole = lambda shp: pl.BlockSpec(shp, lambda i, r: (0,) * len(shp))
    y_spec = lambda k: pl.BlockSpec((None, tm, d // 2), lambda i, r: (k, i, 0))
    shapes, specs = _tail_out(t, d, tm)
    grid_spec = pltpu.PrefetchScalarGridSpec(
        num_scalar_prefetch=1, grid=(nt,),
        in_specs=[pl.BlockSpec((tm, d), lambda i, r: (i, 0)), y_spec(0), y_spec(1), y_spec(2), y_spec(3),
                  pl.BlockSpec((tm, LANES), lambda i, r: (i, 0)), mod_spec(5, mod_base - 6 * 16),
                  whole((1, d)), whole((1, d)), mod_spec(1), mod_spec(0),
                  whole(w_in.shape), whole(b_in.shape), whole(g_v.shape), whole(b_v.shape),
                  whole(w_sp.shape), whole(b_sp.shape), whole(w_out.shape),
                  mod_spec(2), whole((1, d)), whole((1, d)), mod_spec(4), mod_spec(3),
                  whole(w_r.shape), whole(b_r.shape)],
        out_specs=specs,
        scratch_shapes=_TAIL_SCRATCH,
    )
    return pl.pallas_call(
        functools.partial(_gmlp_kernel, alpha=alpha),
        grid_spec=grid_spec, out_shape=shapes, compiler_params=_cparams(1, VMEM_LIMIT_GMLP), name="gmlp",
    )(rows, x1p, y4, y4, y4, y4, pgate, mod3, plg, plb, mod3, mod3, w_in, b_in, g_v, b_v, w_sp, b_sp, w_out,
      mod3, ln_g, ln_b, mod3, mod3, w_r, b_r)


def _moe_kernel(be_ref, bf_ref, nx_ref, sl_ref, hf_ref, nu_ref, x_ref, wgu_hbm, bgu_ref, wdn_hbm, bdn_ref, o_ref,
                wgu_f, wdn_f, wgu_s, wdn_s, sem, *, layer):
    i = pl.program_id(0)
    d_ff = wdn_s.shape[0]

    def weight_copies(e, slot):
        return (pltpu.make_async_copy(wgu_hbm.at[layer, e], wgu_f.at[slot], sem.at[0, slot]),
                pltpu.make_async_copy(wdn_hbm.at[layer, e], wdn_f.at[slot], sem.at[1, slot]))

    @pl.when(i < nu_ref[0])
    def _():
        @pl.when(bf_ref[i] == 1)
        def _():
            slot = sl_ref[i]

            @pl.when(i == 0)
            def _():
                for c in weight_copies(be_ref[i], slot):
                    c.start()

            for c in weight_copies(be_ref[i], slot):
                c.wait()
            wgu_s[...] = wgu_f[slot].astype(BF16)
            wdn_s[...] = wdn_f[slot].astype(BF16)

            @pl.when(nx_ref[i] >= 0)
            def _():
                for c in weight_copies(nx_ref[i], 1 - slot):
                    c.start()

        def up(s):
            return dict(s, gu=_dot(_unpack_bf16_pairs(x_ref[s["rows"], :]), wgu_s[...]) + bgu_ref[...])

        def act(s):
            gu = s["gu"]
            gate = jnp.minimum(gu[:, :d_ff], SWIGLU_LIMIT)
            lin = jnp.clip(gu[:, d_ff:], -SWIGLU_LIMIT, SWIGLU_LIMIT)
            glu = gate * jax.nn.sigmoid(SWIGLU_ALPHA * gate)
            return dict(s, gu=None, hid=((lin + 1.0) * glu).astype(BF16))

        def down(s):
            o_ref[s["rows"], :] = _pack_bf16_pairs(_dot(s["hid"], wdn_s[...]) + bdn_ref[...])
            return dict(s, hid=None)

        def run(n_chains, n_rows):
            _run_skewed([up, act, down], [dict(rows=rs) for rs in _row_chains(n_rows, n_chains)])

        @pl.when(hf_ref[i] == 0)
        def _():
            run(2, TM_MOE)

        @pl.when(hf_ref[i] == 1)
        def _():
            run(1, TM_MOE // 2)


def _moe(xs, plan, w_gu, b_gu, w_dn, b_dn, layer):
    n_rows = xs.shape[0]
    depth, n_e, d, ff2 = w_gu.shape
    d_ff = ff2 // 2
    nt = n_rows // TM_MOE
    n_plan = len(plan)
    row_spec = pl.BlockSpec((TM_MOE, d // 2), lambda i, *p: (jnp.minimum(i, p[-1][0] - 1), 0))
    bias_spec = lambda w: pl.BlockSpec((None, None, 1, w), lambda i, *p: (layer, p[0][i], 0, 0))
    grid_spec = pltpu.PrefetchScalarGridSpec(
        num_scalar_prefetch=n_plan, grid=(nt,),
        in_specs=[row_spec, pl.BlockSpec(memory_space=pl.ANY), bias_spec(ff2),
                  pl.BlockSpec(memory_space=pl.ANY), bias_spec(d)],
        out_specs=row_spec,
        scratch_shapes=[pltpu.VMEM((2, d, ff2), F32), pltpu.VMEM((2, d_ff, d), F32),
                        pltpu.VMEM((d, ff2), BF16), pltpu.VMEM((d_ff, d), BF16),
                        pltpu.SemaphoreType.DMA((2, 2))],
    )
    return pl.pallas_call(
        functools.partial(_moe_kernel, layer=layer), grid_spec=grid_spec,
        out_shape=jax.ShapeDtypeStruct((n_rows, d // 2), jnp.int32),
        compiler_params=_cparams(1), name="moe_experts",
    )(*plan, xs, w_gu, b_gu.reshape(depth, n_e, 1, ff2), w_dn, b_dn.reshape(depth, n_e, 1, d))


def _combine_kernel(rows_ref, x_ref, y0_ref, y1_ref, y2_ref, y3_ref, gate_ref, gf_ref, lg_ref, lb_ref, *o_refs,
                    alpha, n_ctx_tiles):
    del rows_ref
    out = _ffn_residual(x_ref[...], [r[...] for r in (y0_ref, y1_ref, y2_ref, y3_ref)], gate_ref[...], gf_ref[...],
                        lg_ref[...], lb_ref[...], alpha)
    if n_ctx_tiles is None:
        o_refs[0][...] = out
    else:
        @pl.when(pl.program_id(0) < n_ctx_tiles)
        def _():
            o_refs[0][...] = out

        @pl.when(pl.program_id(0) >= n_ctx_tiles)
        def _():
            o_refs[1][...] = out


def _combine(x1, y4, gate, mod3, mod_base, rows, ln_g, ln_b, alpha, t_ctx=None):
    t, d = x1.shape
    nt = rows.shape[0]
    tm = t // nt
    y_spec = lambda k: pl.BlockSpec((None, tm, d // 2), lambda i, r: (k, i, 0))
    if t_ctx is None:
        n_ctx_tiles = None
        out_specs = pl.BlockSpec((tm, d), lambda i, r: (i, 0))
        out_shape = jax.ShapeDtypeStruct((t, d), F32)
    else:
        n_ctx_tiles = t_ctx // tm
        out_specs = [pl.BlockSpec((tm, d), lambda i, r: (jnp.minimum(i, n_ctx_tiles - 1), 0)),
                     pl.BlockSpec((tm, d), lambda i, r: (jnp.maximum(i - n_ctx_tiles, 0), 0))]
        out_shape = [jax.ShapeDtypeStruct((t_ctx, d), F32), jax.ShapeDtypeStruct((t - t_ctx, d), F32)]
    grid_spec = pltpu.PrefetchScalarGridSpec(
        num_scalar_prefetch=1, grid=(nt,),
        in_specs=[pl.BlockSpec((tm, d), lambda i, r: (i, 0)), y_spec(0), y_spec(1), y_spec(2), y_spec(3),
                  pl.BlockSpec((tm, LANES), lambda i, r: (i, 0)),
                  pl.BlockSpec((None, 1, d), lambda i, r: (mod_base + 5 * 16 + r[i], 0, 0)),
                  pl.BlockSpec((1, d), lambda i, r: (0, 0)), pl.BlockSpec((1, d), lambda i, r: (0, 0))],
        out_specs=out_specs,
    )
    return pl.pallas_call(
        functools.partial(_combine_kernel, alpha=alpha, n_ctx_tiles=n_ctx_tiles), grid_spec=grid_spec,
        out_shape=out_shape, compiler_params=_cparams(1), name="moe_combine",
    )(rows, x1, y4, y4, y4, y4, gate, mod3, ln_g, ln_b)


def _route_plan(route, counts, n_tiles):
    idx, rank = route[:TOP_K], route[TOP_K:]
    padded = (counts + TM_MOE - 1) // TM_MOE * TM_MOE
    pend = jnp.cumsum(padded)
    pstart = pend - padded
    onehot = idx[:, :, None] == jnp.arange(N_EXPERTS, dtype=jnp.int32)[None, None, :]
    dest = jnp.sum(jnp.where(onehot, pstart[None, None, :], 0), axis=-1) + rank
    n_used = (pend[-1] // TM_MOE).astype(jnp.int32)
    tile_start = jnp.arange(n_tiles, dtype=jnp.int32) * TM_MOE
    blk_e = jnp.sum((tile_start[:, None] >= pend[None, :]).astype(jnp.int32), axis=1)
    last_e = jnp.sum((jnp.maximum(n_used - 1, 0) * TM_MOE >= pend).astype(jnp.int32))
    blk_e = jnp.minimum(jnp.where(jnp.arange(n_tiles) < n_used, blk_e, last_e), N_EXPERTS - 1).astype(jnp.int32)
    blk_first = jnp.concatenate([jnp.ones((1,), jnp.int32), (blk_e[1:] != blk_e[:-1]).astype(jnp.int32)])
    ar = jnp.arange(N_EXPERTS, dtype=jnp.int32)
    later = jnp.logical_and((counts > 0)[None, :], ar[None, :] > ar[:, None])
    next_e = jnp.min(jnp.where(later, ar[None, :], N_EXPERTS), axis=1)
    next_e = jnp.where(next_e == N_EXPERTS, -1, next_e)
    blk_next = jnp.sum(jnp.where(blk_e[:, None] == ar[None, :], next_e[None, :], 0), axis=1).astype(jnp.int32)
    blk_slot = ((jnp.cumsum(blk_first) - 1) % 2).astype(jnp.int32)
    left = jnp.sum(jnp.where(blk_e[:, None] == ar[None, :], (pstart + counts)[None, :], 0), axis=1) - tile_start
    blk_half = (left <= TM_MOE // 2).astype(jnp.int32)
    return dest, (blk_e, blk_first, blk_next, blk_slot, blk_half, n_used.reshape(1))


SC_CORES = 2
SC_SUBCORES = 16


def _sc_mesh():
    return plsc.VectorSubcoreMesh(core_axis_name="core", subcore_axis_name="subcore")


def _sc_scatter_rows(src, dest, n_rows, window):
    n_slots, t = dest.shape
    w = src.shape[1]
    per_worker = t // (SC_CORES * SC_SUBCORES)
    assert per_worker * SC_CORES * SC_SUBCORES == t and per_worker % window == 0

    @functools.partial(pl.kernel, out_type=jax.ShapeDtypeStruct((n_rows, w), src.dtype), mesh=_sc_mesh(),
                       scratch_types=[pltpu.VMEM((n_slots, window), jnp.int32), pltpu.VMEM((window, w), src.dtype)],
                       name="sc_dispatch")
    def scatter(src_hbm, dest_hbm, out_hbm, idx_v, rows_v):
        worker = lax.axis_index("subcore") * SC_CORES + lax.axis_index("core")

        @pl.loop(0, per_worker // window)
        def _(c):
            base = pl.multiple_of(worker * per_worker + c * window, window)
            pltpu.sync_copy(src_hbm.at[pl.ds(base, window)], rows_v)
            pltpu.sync_copy(dest_hbm.at[:, pl.ds(base, window)], idx_v)
            for k in range(n_slots):
                pltpu.sync_copy(rows_v, out_hbm.at[idx_v.at[k]])

    return scatter(src, dest)


def _sc_gather_rows(table, idx, window):
    n = idx.shape[0]
    w = table.shape[1]
    per_worker = n // (SC_CORES * SC_SUBCORES)
    assert per_worker * SC_CORES * SC_SUBCORES == n and per_worker % window == 0

    @functools.partial(pl.kernel, out_type=jax.ShapeDtypeStruct((n, w), table.dtype), mesh=_sc_mesh(),
                       scratch_types=[pltpu.VMEM((window,), jnp.int32), pltpu.VMEM((window, w), table.dtype)],
                       name="sc_collect")
    def gather(table_hbm, idx_hbm, out_hbm, idx_v, rows_v):
        worker = lax.axis_index("subcore") * SC_CORES + lax.axis_index("core")

        @pl.loop(0, per_worker // window)
        def _(c):
            base = pl.multiple_of(worker * per_worker + c * window, window)
            pltpu.sync_copy(idx_hbm.at[pl.ds(base, window)], idx_v)
            pltpu.sync_copy(table_hbm.at[idx_v], rows_v)
            pltpu.sync_copy(rows_v, out_hbm.at[pl.ds(base, window)])

    return gather(table, idx)


def _dft_tables(n):
    k = jnp.arange(n, dtype=jnp.int32)

    def trig(rows):
        ang = ((rows[:, None] * k[None, :]) % n).astype(F32) * (2.0 * math.pi / n)
        return jnp.cos(ang), jnp.sin(ang)

    n2 = 64
    if n < 4 * n2:
        return trig(k)
    ca, sa = (z[:, None, :] for z in trig(jnp.arange(n // n2, dtype=jnp.int32) * n2))
    cb, sb = (z[None, :, :] for z in trig(jnp.arange(n2, dtype=jnp.int32)))
    return (ca * cb - sa * sb).reshape(n, n), (sa * cb + ca * sb).reshape(n, n)


def _rope_tables(n_lat, n_ctx_rows):
    half, quarter = HEAD_DIM // 2, HEAD_DIM // 4
    tpos = jnp.arange(n_lat, dtype=jnp.int32)
    lane = jnp.arange(LANES, dtype=jnp.int32) % HEAD_DIM
    pos = jnp.where(lane[None, :] < half, (tpos // GRID_W)[:, None], (tpos % GRID_W)[:, None]).astype(F32)
    fidx = (lane % quarter).astype(F32)
    freqs = ROPE_THETA ** (-fidx / quarter)
    ang = pos * freqs[None, :]
    cos, sin = jnp.cos(ang), jnp.sin(ang)
    first = (lane % half) < quarter
    sa = jnp.where(first[None, :], -sin, 0.0)
    sb = jnp.where(first[None, :], 0.0, sin)
    ident = lambda v: jnp.full((n_ctx_rows, LANES), v, F32)
    return (jnp.concatenate([ident(1.0), cos]), jnp.concatenate([ident(0.0), sa]), jnp.concatenate([ident(0.0), sb]))


def kernel(x_prompt, x_sample, cache_k_ab, cache_v_ab, c, c_ctx, w_ada, b_ada, ln_mix_g, ln_mix_b, ln_ffn_g, ln_ffn_b, w_in_ab, w_out_ab, sink_ab, w_in_c, b_in_c, ln_v_g, ln_v_b, w_sp, b_sp, w_out_c, w_router, b_router, w_gu, b_gu, w_dn, b_dn):
    n_ctx_b, ctx_seq, d = x_prompt.shape
    n_lat_b, lat_seq, _ = x_sample.shape
    depth = w_ada.shape[0]
    t_ctx, t_lat = n_ctx_b * ctx_seq, n_lat_b * lat_seq
    t = t_ctx + t_lat
    alpha = (2 * depth) ** 0.25
    kv_w = N_KV_HEADS * HEAD_DIM
    a_w = d // 4
    q_w = d - a_w
    assert t_ctx % TM == 0 and lat_seq % TM == 0 and t_ctx % lat_seq == 0 and n_lat_b + 1 <= 16

    cond_rows = lambda tm: jnp.asarray(np.concatenate(
        [np.zeros(t_ctx // tm, np.int32), 1 + np.arange(t_lat // tm, dtype=np.int32) // (lat_seq // tm)]))
    rows, rows_c = cond_rows(TM), cond_rows(TM_COMBINE)
    rblk = jnp.asarray(np.concatenate(
        [np.zeros(t_ctx // TM, np.int32), 1 + np.arange(t_lat // TM, dtype=np.int32) % (lat_seq // TM)]))

    cond16 = jnp.zeros((16, d), F32).at[0].set(c_ctx).at[1:1 + n_lat_b].set(c)
    mod = _ada(cond16, w_ada, b_ada)
    mod3 = mod.reshape(depth, 16, 6, d).transpose(0, 2, 1, 3).reshape(depth * 6 * 16, 1, d)

    cos, sa, sb = _rope_tables(lat_seq, TM)
    cs_ctx, ss_ctx = (z.astype(BF16) for z in _dft_tables(ctx_seq))
    cs_lat, ss_lat = (z.astype(BF16) for z in _dft_tables(lat_seq))
    gd = a_w // A_GROUPS
    cd, sd = _dft_tables(gd)
    eye = jnp.eye(A_GROUPS, dtype=F32)
    bdc, bds = jnp.kron(eye, cd).astype(BF16), jnp.kron(eye, sd).astype(BF16)

    past = cache_k_ab.shape[2]
    cache_k = cache_k_ab.reshape(n_lat_b, -1, past, kv_w).astype(BF16)
    cache_v = cache_v_ab.reshape(n_lat_b, -1, past, kv_w).astype(BF16)

    x = (x_prompt.reshape(t_ctx, d), x_sample.reshape(t_lat, d))
    row2 = lambda v: v.reshape(1, -1)
    ks, vs = [], []
    for l in range(depth):
        j = l // 2
        mod_base = l * 6 * 16
        w_r, b_r = w_router[l].T, b_router[l].reshape(-1, 1)
        if l % 2 == 0:
            ac, as_, q, k32, v32, k, v = _inproj(x, t_ctx // TM, mod3, mod_base, rows, rblk, w_in_ab[j].astype(BF16),
                                                 cos, sa, sb, bdc, bds, a_w, q_w, kv_w)
            ks.append(k32[:t_ctx].reshape(n_ctx_b, ctx_seq, N_KV_HEADS, HEAD_DIM))
            vs.append(v32[:t_ctx].reshape(n_ctx_b, ctx_seq, N_KV_HEADS, HEAD_DIM))
            ya_p = _fourier(ac, as_, cs_ctx, ss_ctx, n_ctx_b, ctx_seq, 0, min(ctx_seq, 512))
            ya_s = _fourier(ac, as_, cs_lat, ss_lat, n_lat_b, lat_seq, t_ctx // lat_seq, min(lat_seq, 512))
            yb_p = _attn_ctx(sink_ab[j], q, k, v, n_ctx_b, ctx_seq)
            yb_s = _attn_lat(sink_ab[j], q, k, v, cache_k, cache_v, j, n_lat_b, lat_seq, t_ctx)
            x1, hf, route, gate, cnt = _post_ab(x, ya_p, yb_p, ya_s, yb_s, w_out_ab[j].astype(BF16), mod3, mod_base,
                                                rows, row2(ln_mix_g[l]), row2(ln_mix_b[l]), w_r, b_r, alpha)
        else:
            x1, hf, route, gate, cnt = _gmlp(prev, mod3, mod_base, rows, w_in_c[j].astype(BF16), row2(b_in_c[j]),
                                             row2(ln_v_g[j]), row2(ln_v_b[j]), w_sp[j].astype(BF16),
                                             b_sp[j][:, :, None], w_out_c[j].astype(BF16), row2(ln_mix_g[l]),
                                             row2(ln_mix_b[l]), w_r, b_r, alpha)
        n_rows = t * TOP_K + N_EXPERTS * TM_MOE
        dest, plan = _route_plan(route, cnt[:, 0].astype(jnp.int32), n_rows // TM_MOE)
        xs = _sc_scatter_rows(hf, dest, n_rows, 128)
        out_sorted = _moe(xs, plan, w_gu, b_gu, w_dn, b_dn, l)
        y4 = _sc_gather_rows(out_sorted, dest.reshape(-1), 128).reshape(TOP_K, t, d // 2)
        if l % 2 == 0 and l + 1 < depth:
            prev = (x1, y4, gate, row2(ln_ffn_g[l]), row2(ln_ffn_b[l]))
        else:
            x = _combine(x1, y4, gate, mod3, mod_base, rows_c, row2(ln_ffn_g[l]), row2(ln_ffn_b[l]), alpha,
                         t_ctx if l == depth - 1 else None)

    y_prompt = x[0].reshape(n_ctx_b, ctx_seq, d)
    y_sample = x[1].reshape(n_lat_b, lat_seq, d)
    return (y_prompt, y_sample, jnp.stack(ks, axis=1), jnp.stack(vs, axis=1))
```

```python
import functools
import math

import numpy as np
import jax
import jax.numpy as jnp
from jax import lax
from jax.experimental import pallas as pl
from jax.experimental.pallas import tpu as pltpu
from jax.experimental.pallas import tpu_sc as plsc

GRID_W = 64
BLK = 128
WINDOW = 128
HEAD_DIM = 64
A_GROUPS = 4
N_KV_HEADS = 4
C_GROUPS = 8
CHUNK = 128
N_EXPERTS = 32
TOP_K = 4
SWIGLU_LIMIT = 7.0
SWIGLU_ALPHA = 1.702
ROPE_THETA = 10000.0
LN_EPS = 1e-6
NEG_INF = -1e30
LOG2_E = math.log2(math.e)

LANES = 128
TM = 1024
TM_COMBINE = 512
TM_MOE = 512
VMEM_LIMIT = 52 * 1024 * 1024
VMEM_LIMIT_GMLP = 60 * 1024 * 1024

F32 = jnp.float32
BF16 = jnp.bfloat16


def _cparams(n_axes, vmem_limit=VMEM_LIMIT):
    return pltpu.CompilerParams(dimension_semantics=("arbitrary",) * n_axes, vmem_limit_bytes=vmem_limit)


def _ln(x):
    mu = jnp.mean(x, axis=-1, keepdims=True)
    xc = x - mu
    var = jnp.mean(xc * xc, axis=-1, keepdims=True)
    return xc * lax.rsqrt(var + LN_EPS)


def _dot(a, b):
    return jnp.dot(a, b, preferred_element_type=F32)


def _dot_nt(a, b):
    return lax.dot_general(a, b, (((1,), (1,)), ((), ())), preferred_element_type=F32)


def _split(a):
    hi = a.astype(BF16)
    lo = (a - hi.astype(F32)).astype(BF16)
    return hi, lo


def _dot_3pass(a, b):
    a_hi, a_lo = _split(a)
    b_hi, b_lo = _split(b)
    return _dot(a_hi, b_hi) + (_dot(a_hi, b_lo) + _dot(a_lo, b_hi))


def _lane_select(cols, width):
    m = cols[0].shape[0]
    lane = lax.broadcasted_iota(jnp.int32, (m, width), 1)
    out = jnp.zeros((m, width), cols[0].dtype)
    for j, c in enumerate(cols):
        out = jnp.where(lane == j, c, out)
    return out


def _pack_bf16_pairs(v):
    n = v.shape[1] // 2
    lo = pltpu.bitcast(v[:, :n].astype(BF16).astype(F32), jnp.uint32) >> 16
    hi = pltpu.bitcast(v[:, n:].astype(BF16).astype(F32), jnp.uint32) & jnp.uint32(0xFFFF0000)
    return pltpu.bitcast(lo | hi, jnp.int32)


def _unpack_bf16_pairs(p):
    u = pltpu.bitcast(p, jnp.uint32)
    lo = pltpu.bitcast(u << 16, F32)
    hi = pltpu.bitcast(u & jnp.uint32(0xFFFF0000), F32)
    return jnp.concatenate([lo, hi], axis=1).astype(BF16)


def _row_chains(n_rows, n_chains=2):
    step = n_rows // n_chains
    return [slice(c * step, (c + 1) * step) for c in range(n_chains)]


def _run_skewed(stages, states):
    states = list(states)
    for step in range(len(stages) + len(states) - 1):
        for c in range(len(states)):
            if 0 <= step - c < len(stages):
                states[c] = stages[step - c](states[c])
    return states


def _tail_rows(x, y, g_m, ln_g, ln_b, sc_f, sh_f, w_r, b_r, alpha):
    x1 = _ln(alpha * x + g_m * y) * ln_g + ln_b
    hf = _ln(x1) * (1.0 + sc_f) + sh_f
    hf_hi, hf_lo = _split(hf)
    w_hi, w_lo = _split(w_r)
    logits = _dot_nt(w_hi, hf_hi) + (_dot_nt(w_hi, hf_lo) + _dot_nt(w_lo, hf_hi)) + b_r
    return x1, _pack_bf16_pairs(hf), logits


def _tail_route(logits, route_ref, gate_ref, cnt_ref, carry_ref):
    @pl.when(pl.program_id(0) == 0)
    def _():
        carry_ref[...] = jnp.zeros_like(carry_ref)

    tm = logits.shape[1]
    sub = lax.broadcasted_iota(jnp.int32, logits.shape, 0)
    vals = logits
    top_v, top_i = [], []
    for _ in range(TOP_K):
        m = jnp.max(vals, axis=0, keepdims=True)
        am = jnp.min(jnp.where(vals == m, sub, N_EXPERTS), axis=0, keepdims=True)
        top_v.append(m)
        top_i.append(am)
        vals = jnp.where(sub == am, -jnp.inf, vals)
    e = [jnp.exp(v - top_v[0]) for v in top_v]
    denom = e[0] + e[1] + e[2] + e[3]
    gates_t = jnp.concatenate([ek / denom for ek in e] + [jnp.zeros((LANES - TOP_K, tm), F32)], axis=0)
    gate_ref[...] = gates_t.T

    member = jnp.zeros(logits.shape, F32)
    for am in top_i:
        member = jnp.where(sub == am, 1.0, member)
    r_i = lax.broadcasted_iota(jnp.int32, (tm, tm), 0)
    c_i = lax.broadcasted_iota(jnp.int32, (tm, tm), 1)
    earlier = jnp.where(r_i < c_i, 1.0, 0.0).astype(BF16)
    before = _dot(member.astype(BF16), earlier) + carry_ref[...]
    ranks = [jnp.sum(jnp.where(sub == am, before, 0.0), axis=0, keepdims=True).astype(jnp.int32) for am in top_i]
    route_ref[...] = jnp.concatenate(top_i + ranks, axis=0)
    carry = carry_ref[...] + jnp.sum(member, axis=1, keepdims=True)
    carry_ref[...] = carry
    cnt_ref[...] = jnp.broadcast_to(carry, cnt_ref.shape)


def _ada_kernel(cond_ref, w_ref, b_ref, o_ref):
    c = cond_ref[...]
    s = (c * jax.nn.sigmoid(c)).astype(BF16)
    o_ref[...] = _dot(s, w_ref[...].astype(BF16)) + b_ref[...]


def _ada(cond16, w_ada, b_ada):
    depth, d, n = w_ada.shape
    tn = 1536
    return pl.pallas_call(
        _ada_kernel,
        grid=(depth, n // tn),
        in_specs=[
            pl.BlockSpec((16, d), lambda l, j: (0, 0)),
            pl.BlockSpec((None, d, tn), lambda l, j: (l, 0, j)),
            pl.BlockSpec((None, 1, tn), lambda l, j: (l, 0, j)),
        ],
        out_specs=pl.BlockSpec((None, 16, tn), lambda l, j: (l, 0, j)),
        out_shape=jax.ShapeDtypeStruct((depth, 16, n), F32),
        compiler_params=_cparams(2),
        name="ada_mod",
    )(cond16, w_ada, b_ada.reshape(depth, 1, n))


def _group_rows(xc_ref, xl_ref, rows, n_ctx_tiles):
    return jnp.where(pl.program_id(0) < n_ctx_tiles, xc_ref[rows, :], xl_ref[rows, :])


def _group_specs(x, n_ctx_tiles):
    if isinstance(x, tuple):
        arrays, lat = x, (lambda i, *_: (jnp.maximum(i - n_ctx_tiles, 0), 0))
    else:
        arrays, lat = (x, x), (lambda i, *_: (jnp.maximum(i, n_ctx_tiles), 0))
    d = arrays[0].shape[1]
    ctx = lambda i, *_: (jnp.minimum(i, n_ctx_tiles - 1), 0)
    return arrays, [pl.BlockSpec((TM, d), ctx), pl.BlockSpec((TM, d), lat)]


def _inproj_kernel(rows_ref, rblk_ref, xc_ref, xl_ref, sc_ref, sh_ref, w_ref, cos_ref, sa_ref, sb_ref, bdc_ref,
                   bds_ref, ac_ref, as_ref, q_ref, k_ref, v_ref, kb_ref, vb_ref, *, a_w, q_w, kv_w, n_ctx_tiles):
    del rows_ref, rblk_ref
    g = q_w // kv_w

    def norm_in(s):
        x = _group_rows(xc_ref, xl_ref, s["rows"], n_ctx_tiles)
        return dict(s, h=(_ln(x) * (1.0 + sc_ref[...]) + sh_ref[...]).astype(BF16))

    def proj(s):
        return dict(s, p=_dot(s["h"], w_ref[...]))

    def finish(s):
        rs, p = s["rows"], s["p"]
        a = p[:, :a_w].astype(BF16)
        ac_ref[rs, :] = _dot(a, bdc_ref[...]).astype(BF16)
        as_ref[rs, :] = _dot(a, bds_ref[...]).astype(BF16)
        cos, sa, sb = cos_ref[rs, :], sa_ref[rs, :], sb_ref[rs, :]

        def rope(t):
            w = t.shape[1]
            reps = w // LANES
            c, a_, b_ = (jnp.tile(z, (1, reps)) for z in (cos, sa, sb))
            nxt = pltpu.roll(t, w - HEAD_DIM // 4, 1)
            prv = pltpu.roll(t, HEAD_DIM // 4, 1)
            return t * c + nxt * a_ + prv * b_

        q = rope(p[:, a_w:a_w + q_w]) * (HEAD_DIM ** -0.5 * LOG2_E)
        lane = lax.broadcasted_iota(jnp.int32, (q.shape[0], LANES), 1)
        for j in range(q_w // HEAD_DIM):
            tile = q[:, (j // 2) * LANES:(j // 2 + 1) * LANES]
            dst_low = (j // g) % 2 == 0
            if (j % 2 == 0) != dst_low:
                tile = pltpu.roll(tile, HEAD_DIM, 1)
            keep = (lane < HEAD_DIM) if dst_low else (lane >= HEAD_DIM)
            q_ref[rs, j * LANES:(j + 1) * LANES] = jnp.where(keep, tile, 0.0).astype(BF16)
        k = rope(p[:, a_w + q_w:a_w + q_w + kv_w])
        v = p[:, a_w + q_w + kv_w:]
        k_ref[rs, :] = k
        v_ref[rs, :] = v
        kb_ref[rs, :] = k.astype(BF16)
        vb_ref[rs, :] = v.astype(BF16)
        return s

    _run_skewed([norm_in, proj, finish], [dict(rows=rs) for rs in _row_chains(xc_ref.shape[0])])


def _inproj(x, n_ctx_tiles, mod3, mod_base, rows, rblk, w_in, cos, sa, sb, bdc, bds, a_w, q_w, kv_w):
    (xc, xl), x_specs = _group_specs(x, n_ctx_tiles)
    d = xc.shape[1]
    t = rows.shape[0] * TM
    n = w_in.shape[1]
    nt = t // TM
    mod_spec = lambda j: pl.BlockSpec((None, 1, d), lambda i, r, rb: (mod_base + j * 16 + r[i], 0, 0))
    whole = lambda shp: pl.BlockSpec(shp, lambda i, r, rb: (0,) * len(shp))
    rope_spec = pl.BlockSpec((TM, LANES), lambda i, r, rb: (rb[i], 0))
    tok = lambda w: pl.BlockSpec((TM, w), lambda i, r, rb: (i, 0))
    grid_spec = pltpu.PrefetchScalarGridSpec(
        num_scalar_prefetch=2, grid=(nt,),
        in_specs=x_specs + [mod_spec(1), mod_spec(0), whole((d, n)), rope_spec, rope_spec, rope_spec,
                            whole((a_w, a_w)), whole((a_w, a_w))],
        out_specs=[tok(a_w), tok(a_w), tok(2 * q_w), tok(kv_w), tok(kv_w), tok(kv_w), tok(kv_w)],
    )
    return pl.pallas_call(
        functools.partial(_inproj_kernel, a_w=a_w, q_w=q_w, kv_w=kv_w, n_ctx_tiles=n_ctx_tiles),
        grid_spec=grid_spec,
        out_shape=[jax.ShapeDtypeStruct((t, a_w), BF16), jax.ShapeDtypeStruct((t, a_w), BF16),
                   jax.ShapeDtypeStruct((t, 2 * q_w), BF16), jax.ShapeDtypeStruct((t, kv_w), F32),
                   jax.ShapeDtypeStruct((t, kv_w), F32), jax.ShapeDtypeStruct((t, kv_w), BF16),
                   jax.ShapeDtypeStruct((t, kv_w), BF16)],
        compiler_params=_cparams(1),
        name="inproj_ab",
    )(rows, rblk, xc, xl, mod3, mod3, w_in, cos, sa, sb, bdc, bds)


def _fourier_kernel(c_ref, s_ref, ac_ref, as_ref, o_ref, *, scale):
    y = _dot(c_ref[...], ac_ref[...]) - _dot(s_ref[...], as_ref[...])
    o_ref[...] = (y * scale).astype(BF16)


def _fourier(ac, as_, cs, ss, n_batch, seq, blk_off, tm):
    a_w = ac.shape[1]
    nr = seq // tm
    a_spec = pl.BlockSpec((seq, a_w), lambda r, b: (blk_off + b, 0))
    t_spec = pl.BlockSpec((tm, seq), lambda r, b: (r, 0))
    return pl.pallas_call(
        functools.partial(_fourier_kernel, scale=(seq * (a_w // A_GROUPS)) ** -0.5),
        grid=(nr, n_batch),
        in_specs=[t_spec, t_spec, a_spec, a_spec],
        out_specs=pl.BlockSpec((tm, a_w), lambda r, b: (b * nr + r, 0)),
        out_shape=jax.ShapeDtypeStruct((n_batch * seq, a_w), BF16),
        compiler_params=_cparams(2),
        name="fourier_%d" % seq,
    )(cs, ss, ac, as_)


def _attend_pairs(sink_ref, q_ref, keys, values, mask, o_ref, *, g):
    rows = q_ref.shape[0]
    lane = lax.broadcasted_iota(jnp.int32, (rows, LANES), 1)
    full_mask = None if mask is None else jnp.concatenate([mask] * g, axis=0)

    def scores(s):
        h0, p = s["h0"], s["h0"] // (2 * g)
        qp = jnp.concatenate([q_ref[:, (h0 + j) * LANES:(h0 + j + 1) * LANES] for j in range(g)], axis=0)
        sc = [_dot_nt(qp, kf(p)) for kf in keys]
        if full_mask is not None:
            sc[0] = jnp.where(full_mask, sc[0], NEG_INF)
        return dict(s, sc=sc)

    def row_max(s):
        m = jnp.concatenate([jnp.full((rows, 1), sink_ref[s["h0"] + j] * LOG2_E, F32) for j in range(g)], axis=0)
        sink_col = m
        for sc in s["sc"]:
            m = jnp.maximum(m, jnp.max(sc, axis=-1, keepdims=True))
        return dict(s, m=m, sink=jnp.exp2(sink_col - m))

    def weights(s):
        return dict(s, e=[jnp.exp2(sc - s["m"]).astype(BF16) for sc in s["sc"]], sc=None)

    def weighted_values(s):
        p = s["h0"] // (2 * g)
        acc = None
        for e, vf in zip(s["e"], values):
            v = vf(p)
            pv = _dot(e, jnp.concatenate([v, jnp.ones_like(v)], axis=1))
            acc = pv if acc is None else acc + pv
        return dict(s, o=acc[:, :LANES] / (acc[:, LANES:LANES + 1] + s["sink"]), e=None)

    states = _run_skewed([scores, row_max, weights, weighted_values],
                         [dict(h0=h * g) for h in range(N_KV_HEADS)])
    heads = {}
    for s in states:
        for j in range(g):
            blk = s["o"][j * rows:(j + 1) * rows]
            head = s["h0"] + j
            if (((head // g) % 2 == 0) != (head % 2 == 0)):
                blk = pltpu.roll(blk, HEAD_DIM, 1)
            heads[head] = blk
    for t in range(len(heads) // 2):
        tile = jnp.where(lane < HEAD_DIM, heads[2 * t], heads[2 * t + 1])
        o_ref[:, t * LANES:(t + 1) * LANES] = tile.astype(BF16)


def _attn_ctx_kernel(sink_ref, q_ref, k_ref, v_ref, o_ref, *, g):
    pair = lambda ref: (lambda p: ref[:, p * LANES:(p + 1) * LANES])
    _attend_pairs(sink_ref, q_ref, [pair(k_ref)], [pair(v_ref)], None, o_ref, g=g)


def _attn_ctx(sink, q, k, v, n_batch, seq):
    qp_w, kv_w = q.shape[1], k.shape[1]
    q_w = qp_w // 2
    g = q_w // kv_w
    return pl.pallas_call(
        functools.partial(_attn_ctx_kernel, g=g),
        grid=(n_batch,),
        in_specs=[pl.BlockSpec(memory_space=pltpu.SMEM),
                  pl.BlockSpec((seq, qp_w), lambda b: (b, 0)),
                  pl.BlockSpec((seq, kv_w), lambda b: (b, 0)),
                  pl.BlockSpec((seq, kv_w), lambda b: (b, 0))],
        out_specs=pl.BlockSpec((seq, q_w), lambda b: (b, 0)),
        out_shape=jax.ShapeDtypeStruct((n_batch * seq, q_w), BF16),
        compiler_params=_cparams(1),
        name="attn_ctx",
    )(sink, q, k, v)


def _attn_lat_kernel(sink_ref, q_ref, k_ref, v_ref, ck_ref, cv_ref, o_ref, *, g, seq):
    i = pl.program_id(1)
    n_loc = 3 * BLK
    start = pl.multiple_of(jnp.clip((i - 1) * BLK, 0, seq - n_loc), BLK)
    row = lax.broadcasted_iota(jnp.int32, (BLK, n_loc), 0)
    col = lax.broadcasted_iota(jnp.int32, (BLK, n_loc), 1)
    band = jnp.abs(row + (i * BLK - start) - col) <= WINDOW
    loc = lambda ref: (lambda p: ref[pl.ds(start, n_loc), p * LANES:(p + 1) * LANES])
    ctx = lambda ref: (lambda p: ref[:, p * LANES:(p + 1) * LANES])
    _attend_pairs(sink_ref, q_ref, [loc(k_ref), ctx(ck_ref)], [loc(v_ref), ctx(cv_ref)], band, o_ref, g=g)


def _attn_lat(sink, q, k, v, cache_k, cache_v, layer_slot, n_batch, seq, tok_off):
    qp_w, kv_w = q.shape[1], k.shape[1]
    q_w = qp_w // 2
    g = q_w // kv_w
    nb = seq // BLK
    past = cache_k.shape[2]
    assert seq >= 3 * BLK
    kv_spec = pl.BlockSpec((seq, kv_w), lambda b, i: (tok_off // seq + b, 0))
    c_spec = pl.BlockSpec((None, None, past, kv_w), lambda b, i: (b, layer_slot, 0, 0))
    return pl.pallas_call(
        functools.partial(_attn_lat_kernel, g=g, seq=seq),
        grid=(n_batch, nb),
        in_specs=[pl.BlockSpec(memory_space=pltpu.SMEM),
                  pl.BlockSpec((BLK, qp_w), lambda b, i: (tok_off // BLK + b * nb + i, 0)),
                  kv_spec, kv_spec, c_spec, c_spec],
        out_specs=pl.BlockSpec((BLK, q_w), lambda b, i: (b * nb + i, 0)),
        out_shape=jax.ShapeDtypeStruct((n_batch * seq, q_w), BF16),
        compiler_params=_cparams(2),
        name="attn_lat",
    )(sink, q, k, v, cache_k, cache_v)


def _post_ab_kernel(rows_ref, xc_ref, xl_ref, yap_ref, ybp_ref, yas_ref, ybs_ref, w_ref, gm_ref, lg_ref, lb_ref,
                    scf_ref, shf_ref, wr_ref, br_ref, x1_ref, hf_ref, route_ref, gate_ref, cnt_ref, carry_ref,
                    *, n_ctx_tiles, alpha):
    del rows_ref
    is_ctx = pl.program_id(0) < n_ctx_tiles
    a_w = yap_ref.shape[1]

    def proj_out(s):
        rs = s["rows"]
        pick = lambda p_ref, s_ref: pltpu.bitcast(
            jnp.where(is_ctx, pltpu.bitcast(p_ref[rs, :], jnp.uint32), pltpu.bitcast(s_ref[rs, :], jnp.uint32)), BF16)
        ya = pick(yap_ref, yas_ref)
        yb = pick(ybp_ref, ybs_ref)
        return dict(s, y=_dot(ya, w_ref[:a_w, :]) + _dot(yb, w_ref[a_w:, :]))

    def tail(s):
        rs = s["rows"]
        x1, hf, lg = _tail_rows(_group_rows(xc_ref, xl_ref, rs, n_ctx_tiles), s["y"], gm_ref[...], lg_ref[...],
                                lb_ref[...], scf_ref[...], shf_ref[...], wr_ref[...], br_ref[...], alpha)
        x1_ref[rs, :] = x1
        hf_ref[rs, :] = hf
        return dict(s, logits=lg)

    states = _run_skewed([proj_out, tail], [dict(rows=rs) for rs in _row_chains(xc_ref.shape[0], 1)])
    _tail_route(jnp.concatenate([s["logits"] for s in states], axis=1), route_ref, gate_ref, cnt_ref, carry_ref)


def _tail_out(t, d, tm):
    shapes = [jax.ShapeDtypeStruct((t, d), F32), jax.ShapeDtypeStruct((t, d // 2), jnp.int32),
              jax.ShapeDtypeStruct((2 * TOP_K, t), jnp.int32), jax.ShapeDtypeStruct((t, LANES), F32),
              jax.ShapeDtypeStruct((N_EXPERTS, LANES), F32)]
    specs = [pl.BlockSpec((tm, d), lambda i, r: (i, 0)), pl.BlockSpec((tm, d // 2), lambda i, r: (i, 0)),
             pl.BlockSpec((2 * TOP_K, tm), lambda i, r: (0, i)), pl.BlockSpec((tm, LANES), lambda i, r: (i, 0)),
             pl.BlockSpec((N_EXPERTS, LANES), lambda i, r: (0, 0))]
    return shapes, specs


_TAIL_SCRATCH = [pltpu.VMEM((N_EXPERTS, 1), F32)]


def _post_ab(x, ya_p, yb_p, ya_s, yb_s, w_out, mod3, mod_base, rows, ln_g, ln_b, w_r, b_r, alpha):
    n_ctx_tiles = ya_p.shape[0] // TM
    (xc, xl), x_specs = _group_specs(x, n_ctx_tiles)
    d = xc.shape[1]
    nt = rows.shape[0]
    t = nt * TM
    a_w, q_w = ya_p.shape[1], yb_p.shape[1]
    mod_spec = lambda j: pl.BlockSpec((None, 1, d), lambda i, r: (mod_base + j * 16 + r[i], 0, 0))
    whole = lambda shp: pl.BlockSpec(shp, lambda i, r: (0,) * len(shp))
    ctx = lambda w: pl.BlockSpec((TM, w), lambda i, r: (jnp.minimum(i, n_ctx_tiles - 1), 0))
    lat = lambda w: pl.BlockSpec((TM, w), lambda i, r: (jnp.maximum(i - n_ctx_tiles, 0), 0))
    shapes, specs = _tail_out(t, d, TM)
    grid_spec = pltpu.PrefetchScalarGridSpec(
        num_scalar_prefetch=1, grid=(nt,),
        in_specs=x_specs + [ctx(a_w), ctx(q_w), lat(a_w), lat(q_w),
                            whole(w_out.shape), mod_spec(2), whole((1, d)), whole((1, d)), mod_spec(4), mod_spec(3),
                            whole(w_r.shape), whole(b_r.shape)],
        out_specs=specs,
        scratch_shapes=_TAIL_SCRATCH,
    )
    return pl.pallas_call(
        functools.partial(_post_ab_kernel, n_ctx_tiles=n_ctx_tiles, alpha=alpha),
        grid_spec=grid_spec, out_shape=shapes, compiler_params=_cparams(1), name="post_ab",
    )(rows, xc, xl, ya_p, yb_p, ya_s, yb_s, w_out, mod3, ln_g, ln_b, mod3, mod3, w_r, b_r)


def _unpack_f32_pairs(p):
    u = pltpu.bitcast(p, jnp.uint32)
    return jnp.concatenate([pltpu.bitcast(u << 16, F32), pltpu.bitcast(u & jnp.uint32(0xFFFF0000), F32)], axis=1)


def _ffn_residual(x1, y_packed, gate, g_f, ln_g, ln_b, alpha):
    y = gate[:, 0:1] * _unpack_f32_pairs(y_packed[0])
    for k in range(1, TOP_K):
        y = y + gate[:, k:k + 1] * _unpack_f32_pairs(y_packed[k])
    return _ln(alpha * x1 + g_f * y) * ln_g + ln_b


def _gmlp_kernel(rows_ref, xp_ref, y0_ref, y1_ref, y2_ref, y3_ref, pgate_ref, pgf_ref, plg_ref, plb_ref,
                 scm_ref, shm_ref, win_ref, bin_ref, gv_ref, bv_ref, wsp_ref, bsp_ref, wout_ref,
                 gm_ref, lg_ref, lb_ref, scf_ref, shf_ref, wr_ref, br_ref, x1_ref, hf_ref, route_ref, gate_ref,
                 cnt_ref, carry_ref, *, alpha):
    del rows_ref
    assert (xp_ref.shape[0] // 2) % CHUNK == 0
    c_w = win_ref.shape[1] // 2
    gd = c_w // C_GROUPS

    def norm_in(s):
        rs = s["rows"]
        x = _ffn_residual(xp_ref[rs, :], [r[rs, :] for r in (y0_ref, y1_ref, y2_ref, y3_ref)], pgate_ref[rs, :],
                          pgf_ref[...], plg_ref[...], plb_ref[...], alpha)
        return dict(s, x=x, h=(_ln(x) * (1.0 + scm_ref[...]) + shm_ref[...]).astype(BF16))

    def proj_in(s):
        return dict(s, z=_dot(s["h"], win_ref[...]) + bin_ref[...])

    def gate_split(s):
        z = s["z"]
        z = 0.5 * z * (1.0 + lax.erf(z * (2.0 ** -0.5)))
        return dict(s, u=z[:, :c_w], v=(_ln(z[:, c_w:]) * gv_ref[...] + bv_ref[...]).astype(BF16))

    def spatial(s):
        v = s["v"]
        chunks = []
        for n in range(v.shape[0] // CHUNK):
            groups = [_dot(wsp_ref[g], v[n * CHUNK:(n + 1) * CHUNK, g * gd:(g + 1) * gd]) + bsp_ref[g]
                      for g in range(C_GROUPS)]
            chunks.append(jnp.concatenate(groups, axis=1))
        return dict(s, t=(s["u"] * jnp.concatenate(chunks, axis=0)).astype(BF16))

    def proj_out(s):
        return dict(s, y=_dot(s["t"], wout_ref[...]))

    def tail(s):
        x1, hf, lg = _tail_rows(s["x"], s["y"], gm_ref[...], lg_ref[...], lb_ref[...], scf_ref[...], shf_ref[...],
                                wr_ref[...], br_ref[...], alpha)
        x1_ref[s["rows"], :] = x1
        hf_ref[s["rows"], :] = hf
        return dict(s, logits=lg)

    states = _run_skewed([norm_in, proj_in, gate_split, spatial, proj_out, tail],
                         [dict(rows=rs) for rs in _row_chains(xp_ref.shape[0])])
    _tail_route(jnp.concatenate([s["logits"] for s in states], axis=1), route_ref, gate_ref, cnt_ref, carry_ref)


def _gmlp(prev, mod3, mod_base, rows, w_in, b_in, g_v, b_v, w_sp, b_sp, w_out, ln_g, ln_b, w_r, b_r, alpha):
    x1p, y4, pgate, plg, plb = prev
    t, d = x1p.shape
    nt = rows.shape[0]
    tm = t // nt
    mod_spec = lambda j, base=mod_base: pl.BlockSpec((None, 1, d), lambda i, r: (base + j * 16 + r[i], 0, 0))
    whole = lambda shp: pl.BlockSpec(shp, lambda i, r: (0,) * len(shp))
    y_spec = lambda k: pl.BlockSpec((None, tm, d // 2), lambda i, r: (k, i, 0))
    shapes, specs = _tail_out(t, d, tm)
    grid_spec = pltpu.PrefetchScalarGridSpec(
        num_scalar_prefetch=1, grid=(nt,),
        in_specs=[pl.BlockSpec((tm, d), lambda i, r: (i, 0)), y_spec(0), y_spec(1), y_spec(2), y_spec(3),
                  pl.BlockSpec((tm, LANES), lambda i, r: (i, 0)), mod_spec(5, mod_base - 6 * 16),
                  whole((1, d)), whole((1, d)), mod_spec(1), mod_spec(0),
                  whole(w_in.shape), whole(b_in.shape), whole(g_v.shape), whole(b_v.shape),
                  whole(w_sp.shape), whole(b_sp.shape), whole(w_out.shape),
                  mod_spec(2), whole((1, d)), whole((1, d)), mod_spec(4), mod_spec(3),
                  whole(w_r.shape), whole(b_r.shape)],
        out_specs=specs,
        scratch_shapes=_TAIL_SCRATCH,
    )
    return pl.pallas_call(
        functools.partial(_gmlp_kernel, alpha=alpha),
        grid_spec=grid_spec, out_shape=shapes, compiler_params=_cparams(1, VMEM_LIMIT_GMLP), name="gmlp",
    )(rows, x1p, y4, y4, y4, y4, pgate, mod3, plg, plb, mod3, mod3, w_in, b_in, g_v, b_v, w_sp, b_sp, w_out,
      mod3, ln_g, ln_b, mod3, mod3, w_r, b_r)


def _moe_kernel(be_ref, bf_ref, nx_ref, sl_ref, hf_ref, nu_ref, x_ref, wgu_hbm, bgu_ref, wdn_hbm, bdn_ref, o_ref,
                wgu_f, wdn_f, wgu_s, wdn_s, sem, *, layer):
    i = pl.program_id(0)
    d_ff = wdn_s.shape[0]

    def weight_copies(e, slot):
        return (pltpu.make_async_copy(wgu_hbm.at[layer, e], wgu_f.at[slot], sem.at[0, slot]),
                pltpu.make_async_copy(wdn_hbm.at[layer, e], wdn_f.at[slot], sem.at[1, slot]))

    @pl.when(i < nu_ref[0])
    def _():
        @pl.when(bf_ref[i] == 1)
        def _():
            slot = sl_ref[i]

            @pl.when(i == 0)
            def _():
                for c in weight_copies(be_ref[i], slot):
                    c.start()

            for c in weight_copies(be_ref[i], slot):
                c.wait()
            wgu_s[...] = wgu_f[slot].astype(BF16)
            wdn_s[...] = wdn_f[slot].astype(BF16)

            @pl.when(nx_ref[i] >= 0)
            def _():
                for c in weight_copies(nx_ref[i], 1 - slot):
                    c.start()

        def up(s):
            return dict(s, gu=_dot(_unpack_bf16_pairs(x_ref[s["rows"], :]), wgu_s[...]) + bgu_ref[...])

        def act(s):
            gu = s["gu"]
            gate = jnp.minimum(gu[:, :d_ff], SWIGLU_LIMIT)
            lin = jnp.clip(gu[:, d_ff:], -SWIGLU_LIMIT, SWIGLU_LIMIT)
            glu = gate * jax.nn.sigmoid(SWIGLU_ALPHA * gate)
            return dict(s, gu=None, hid=((lin + 1.0) * glu).astype(BF16))

        def down(s):
            o_ref[s["rows"], :] = _pack_bf16_pairs(_dot(s["hid"], wdn_s[...]) + bdn_ref[...])
            return dict(s, hid=None)

        def run(n_chains, n_rows):
            _run_skewed([up, act, down], [dict(rows=rs) for rs in _row_chains(n_rows, n_chains)])

        @pl.when(hf_ref[i] == 0)
        def _():
            run(2, TM_MOE)

        @pl.when(hf_ref[i] == 1)
        def _():
            run(1, TM_MOE // 2)


def _moe(xs, plan, w_gu, b_gu, w_dn, b_dn, layer):
    n_rows = xs.shape[0]
    depth, n_e, d, ff2 = w_gu.shape
    d_ff = ff2 // 2
    nt = n_rows // TM_MOE
    n_plan = len(plan)
    row_spec = pl.BlockSpec((TM_MOE, d // 2), lambda i, *p: (jnp.minimum(i, p[-1][0] - 1), 0))
    bias_spec = lambda w: pl.BlockSpec((None, None, 1, w), lambda i, *p: (layer, p[0][i], 0, 0))
    grid_spec = pltpu.PrefetchScalarGridSpec(
        num_scalar_prefetch=n_plan, grid=(nt,),
        in_specs=[row_spec, pl.BlockSpec(memory_space=pl.ANY), bias_spec(ff2),
                  pl.BlockSpec(memory_space=pl.ANY), bias_spec(d)],
        out_specs=row_spec,
        scratch_shapes=[pltpu.VMEM((2, d, ff2), F32), pltpu.VMEM((2, d_ff, d), F32),
                        pltpu.VMEM((d, ff2), BF16), pltpu.VMEM((d_ff, d), BF16),
                        pltpu.SemaphoreType.DMA((2, 2))],
    )
    return pl.pallas_call(
        functools.partial(_moe_kernel, layer=layer), grid_spec=grid_spec,
        out_shape=jax.ShapeDtypeStruct((n_rows, d // 2), jnp.int32),
        compiler_params=_cparams(1), name="moe_experts",
    )(*plan, xs, w_gu, b_gu.reshape(depth, n_e, 1, ff2), w_dn, b_dn.reshape(depth, n_e, 1, d))


def _combine_kernel(rows_ref, x_ref, y0_ref, y1_ref, y2_ref, y3_ref, gate_ref, gf_ref, lg_ref, lb_ref, *o_refs,
                    alpha, n_ctx_tiles):
    del rows_ref
    out = _ffn_residual(x_ref[...], [r[...] for r in (y0_ref, y1_ref, y2_ref, y3_ref)], gate_ref[...], gf_ref[...],
                        lg_ref[...], lb_ref[...], alpha)
    if n_ctx_tiles is None:
        o_refs[0][...] = out
    else:
        @pl.when(pl.program_id(0) < n_ctx_tiles)
        def _():
            o_refs[0][...] = out

        @pl.when(pl.program_id(0) >= n_ctx_tiles)
        def _():
            o_refs[1][...] = out


def _combine(x1, y4, gate, mod3, mod_base, rows, ln_g, ln_b, alpha, t_ctx=None):
    t, d = x1.shape
    nt = rows.shape[0]
    tm = t // nt
    y_spec = lambda k: pl.BlockSpec((None, tm, d // 2), lambda i, r: (k, i, 0))
    if t_ctx is None:
        n_ctx_tiles = None
        out_specs = pl.BlockSpec((tm, d), lambda i, r: (i, 0))
        out_shape = jax.ShapeDtypeStruct((t, d), F32)
    else:
        n_ctx_tiles = t_ctx // tm
        out_specs = [pl.BlockSpec((tm, d), lambda i, r: (jnp.minimum(i, n_ctx_tiles - 1), 0)),
                     pl.BlockSpec((tm, d), lambda i, r: (jnp.maximum(i - n_ctx_tiles, 0), 0))]
        out_shape = [jax.ShapeDtypeStruct((t_ctx, d), F32), jax.ShapeDtypeStruct((t - t_ctx, d), F32)]
    grid_spec = pltpu.PrefetchScalarGridSpec(
        num_scalar_prefetch=1, grid=(nt,),
        in_specs=[pl.BlockSpec((tm, d), lambda i, r: (i, 0)), y_spec(0), y_spec(1), y_spec(2), y_spec(3),
                  pl.BlockSpec((tm, LANES), lambda i, r: (i, 0)),
                  pl.BlockSpec((None, 1, d), lambda i, r: (mod_base + 5 * 16 + r[i], 0, 0)),
                  pl.BlockSpec((1, d), lambda i, r: (0, 0)), pl.BlockSpec((1, d), lambda i, r: (0, 0))],
        out_specs=out_specs,
    )
    return pl.pallas_call(
        functools.partial(_combine_kernel, alpha=alpha, n_ctx_tiles=n_ctx_tiles), grid_spec=grid_spec,
        out_shape=out_shape, compiler_params=_cparams(1), name="moe_combine",
    )(rows, x1, y4, y4, y4, y4, gate, mod3, ln_g, ln_b)


def _route_plan(route, counts, n_tiles):
    idx, rank = route[:TOP_K], route[TOP_K:]
    padded = (counts + TM_MOE - 1) // TM_MOE * TM_MOE
    pend = jnp.cumsum(padded)
    pstart = pend - padded
    onehot = idx[:, :, None] == jnp.arange(N_EXPERTS, dtype=jnp.int32)[None, None, :]
    dest = jnp.sum(jnp.where(onehot, pstart[None, None, :], 0), axis=-1) + rank
    n_used = (pend[-1] // TM_MOE).astype(jnp.int32)
    tile_start = jnp.arange(n_tiles, dtype=jnp.int32) * TM_MOE
    blk_e = jnp.sum((tile_start[:, None] >= pend[None, :]).astype(jnp.int32), axis=1)
    last_e = jnp.sum((jnp.maximum(n_used - 1, 0) * TM_MOE >= pend).astype(jnp.int32))
    blk_e = jnp.minimum(jnp.where(jnp.arange(n_tiles) < n_used, blk_e, last_e), N_EXPERTS - 1).astype(jnp.int32)
    blk_first = jnp.concatenate([jnp.ones((1,), jnp.int32), (blk_e[1:] != blk_e[:-1]).astype(jnp.int32)])
    ar = jnp.arange(N_EXPERTS, dtype=jnp.int32)
    later = jnp.logical_and((counts > 0)[None, :], ar[None, :] > ar[:, None])
    next_e = jnp.min(jnp.where(later, ar[None, :], N_EXPERTS), axis=1)
    next_e = jnp.where(next_e == N_EXPERTS, -1, next_e)
    blk_next = jnp.sum(jnp.where(blk_e[:, None] == ar[None, :], next_e[None, :], 0), axis=1).astype(jnp.int32)
    blk_slot = ((jnp.cumsum(blk_first) - 1) % 2).astype(jnp.int32)
    left = jnp.sum(jnp.where(blk_e[:, None] == ar[None, :], (pstart + counts)[None, :], 0), axis=1) - tile_start
    blk_half = (left <= TM_MOE // 2).astype(jnp.int32)
    return dest, (blk_e, blk_first, blk_next, blk_slot, blk_half, n_used.reshape(1))


SC_CORES = 2
SC_SUBCORES = 16


def _sc_mesh():
    return plsc.VectorSubcoreMesh(core_axis_name="core", subcore_axis_name="subcore")


def _sc_scatter_rows(src, dest, n_rows, window):
    n_slots, t = dest.shape
    w = src.shape[1]
    per_worker = t // (SC_CORES * SC_SUBCORES)
    assert per_worker * SC_CORES * SC_SUBCORES == t and per_worker % window == 0

    @functools.partial(pl.kernel, out_type=jax.ShapeDtypeStruct((n_rows, w), src.dtype), mesh=_sc_mesh(),
                       scratch_types=[pltpu.VMEM((n_slots, window), jnp.int32), pltpu.VMEM((window, w), src.dtype)],
                       name="sc_dispatch")
    def scatter(src_hbm, dest_hbm, out_hbm, idx_v, rows_v):
        worker = lax.axis_index("subcore") * SC_CORES + lax.axis_index("core")

        @pl.loop(0, per_worker // window)
        def _(c):
            base = pl.multiple_of(worker * per_worker + c * window, window)
            pltpu.sync_copy(src_hbm.at[pl.ds(base, window)], rows_v)
            pltpu.sync_copy(dest_hbm.at[:, pl.ds(base, window)], idx_v)
            for k in range(n_slots):
                pltpu.sync_copy(rows_v, out_hbm.at[idx_v.at[k]])

    return scatter(src, dest)


def _sc_gather_rows(table, idx, window):
    n = idx.shape[0]
    w = table.shape[1]
    per_worker = n // (SC_CORES * SC_SUBCORES)
    assert per_worker * SC_CORES * SC_SUBCORES == n and per_worker % window == 0

    @functools.partial(pl.kernel, out_type=jax.ShapeDtypeStruct((n, w), table.dtype), mesh=_sc_mesh(),
                       scratch_types=[pltpu.VMEM((window,), jnp.int32), pltpu.VMEM((window, w), table.dtype)],
                       name="sc_collect")
    def gather(table_hbm, idx_hbm, out_hbm, idx_v, rows_v):
        worker = lax.axis_index("subcore") * SC_CORES + lax.axis_index("core")

        @pl.loop(0, per_worker // window)
        def _(c):
            base = pl.multiple_of(worker * per_worker + c * window, window)
            pltpu.sync_copy(idx_hbm.at[pl.ds(base, window)], idx_v)
            pltpu.sync_copy(table_hbm.at[idx_v], rows_v)
            pltpu.sync_copy(rows_v, out_hbm.at[pl.ds(base, window)])

    return gather(table, idx)


def _dft_tables(n):
    k = jnp.arange(n, dtype=jnp.int32)

    def trig(rows):
        ang = ((rows[:, None] * k[None, :]) % n).astype(F32) * (2.0 * math.pi / n)
        return jnp.cos(ang), jnp.sin(ang)

    n2 = 64
    if n < 4 * n2:
        return trig(k)
    ca, sa = (z[:, None, :] for z in trig(jnp.arange(n // n2, dtype=jnp.int32) * n2))
    cb, sb = (z[None, :, :] for z in trig(jnp.arange(n2, dtype=jnp.int32)))
    return (ca * cb - sa * sb).reshape(n, n), (sa * cb + ca * sb).reshape(n, n)


def _rope_tables(n_lat, n_ctx_rows):
    half, quarter = HEAD_DIM // 2, HEAD_DIM // 4
    tpos = jnp.arange(n_lat, dtype=jnp.int32)
    lane = jnp.arange(LANES, dtype=jnp.int32) % HEAD_DIM
    pos = jnp.where(lane[None, :] < half, (tpos // GRID_W)[:, None], (tpos % GRID_W)[:, None]).astype(F32)
    fidx = (lane % quarter).astype(F32)
    freqs = ROPE_THETA ** (-fidx / quarter)
    ang = pos * freqs[None, :]
    cos, sin = jnp.cos(ang), jnp.sin(ang)
    first = (lane % half) < quarter
    sa = jnp.where(first[None, :], -sin, 0.0)
    sb = jnp.where(first[None, :], 0.0, sin)
    ident = lambda v: jnp.full((n_ctx_rows, LANES), v, F32)
    return (jnp.concatenate([ident(1.0), cos]), jnp.concatenate([ident(0.0), sa]), jnp.concatenate([ident(0.0), sb]))


def kernel(x_prompt, x_sample, cache_k_ab, cache_v_ab, c, c_ctx, w_ada, b_ada, ln_mix_g, ln_mix_b, ln_ffn_g, ln_ffn_b, w_in_ab, w_out_ab, sink_ab, w_in_c, b_in_c, ln_v_g, ln_v_b, w_sp, b_sp, w_out_c, w_router, b_router, w_gu, b_gu, w_dn, b_dn):
    n_ctx_b, ctx_seq, d = x_prompt.shape
    n_lat_b, lat_seq, _ = x_sample.shape
    depth = w_ada.shape[0]
    t_ctx, t_lat = n_ctx_b * ctx_seq, n_lat_b * lat_seq
    t = t_ctx + t_lat
    alpha = (2 * depth) ** 0.25
    kv_w = N_KV_HEADS * HEAD_DIM
    a_w = d // 4
    q_w = d - a_w
    assert t_ctx % TM == 0 and lat_seq % TM == 0 and t_ctx % lat_seq == 0 and n_lat_b + 1 <= 16

    cond_rows = lambda tm: jnp.asarray(np.concatenate(
        [np.zeros(t_ctx // tm, np.int32), 1 + np.arange(t_lat // tm, dtype=np.int32) // (lat_seq // tm)]))
    rows, rows_c = cond_rows(TM), cond_rows(TM_COMBINE)
    rblk = jnp.asarray(np.concatenate(
        [np.zeros(t_ctx // TM, np.int32), 1 + np.arange(t_lat // TM, dtype=np.int32) % (lat_seq // TM)]))

    cond16 = jnp.zeros((16, d), F32).at[0].set(c_ctx).at[1:1 + n_lat_b].set(c)
    mod = _ada(cond16, w_ada, b_ada)
    mod3 = mod.reshape(depth, 16, 6, d).transpose(0, 2, 1, 3).reshape(depth * 6 * 16, 1, d)

    cos, sa, sb = _rope_tables(lat_seq, TM)
    cs_ctx, ss_ctx = (z.astype(BF16) for z in _dft_tables(ctx_seq))
    cs_lat, ss_lat = (z.astype(BF16) for z in _dft_tables(lat_seq))
    gd = a_w // A_GROUPS
    cd, sd = _dft_tables(gd)
    eye = jnp.eye(A_GROUPS, dtype=F32)
    bdc, bds = jnp.kron(eye, cd).astype(BF16), jnp.kron(eye, sd).astype(BF16)

    past = cache_k_ab.shape[2]
    cache_k = cache_k_ab.reshape(n_lat_b, -1, past, kv_w).astype(BF16)
    cache_v = cache_v_ab.reshape(n_lat_b, -1, past, kv_w).astype(BF16)

    x = (x_prompt.reshape(t_ctx, d), x_sample.reshape(t_lat, d))
    row2 = lambda v: v.reshape(1, -1)
    ks, vs = [], []
    for l in range(depth):
        j = l // 2
        mod_base = l * 6 * 16
        w_r, b_r = w_router[l].T, b_router[l].reshape(-1, 1)
        if l % 2 == 0:
            ac, as_, q, k32, v32, k, v = _inproj(x, t_ctx // TM, mod3, mod_base, rows, rblk, w_in_ab[j].astype(BF16),
                                                 cos, sa, sb, bdc, bds, a_w, q_w, kv_w)
            ks.append(k32[:t_ctx].reshape(n_ctx_b, ctx_seq, N_KV_HEADS, HEAD_DIM))
            vs.append(v32[:t_ctx].reshape(n_ctx_b, ctx_seq, N_KV_HEADS, HEAD_DIM))
            ya_p = _fourier(ac, as_, cs_ctx, ss_ctx, n_ctx_b, ctx_seq, 0, min(ctx_seq, 512))
            ya_s = _fourier(ac, as_, cs_lat, ss_lat, n_lat_b, lat_seq, t_ctx // lat_seq, min(lat_seq, 1024))
            yb_p = _attn_ctx(sink_ab[j], q, k, v, n_ctx_b, ctx_seq)
            yb_s = _attn_lat(sink_ab[j], q, k, v, cache_k, cache_v, j, n_lat_b, lat_seq, t_ctx)
            x1, hf, route, gate, cnt = _post_ab(x, ya_p, yb_p, ya_s, yb_s, w_out_ab[j].astype(BF16), mod3, mod_base,
                                                rows, row2(ln_mix_g[l]), row2(ln_mix_b[l]), w_r, b_r, alpha)
        else:
            x1, hf, route, gate, cnt = _gmlp(prev, mod3, mod_base, rows, w_in_c[j].astype(BF16), row2(b_in_c[j]),
                                             row2(ln_v_g[j]), row2(ln_v_b[j]), w_sp[j].astype(BF16),
                                             b_sp[j][:, :, None], w_out_c[j].astype(BF16), row2(ln_mix_g[l]),
                                             row2(ln_mix_b[l]), w_r, b_r, alpha)
        n_rows = t * TOP_K + N_EXPERTS * TM_MOE
        dest, plan = _route_plan(route, cnt[:, 0].astype(jnp.int32), n_rows // TM_MOE)
        xs = _sc_scatter_rows(hf, dest, n_rows, 128)
        out_sorted = _moe(xs, plan, w_gu, b_gu, w_dn, b_dn, l)
        y4 = _sc_gather_rows(out_sorted, dest.reshape(-1), 128).reshape(TOP_K, t, d // 2)
        if l % 2 == 0 and l + 1 < depth:
            prev = (x1, y4, gate, row2(ln_ffn_g[l]), row2(ln_ffn_b[l]))
        else:
            x = _combine(x1, y4, gate, mod3, mod_base, rows_c, row2(ln_ffn_g[l]), row2(ln_ffn_b[l]), alpha,
                         t_ctx if l == depth - 1 else None)

    y_prompt = x[0].reshape(n_ctx_b, ctx_seq, d)
    y_sample = x[1].reshape(n_lat_b, lat_seq, d)
    return (y_prompt, y_sample, jnp.stack(ks, axis=1), jnp.stack(vs, axis=1))
```

```python
import functools
import math

import numpy as np
import jax
import jax.numpy as jnp
from jax import lax
from jax.experimental import pallas as pl
from jax.experimental.pallas import tpu as pltpu
from jax.experimental.pallas import tpu_sc as plsc

GRID_W = 64
BLK = 128
WINDOW = 128
HEAD_DIM = 64
A_GROUPS = 4
N_KV_HEADS = 4
C_GROUPS = 8
CHUNK = 128
N_EXPERTS = 32
TOP_K = 4
SWIGLU_LIMIT = 7.0
SWIGLU_ALPHA = 1.702
ROPE_THETA = 10000.0
LN_EPS = 1e-6
NEG_INF = -1e30
LOG2_E = math.log2(math.e)

LANES = 128
TM = 1024
TM_COMBINE = 512
ATTN_BLOCKS_PER_STEP = 2
TM_MOE = 512
VMEM_LIMIT = 52 * 1024 * 1024
VMEM_LIMIT_GMLP = 60 * 1024 * 1024

F32 = jnp.float32
BF16 = jnp.bfloat16


def _cparams(n_axes, vmem_limit=VMEM_LIMIT):
    return pltpu.CompilerParams(dimension_semantics=("arbitrary",) * n_axes, vmem_limit_bytes=vmem_limit)


def _ln(x):
    mu = jnp.mean(x, axis=-1, keepdims=True)
    xc = x - mu
    var = jnp.mean(xc * xc, axis=-1, keepdims=True)
    return xc * lax.rsqrt(var + LN_EPS)


def _dot(a, b):
    return jnp.dot(a, b, preferred_element_type=F32)


def _dot_nt(a, b):
    return lax.dot_general(a, b, (((1,), (1,)), ((), ())), preferred_element_type=F32)


def _split(a):
    hi = a.astype(BF16)
    lo = (a - hi.astype(F32)).astype(BF16)
    return hi, lo


def _dot_3pass(a, b):
    a_hi, a_lo = _split(a)
    b_hi, b_lo = _split(b)
    return _dot(a_hi, b_hi) + (_dot(a_hi, b_lo) + _dot(a_lo, b_hi))


def _lane_select(cols, width):
    m = cols[0].shape[0]
    lane = lax.broadcasted_iota(jnp.int32, (m, width), 1)
    out = jnp.zeros((m, width), cols[0].dtype)
    for j, c in enumerate(cols):
        out = jnp.where(lane == j, c, out)
    return out


def _pack_bf16_pairs(v):
    n = v.shape[1] // 2
    lo = pltpu.bitcast(v[:, :n].astype(BF16).astype(F32), jnp.uint32) >> 16
    hi = pltpu.bitcast(v[:, n:].astype(BF16).astype(F32), jnp.uint32) & jnp.uint32(0xFFFF0000)
    return pltpu.bitcast(lo | hi, jnp.int32)


def _unpack_bf16_pairs(p):
    u = pltpu.bitcast(p, jnp.uint32)
    lo = pltpu.bitcast(u << 16, F32)
    hi = pltpu.bitcast(u & jnp.uint32(0xFFFF0000), F32)
    return jnp.concatenate([lo, hi], axis=1).astype(BF16)


def _row_chains(n_rows, n_chains=2):
    step = n_rows // n_chains
    return [slice(c * step, (c + 1) * step) for c in range(n_chains)]


def _run_skewed(stages, states):
    states = list(states)
    for step in range(len(stages) + len(states) - 1):
        for c in range(len(states)):
            if 0 <= step - c < len(stages):
                states[c] = stages[step - c](states[c])
    return states


def _tail_rows(x, y, g_m, ln_g, ln_b, sc_f, sh_f, w_r, b_r, alpha):
    x1 = _ln(alpha * x + g_m * y) * ln_g + ln_b
    hf = _ln(x1) * (1.0 + sc_f) + sh_f
    hf_hi, hf_lo = _split(hf)
    w_hi, w_lo = _split(w_r)
    logits = _dot_nt(w_hi, hf_hi) + (_dot_nt(w_hi, hf_lo) + _dot_nt(w_lo, hf_hi)) + b_r
    return x1, _pack_bf16_pairs(hf), logits


def _tail_route(logits, route_ref, gate_ref, cnt_ref, carry_ref):
    @pl.when(pl.program_id(0) == 0)
    def _():
        carry_ref[...] = jnp.zeros_like(carry_ref)

    tm = logits.shape[1]
    sub = lax.broadcasted_iota(jnp.int32, logits.shape, 0)
    vals = logits
    top_v, top_i = [], []
    for _ in range(TOP_K):
        m = jnp.max(vals, axis=0, keepdims=True)
        am = jnp.min(jnp.where(vals == m, sub, N_EXPERTS), axis=0, keepdims=True)
        top_v.append(m)
        top_i.append(am)
        vals = jnp.where(sub == am, -jnp.inf, vals)
    e = [jnp.exp(v - top_v[0]) for v in top_v]
    denom = e[0] + e[1] + e[2] + e[3]
    gates_t = jnp.concatenate([ek / denom for ek in e] + [jnp.zeros((LANES - TOP_K, tm), F32)], axis=0)
    gate_ref[...] = gates_t.T

    member = jnp.zeros(logits.shape, F32)
    for am in top_i:
        member = jnp.where(sub == am, 1.0, member)
    r_i = lax.broadcasted_iota(jnp.int32, (tm, tm), 0)
    c_i = lax.broadcasted_iota(jnp.int32, (tm, tm), 1)
    earlier = jnp.where(r_i < c_i, 1.0, 0.0).astype(BF16)
    before = _dot(member.astype(BF16), earlier) + carry_ref[...]
    ranks = [jnp.sum(jnp.where(sub == am, before, 0.0), axis=0, keepdims=True).astype(jnp.int32) for am in top_i]
    route_ref[...] = jnp.concatenate(top_i + ranks, axis=0)
    carry = carry_ref[...] + jnp.sum(member, axis=1, keepdims=True)
    carry_ref[...] = carry
    cnt_ref[...] = jnp.broadcast_to(carry, cnt_ref.shape)


def _ada_kernel(cond_ref, w_ref, b_ref, o_ref):
    c = cond_ref[...]
    s = (c * jax.nn.sigmoid(c)).astype(BF16)
    o_ref[...] = _dot(s, w_ref[...].astype(BF16)) + b_ref[...]


def _ada(cond16, w_ada, b_ada):
    depth, d, n = w_ada.shape
    tn = 1536
    return pl.pallas_call(
        _ada_kernel,
        grid=(depth, n // tn),
        in_specs=[
            pl.BlockSpec((16, d), lambda l, j: (0, 0)),
            pl.BlockSpec((None, d, tn), lambda l, j: (l, 0, j)),
            pl.BlockSpec((None, 1, tn), lambda l, j: (l, 0, j)),
        ],
        out_specs=pl.BlockSpec((None, 16, tn), lambda l, j: (l, 0, j)),
        out_shape=jax.ShapeDtypeStruct((depth, 16, n), F32),
        compiler_params=_cparams(2),
        name="ada_mod",
    )(cond16, w_ada, b_ada.reshape(depth, 1, n))


def _group_rows(xc_ref, xl_ref, rows, n_ctx_tiles):
    return jnp.where(pl.program_id(0) < n_ctx_tiles, xc_ref[rows, :], xl_ref[rows, :])


def _group_specs(x, n_ctx_tiles):
    if isinstance(x, tuple):
        arrays, lat = x, (lambda i, *_: (jnp.maximum(i - n_ctx_tiles, 0), 0))
    else:
        arrays, lat = (x, x), (lambda i, *_: (jnp.maximum(i, n_ctx_tiles), 0))
    d = arrays[0].shape[1]
    ctx = lambda i, *_: (jnp.minimum(i, n_ctx_tiles - 1), 0)
    return arrays, [pl.BlockSpec((TM, d), ctx), pl.BlockSpec((TM, d), lat)]


def _inproj_kernel(rows_ref, rblk_ref, xc_ref, xl_ref, sc_ref, sh_ref, w_ref, cos_ref, sa_ref, sb_ref, bdc_ref,
                   bds_ref, ac_ref, as_ref, q_ref, k_ref, v_ref, kb_ref, vb_ref, *, a_w, q_w, kv_w, n_ctx_tiles):
    del rows_ref, rblk_ref
    g = q_w // kv_w

    def norm_in(s):
        x = _group_rows(xc_ref, xl_ref, s["rows"], n_ctx_tiles)
        return dict(s, h=(_ln(x) * (1.0 + sc_ref[...]) + sh_ref[...]).astype(BF16))

    def proj(s):
        return dict(s, p=_dot(s["h"], w_ref[...]))

    def finish(s):
        rs, p = s["rows"], s["p"]
        a = p[:, :a_w].astype(BF16)
        ac_ref[rs, :] = _dot(a, bdc_ref[...]).astype(BF16)
        as_ref[rs, :] = _dot(a, bds_ref[...]).astype(BF16)
        cos, sa, sb = cos_ref[rs, :], sa_ref[rs, :], sb_ref[rs, :]

        def rope(t):
            w = t.shape[1]
            reps = w // LANES
            c, a_, b_ = (jnp.tile(z, (1, reps)) for z in (cos, sa, sb))
            nxt = pltpu.roll(t, w - HEAD_DIM // 4, 1)
            prv = pltpu.roll(t, HEAD_DIM // 4, 1)
            return t * c + nxt * a_ + prv * b_

        q = rope(p[:, a_w:a_w + q_w]) * (HEAD_DIM ** -0.5 * LOG2_E)
        lane = lax.broadcasted_iota(jnp.int32, (q.shape[0], LANES), 1)
        for j in range(q_w // HEAD_DIM):
            tile = q[:, (j // 2) * LANES:(j // 2 + 1) * LANES]
            dst_low = (j // g) % 2 == 0
            if (j % 2 == 0) != dst_low:
                tile = pltpu.roll(tile, HEAD_DIM, 1)
            keep = (lane < HEAD_DIM) if dst_low else (lane >= HEAD_DIM)
            q_ref[rs, j * LANES:(j + 1) * LANES] = jnp.where(keep, tile, 0.0).astype(BF16)
        k = rope(p[:, a_w + q_w:a_w + q_w + kv_w])
        v = p[:, a_w + q_w + kv_w:]
        k_ref[rs, :] = k
        v_ref[rs, :] = v
        kb_ref[rs, :] = k.astype(BF16)
        vb_ref[rs, :] = v.astype(BF16)
        return s

    _run_skewed([norm_in, proj, finish], [dict(rows=rs) for rs in _row_chains(xc_ref.shape[0])])


def _inproj(x, n_ctx_tiles, mod3, mod_base, rows, rblk, w_in, cos, sa, sb, bdc, bds, a_w, q_w, kv_w):
    (xc, xl), x_specs = _group_specs(x, n_ctx_tiles)
    d = xc.shape[1]
    t = rows.shape[0] * TM
    n = w_in.shape[1]
    nt = t // TM
    mod_spec = lambda j: pl.BlockSpec((None, 1, d), lambda i, r, rb: (mod_base + j * 16 + r[i], 0, 0))
    whole = lambda shp: pl.BlockSpec(shp, lambda i, r, rb: (0,) * len(shp))
    rope_spec = pl.BlockSpec((TM, LANES), lambda i, r, rb: (rb[i], 0))
    tok = lambda w: pl.BlockSpec((TM, w), lambda i, r, rb: (i, 0))
    grid_spec = pltpu.PrefetchScalarGridSpec(
        num_scalar_prefetch=2, grid=(nt,),
        in_specs=x_specs + [mod_spec(1), mod_spec(0), whole((d, n)), rope_spec, rope_spec, rope_spec,
                            whole((a_w, a_w)), whole((a_w, a_w))],
        out_specs=[tok(a_w), tok(a_w), tok(2 * q_w), tok(kv_w), tok(kv_w), tok(kv_w), tok(kv_w)],
    )
    return pl.pallas_call(
        functools.partial(_inproj_kernel, a_w=a_w, q_w=q_w, kv_w=kv_w, n_ctx_tiles=n_ctx_tiles),
        grid_spec=grid_spec,
        out_shape=[jax.ShapeDtypeStruct((t, a_w), BF16), jax.ShapeDtypeStruct((t, a_w), BF16),
                   jax.ShapeDtypeStruct((t, 2 * q_w), BF16), jax.ShapeDtypeStruct((t, kv_w), F32),
                   jax.ShapeDtypeStruct((t, kv_w), F32), jax.ShapeDtypeStruct((t, kv_w), BF16),
                   jax.ShapeDtypeStruct((t, kv_w), BF16)],
        compiler_params=_cparams(1),
        name="inproj_ab",
    )(rows, rblk, xc, xl, mod3, mod3, w_in, cos, sa, sb, bdc, bds)


def _fourier_kernel(c_ref, s_ref, ac_ref, as_ref, o_ref, *, scale):
    y = _dot(c_ref[...], ac_ref[...]) - _dot(s_ref[...], as_ref[...])
    o_ref[...] = (y * scale).astype(BF16)


def _fourier(ac, as_, cs, ss, n_batch, seq, blk_off, tm):
    a_w = ac.shape[1]
    nr = seq // tm
    a_spec = pl.BlockSpec((seq, a_w), lambda r, b: (blk_off + b, 0))
    t_spec = pl.BlockSpec((tm, seq), lambda r, b: (r, 0))
    return pl.pallas_call(
        functools.partial(_fourier_kernel, scale=(seq * (a_w // A_GROUPS)) ** -0.5),
        grid=(nr, n_batch),
        in_specs=[t_spec, t_spec, a_spec, a_spec],
        out_specs=pl.BlockSpec((tm, a_w), lambda r, b: (b * nr + r, 0)),
        out_shape=jax.ShapeDtypeStruct((n_batch * seq, a_w), BF16),
        compiler_params=_cparams(2),
        name="fourier_%d" % seq,
    )(cs, ss, ac, as_)


def _attend_pairs(sink_ref, q_ref, keys, values, mask, o_ref, *, g):
    rows = q_ref.shape[0]
    lane = lax.broadcasted_iota(jnp.int32, (rows, LANES), 1)
    full_mask = None if mask is None else jnp.concatenate([mask] * g, axis=0)

    def scores(s):
        h0, p = s["h0"], s["h0"] // (2 * g)
        qp = jnp.concatenate([q_ref[:, (h0 + j) * LANES:(h0 + j + 1) * LANES] for j in range(g)], axis=0)
        sc = [_dot_nt(qp, kf(p)) for kf in keys]
        if full_mask is not None:
            sc[0] = jnp.where(full_mask, sc[0], NEG_INF)
        return dict(s, sc=sc)

    def row_max(s):
        m = jnp.concatenate([jnp.full((rows, 1), sink_ref[s["h0"] + j] * LOG2_E, F32) for j in range(g)], axis=0)
        sink_col = m
        for sc in s["sc"]:
            m = jnp.maximum(m, jnp.max(sc, axis=-1, keepdims=True))
        return dict(s, m=m, sink=jnp.exp2(sink_col - m))

    def weights(s):
        return dict(s, e=[jnp.exp2(sc - s["m"]).astype(BF16) for sc in s["sc"]], sc=None)

    def weighted_values(s):
        p = s["h0"] // (2 * g)
        acc = None
        for e, vf in zip(s["e"], values):
            v = vf(p)
            pv = _dot(e, jnp.concatenate([v, jnp.ones_like(v)], axis=1))
            acc = pv if acc is None else acc + pv
        return dict(s, o=acc[:, :LANES] / (acc[:, LANES:LANES + 1] + s["sink"]), e=None)

    states = _run_skewed([scores, row_max, weights, weighted_values],
                         [dict(h0=h * g) for h in range(N_KV_HEADS)])
    heads = {}
    for s in states:
        for j in range(g):
            blk = s["o"][j * rows:(j + 1) * rows]
            head = s["h0"] + j
            if (((head // g) % 2 == 0) != (head % 2 == 0)):
                blk = pltpu.roll(blk, HEAD_DIM, 1)
            heads[head] = blk
    for t in range(len(heads) // 2):
        tile = jnp.where(lane < HEAD_DIM, heads[2 * t], heads[2 * t + 1])
        o_ref[:, t * LANES:(t + 1) * LANES] = tile.astype(BF16)


def _attn_ctx_kernel(sink_ref, q_ref, k_ref, v_ref, o_ref, *, g):
    pair = lambda ref: (lambda p: ref[:, p * LANES:(p + 1) * LANES])
    _attend_pairs(sink_ref, q_ref, [pair(k_ref)], [pair(v_ref)], None, o_ref, g=g)


def _attn_ctx(sink, q, k, v, n_batch, seq):
    qp_w, kv_w = q.shape[1], k.shape[1]
    q_w = qp_w // 2
    g = q_w // kv_w
    return pl.pallas_call(
        functools.partial(_attn_ctx_kernel, g=g),
        grid=(n_batch,),
        in_specs=[pl.BlockSpec(memory_space=pltpu.SMEM),
                  pl.BlockSpec((seq, qp_w), lambda b: (b, 0)),
                  pl.BlockSpec((seq, kv_w), lambda b: (b, 0)),
                  pl.BlockSpec((seq, kv_w), lambda b: (b, 0))],
        out_specs=pl.BlockSpec((seq, q_w), lambda b: (b, 0)),
        out_shape=jax.ShapeDtypeStruct((n_batch * seq, q_w), BF16),
        compiler_params=_cparams(1),
        name="attn_ctx",
    )(sink, q, k, v)


def _attn_lat_kernel(sink_ref, q_ref, k_ref, v_ref, ck_ref, cv_ref, o_ref, *, g, seq):
    n_loc = 3 * BLK
    row = lax.broadcasted_iota(jnp.int32, (BLK, n_loc), 0)
    col = lax.broadcasted_iota(jnp.int32, (BLK, n_loc), 1)
    ctx = lambda ref: (lambda p: ref[:, p * LANES:(p + 1) * LANES])
    for sub in range(q_ref.shape[0] // BLK):
        i = pl.program_id(1) * (q_ref.shape[0] // BLK) + sub
        start = pl.multiple_of(jnp.clip((i - 1) * BLK, 0, seq - n_loc), BLK)
        band = jnp.abs(row + (i * BLK - start) - col) <= WINDOW
        loc = lambda ref, start=start: (lambda p: ref[pl.ds(start, n_loc), p * LANES:(p + 1) * LANES])
        rows = slice(sub * BLK, (sub + 1) * BLK)
        _attend_pairs(sink_ref, q_ref.at[rows, :], [loc(k_ref), ctx(ck_ref)], [loc(v_ref), ctx(cv_ref)], band,
                      o_ref.at[rows, :], g=g)


def _attn_lat(sink, q, k, v, cache_k, cache_v, layer_slot, n_batch, seq, tok_off):
    qp_w, kv_w = q.shape[1], k.shape[1]
    q_w = qp_w // 2
    g = q_w // kv_w
    rows = BLK * ATTN_BLOCKS_PER_STEP
    nb = seq // rows
    past = cache_k.shape[2]
    assert seq >= 3 * BLK and seq % rows == 0 and tok_off % rows == 0
    kv_spec = pl.BlockSpec((seq, kv_w), lambda b, i: (tok_off // seq + b, 0))
    c_spec = pl.BlockSpec((None, None, past, kv_w), lambda b, i: (b, layer_slot, 0, 0))
    return pl.pallas_call(
        functools.partial(_attn_lat_kernel, g=g, seq=seq),
        grid=(n_batch, nb),
        in_specs=[pl.BlockSpec(memory_space=pltpu.SMEM),
                  pl.BlockSpec((rows, qp_w), lambda b, i: (tok_off // rows + b * nb + i, 0)),
                  kv_spec, kv_spec, c_spec, c_spec],
        out_specs=pl.BlockSpec((rows, q_w), lambda b, i: (b * nb + i, 0)),
        out_shape=jax.ShapeDtypeStruct((n_batch * seq, q_w), BF16),
        compiler_params=_cparams(2),
        name="attn_lat",
    )(sink, q, k, v, cache_k, cache_v)


def _post_ab_kernel(rows_ref, xc_ref, xl_ref, yap_ref, ybp_ref, yas_ref, ybs_ref, w_ref, gm_ref, lg_ref, lb_ref,
                    scf_ref, shf_ref, wr_ref, br_ref, x1_ref, hf_ref, route_ref, gate_ref, cnt_ref, carry_ref,
                    *, n_ctx_tiles, alpha):
    del rows_ref
    is_ctx = pl.program_id(0) < n_ctx_tiles
    a_w = yap_ref.shape[1]

    def proj_out(s):
        rs = s["rows"]
        pick = lambda p_ref, s_ref: pltpu.bitcast(
            jnp.where(is_ctx, pltpu.bitcast(p_ref[rs, :], jnp.uint32), pltpu.bitcast(s_ref[rs, :], jnp.uint32)), BF16)
        ya = pick(yap_ref, yas_ref)
        yb = pick(ybp_ref, ybs_ref)
        return dict(s, y=_dot(ya, w_ref[:a_w, :]) + _dot(yb, w_ref[a_w:, :]))

    def tail(s):
        rs = s["rows"]
        x1, hf, lg = _tail_rows(_group_rows(xc_ref, xl_ref, rs, n_ctx_tiles), s["y"], gm_ref[...], lg_ref[...],
                                lb_ref[...], scf_ref[...], shf_ref[...], wr_ref[...], br_ref[...], alpha)
        x1_ref[rs, :] = x1
        hf_ref[rs, :] = hf
        return dict(s, logits=lg)

    states = _run_skewed([proj_out, tail], [dict(rows=rs) for rs in _row_chains(xc_ref.shape[0], 1)])
    _tail_route(jnp.concatenate([s["logits"] for s in states], axis=1), route_ref, gate_ref, cnt_ref, carry_ref)


def _tail_out(t, d, tm):
    shapes = [jax.ShapeDtypeStruct((t, d), F32), jax.ShapeDtypeStruct((t, d // 2), jnp.int32),
              jax.ShapeDtypeStruct((2 * TOP_K, t), jnp.int32), jax.ShapeDtypeStruct((t, LANES), F32),
              jax.ShapeDtypeStruct((N_EXPERTS, LANES), F32)]
    specs = [pl.BlockSpec((tm, d), lambda i, r: (i, 0)), pl.BlockSpec((tm, d // 2), lambda i, r: (i, 0)),
             pl.BlockSpec((2 * TOP_K, tm), lambda i, r: (0, i)), pl.BlockSpec((tm, LANES), lambda i, r: (i, 0)),
             pl.BlockSpec((N_EXPERTS, LANES), lambda i, r: (0, 0))]
    return shapes, specs


_TAIL_SCRATCH = [pltpu.VMEM((N_EXPERTS, 1), F32)]


def _post_ab(x, ya_p, yb_p, ya_s, yb_s, w_out, mod3, mod_base, rows, ln_g, ln_b, w_r, b_r, alpha):
    n_ctx_tiles = ya_p.shape[0] // TM
    (xc, xl), x_specs = _group_specs(x, n_ctx_tiles)
    d = xc.shape[1]
    nt = rows.shape[0]
    t = nt * TM
    a_w, q_w = ya_p.shape[1], yb_p.shape[1]
    mod_spec = lambda j: pl.BlockSpec((None, 1, d), lambda i, r: (mod_base + j * 16 + r[i], 0, 0))
    whole = lambda shp: pl.BlockSpec(shp, lambda i, r: (0,) * len(shp))
    ctx = lambda w: pl.BlockSpec((TM, w), lambda i, r: (jnp.minimum(i, n_ctx_tiles - 1), 0))
    lat = lambda w: pl.BlockSpec((TM, w), lambda i, r: (jnp.maximum(i - n_ctx_tiles, 0), 0))
    shapes, specs = _tail_out(t, d, TM)
    grid_spec = pltpu.PrefetchScalarGridSpec(
        num_scalar_prefetch=1, grid=(nt,),
        in_specs=x_specs + [ctx(a_w), ctx(q_w), lat(a_w), lat(q_w),
                            whole(w_out.shape), mod_spec(2), whole((1, d)), whole((1, d)), mod_spec(4), mod_spec(3),
                            whole(w_r.shape), whole(b_r.shape)],
        out_specs=specs,
        scratch_shapes=_TAIL_SCRATCH,
    )
    return pl.pallas_call(
        functools.partial(_post_ab_kernel, n_ctx_tiles=n_ctx_tiles, alpha=alpha),
        grid_spec=grid_spec, out_shape=shapes, compiler_params=_cparams(1), name="post_ab",
    )(rows, xc, xl, ya_p, yb_p, ya_s, yb_s, w_out, mod3, ln_g, ln_b, mod3, mod3, w_r, b_r)


def _unpack_f32_pairs(p):
    u = pltpu.bitcast(p, jnp.uint32)
    return jnp.concatenate([pltpu.bitcast(u << 16, F32), pltpu.bitcast(u & jnp.uint32(0xFFFF0000), F32)], axis=1)


def _ffn_residual(x1, y_packed, gate, g_f, ln_g, ln_b, alpha):
    y = gate[:, 0:1] * _unpack_f32_pairs(y_packed[0])
    for k in range(1, TOP_K):
        y = y + gate[:, k:k + 1] * _unpack_f32_pairs(y_packed[k])
    return _ln(alpha * x1 + g_f * y) * ln_g + ln_b


def _gmlp_kernel(rows_ref, xp_ref, y0_ref, y1_ref, y2_ref, y3_ref, pgate_ref, pgf_ref, plg_ref, plb_ref,
                 scm_ref, shm_ref, win_ref, bin_ref, gv_ref, bv_ref, wsp_ref, bsp_ref, wout_ref,
                 gm_ref, lg_ref, lb_ref, scf_ref, shf_ref, wr_ref, br_ref, x1_ref, hf_ref, route_ref, gate_ref,
                 cnt_ref, carry_ref, *, alpha):
    del rows_ref
    assert (xp_ref.shape[0] // 2) % CHUNK == 0
    c_w = win_ref.shape[1] // 2
    gd = c_w // C_GROUPS

    def norm_in(s):
        rs = s["rows"]
        x = _ffn_residual(xp_ref[rs, :], [r[rs, :] for r in (y0_ref, y1_ref, y2_ref, y3_ref)], pgate_ref[rs, :],
                          pgf_ref[...], plg_ref[...], plb_ref[...], alpha)
        return dict(s, x=x, h=(_ln(x) * (1.0 + scm_ref[...]) + shm_ref[...]).astype(BF16))

    def proj_in(s):
        return dict(s, z=_dot(s["h"], win_ref[...]) + bin_ref[...])

    def gate_split(s):
        z = s["z"]
        z = 0.5 * z * (1.0 + lax.erf(z * (2.0 ** -0.5)))
        return dict(s, u=z[:, :c_w], v=(_ln(z[:, c_w:]) * gv_ref[...] + bv_ref[...]).astype(BF16))

    def spatial(s):
        v = s["v"]
        chunks = []
        for n in range(v.shape[0] // CHUNK):
            groups = [_dot(wsp_ref[g], v[n * CHUNK:(n + 1) * CHUNK, g * gd:(g + 1) * gd]) + bsp_ref[g]
                      for g in range(C_GROUPS)]
            chunks.append(jnp.concatenate(groups, axis=1))
        return dict(s, t=(s["u"] * jnp.concatenate(chunks, axis=0)).astype(BF16))

    def proj_out(s):
        return dict(s, y=_dot(s["t"], wout_ref[...]))

    def tail(s):
        x1, hf, lg = _tail_rows(s["x"], s["y"], gm_ref[...], lg_ref[...], lb_ref[...], scf_ref[...], shf_ref[...],
                                wr_ref[...], br_ref[...], alpha)
        x1_ref[s["rows"], :] = x1
        hf_ref[s["rows"], :] = hf
        return dict(s, logits=lg)

    states = _run_skewed([norm_in, proj_in, gate_split, spatial, proj_out, tail],
                         [dict(rows=rs) for rs in _row_chains(xp_ref.shape[0])])
    _tail_route(jnp.concatenate([s["logits"] for s in states], axis=1), route_ref, gate_ref, cnt_ref, carry_ref)


def _gmlp(prev, mod3, mod_base, rows, w_in, b_in, g_v, b_v, w_sp, b_sp, w_out, ln_g, ln_b, w_r, b_r, alpha):
    x1p, y4, pgate, plg, plb = prev
    t, d = x1p.shape
    nt = rows.shape[0]
    tm = t // nt
    mod_spec = lambda j, base=mod_base: pl.BlockSpec((None, 1, d), lambda i, r: (base + j * 16 + r[i], 0, 0))
    whole = lambda shp: pl.BlockSpec(shp, lambda i, r: (0,) * len(shp))
    y_spec = lambda k: pl.BlockSpec((None, tm, d // 2), lambda i, r: (k, i, 0))
    shapes, specs = _tail_out(t, d, tm)
    grid_spec = pltpu.PrefetchScalarGridSpec(
        num_scalar_prefetch=1, grid=(nt,),
        in_specs=[pl.BlockSpec((tm, d), lambda i, r: (i, 0)), y_spec(0), y_spec(1), y_spec(2), y_spec(3),
                  pl.BlockSpec((tm, LANES), lambda i, r: (i, 0)), mod_spec(5, mod_base - 6 * 16),
                  whole((1, d)), whole((1, d)), mod_spec(1), mod_spec(0),
                  whole(w_in.shape), whole(b_in.shape), whole(g_v.shape), whole(b_v.shape),
                  whole(w_sp.shape), whole(b_sp.shape), whole(w_out.shape),
                  mod_spec(2), whole((1, d)), whole((1, d)), mod_spec(4), mod_spec(3),
                  whole(w_r.shape), whole(b_r.shape)],
        out_specs=specs,
        scratch_shapes=_TAIL_SCRATCH,
    )
    return pl.pallas_call(
        functools.partial(_gmlp_kernel, alpha=alpha),
        grid_spec=grid_spec, out_shape=shapes, compiler_params=_cparams(1, VMEM_LIMIT_GMLP), name="gmlp",
    )(rows, x1p, y4, y4, y4, y4, pgate, mod3, plg, plb, mod3, mod3, w_in, b_in, g_v, b_v, w_sp, b_sp, w_out,
      mod3, ln_g, ln_b, mod3, mod3, w_r, b_r)


def _moe_kernel(be_ref, bf_ref, nx_ref, sl_ref, hf_ref, nu_ref, x_ref, wgu_hbm, bgu_ref, wdn_hbm, bdn_ref, o_ref,
                wgu_f, wdn_f, wgu_s, wdn_s, sem, *, layer):
    i = pl.program_id(0)
    d_ff = wdn_s.shape[0]

    def weight_copies(e, slot):
        return (pltpu.make_async_copy(wgu_hbm.at[layer, e], wgu_f.at[slot], sem.at[0, slot]),
                pltpu.make_async_copy(wdn_hbm.at[layer, e], wdn_f.at[slot], sem.at[1, slot]))

    @pl.when(i < nu_ref[0])
    def _():
        @pl.when(bf_ref[i] == 1)
        def _():
            slot = sl_ref[i]

            @pl.when(i == 0)
            def _():
                for c in weight_copies(be_ref[i], slot):
                    c.start()

            for c in weight_copies(be_ref[i], slot):
                c.wait()
            wgu_s[...] = wgu_f[slot].astype(BF16)
            wdn_s[...] = wdn_f[slot].astype(BF16)

            @pl.when(nx_ref[i] >= 0)
            def _():
                for c in weight_copies(nx_ref[i], 1 - slot):
                    c.start()

        def up(s):
            return dict(s, gu=_dot(_unpack_bf16_pairs(x_ref[s["rows"], :]), wgu_s[...]) + bgu_ref[...])

        def act(s):
            gu = s["gu"]
            gate = jnp.minimum(gu[:, :d_ff], SWIGLU_LIMIT)
            lin = jnp.clip(gu[:, d_ff:], -SWIGLU_LIMIT, SWIGLU_LIMIT)
            glu = gate * jax.nn.sigmoid(SWIGLU_ALPHA * gate)
            return dict(s, gu=None, hid=((lin + 1.0) * glu).astype(BF16))

        def down(s):
            o_ref[s["rows"], :] = _pack_bf16_pairs(_dot(s["hid"], wdn_s[...]) + bdn_ref[...])
            return dict(s, hid=None)

        def run(n_chains, n_rows):
            _run_skewed([up, act, down], [dict(rows=rs) for rs in _row_chains(n_rows, n_chains)])

        @pl.when(hf_ref[i] == 0)
        def _():
            run(2, TM_MOE)

        @pl.when(hf_ref[i] == 1)
        def _():
            run(1, TM_MOE // 2)


def _moe(xs, plan, w_gu, b_gu, w_dn, b_dn, layer):
    n_rows = xs.shape[0]
    depth, n_e, d, ff2 = w_gu.shape
    d_ff = ff2 // 2
    nt = n_rows // TM_MOE
    n_plan = len(plan)
    row_spec = pl.BlockSpec((TM_MOE, d // 2), lambda i, *p: (jnp.minimum(i, p[-1][0] - 1), 0))
    bias_spec = lambda w: pl.BlockSpec((None, None, 1, w), lambda i, *p: (layer, p[0][i], 0, 0))
    grid_spec = pltpu.PrefetchScalarGridSpec(
        num_scalar_prefetch=n_plan, grid=(nt,),
        in_specs=[row_spec, pl.BlockSpec(memory_space=pl.ANY), bias_spec(ff2),
                  pl.BlockSpec(memory_space=pl.ANY), bias_spec(d)],
        out_specs=row_spec,
        scratch_shapes=[pltpu.VMEM((2, d, ff2), F32), pltpu.VMEM((2, d_ff, d), F32),
                        pltpu.VMEM((d, ff2), BF16), pltpu.VMEM((d_ff, d), BF16),
                        pltpu.SemaphoreType.DMA((2, 2))],
    )
    return pl.pallas_call(
        functools.partial(_moe_kernel, layer=layer), grid_spec=grid_spec,
        out_shape=jax.ShapeDtypeStruct((n_rows, d // 2), jnp.int32),
        compiler_params=_cparams(1), name="moe_experts",
    )(*plan, xs, w_gu, b_gu.reshape(depth, n_e, 1, ff2), w_dn, b_dn.reshape(depth, n_e, 1, d))


def _combine_kernel(rows_ref, x_ref, y0_ref, y1_ref, y2_ref, y3_ref, gate_ref, gf_ref, lg_ref, lb_ref, *o_refs,
                    alpha, n_ctx_tiles):
    del rows_ref
    out = _ffn_residual(x_ref[...], [r[...] for r in (y0_ref, y1_ref, y2_ref, y3_ref)], gate_ref[...], gf_ref[...],
                        lg_ref[...], lb_ref[...], alpha)
    if n_ctx_tiles is None:
        o_refs[0][...] = out
    else:
        @pl.when(pl.program_id(0) < n_ctx_tiles)
        def _():
            o_refs[0][...] = out

        @pl.when(pl.program_id(0) >= n_ctx_tiles)
        def _():
            o_refs[1][...] = out


def _combine(x1, y4, gate, mod3, mod_base, rows, ln_g, ln_b, alpha, t_ctx=None):
    t, d = x1.shape
    nt = rows.shape[0]
    tm = t // nt
    y_spec = lambda k: pl.BlockSpec((None, tm, d // 2), lambda i, r: (k, i, 0))
    if t_ctx is None:
        n_ctx_tiles = None
        out_specs = pl.BlockSpec((tm, d), lambda i, r: (i, 0))
        out_shape = jax.ShapeDtypeStruct((t, d), F32)
    else:
        n_ctx_tiles = t_ctx // tm
        out_specs = [pl.BlockSpec((tm, d), lambda i, r: (jnp.minimum(i, n_ctx_tiles - 1), 0)),
                     pl.BlockSpec((tm, d), lambda i, r: (jnp.maximum(i - n_ctx_tiles, 0), 0))]
        out_shape = [jax.ShapeDtypeStruct((t_ctx, d), F32), jax.ShapeDtypeStruct((t - t_ctx, d), F32)]
    grid_spec = pltpu.PrefetchScalarGridSpec(
        num_scalar_prefetch=1, grid=(nt,),
        in_specs=[pl.BlockSpec((tm, d), lambda i, r: (i, 0)), y_spec(0), y_spec(1), y_spec(2), y_spec(3),
                  pl.BlockSpec((tm, LANES), lambda i, r: (i, 0)),
                  pl.BlockSpec((None, 1, d), lambda i, r: (mod_base + 5 * 16 + r[i], 0, 0)),
                  pl.BlockSpec((1, d), lambda i, r: (0, 0)), pl.BlockSpec((1, d), lambda i, r: (0, 0))],
        out_specs=out_specs,
    )
    return pl.pallas_call(
        functools.partial(_combine_kernel, alpha=alpha, n_ctx_tiles=n_ctx_tiles), grid_spec=grid_spec,
        out_shape=out_shape, compiler_params=_cparams(1), name="moe_combine",
    )(rows, x1, y4, y4, y4, y4, gate, mod3, ln_g, ln_b)


def _route_plan(route, counts, n_tiles):
    idx, rank = route[:TOP_K], route[TOP_K:]
    padded = (counts + TM_MOE - 1) // TM_MOE * TM_MOE
    pend = jnp.cumsum(padded)
    pstart = pend - padded
    onehot = idx[:, :, None] == jnp.arange(N_EXPERTS, dtype=jnp.int32)[None, None, :]
    dest = jnp.sum(jnp.where(onehot, pstart[None, None, :], 0), axis=-1) + rank
    n_used = (pend[-1] // TM_MOE).astype(jnp.int32)
    tile_start = jnp.arange(n_tiles, dtype=jnp.int32) * TM_MOE
    blk_e = jnp.sum((tile_start[:, None] >= pend[None, :]).astype(jnp.int32), axis=1)
    last_e = jnp.sum((jnp.maximum(n_used - 1, 0) * TM_MOE >= pend).astype(jnp.int32))
    blk_e = jnp.minimum(jnp.where(jnp.arange(n_tiles) < n_used, blk_e, last_e), N_EXPERTS - 1).astype(jnp.int32)
    blk_first = jnp.concatenate([jnp.ones((1,), jnp.int32), (blk_e[1:] != blk_e[:-1]).astype(jnp.int32)])
    ar = jnp.arange(N_EXPERTS, dtype=jnp.int32)
    later = jnp.logical_and((counts > 0)[None, :], ar[None, :] > ar[:, None])
    next_e = jnp.min(jnp.where(later, ar[None, :], N_EXPERTS), axis=1)
    next_e = jnp.where(next_e == N_EXPERTS, -1, next_e)
    blk_next = jnp.sum(jnp.where(blk_e[:, None] == ar[None, :], next_e[None, :], 0), axis=1).astype(jnp.int32)
    blk_slot = ((jnp.cumsum(blk_first) - 1) % 2).astype(jnp.int32)
    left = jnp.sum(jnp.where(blk_e[:, None] == ar[None, :], (pstart + counts)[None, :], 0), axis=1) - tile_start
    blk_half = (left <= TM_MOE // 2).astype(jnp.int32)
    return dest, (blk_e, blk_first, blk_next, blk_slot, blk_half, n_used.reshape(1))


SC_CORES = 2
SC_SUBCORES = 16


def _sc_mesh():
    return plsc.VectorSubcoreMesh(core_axis_name="core", subcore_axis_name="subcore")


def _sc_scatter_rows(src, dest, n_rows, window):
    n_slots, t = dest.shape
    w = src.shape[1]
    per_worker = t // (SC_CORES * SC_SUBCORES)
    assert per_worker * SC_CORES * SC_SUBCORES == t and per_worker % window == 0

    @functools.partial(pl.kernel, out_type=jax.ShapeDtypeStruct((n_rows, w), src.dtype), mesh=_sc_mesh(),
                       scratch_types=[pltpu.VMEM((n_slots, window), jnp.int32), pltpu.VMEM((window, w), src.dtype)],
                       name="sc_dispatch")
    def scatter(src_hbm, dest_hbm, out_hbm, idx_v, rows_v):
        worker = lax.axis_index("subcore") * SC_CORES + lax.axis_index("core")

        @pl.loop(0, per_worker // window)
        def _(c):
            base = pl.multiple_of(worker * per_worker + c * window, window)
            pltpu.sync_copy(src_hbm.at[pl.ds(base, window)], rows_v)
            pltpu.sync_copy(dest_hbm.at[:, pl.ds(base, window)], idx_v)
            for k in range(n_slots):
                pltpu.sync_copy(rows_v, out_hbm.at[idx_v.at[k]])

    return scatter(src, dest)


def _sc_gather_rows(table, idx, window):
    n = idx.shape[0]
    w = table.shape[1]
    per_worker = n // (SC_CORES * SC_SUBCORES)
    assert per_worker * SC_CORES * SC_SUBCORES == n and per_worker % window == 0

    @functools.partial(pl.kernel, out_type=jax.ShapeDtypeStruct((n, w), table.dtype), mesh=_sc_mesh(),
                       scratch_types=[pltpu.VMEM((window,), jnp.int32), pltpu.VMEM((window, w), table.dtype)],
                       name="sc_collect")
    def gather(table_hbm, idx_hbm, out_hbm, idx_v, rows_v):
        worker = lax.axis_index("subcore") * SC_CORES + lax.axis_index("core")

        @pl.loop(0, per_worker // window)
        def _(c):
            base = pl.multiple_of(worker * per_worker + c * window, window)
            pltpu.sync_copy(idx_hbm.at[pl.ds(base, window)], idx_v)
            pltpu.sync_copy(table_hbm.at[idx_v], rows_v)
            pltpu.sync_copy(rows_v, out_hbm.at[pl.ds(base, window)])

    return gather(table, idx)


def _dft_tables(n):
    k = jnp.arange(n, dtype=jnp.int32)

    def trig(rows):
        ang = ((rows[:, None] * k[None, :]) % n).astype(F32) * (2.0 * math.pi / n)
        return jnp.cos(ang), jnp.sin(ang)

    n2 = 64
    if n < 4 * n2:
        return trig(k)
    ca, sa = (z[:, None, :] for z in trig(jnp.arange(n // n2, dtype=jnp.int32) * n2))
    cb, sb = (z[None, :, :] for z in trig(jnp.arange(n2, dtype=jnp.int32)))
    return (ca * cb - sa * sb).reshape(n, n), (sa * cb + ca * sb).reshape(n, n)


def _rope_tables(n_lat, n_ctx_rows):
    half, quarter = HEAD_DIM // 2, HEAD_DIM // 4
    tpos = jnp.arange(n_lat, dtype=jnp.int32)
    lane = jnp.arange(LANES, dtype=jnp.int32) % HEAD_DIM
    pos = jnp.where(lane[None, :] < half, (tpos // GRID_W)[:, None], (tpos % GRID_W)[:, None]).astype(F32)
    fidx = (lane % quarter).astype(F32)
    freqs = ROPE_THETA ** (-fidx / quarter)
    ang = pos * freqs[None, :]
    cos, sin = jnp.cos(ang), jnp.sin(ang)
    first = (lane % half) < quarter
    sa = jnp.where(first[None, :], -sin, 0.0)
    sb = jnp.where(first[None, :], 0.0, sin)
    ident = lambda v: jnp.full((n_ctx_rows, LANES), v, F32)
    return (jnp.concatenate([ident(1.0), cos]), jnp.concatenate([ident(0.0), sa]), jnp.concatenate([ident(0.0), sb]))


def kernel(x_prompt, x_sample, cache_k_ab, cache_v_ab, c, c_ctx, w_ada, b_ada, ln_mix_g, ln_mix_b, ln_ffn_g, ln_ffn_b, w_in_ab, w_out_ab, sink_ab, w_in_c, b_in_c, ln_v_g, ln_v_b, w_sp, b_sp, w_out_c, w_router, b_router, w_gu, b_gu, w_dn, b_dn):
    n_ctx_b, ctx_seq, d = x_prompt.shape
    n_lat_b, lat_seq, _ = x_sample.shape
    depth = w_ada.shape[0]
    t_ctx, t_lat = n_ctx_b * ctx_seq, n_lat_b * lat_seq
    t = t_ctx + t_lat
    alpha = (2 * depth) ** 0.25
    kv_w = N_KV_HEADS * HEAD_DIM
    a_w = d // 4
    q_w = d - a_w
    assert t_ctx % TM == 0 and lat_seq % TM == 0 and t_ctx % lat_seq == 0 and n_lat_b + 1 <= 16

    cond_rows = lambda tm: jnp.asarray(np.concatenate(
        [np.zeros(t_ctx // tm, np.int32), 1 + np.arange(t_lat // tm, dtype=np.int32) // (lat_seq // tm)]))
    rows, rows_c = cond_rows(TM), cond_rows(TM_COMBINE)
    rblk = jnp.asarray(np.concatenate(
        [np.zeros(t_ctx // TM, np.int32), 1 + np.arange(t_lat // TM, dtype=np.int32) % (lat_seq // TM)]))

    cond16 = jnp.zeros((16, d), F32).at[0].set(c_ctx).at[1:1 + n_lat_b].set(c)
    mod = _ada(cond16, w_ada, b_ada)
    mod3 = mod.reshape(depth, 16, 6, d).transpose(0, 2, 1, 3).reshape(depth * 6 * 16, 1, d)

    cos, sa, sb = _rope_tables(lat_seq, TM)
    cs_ctx, ss_ctx = (z.astype(BF16) for z in _dft_tables(ctx_seq))
    cs_lat, ss_lat = (z.astype(BF16) for z in _dft_tables(lat_seq))
    gd = a_w // A_GROUPS
    cd, sd = _dft_tables(gd)
    eye = jnp.eye(A_GROUPS, dtype=F32)
    bdc, bds = jnp.kron(eye, cd).astype(BF16), jnp.kron(eye, sd).astype(BF16)

    past = cache_k_ab.shape[2]
    cache_k = cache_k_ab.reshape(n_lat_b, -1, past, kv_w).astype(BF16)
    cache_v = cache_v_ab.reshape(n_lat_b, -1, past, kv_w).astype(BF16)

    x = (x_prompt.reshape(t_ctx, d), x_sample.reshape(t_lat, d))
    row2 = lambda v: v.reshape(1, -1)
    ks, vs = [], []
    for l in range(depth):
        j = l // 2
        mod_base = l * 6 * 16
        w_r, b_r = w_router[l].T, b_router[l].reshape(-1, 1)
        if l % 2 == 0:
            ac, as_, q, k32, v32, k, v = _inproj(x, t_ctx // TM, mod3, mod_base, rows, rblk, w_in_ab[j].astype(BF16),
                                                 cos, sa, sb, bdc, bds, a_w, q_w, kv_w)
            ks.append(k32[:t_ctx].reshape(n_ctx_b, ctx_seq, N_KV_HEADS, HEAD_DIM))
            vs.append(v32[:t_ctx].reshape(n_ctx_b, ctx_seq, N_KV_HEADS, HEAD_DIM))
            ya_p = _fourier(ac, as_, cs_ctx, ss_ctx, n_ctx_b, ctx_seq, 0, min(ctx_seq, 512))
            ya_s = _fourier(ac, as_, cs_lat, ss_lat, n_lat_b, lat_seq, t_ctx // lat_seq, min(lat_seq, 1024))
            yb_p = _attn_ctx(sink_ab[j], q, k, v, n_ctx_b, ctx_seq)
            yb_s = _attn_lat(sink_ab[j], q, k, v, cache_k, cache_v, j, n_lat_b, lat_seq, t_ctx)
            x1, hf, route, gate, cnt = _post_ab(x, ya_p, yb_p, ya_s, yb_s, w_out_ab[j].astype(BF16), mod3, mod_base,
                                                rows, row2(ln_mix_g[l]), row2(ln_mix_b[l]), w_r, b_r, alpha)
        else:
            x1, hf, route, gate, cnt = _gmlp(prev, mod3, mod_base, rows, w_in_c[j].astype(BF16), row2(b_in_c[j]),
                                             row2(ln_v_g[j]), row2(ln_v_b[j]), w_sp[j].astype(BF16),
                                             b_sp[j][:, :, None], w_out_c[j].astype(BF16), row2(ln_mix_g[l]),
                                             row2(ln_mix_b[l]), w_r, b_r, alpha)
        n_rows = t * TOP_K + N_EXPERTS * TM_MOE
        dest, plan = _route_plan(route, cnt[:, 0].astype(jnp.int32), n_rows // TM_MOE)
        xs = _sc_scatter_rows(hf, dest, n_rows, 128)
        out_sorted = _moe(xs, plan, w_gu, b_gu, w_dn, b_dn, l)
        y4 = _sc_gather_rows(out_sorted, dest.reshape(-1), 128).reshape(TOP_K, t, d // 2)
        if l % 2 == 0 and l + 1 < depth:
            prev = (x1, y4, gate, row2(ln_ffn_g[l]), row2(ln_ffn_b[l]))
        else:
            x = _combine(x1, y4, gate, mod3, mod_base, rows_c, row2(ln_ffn_g[l]), row2(ln_ffn_b[l]), alpha,
                         t_ctx if l == depth - 1 else None)

    y_prompt = x[0].reshape(n_ctx_b, ctx_seq, d)
    y_sample = x[1].reshape(n_lat_b, lat_seq, d)
    return (y_prompt, y_sample, jnp.stack(ks, axis=1), jnp.stack(vs, axis=1))
```

```python
import functools
import math

import numpy as np
import jax
import jax.numpy as jnp
from jax import lax
from jax.experimental import pallas as pl
from jax.experimental.pallas import tpu as pltpu
from jax.experimental.pallas import tpu_sc as plsc

GRID_W = 64
BLK = 128
WINDOW = 128
HEAD_DIM = 64
A_GROUPS = 4
N_KV_HEADS = 4
C_GROUPS = 8
CHUNK = 128
N_EXPERTS = 32
TOP_K = 4
SWIGLU_LIMIT = 7.0
SWIGLU_ALPHA = 1.702
ROPE_THETA = 10000.0
LN_EPS = 1e-6
NEG_INF = -1e30
LOG2_E = math.log2(math.e)

LANES = 128
TM = 1024
TM_COMBINE = 512
ATTN_BLOCKS_PER_STEP = 4
TM_MOE = 512
VMEM_LIMIT = 52 * 1024 * 1024
VMEM_LIMIT_GMLP = 60 * 1024 * 1024

F32 = jnp.float32
BF16 = jnp.bfloat16


def _cparams(n_axes, vmem_limit=VMEM_LIMIT):
    return pltpu.CompilerParams(dimension_semantics=("arbitrary",) * n_axes, vmem_limit_bytes=vmem_limit)


def _ln(x):
    mu = jnp.mean(x, axis=-1, keepdims=True)
    xc = x - mu
    var = jnp.mean(xc * xc, axis=-1, keepdims=True)
    return xc * lax.rsqrt(var + LN_EPS)


def _dot(a, b):
    return jnp.dot(a, b, preferred_element_type=F32)


def _dot_nt(a, b):
    return lax.dot_general(a, b, (((1,), (1,)), ((), ())), preferred_element_type=F32)


def _split(a):
    hi = a.astype(BF16)
    lo = (a - hi.astype(F32)).astype(BF16)
    return hi, lo


def _dot_3pass(a, b):
    a_hi, a_lo = _split(a)
    b_hi, b_lo = _split(b)
    return _dot(a_hi, b_hi) + (_dot(a_hi, b_lo) + _dot(a_lo, b_hi))


def _lane_select(cols, width):
    m = cols[0].shape[0]
    lane = lax.broadcasted_iota(jnp.int32, (m, width), 1)
    out = jnp.zeros((m, width), cols[0].dtype)
    for j, c in enumerate(cols):
        out = jnp.where(lane == j, c, out)
    return out


def _pack_bf16_pairs(v):
    n = v.shape[1] // 2
    lo = pltpu.bitcast(v[:, :n].astype(BF16).astype(F32), jnp.uint32) >> 16
    hi = pltpu.bitcast(v[:, n:].astype(BF16).astype(F32), jnp.uint32) & jnp.uint32(0xFFFF0000)
    return pltpu.bitcast(lo | hi, jnp.int32)


def _unpack_bf16_pairs(p):
    u = pltpu.bitcast(p, jnp.uint32)
    lo = pltpu.bitcast(u << 16, F32)
    hi = pltpu.bitcast(u & jnp.uint32(0xFFFF0000), F32)
    return jnp.concatenate([lo, hi], axis=1).astype(BF16)


def _row_chains(n_rows, n_chains=2):
    step = n_rows // n_chains
    return [slice(c * step, (c + 1) * step) for c in range(n_chains)]


def _run_skewed(stages, states):
    states = list(states)
    for step in range(len(stages) + len(states) - 1):
        for c in range(len(states)):
            if 0 <= step - c < len(stages):
                states[c] = stages[step - c](states[c])
    return states


def _tail_rows(x, y, g_m, ln_g, ln_b, sc_f, sh_f, w_r, b_r, alpha):
    x1, hf = _tail_norm(x, y, g_m, ln_g, ln_b, sc_f, sh_f, alpha)
    packed, logits = _tail_logits(hf, w_r, b_r)
    return x1, packed, logits


def _tail_norm(x, y, g_m, ln_g, ln_b, sc_f, sh_f, alpha):
    x1 = _ln(alpha * x + g_m * y) * ln_g + ln_b
    return x1, _ln(x1) * (1.0 + sc_f) + sh_f


def _tail_logits(hf, w_r, b_r):
    hf_hi, hf_lo = _split(hf)
    w_hi, w_lo = _split(w_r)
    logits = _dot_nt(w_hi, hf_hi) + (_dot_nt(w_hi, hf_lo) + _dot_nt(w_lo, hf_hi)) + b_r
    return _pack_bf16_pairs(hf), logits


def _tail_route(logits, route_ref, gate_ref, cnt_ref, carry_ref):
    @pl.when(pl.program_id(0) == 0)
    def _():
        carry_ref[...] = jnp.zeros_like(carry_ref)

    tm = logits.shape[1]
    sub = lax.broadcasted_iota(jnp.int32, logits.shape, 0)
    vals = logits
    top_v, top_i = [], []
    for _ in range(TOP_K):
        m = jnp.max(vals, axis=0, keepdims=True)
        am = jnp.min(jnp.where(vals == m, sub, N_EXPERTS), axis=0, keepdims=True)
        top_v.append(m)
        top_i.append(am)
        vals = jnp.where(sub == am, -jnp.inf, vals)
    e = [jnp.exp(v - top_v[0]) for v in top_v]
    denom = e[0] + e[1] + e[2] + e[3]
    gates_t = jnp.concatenate([ek / denom for ek in e] + [jnp.zeros((LANES - TOP_K, tm), F32)], axis=0)
    gate_ref[...] = gates_t.T

    member = jnp.zeros(logits.shape, F32)
    for am in top_i:
        member = jnp.where(sub == am, 1.0, member)
    r_i = lax.broadcasted_iota(jnp.int32, (tm, tm), 0)
    c_i = lax.broadcasted_iota(jnp.int32, (tm, tm), 1)
    earlier = jnp.where(r_i < c_i, 1.0, 0.0).astype(BF16)
    before = _dot(member.astype(BF16), earlier) + carry_ref[...]
    ranks = [jnp.sum(jnp.where(sub == am, before, 0.0), axis=0, keepdims=True).astype(jnp.int32) for am in top_i]
    route_ref[...] = jnp.concatenate(top_i + ranks, axis=0)
    carry = carry_ref[...] + jnp.sum(member, axis=1, keepdims=True)
    carry_ref[...] = carry
    cnt_ref[...] = jnp.broadcast_to(carry, cnt_ref.shape)


def _ada_kernel(cond_ref, w_ref, b_ref, o_ref):
    c = cond_ref[...]
    s = (c * jax.nn.sigmoid(c)).astype(BF16)
    o_ref[...] = _dot(s, w_ref[...].astype(BF16)) + b_ref[...]


def _ada(cond16, w_ada, b_ada):
    depth, d, n = w_ada.shape
    tn = 1536
    return pl.pallas_call(
        _ada_kernel,
        grid=(depth, n // tn),
        in_specs=[
            pl.BlockSpec((16, d), lambda l, j: (0, 0)),
            pl.BlockSpec((None, d, tn), lambda l, j: (l, 0, j)),
            pl.BlockSpec((None, 1, tn), lambda l, j: (l, 0, j)),
        ],
        out_specs=pl.BlockSpec((None, 16, tn), lambda l, j: (l, 0, j)),
        out_shape=jax.ShapeDtypeStruct((depth, 16, n), F32),
        compiler_params=_cparams(2),
        name="ada_mod",
    )(cond16, w_ada, b_ada.reshape(depth, 1, n))


def _group_rows(xc_ref, xl_ref, rows, n_ctx_tiles):
    return jnp.where(pl.program_id(0) < n_ctx_tiles, xc_ref[rows, :], xl_ref[rows, :])


def _group_specs(x, n_ctx_tiles):
    if isinstance(x, tuple):
        arrays, lat = x, (lambda i, *_: (jnp.maximum(i - n_ctx_tiles, 0), 0))
    else:
        arrays, lat = (x, x), (lambda i, *_: (jnp.maximum(i, n_ctx_tiles), 0))
    d = arrays[0].shape[1]
    ctx = lambda i, *_: (jnp.minimum(i, n_ctx_tiles - 1), 0)
    return arrays, [pl.BlockSpec((TM, d), ctx), pl.BlockSpec((TM, d), lat)]


def _inproj_kernel(rows_ref, rblk_ref, xc_ref, xl_ref, sc_ref, sh_ref, w_ref, cos_ref, sa_ref, sb_ref, bdc_ref,
                   bds_ref, ac_ref, as_ref, q_ref, k_ref, v_ref, kb_ref, vb_ref, *, a_w, q_w, kv_w, n_ctx_tiles):
    del rows_ref, rblk_ref
    g = q_w // kv_w

    def norm_in(s):
        x = _group_rows(xc_ref, xl_ref, s["rows"], n_ctx_tiles)
        return dict(s, h=(_ln(x) * (1.0 + sc_ref[...]) + sh_ref[...]).astype(BF16))

    def proj(s):
        return dict(s, p=_dot(s["h"], w_ref[...]))

    def finish(s):
        rs, p = s["rows"], s["p"]
        a = p[:, :a_w].astype(BF16)
        ac_ref[rs, :] = _dot(a, bdc_ref[...]).astype(BF16)
        as_ref[rs, :] = _dot(a, bds_ref[...]).astype(BF16)
        cos, sa, sb = cos_ref[rs, :], sa_ref[rs, :], sb_ref[rs, :]

        def rope(t):
            w = t.shape[1]
            reps = w // LANES
            c, a_, b_ = (jnp.tile(z, (1, reps)) for z in (cos, sa, sb))
            nxt = pltpu.roll(t, w - HEAD_DIM // 4, 1)
            prv = pltpu.roll(t, HEAD_DIM // 4, 1)
            return t * c + nxt * a_ + prv * b_

        q = rope(p[:, a_w:a_w + q_w]) * (HEAD_DIM ** -0.5 * LOG2_E)
        lane = lax.broadcasted_iota(jnp.int32, (q.shape[0], LANES), 1)
        for j in range(q_w // HEAD_DIM):
            tile = q[:, (j // 2) * LANES:(j // 2 + 1) * LANES]
            dst_low = (j // g) % 2 == 0
            if (j % 2 == 0) != dst_low:
                tile = pltpu.roll(tile, HEAD_DIM, 1)
            keep = (lane < HEAD_DIM) if dst_low else (lane >= HEAD_DIM)
            q_ref[rs, j * LANES:(j + 1) * LANES] = jnp.where(keep, tile, 0.0).astype(BF16)
        k = rope(p[:, a_w + q_w:a_w + q_w + kv_w])
        v = p[:, a_w + q_w + kv_w:]
        k_ref[rs, :] = k
        v_ref[rs, :] = v
        kb_ref[rs, :] = k.astype(BF16)
        vb_ref[rs, :] = v.astype(BF16)
        return s

    _run_skewed([norm_in, proj, finish], [dict(rows=rs) for rs in _row_chains(xc_ref.shape[0])])


def _inproj(x, n_ctx_tiles, mod3, mod_base, rows, rblk, w_in, cos, sa, sb, bdc, bds, a_w, q_w, kv_w):
    (xc, xl), x_specs = _group_specs(x, n_ctx_tiles)
    d = xc.shape[1]
    t = rows.shape[0] * TM
    n = w_in.shape[1]
    nt = t // TM
    mod_spec = lambda j: pl.BlockSpec((None, 1, d), lambda i, r, rb: (mod_base + j * 16 + r[i], 0, 0))
    whole = lambda shp: pl.BlockSpec(shp, lambda i, r, rb: (0,) * len(shp))
    rope_spec = pl.BlockSpec((TM, LANES), lambda i, r, rb: (rb[i], 0))
    tok = lambda w: pl.BlockSpec((TM, w), lambda i, r, rb: (i, 0))
    grid_spec = pltpu.PrefetchScalarGridSpec(
        num_scalar_prefetch=2, grid=(nt,),
        in_specs=x_specs + [mod_spec(1), mod_spec(0), whole((d, n)), rope_spec, rope_spec, rope_spec,
                            whole((a_w, a_w)), whole((a_w, a_w))],
        out_specs=[tok(a_w), tok(a_w), tok(2 * q_w), tok(kv_w), tok(kv_w), tok(kv_w), tok(kv_w)],
    )
    return pl.pallas_call(
        functools.partial(_inproj_kernel, a_w=a_w, q_w=q_w, kv_w=kv_w, n_ctx_tiles=n_ctx_tiles),
        grid_spec=grid_spec,
        out_shape=[jax.ShapeDtypeStruct((t, a_w), BF16), jax.ShapeDtypeStruct((t, a_w), BF16),
                   jax.ShapeDtypeStruct((t, 2 * q_w), BF16), jax.ShapeDtypeStruct((t, kv_w), F32),
                   jax.ShapeDtypeStruct((t, kv_w), F32), jax.ShapeDtypeStruct((t, kv_w), BF16),
                   jax.ShapeDtypeStruct((t, kv_w), BF16)],
        compiler_params=_cparams(1),
        name="inproj_ab",
    )(rows, rblk, xc, xl, mod3, mod3, w_in, cos, sa, sb, bdc, bds)


def _fourier_kernel(c_ref, s_ref, ac_ref, as_ref, o_ref, *, scale):
    y = _dot(c_ref[...], ac_ref[...]) - _dot(s_ref[...], as_ref[...])
    o_ref[...] = (y * scale).astype(BF16)


def _fourier(ac, as_, cs, ss, n_batch, seq, blk_off, tm):
    a_w = ac.shape[1]
    nr = seq // tm
    a_spec = pl.BlockSpec((seq, a_w), lambda r, b: (blk_off + b, 0))
    t_spec = pl.BlockSpec((tm, seq), lambda r, b: (r, 0))
    return pl.pallas_call(
        functools.partial(_fourier_kernel, scale=(seq * (a_w // A_GROUPS)) ** -0.5),
        grid=(nr, n_batch),
        in_specs=[t_spec, t_spec, a_spec, a_spec],
        out_specs=pl.BlockSpec((tm, a_w), lambda r, b: (b * nr + r, 0)),
        out_shape=jax.ShapeDtypeStruct((n_batch * seq, a_w), BF16),
        compiler_params=_cparams(2),
        name="fourier_%d" % seq,
    )(cs, ss, ac, as_)


def _attend_pairs(sink_ref, problems, *, g):
    def scores(s):
        pr, h0, p = s["pr"], s["h0"], s["h0"] // (2 * g)
        qp = jnp.concatenate([pr["q"][:, (h0 + j) * LANES:(h0 + j + 1) * LANES] for j in range(g)], axis=0)
        sc = [_dot_nt(qp, kf(p)) for kf in pr["keys"]]
        if pr["mask"] is not None:
            sc[0] = jnp.where(jnp.concatenate([pr["mask"]] * g, axis=0), sc[0], NEG_INF)
        return dict(s, sc=sc)

    def row_max(s):
        rows = s["pr"]["q"].shape[0]
        m = jnp.concatenate([jnp.full((rows, 1), sink_ref[s["h0"] + j] * LOG2_E, F32) for j in range(g)], axis=0)
        sink_col = m
        for sc in s["sc"]:
            m = jnp.maximum(m, jnp.max(sc, axis=-1, keepdims=True))
        return dict(s, m=m, sink=jnp.exp2(sink_col - m))

    def weights(s):
        return dict(s, e=[jnp.exp2(sc - s["m"]).astype(BF16) for sc in s["sc"]], sc=None)

    def weighted_values(s):
        p = s["h0"] // (2 * g)
        acc = None
        for e, vf in zip(s["e"], s["pr"]["values"]):
            v = vf(p)
            pv = _dot(e, jnp.concatenate([v, jnp.ones_like(v)], axis=1))
            acc = pv if acc is None else acc + pv
        return dict(s, acc=acc, e=None)

    def normalise(s):
        acc = s["acc"]
        return dict(s, o=acc[:, :LANES] / (acc[:, LANES:LANES + 1] + s["sink"]), acc=None)

    states = _run_skewed([scores, row_max, weights, weighted_values, normalise],
                         [dict(pr=pr, h0=h * g) for pr in problems for h in range(N_KV_HEADS)])
    for n, pr in enumerate(problems):
        rows = pr["q"].shape[0]
        lane = lax.broadcasted_iota(jnp.int32, (rows, LANES), 1)
        heads = {}
        for s in states[n * N_KV_HEADS:(n + 1) * N_KV_HEADS]:
            for j in range(g):
                blk = s["o"][j * rows:(j + 1) * rows]
                head = s["h0"] + j
                if (((head // g) % 2 == 0) != (head % 2 == 0)):
                    blk = pltpu.roll(blk, HEAD_DIM, 1)
                heads[head] = blk
        for t in range(len(heads) // 2):
            tile = jnp.where(lane < HEAD_DIM, heads[2 * t], heads[2 * t + 1])
            pr["o"][:, t * LANES:(t + 1) * LANES] = tile.astype(BF16)


def _attn_ctx_kernel(sink_ref, q_ref, k_ref, v_ref, o_ref, *, g):
    pair = lambda ref: (lambda p: ref[:, p * LANES:(p + 1) * LANES])
    _attend_pairs(sink_ref, [dict(q=q_ref, keys=[pair(k_ref)], values=[pair(v_ref)], mask=None, o=o_ref)], g=g)


def _attn_ctx(sink, q, k, v, n_batch, seq):
    qp_w, kv_w = q.shape[1], k.shape[1]
    q_w = qp_w // 2
    g = q_w // kv_w
    return pl.pallas_call(
        functools.partial(_attn_ctx_kernel, g=g),
        grid=(n_batch,),
        in_specs=[pl.BlockSpec(memory_space=pltpu.SMEM),
                  pl.BlockSpec((seq, qp_w), lambda b: (b, 0)),
                  pl.BlockSpec((seq, kv_w), lambda b: (b, 0)),
                  pl.BlockSpec((seq, kv_w), lambda b: (b, 0))],
        out_specs=pl.BlockSpec((seq, q_w), lambda b: (b, 0)),
        out_shape=jax.ShapeDtypeStruct((n_batch * seq, q_w), BF16),
        compiler_params=_cparams(1),
        name="attn_ctx",
    )(sink, q, k, v)


def _attn_lat_kernel(sink_ref, q_ref, k_ref, v_ref, ck_ref, cv_ref, o_ref, *, g, seq):
    n_loc = 3 * BLK
    row = lax.broadcasted_iota(jnp.int32, (BLK, n_loc), 0)
    col = lax.broadcasted_iota(jnp.int32, (BLK, n_loc), 1)
    ctx = lambda ref: (lambda p: ref[:, p * LANES:(p + 1) * LANES])
    problems = []
    for sub in range(q_ref.shape[0] // BLK):
        i = pl.program_id(1) * (q_ref.shape[0] // BLK) + sub
        start = pl.multiple_of(jnp.clip((i - 1) * BLK, 0, seq - n_loc), BLK)
        band = jnp.abs(row + (i * BLK - start) - col) <= WINDOW
        loc = lambda ref, start=start: (lambda p: ref[pl.ds(start, n_loc), p * LANES:(p + 1) * LANES])
        rows = slice(sub * BLK, (sub + 1) * BLK)
        problems.append(dict(q=q_ref.at[rows, :], keys=[loc(k_ref), ctx(ck_ref)], values=[loc(v_ref), ctx(cv_ref)],
                             mask=band, o=o_ref.at[rows, :]))
    _attend_pairs(sink_ref, problems, g=g)


def _attn_lat(sink, q, k, v, cache_k, cache_v, layer_slot, n_batch, seq, tok_off):
    qp_w, kv_w = q.shape[1], k.shape[1]
    q_w = qp_w // 2
    g = q_w // kv_w
    rows = BLK * ATTN_BLOCKS_PER_STEP
    nb = seq // rows
    past = cache_k.shape[2]
    assert seq >= 3 * BLK and seq % rows == 0 and tok_off % rows == 0
    kv_spec = pl.BlockSpec((seq, kv_w), lambda b, i: (tok_off // seq + b, 0))
    c_spec = pl.BlockSpec((None, None, past, kv_w), lambda b, i: (b, layer_slot, 0, 0))
    return pl.pallas_call(
        functools.partial(_attn_lat_kernel, g=g, seq=seq),
        grid=(n_batch, nb),
        in_specs=[pl.BlockSpec(memory_space=pltpu.SMEM),
                  pl.BlockSpec((rows, qp_w), lambda b, i: (tok_off // rows + b * nb + i, 0)),
                  kv_spec, kv_spec, c_spec, c_spec],
        out_specs=pl.BlockSpec((rows, q_w), lambda b, i: (b * nb + i, 0)),
        out_shape=jax.ShapeDtypeStruct((n_batch * seq, q_w), BF16),
        compiler_params=_cparams(2),
        name="attn_lat",
    )(sink, q, k, v, cache_k, cache_v)


def _post_ab_kernel(rows_ref, xc_ref, xl_ref, yap_ref, ybp_ref, yas_ref, ybs_ref, w_ref, gm_ref, lg_ref, lb_ref,
                    scf_ref, shf_ref, wr_ref, br_ref, x1_ref, hf_ref, route_ref, gate_ref, cnt_ref, carry_ref,
                    *, n_ctx_tiles, alpha):
    del rows_ref
    is_ctx = pl.program_id(0) < n_ctx_tiles
    a_w = yap_ref.shape[1]

    def proj_out(s):
        rs = s["rows"]
        pick = lambda p_ref, s_ref: pltpu.bitcast(
            jnp.where(is_ctx, pltpu.bitcast(p_ref[rs, :], jnp.uint32), pltpu.bitcast(s_ref[rs, :], jnp.uint32)), BF16)
        ya = pick(yap_ref, yas_ref)
        yb = pick(ybp_ref, ybs_ref)
        return dict(s, y=_dot(ya, w_ref[:a_w, :]) + _dot(yb, w_ref[a_w:, :]))

    def tail(s):
        rs = s["rows"]
        x1, hf, lg = _tail_rows(_group_rows(xc_ref, xl_ref, rs, n_ctx_tiles), s["y"], gm_ref[...], lg_ref[...],
                                lb_ref[...], scf_ref[...], shf_ref[...], wr_ref[...], br_ref[...], alpha)
        x1_ref[rs, :] = x1
        hf_ref[rs, :] = hf
        return dict(s, logits=lg)

    states = _run_skewed([proj_out, tail], [dict(rows=rs) for rs in _row_chains(xc_ref.shape[0], 1)])
    _tail_route(jnp.concatenate([s["logits"] for s in states], axis=1), route_ref, gate_ref, cnt_ref, carry_ref)


def _tail_out(t, d, tm):
    shapes = [jax.ShapeDtypeStruct((t, d), F32), jax.ShapeDtypeStruct((t, d // 2), jnp.int32),
              jax.ShapeDtypeStruct((2 * TOP_K, t), jnp.int32), jax.ShapeDtypeStruct((t, LANES), F32),
              jax.ShapeDtypeStruct((N_EXPERTS, LANES), F32)]
    specs = [pl.BlockSpec((tm, d), lambda i, r: (i, 0)), pl.BlockSpec((tm, d // 2), lambda i, r: (i, 0)),
             pl.BlockSpec((2 * TOP_K, tm), lambda i, r: (0, i)), pl.BlockSpec((tm, LANES), lambda i, r: (i, 0)),
             pl.BlockSpec((N_EXPERTS, LANES), lambda i, r: (0, 0))]
    return shapes, specs


_TAIL_SCRATCH = [pltpu.VMEM((N_EXPERTS, 1), F32)]


def _post_ab(x, ya_p, yb_p, ya_s, yb_s, w_out, mod3, mod_base, rows, ln_g, ln_b, w_r, b_r, alpha):
    n_ctx_tiles = ya_p.shape[0] // TM
    (xc, xl), x_specs = _group_specs(x, n_ctx_tiles)
    d = xc.shape[1]
    nt = rows.shape[0]
    t = nt * TM
    a_w, q_w = ya_p.shape[1], yb_p.shape[1]
    mod_spec = lambda j: pl.BlockSpec((None, 1, d), lambda i, r: (mod_base + j * 16 + r[i], 0, 0))
    whole = lambda shp: pl.BlockSpec(shp, lambda i, r: (0,) * len(shp))
    ctx = lambda w: pl.BlockSpec((TM, w), lambda i, r: (jnp.minimum(i, n_ctx_tiles - 1), 0))
    lat = lambda w: pl.BlockSpec((TM, w), lambda i, r: (jnp.maximum(i - n_ctx_tiles, 0), 0))
    shapes, specs = _tail_out(t, d, TM)
    grid_spec = pltpu.PrefetchScalarGridSpec(
        num_scalar_prefetch=1, grid=(nt,),
        in_specs=x_specs + [ctx(a_w), ctx(q_w), lat(a_w), lat(q_w),
                            whole(w_out.shape), mod_spec(2), whole((1, d)), whole((1, d)), mod_spec(4), mod_spec(3),
                            whole(w_r.shape), whole(b_r.shape)],
        out_specs=specs,
        scratch_shapes=_TAIL_SCRATCH,
    )
    return pl.pallas_call(
        functools.partial(_post_ab_kernel, n_ctx_tiles=n_ctx_tiles, alpha=alpha),
        grid_spec=grid_spec, out_shape=shapes, compiler_params=_cparams(1), name="post_ab",
    )(rows, xc, xl, ya_p, yb_p, ya_s, yb_s, w_out, mod3, ln_g, ln_b, mod3, mod3, w_r, b_r)


def _unpack_f32_pairs(p):
    u = pltpu.bitcast(p, jnp.uint32)
    return jnp.concatenate([pltpu.bitcast(u << 16, F32), pltpu.bitcast(u & jnp.uint32(0xFFFF0000), F32)], axis=1)


def _ffn_residual(x1, y_packed, gate, g_f, ln_g, ln_b, alpha):
    y = gate[:, 0:1] * _unpack_f32_pairs(y_packed[0])
    for k in range(1, TOP_K):
        y = y + gate[:, k:k + 1] * _unpack_f32_pairs(y_packed[k])
    return _ln(alpha * x1 + g_f * y) * ln_g + ln_b


def _gmlp_kernel(rows_ref, xp_ref, y0_ref, y1_ref, y2_ref, y3_ref, pgate_ref, pgf_ref, plg_ref, plb_ref,
                 scm_ref, shm_ref, win_ref, bin_ref, gv_ref, bv_ref, wsp_ref, bsp_ref, wout_ref,
                 gm_ref, lg_ref, lb_ref, scf_ref, shf_ref, wr_ref, br_ref, x1_ref, hf_ref, route_ref, gate_ref,
                 cnt_ref, carry_ref, *, alpha):
    del rows_ref
    assert (xp_ref.shape[0] // 2) % CHUNK == 0
    c_w = win_ref.shape[1] // 2
    gd = c_w // C_GROUPS

    def norm_in(s):
        rs = s["rows"]
        x = _ffn_residual(xp_ref[rs, :], [r[rs, :] for r in (y0_ref, y1_ref, y2_ref, y3_ref)], pgate_ref[rs, :],
                          pgf_ref[...], plg_ref[...], plb_ref[...], alpha)
        return dict(s, x=x, h=(_ln(x) * (1.0 + scm_ref[...]) + shm_ref[...]).astype(BF16))

    def proj_in(s):
        return dict(s, z=_dot(s["h"], win_ref[...]) + bin_ref[...])

    def gate_split(s):
        z = s["z"]
        z = 0.5 * z * (1.0 + lax.erf(z * (2.0 ** -0.5)))
        return dict(s, z=None, u=z[:, :c_w], v=z[:, c_w:])

    def norm_v(s):
        return dict(s, v=(_ln(s["v"]) * gv_ref[...] + bv_ref[...]).astype(BF16))

    def spatial(s):
        v = s["v"]
        chunks = []
        for n in range(v.shape[0] // CHUNK):
            groups = [_dot(wsp_ref[g], v[n * CHUNK:(n + 1) * CHUNK, g * gd:(g + 1) * gd]) + bsp_ref[g]
                      for g in range(C_GROUPS)]
            chunks.append(jnp.concatenate(groups, axis=1))
        return dict(s, t=(s["u"] * jnp.concatenate(chunks, axis=0)).astype(BF16))

    def proj_out(s):
        return dict(s, y=_dot(s["t"], wout_ref[...]))

    def tail_norm(s):
        x1, hf = _tail_norm(s["x"], s["y"], gm_ref[...], lg_ref[...], lb_ref[...], scf_ref[...], shf_ref[...], alpha)
        x1_ref[s["rows"], :] = x1
        return dict(s, hf=hf, x=None, y=None)

    def tail_logits(s):
        packed, lg = _tail_logits(s["hf"], wr_ref[...], br_ref[...])
        hf_ref[s["rows"], :] = packed
        return dict(s, logits=lg, hf=None)

    states = _run_skewed([norm_in, proj_in, gate_split, norm_v, spatial, proj_out, tail_norm, tail_logits],
                         [dict(rows=rs) for rs in _row_chains(xp_ref.shape[0])])
    _tail_route(jnp.concatenate([s["logits"] for s in states], axis=1), route_ref, gate_ref, cnt_ref, carry_ref)


def _gmlp(prev, mod3, mod_base, rows, w_in, b_in, g_v, b_v, w_sp, b_sp, w_out, ln_g, ln_b, w_r, b_r, alpha):
    x1p, y4, pgate, plg, plb = prev
    t, d = x1p.shape
    nt = rows.shape[0]
    tm = t // nt
    mod_spec = lambda j, base=mod_base: pl.BlockSpec((None, 1, d), lambda i, r: (base + j * 16 + r[i], 0, 0))
    whole = lambda shp: pl.BlockSpec(shp, lambda i, r: (0,) * len(shp))
    y_spec = lambda k: pl.BlockSpec((None, tm, d // 2), lambda i, r: (k, i, 0))
    shapes, specs = _tail_out(t, d, tm)
    grid_spec = pltpu.PrefetchScalarGridSpec(
        num_scalar_prefetch=1, grid=(nt,),
        in_specs=[pl.BlockSpec((tm, d), lambda i, r: (i, 0)), y_spec(0), y_spec(1), y_spec(2), y_spec(3),
                  pl.BlockSpec((tm, LANES), lambda i, r: (i, 0)), mod_spec(5, mod_base - 6 * 16),
                  whole((1, d)), whole((1, d)), mod_spec(1), mod_spec(0),
                  whole(w_in.shape), whole(b_in.shape), whole(g_v.shape), whole(b_v.shape),
                  whole(w_sp.shape), whole(b_sp.shape), whole(w_out.shape),
                  mod_spec(2), whole((1, d)), whole((1, d)), mod_spec(4), mod_spec(3),
                  whole(w_r.shape), whole(b_r.shape)],
        out_specs=specs,
        scratch_shapes=_TAIL_SCRATCH,
    )
    return pl.pallas_call(
        functools.partial(_gmlp_kernel, alpha=alpha),
        grid_spec=grid_spec, out_shape=shapes, compiler_params=_cparams(1, VMEM_LIMIT_GMLP), name="gmlp",
    )(rows, x1p, y4, y4, y4, y4, pgate, mod3, plg, plb, mod3, mod3, w_in, b_in, g_v, b_v, w_sp, b_sp, w_out,
      mod3, ln_g, ln_b, mod3, mod3, w_r, b_r)


def _moe_kernel(be_ref, bf_ref, nx_ref, sl_ref, hf_ref, nu_ref, x_ref, wgu_hbm, bgu_ref, wdn_hbm, bdn_ref, o_ref,
                wgu_f, wdn_f, wgu_s, wdn_s, sem, *, layer):
    i = pl.program_id(0)
    d_ff = wdn_s.shape[0]

    def weight_copies(e, slot):
        return (pltpu.make_async_copy(wgu_hbm.at[layer, e], wgu_f.at[slot], sem.at[0, slot]),
                pltpu.make_async_copy(wdn_hbm.at[layer, e], wdn_f.at[slot], sem.at[1, slot]))

    @pl.when(i < nu_ref[0])
    def _():
        @pl.when(bf_ref[i] == 1)
        def _():
            slot = sl_ref[i]

            @pl.when(i == 0)
            def _():
                for c in weight_copies(be_ref[i], slot):
                    c.start()

            for c in weight_copies(be_ref[i], slot):
                c.wait()
            wgu_s[...] = wgu_f[slot].astype(BF16)
            wdn_s[...] = wdn_f[slot].astype(BF16)

            @pl.when(nx_ref[i] >= 0)
            def _():
                for c in weight_copies(nx_ref[i], 1 - slot):
                    c.start()

        def up(s):
            return dict(s, gu=_dot(_unpack_bf16_pairs(x_ref[s["rows"], :]), wgu_s[...]) + bgu_ref[...])

        def act(s):
            gu = s["gu"]
            gate = jnp.minimum(gu[:, :d_ff], SWIGLU_LIMIT)
            lin = jnp.clip(gu[:, d_ff:], -SWIGLU_LIMIT, SWIGLU_LIMIT)
            glu = gate * jax.nn.sigmoid(SWIGLU_ALPHA * gate)
            return dict(s, gu=None, hid=((lin + 1.0) * glu).astype(BF16))

        def down(s):
            o_ref[s["rows"], :] = _pack_bf16_pairs(_dot(s["hid"], wdn_s[...]) + bdn_ref[...])
            return dict(s, hid=None)

        def run(n_chains, n_rows):
            _run_skewed([up, act, down], [dict(rows=rs) for rs in _row_chains(n_rows, n_chains)])

        @pl.when(hf_ref[i] == 0)
        def _():
            run(2, TM_MOE)

        @pl.when(hf_ref[i] == 1)
        def _():
            run(1, TM_MOE // 2)


def _moe(xs, plan, w_gu, b_gu, w_dn, b_dn, layer):
    n_rows = xs.shape[0]
    depth, n_e, d, ff2 = w_gu.shape
    d_ff = ff2 // 2
    nt = n_rows // TM_MOE
    n_plan = len(plan)
    row_spec = pl.BlockSpec((TM_MOE, d // 2), lambda i, *p: (jnp.minimum(i, p[-1][0] - 1), 0))
    bias_spec = lambda w: pl.BlockSpec((None, None, 1, w), lambda i, *p: (layer, p[0][i], 0, 0))
    grid_spec = pltpu.PrefetchScalarGridSpec(
        num_scalar_prefetch=n_plan, grid=(nt,),
        in_specs=[row_spec, pl.BlockSpec(memory_space=pl.ANY), bias_spec(ff2),
                  pl.BlockSpec(memory_space=pl.ANY), bias_spec(d)],
        out_specs=row_spec,
        scratch_shapes=[pltpu.VMEM((2, d, ff2), F32), pltpu.VMEM((2, d_ff, d), F32),
                        pltpu.VMEM((d, ff2), BF16), pltpu.VMEM((d_ff, d), BF16),
                        pltpu.SemaphoreType.DMA((2, 2))],
    )
    return pl.pallas_call(
        functools.partial(_moe_kernel, layer=layer), grid_spec=grid_spec,
        out_shape=jax.ShapeDtypeStruct((n_rows, d // 2), jnp.int32),
        compiler_params=_cparams(1), name="moe_experts",
    )(*plan, xs, w_gu, b_gu.reshape(depth, n_e, 1, ff2), w_dn, b_dn.reshape(depth, n_e, 1, d))


def _combine_kernel(rows_ref, x_ref, y0_ref, y1_ref, y2_ref, y3_ref, gate_ref, gf_ref, lg_ref, lb_ref, *o_refs,
                    alpha, n_ctx_tiles):
    del rows_ref
    out = _ffn_residual(x_ref[...], [r[...] for r in (y0_ref, y1_ref, y2_ref, y3_ref)], gate_ref[...], gf_ref[...],
                        lg_ref[...], lb_ref[...], alpha)
    if n_ctx_tiles is None:
        o_refs[0][...] = out
    else:
        @pl.when(pl.program_id(0) < n_ctx_tiles)
        def _():
            o_refs[0][...] = out

        @pl.when(pl.program_id(0) >= n_ctx_tiles)
        def _():
            o_refs[1][...] = out


def _combine(x1, y4, gate, mod3, mod_base, rows, ln_g, ln_b, alpha, t_ctx=None):
    t, d = x1.shape
    nt = rows.shape[0]
    tm = t // nt
    y_spec = lambda k: pl.BlockSpec((None, tm, d // 2), lambda i, r: (k, i, 0))
    if t_ctx is None:
        n_ctx_tiles = None
        out_specs = pl.BlockSpec((tm, d), lambda i, r: (i, 0))
        out_shape = jax.ShapeDtypeStruct((t, d), F32)
    else:
        n_ctx_tiles = t_ctx // tm
        out_specs = [pl.BlockSpec((tm, d), lambda i, r: (jnp.minimum(i, n_ctx_tiles - 1), 0)),
                     pl.BlockSpec((tm, d), lambda i, r: (jnp.maximum(i - n_ctx_tiles, 0), 0))]
        out_shape = [jax.ShapeDtypeStruct((t_ctx, d), F32), jax.ShapeDtypeStruct((t - t_ctx, d), F32)]
    grid_spec = pltpu.PrefetchScalarGridSpec(
        num_scalar_prefetch=1, grid=(nt,),
        in_specs=[pl.BlockSpec((tm, d), lambda i, r: (i, 0)), y_spec(0), y_spec(1), y_spec(2), y_spec(3),
                  pl.BlockSpec((tm, LANES), lambda i, r: (i, 0)),
                  pl.BlockSpec((None, 1, d), lambda i, r: (mod_base + 5 * 16 + r[i], 0, 0)),
                  pl.BlockSpec((1, d), lambda i, r: (0, 0)), pl.BlockSpec((1, d), lambda i, r: (0, 0))],
        out_specs=out_specs,
    )
    return pl.pallas_call(
        functools.partial(_combine_kernel, alpha=alpha, n_ctx_tiles=n_ctx_tiles), grid_spec=grid_spec,
        out_shape=out_shape, compiler_params=_cparams(1), name="moe_combine",
    )(rows, x1, y4, y4, y4, y4, gate, mod3, ln_g, ln_b)


def _route_plan(route, counts, n_tiles):
    idx, rank = route[:TOP_K], route[TOP_K:]
    padded = (counts + TM_MOE - 1) // TM_MOE * TM_MOE
    pend = jnp.cumsum(padded)
    pstart = pend - padded
    onehot = idx[:, :, None] == jnp.arange(N_EXPERTS, dtype=jnp.int32)[None, None, :]
    dest = jnp.sum(jnp.where(onehot, pstart[None, None, :], 0), axis=-1) + rank
    n_used = (pend[-1] // TM_MOE).astype(jnp.int32)
    tile_start = jnp.arange(n_tiles, dtype=jnp.int32) * TM_MOE
    blk_e = jnp.sum((tile_start[:, None] >= pend[None, :]).astype(jnp.int32), axis=1)
    last_e = jnp.sum((jnp.maximum(n_used - 1, 0) * TM_MOE >= pend).astype(jnp.int32))
    blk_e = jnp.minimum(jnp.where(jnp.arange(n_tiles) < n_used, blk_e, last_e), N_EXPERTS - 1).astype(jnp.int32)
    blk_first = jnp.concatenate([jnp.ones((1,), jnp.int32), (blk_e[1:] != blk_e[:-1]).astype(jnp.int32)])
    ar = jnp.arange(N_EXPERTS, dtype=jnp.int32)
    later = jnp.logical_and((counts > 0)[None, :], ar[None, :] > ar[:, None])
    next_e = jnp.min(jnp.where(later, ar[None, :], N_EXPERTS), axis=1)
    next_e = jnp.where(next_e == N_EXPERTS, -1, next_e)
    blk_next = jnp.sum(jnp.where(blk_e[:, None] == ar[None, :], next_e[None, :], 0), axis=1).astype(jnp.int32)
    blk_slot = ((jnp.cumsum(blk_first) - 1) % 2).astype(jnp.int32)
    left = jnp.sum(jnp.where(blk_e[:, None] == ar[None, :], (pstart + counts)[None, :], 0), axis=1) - tile_start
    blk_half = (left <= TM_MOE // 2).astype(jnp.int32)
    return dest, (blk_e, blk_first, blk_next, blk_slot, blk_half, n_used.reshape(1))


SC_CORES = 2
SC_SUBCORES = 16


def _sc_mesh():
    return plsc.VectorSubcoreMesh(core_axis_name="core", subcore_axis_name="subcore")


def _sc_scatter_rows(src, dest, n_rows, window):
    n_slots, t = dest.shape
    w = src.shape[1]
    per_worker = t // (SC_CORES * SC_SUBCORES)
    assert per_worker * SC_CORES * SC_SUBCORES == t and per_worker % window == 0

    @functools.partial(pl.kernel, out_type=jax.ShapeDtypeStruct((n_rows, w), src.dtype), mesh=_sc_mesh(),
                       scratch_types=[pltpu.VMEM((n_slots, window), jnp.int32), pltpu.VMEM((window, w), src.dtype)],
                       name="sc_dispatch")
    def scatter(src_hbm, dest_hbm, out_hbm, idx_v, rows_v):
        worker = lax.axis_index("subcore") * SC_CORES + lax.axis_index("core")

        @pl.loop(0, per_worker // window)
        def _(c):
            base = pl.multiple_of(worker * per_worker + c * window, window)
            pltpu.sync_copy(src_hbm.at[pl.ds(base, window)], rows_v)
            pltpu.sync_copy(dest_hbm.at[:, pl.ds(base, window)], idx_v)
            for k in range(n_slots):
                pltpu.sync_copy(rows_v, out_hbm.at[idx_v.at[k]])

    return scatter(src, dest)


def _sc_gather_rows(table, idx, window):
    n = idx.shape[0]
    w = table.shape[1]
    per_worker = n // (SC_CORES * SC_SUBCORES)
    assert per_worker * SC_CORES * SC_SUBCORES == n and per_worker % window == 0

    @functools.partial(pl.kernel, out_type=jax.ShapeDtypeStruct((n, w), table.dtype), mesh=_sc_mesh(),
                       scratch_types=[pltpu.VMEM((window,), jnp.int32), pltpu.VMEM((window, w), table.dtype)],
                       name="sc_collect")
    def gather(table_hbm, idx_hbm, out_hbm, idx_v, rows_v):
        worker = lax.axis_index("subcore") * SC_CORES + lax.axis_index("core")

        @pl.loop(0, per_worker // window)
        def _(c):
            base = pl.multiple_of(worker * per_worker + c * window, window)
            pltpu.sync_copy(idx_hbm.at[pl.ds(base, window)], idx_v)
            pltpu.sync_copy(table_hbm.at[idx_v], rows_v)
            pltpu.sync_copy(rows_v, out_hbm.at[pl.ds(base, window)])

    return gather(table, idx)


def _dft_tables(n):
    k = jnp.arange(n, dtype=jnp.int32)

    def trig(rows):
        ang = ((rows[:, None] * k[None, :]) % n).astype(F32) * (2.0 * math.pi / n)
        return jnp.cos(ang), jnp.sin(ang)

    n2 = 64
    if n < 4 * n2:
        return trig(k)
    ca, sa = (z[:, None, :] for z in trig(jnp.arange(n // n2, dtype=jnp.int32) * n2))
    cb, sb = (z[None, :, :] for z in trig(jnp.arange(n2, dtype=jnp.int32)))
    return (ca * cb - sa * sb).reshape(n, n), (sa * cb + ca * sb).reshape(n, n)


def _rope_tables(n_lat, n_ctx_rows):
    half, quarter = HEAD_DIM // 2, HEAD_DIM // 4
    tpos = jnp.arange(n_lat, dtype=jnp.int32)
    lane = jnp.arange(LANES, dtype=jnp.int32) % HEAD_DIM
    pos = jnp.where(lane[None, :] < half, (tpos // GRID_W)[:, None], (tpos % GRID_W)[:, None]).astype(F32)
    fidx = (lane % quarter).astype(F32)
    freqs = ROPE_THETA ** (-fidx / quarter)
    ang = pos * freqs[None, :]
    cos, sin = jnp.cos(ang), jnp.sin(ang)
    first = (lane % half) < quarter
    sa = jnp.where(first[None, :], -sin, 0.0)
    sb = jnp.where(first[None, :], 0.0, sin)
    ident = lambda v: jnp.full((n_ctx_rows, LANES), v, F32)
    return (jnp.concatenate([ident(1.0), cos]), jnp.concatenate([ident(0.0), sa]), jnp.concatenate([ident(0.0), sb]))


def kernel(x_prompt, x_sample, cache_k_ab, cache_v_ab, c, c_ctx, w_ada, b_ada, ln_mix_g, ln_mix_b, ln_ffn_g, ln_ffn_b, w_in_ab, w_out_ab, sink_ab, w_in_c, b_in_c, ln_v_g, ln_v_b, w_sp, b_sp, w_out_c, w_router, b_router, w_gu, b_gu, w_dn, b_dn):
    n_ctx_b, ctx_seq, d = x_prompt.shape
    n_lat_b, lat_seq, _ = x_sample.shape
    depth = w_ada.shape[0]
    t_ctx, t_lat = n_ctx_b * ctx_seq, n_lat_b * lat_seq
    t = t_ctx + t_lat
    alpha = (2 * depth) ** 0.25
    kv_w = N_KV_HEADS * HEAD_DIM
    a_w = d // 4
    q_w = d - a_w
    assert t_ctx % TM == 0 and lat_seq % TM == 0 and t_ctx % lat_seq == 0 and n_lat_b + 1 <= 16

    cond_rows = lambda tm: jnp.asarray(np.concatenate(
        [np.zeros(t_ctx // tm, np.int32), 1 + np.arange(t_lat // tm, dtype=np.int32) // (lat_seq // tm)]))
    rows, rows_c = cond_rows(TM), cond_rows(TM_COMBINE)
    rblk = jnp.asarray(np.concatenate(
        [np.zeros(t_ctx // TM, np.int32), 1 + np.arange(t_lat // TM, dtype=np.int32) % (lat_seq // TM)]))

    cond16 = jnp.zeros((16, d), F32).at[0].set(c_ctx).at[1:1 + n_lat_b].set(c)
    mod = _ada(cond16, w_ada, b_ada)
    mod3 = mod.reshape(depth, 16, 6, d).transpose(0, 2, 1, 3).reshape(depth * 6 * 16, 1, d)

    cos, sa, sb = _rope_tables(lat_seq, TM)
    cs_ctx, ss_ctx = (z.astype(BF16) for z in _dft_tables(ctx_seq))
    cs_lat, ss_lat = (z.astype(BF16) for z in _dft_tables(lat_seq))
    gd = a_w // A_GROUPS
    cd, sd = _dft_tables(gd)
    eye = jnp.eye(A_GROUPS, dtype=F32)
    bdc, bds = jnp.kron(eye, cd).astype(BF16), jnp.kron(eye, sd).astype(BF16)

    past = cache_k_ab.shape[2]
    cache_k = cache_k_ab.reshape(n_lat_b, -1, past, kv_w).astype(BF16)
    cache_v = cache_v_ab.reshape(n_lat_b, -1, past, kv_w).astype(BF16)

    x = (x_prompt.reshape(t_ctx, d), x_sample.reshape(t_lat, d))
    row2 = lambda v: v.reshape(1, -1)
    ks, vs = [], []
    for l in range(depth):
        j = l // 2
        mod_base = l * 6 * 16
        w_r, b_r = w_router[l].T, b_router[l].reshape(-1, 1)
        if l % 2 == 0:
            ac, as_, q, k32, v32, k, v = _inproj(x, t_ctx // TM, mod3, mod_base, rows, rblk, w_in_ab[j].astype(BF16),
                                                 cos, sa, sb, bdc, bds, a_w, q_w, kv_w)
            ks.append(k32[:t_ctx].reshape(n_ctx_b, ctx_seq, N_KV_HEADS, HEAD_DIM))
            vs.append(v32[:t_ctx].reshape(n_ctx_b, ctx_seq, N_KV_HEADS, HEAD_DIM))
            ya_p = _fourier(ac, as_, cs_ctx, ss_ctx, n_ctx_b, ctx_seq, 0, min(ctx_seq, 512))
            ya_s = _fourier(ac, as_, cs_lat, ss_lat, n_lat_b, lat_seq, t_ctx // lat_seq, min(lat_seq, 1024))
            yb_p = _attn_ctx(sink_ab[j], q, k, v, n_ctx_b, ctx_seq)
            yb_s = _attn_lat(sink_ab[j], q, k, v, cache_k, cache_v, j, n_lat_b, lat_seq, t_ctx)
            x1, hf, route, gate, cnt = _post_ab(x, ya_p, yb_p, ya_s, yb_s, w_out_ab[j].astype(BF16), mod3, mod_base,
                                                rows, row2(ln_mix_g[l]), row2(ln_mix_b[l]), w_r, b_r, alpha)
        else:
            x1, hf, route, gate, cnt = _gmlp(prev, mod3, mod_base, rows, w_in_c[j].astype(BF16), row2(b_in_c[j]),
                                             row2(ln_v_g[j]), row2(ln_v_b[j]), w_sp[j].astype(BF16),
                                             b_sp[j][:, :, None], w_out_c[j].astype(BF16), row2(ln_mix_g[l]),
                                             row2(ln_mix_b[l]), w_r, b_r, alpha)
        n_rows = t * TOP_K + N_EXPERTS * TM_MOE
        dest, plan = _route_plan(route, cnt[:, 0].astype(jnp.int32), n_rows // TM_MOE)
        xs = _sc_scatter_rows(hf, dest, n_rows, 128)
        out_sorted = _moe(xs, plan, w_gu, b_gu, w_dn, b_dn, l)
        y4 = _sc_gather_rows(out_sorted, dest.reshape(-1), 128).reshape(TOP_K, t, d // 2)
        if l % 2 == 0 and l + 1 < depth:
            prev = (x1, y4, gate, row2(ln_ffn_g[l]), row2(ln_ffn_b[l]))
        else:
            x = _combine(x1, y4, gate, mod3, mod_base, rows_c, row2(ln_ffn_g[l]), row2(ln_ffn_b[l]), alpha,
                         t_ctx if l == depth - 1 else None)

    y_prompt = x[0].reshape(n_ctx_b, ctx_seq, d)
    y_sample = x[1].reshape(n_lat_b, lat_seq, d)
    return (y_prompt, y_sample, jnp.stack(ks, axis=1), jnp.stack(vs, axis=1))
```

```python
import functools
import math

import numpy as np
import jax
import jax.numpy as jnp
from jax import lax
from jax.experimental import pallas as pl
from jax.experimental.pallas import tpu as pltpu
from jax.experimental.pallas import tpu_sc as plsc

GRID_W = 64
BLK = 128
WINDOW = 128
HEAD_DIM = 64
A_GROUPS = 4
N_KV_HEADS = 4
C_GROUPS = 8
CHUNK = 128
N_EXPERTS = 32
TOP_K = 4
SWIGLU_LIMIT = 7.0
SWIGLU_ALPHA = 1.702
ROPE_THETA = 10000.0
LN_EPS = 1e-6
NEG_INF = -1e30
LOG2_E = math.log2(math.e)

LANES = 128
TM = 1024
TM_COMBINE = 512
ATTN_BLOCKS_PER_STEP = 4
TM_MOE = 512
VMEM_LIMIT = 52 * 1024 * 1024
VMEM_LIMIT_GMLP = 60 * 1024 * 1024

F32 = jnp.float32
BF16 = jnp.bfloat16


def _cparams(n_axes, vmem_limit=VMEM_LIMIT):
    return pltpu.CompilerParams(dimension_semantics=("arbitrary",) * n_axes, vmem_limit_bytes=vmem_limit)


def _ln(x):
    mu = jnp.mean(x, axis=-1, keepdims=True)
    xc = x - mu
    var = jnp.mean(xc * xc, axis=-1, keepdims=True)
    return xc * lax.rsqrt(var + LN_EPS)


def _dot(a, b):
    return jnp.dot(a, b, preferred_element_type=F32)


def _dot_nt(a, b):
    return lax.dot_general(a, b, (((1,), (1,)), ((), ())), preferred_element_type=F32)


def _split(a):
    hi = a.astype(BF16)
    lo = (a - hi.astype(F32)).astype(BF16)
    return hi, lo


def _pack_bf16_pairs(v):
    n = v.shape[1] // 2
    lo = pltpu.bitcast(v[:, :n].astype(BF16).astype(F32), jnp.uint32) >> 16
    hi = pltpu.bitcast(v[:, n:].astype(BF16).astype(F32), jnp.uint32) & jnp.uint32(0xFFFF0000)
    return pltpu.bitcast(lo | hi, jnp.int32)


def _unpack_bf16_pairs(p):
    u = pltpu.bitcast(p, jnp.uint32)
    lo = pltpu.bitcast(u << 16, F32)
    hi = pltpu.bitcast(u & jnp.uint32(0xFFFF0000), F32)
    return jnp.concatenate([lo, hi], axis=1).astype(BF16)


def _row_chains(n_rows, n_chains=2):
    step = n_rows // n_chains
    return [slice(c * step, (c + 1) * step) for c in range(n_chains)]


def _run_skewed(stages, states):
    states = list(states)
    for step in range(len(stages) + len(states) - 1):
        for c in range(len(states)):
            if 0 <= step - c < len(stages):
                states[c] = stages[step - c](states[c])
    return states


def _tail_norm(x, y, g_m, ln_g, ln_b, sc_f, sh_f, alpha):
    x1 = _ln(alpha * x + g_m * y) * ln_g + ln_b
    return x1, _ln(x1) * (1.0 + sc_f) + sh_f


def _tail_logits(hf, w_r, b_r):
    hf_hi, hf_lo = _split(hf)
    w_hi, w_lo = _split(w_r)
    logits = _dot_nt(w_hi, hf_hi) + (_dot_nt(w_hi, hf_lo) + _dot_nt(w_lo, hf_hi)) + b_r
    return _pack_bf16_pairs(hf), logits


def _tail_route(logits, route_ref, gate_ref, cnt_ref, carry_ref):
    @pl.when(pl.program_id(0) == 0)
    def _():
        carry_ref[...] = jnp.zeros_like(carry_ref)

    tm = logits.shape[1]
    sub = lax.broadcasted_iota(jnp.int32, logits.shape, 0)
    vals = logits
    top_v, top_i = [], []
    for _ in range(TOP_K):
        m = jnp.max(vals, axis=0, keepdims=True)
        am = jnp.min(jnp.where(vals == m, sub, N_EXPERTS), axis=0, keepdims=True)
        top_v.append(m)
        top_i.append(am)
        vals = jnp.where(sub == am, -jnp.inf, vals)
    e = [jnp.exp(v - top_v[0]) for v in top_v]
    denom = e[0] + e[1] + e[2] + e[3]
    gates_t = jnp.concatenate([ek / denom for ek in e] + [jnp.zeros((LANES - TOP_K, tm), F32)], axis=0)
    gate_ref[...] = gates_t.T

    member = jnp.zeros(logits.shape, F32)
    for am in top_i:
        member = jnp.where(sub == am, 1.0, member)
    r_i = lax.broadcasted_iota(jnp.int32, (tm, tm), 0)
    c_i = lax.broadcasted_iota(jnp.int32, (tm, tm), 1)
    earlier = jnp.where(r_i < c_i, 1.0, 0.0).astype(BF16)
    before = _dot(member.astype(BF16), earlier) + carry_ref[...]
    ranks = [jnp.sum(jnp.where(sub == am, before, 0.0), axis=0, keepdims=True).astype(jnp.int32) for am in top_i]
    route_ref[...] = jnp.concatenate(top_i + ranks, axis=0)
    carry = carry_ref[...] + jnp.sum(member, axis=1, keepdims=True)
    carry_ref[...] = carry
    cnt_ref[...] = jnp.broadcast_to(carry, cnt_ref.shape)


def _ada_kernel(cond_ref, w_ref, b_ref, o_ref):
    c = cond_ref[...]
    s = (c * jax.nn.sigmoid(c)).astype(BF16)
    o_ref[...] = _dot(s, w_ref[...].astype(BF16)) + b_ref[...]


def _ada(cond16, w_ada, b_ada):
    depth, d, n = w_ada.shape
    tn = 1536
    return pl.pallas_call(
        _ada_kernel,
        grid=(depth, n // tn),
        in_specs=[
            pl.BlockSpec((16, d), lambda l, j: (0, 0)),
            pl.BlockSpec((None, d, tn), lambda l, j: (l, 0, j)),
            pl.BlockSpec((None, 1, tn), lambda l, j: (l, 0, j)),
        ],
        out_specs=pl.BlockSpec((None, 16, tn), lambda l, j: (l, 0, j)),
        out_shape=jax.ShapeDtypeStruct((depth, 16, n), F32),
        compiler_params=_cparams(2),
        name="ada_mod",
    )(cond16, w_ada, b_ada.reshape(depth, 1, n))


def _group_rows(xc_ref, xl_ref, rows, n_ctx_tiles):
    return jnp.where(pl.program_id(0) < n_ctx_tiles, xc_ref[rows, :], xl_ref[rows, :])


def _group_specs(x, n_ctx_tiles):
    if isinstance(x, tuple):
        arrays, lat = x, (lambda i, *_: (jnp.maximum(i - n_ctx_tiles, 0), 0))
    else:
        arrays, lat = (x, x), (lambda i, *_: (jnp.maximum(i, n_ctx_tiles), 0))
    d = arrays[0].shape[1]
    ctx = lambda i, *_: (jnp.minimum(i, n_ctx_tiles - 1), 0)
    return arrays, [pl.BlockSpec((TM, d), ctx), pl.BlockSpec((TM, d), lat)]


def _inproj_kernel(rows_ref, rblk_ref, xc_ref, xl_ref, sc_ref, sh_ref, w_ref, cos_ref, sa_ref, sb_ref, bdc_ref,
                   bds_ref, ac_ref, as_ref, q_ref, k_ref, v_ref, kb_ref, vb_ref, *, a_w, q_w, kv_w, n_ctx_tiles):
    del rows_ref, rblk_ref
    g = q_w // kv_w

    def norm_in(s):
        x = _group_rows(xc_ref, xl_ref, s["rows"], n_ctx_tiles)
        return dict(s, h=(_ln(x) * (1.0 + sc_ref[...]) + sh_ref[...]).astype(BF16))

    def rope(t, rs):
        w = t.shape[1]
        c, a_, b_ = (jnp.tile(z[rs, :], (1, w // LANES)) for z in (cos_ref, sa_ref, sb_ref))
        nxt = pltpu.roll(t, w - HEAD_DIM // 4, 1)
        prv = pltpu.roll(t, HEAD_DIM // 4, 1)
        return t * c + nxt * a_ + prv * b_

    def proj(s):
        p = _dot(s["h"], w_ref[...])
        return dict(s, h=None, a=p[:, :a_w].astype(BF16), q=p[:, a_w:a_w + q_w], kv=p[:, a_w + q_w:])

    def channel_dft(s):
        ac_ref[s["rows"], :] = _dot(s["a"], bdc_ref[...]).astype(BF16)
        as_ref[s["rows"], :] = _dot(s["a"], bds_ref[...]).astype(BF16)
        return dict(s, a=None)

    def rope_q(s):
        rs = s["rows"]
        q = rope(s["q"], rs) * (HEAD_DIM ** -0.5 * LOG2_E)
        lane = lax.broadcasted_iota(jnp.int32, (q.shape[0], LANES), 1)
        for j in range(q_w // HEAD_DIM):
            tile = q[:, (j // 2) * LANES:(j // 2 + 1) * LANES]
            dst_low = (j // g) % 2 == 0
            if (j % 2 == 0) != dst_low:
                tile = pltpu.roll(tile, HEAD_DIM, 1)
            keep = (lane < HEAD_DIM) if dst_low else (lane >= HEAD_DIM)
            q_ref[rs, j * LANES:(j + 1) * LANES] = jnp.where(keep, tile, 0.0).astype(BF16)
        return dict(s, q=None)

    def rope_k(s):
        rs = s["rows"]
        k = rope(s["kv"][:, :kv_w], rs)
        v = s["kv"][:, kv_w:]
        k_ref[rs, :] = k
        v_ref[rs, :] = v
        kb_ref[rs, :] = k.astype(BF16)
        vb_ref[rs, :] = v.astype(BF16)
        return dict(s, kv=None)

    _run_skewed([norm_in, proj, channel_dft, rope_q, rope_k],
                [dict(rows=rs) for rs in _row_chains(xc_ref.shape[0])])


def _inproj(x, n_ctx_tiles, mod3, mod_base, rows, rblk, w_in, cos, sa, sb, bdc, bds, a_w, q_w, kv_w):
    (xc, xl), x_specs = _group_specs(x, n_ctx_tiles)
    d = xc.shape[1]
    t = rows.shape[0] * TM
    n = w_in.shape[1]
    nt = t // TM
    mod_spec = lambda j: pl.BlockSpec((None, 1, d), lambda i, r, rb: (mod_base + j * 16 + r[i], 0, 0))
    whole = lambda shp: pl.BlockSpec(shp, lambda i, r, rb: (0,) * len(shp))
    rope_spec = pl.BlockSpec((TM, LANES), lambda i, r, rb: (rb[i], 0))
    tok = lambda w: pl.BlockSpec((TM, w), lambda i, r, rb: (i, 0))
    grid_spec = pltpu.PrefetchScalarGridSpec(
        num_scalar_prefetch=2, grid=(nt,),
        in_specs=x_specs + [mod_spec(1), mod_spec(0), whole((d, n)), rope_spec, rope_spec, rope_spec,
                            whole((a_w, a_w)), whole((a_w, a_w))],
        out_specs=[tok(a_w), tok(a_w), tok(2 * q_w), tok(kv_w), tok(kv_w), tok(kv_w), tok(kv_w)],
    )
    return pl.pallas_call(
        functools.partial(_inproj_kernel, a_w=a_w, q_w=q_w, kv_w=kv_w, n_ctx_tiles=n_ctx_tiles),
        grid_spec=grid_spec,
        out_shape=[jax.ShapeDtypeStruct((t, a_w), BF16), jax.ShapeDtypeStruct((t, a_w), BF16),
                   jax.ShapeDtypeStruct((t, 2 * q_w), BF16), jax.ShapeDtypeStruct((t, kv_w), F32),
                   jax.ShapeDtypeStruct((t, kv_w), F32), jax.ShapeDtypeStruct((t, kv_w), BF16),
                   jax.ShapeDtypeStruct((t, kv_w), BF16)],
        compiler_params=_cparams(1),
        name="inproj_ab",
    )(rows, rblk, xc, xl, mod3, mod3, w_in, cos, sa, sb, bdc, bds)


def _fourier_kernel(c_ref, s_ref, ac_ref, as_ref, o_ref, *, scale):
    y = _dot(c_ref[...], ac_ref[...]) - _dot(s_ref[...], as_ref[...])
    o_ref[...] = (y * scale).astype(BF16)


def _fourier(ac, as_, cs, ss, n_batch, seq, blk_off, tm):
    a_w = ac.shape[1]
    nr = seq // tm
    a_spec = pl.BlockSpec((seq, a_w), lambda r, b: (blk_off + b, 0))
    t_spec = pl.BlockSpec((tm, seq), lambda r, b: (r, 0))
    return pl.pallas_call(
        functools.partial(_fourier_kernel, scale=(seq * (a_w // A_GROUPS)) ** -0.5),
        grid=(nr, n_batch),
        in_specs=[t_spec, t_spec, a_spec, a_spec],
        out_specs=pl.BlockSpec((tm, a_w), lambda r, b: (b * nr + r, 0)),
        out_shape=jax.ShapeDtypeStruct((n_batch * seq, a_w), BF16),
        compiler_params=_cparams(2),
        name="fourier_%d" % seq,
    )(cs, ss, ac, as_)


def _attend_pairs(sink_ref, problems, *, g):
    def scores(s):
        pr, h0, p = s["pr"], s["h0"], s["h0"] // (2 * g)
        qp = jnp.concatenate([pr["q"][:, (h0 + j) * LANES:(h0 + j + 1) * LANES] for j in range(g)], axis=0)
        sc = [_dot_nt(qp, kf(p)) for kf in pr["keys"]]
        if pr["mask"] is not None:
            sc[0] = jnp.where(jnp.concatenate([pr["mask"]] * g, axis=0), sc[0], NEG_INF)
        return dict(s, sc=sc)

    def row_max(s):
        rows = s["pr"]["q"].shape[0]
        m = jnp.concatenate([jnp.full((rows, 1), sink_ref[s["h0"] + j] * LOG2_E, F32) for j in range(g)], axis=0)
        sink_col = m
        for sc in s["sc"]:
            m = jnp.maximum(m, jnp.max(sc, axis=-1, keepdims=True))
        return dict(s, m=m, sink=jnp.exp2(sink_col - m))

    def weights(s):
        return dict(s, e=[jnp.exp2(sc - s["m"]).astype(BF16) for sc in s["sc"]], sc=None)

    def weighted_values(s):
        p = s["h0"] // (2 * g)
        acc = None
        for e, vf in zip(s["e"], s["pr"]["values"]):
            v = vf(p)
            pv = _dot(e, jnp.concatenate([v, jnp.ones_like(v)], axis=1))
            acc = pv if acc is None else acc + pv
        return dict(s, acc=acc, e=None)

    def normalise(s):
        acc = s["acc"]
        return dict(s, o=acc[:, :LANES] / (acc[:, LANES:LANES + 1] + s["sink"]), acc=None)

    states = _run_skewed([scores, row_max, weights, weighted_values, normalise],
                         [dict(pr=pr, h0=h * g) for pr in problems for h in range(N_KV_HEADS)])
    for n, pr in enumerate(problems):
        rows = pr["q"].shape[0]
        lane = lax.broadcasted_iota(jnp.int32, (rows, LANES), 1)
        heads = {}
        for s in states[n * N_KV_HEADS:(n + 1) * N_KV_HEADS]:
            for j in range(g):
                blk = s["o"][j * rows:(j + 1) * rows]
                head = s["h0"] + j
                if (((head // g) % 2 == 0) != (head % 2 == 0)):
                    blk = pltpu.roll(blk, HEAD_DIM, 1)
                heads[head] = blk
        for t in range(len(heads) // 2):
            tile = jnp.where(lane < HEAD_DIM, heads[2 * t], heads[2 * t + 1])
            pr["o"][:, t * LANES:(t + 1) * LANES] = tile.astype(BF16)


def _attn_ctx_kernel(sink_ref, q_ref, k_ref, v_ref, o_ref, *, g):
    pair = lambda ref: (lambda p: ref[:, p * LANES:(p + 1) * LANES])
    _attend_pairs(sink_ref, [dict(q=q_ref, keys=[pair(k_ref)], values=[pair(v_ref)], mask=None, o=o_ref)], g=g)


def _attn_ctx(sink, q, k, v, n_batch, seq):
    qp_w, kv_w = q.shape[1], k.shape[1]
    q_w = qp_w // 2
    g = q_w // kv_w
    return pl.pallas_call(
        functools.partial(_attn_ctx_kernel, g=g),
        grid=(n_batch,),
        in_specs=[pl.BlockSpec(memory_space=pltpu.SMEM),
                  pl.BlockSpec((seq, qp_w), lambda b: (b, 0)),
                  pl.BlockSpec((seq, kv_w), lambda b: (b, 0)),
                  pl.BlockSpec((seq, kv_w), lambda b: (b, 0))],
        out_specs=pl.BlockSpec((seq, q_w), lambda b: (b, 0)),
        out_shape=jax.ShapeDtypeStruct((n_batch * seq, q_w), BF16),
        compiler_params=_cparams(1),
        name="attn_ctx",
    )(sink, q, k, v)


def _attn_lat_kernel(sink_ref, q_ref, k_ref, v_ref, ck_ref, cv_ref, o_ref, *, g, seq):
    n_loc = 3 * BLK
    row = lax.broadcasted_iota(jnp.int32, (BLK, n_loc), 0)
    col = lax.broadcasted_iota(jnp.int32, (BLK, n_loc), 1)
    ctx = lambda ref: (lambda p: ref[:, p * LANES:(p + 1) * LANES])
    problems = []
    for sub in range(q_ref.shape[0] // BLK):
        i = pl.program_id(1) * (q_ref.shape[0] // BLK) + sub
        start = pl.multiple_of(jnp.clip((i - 1) * BLK, 0, seq - n_loc), BLK)
        band = jnp.abs(row + (i * BLK - start) - col) <= WINDOW
        loc = lambda ref, start=start: (lambda p: ref[pl.ds(start, n_loc), p * LANES:(p + 1) * LANES])
        rows = slice(sub * BLK, (sub + 1) * BLK)
        problems.append(dict(q=q_ref.at[rows, :], keys=[loc(k_ref), ctx(ck_ref)], values=[loc(v_ref), ctx(cv_ref)],
                             mask=band, o=o_ref.at[rows, :]))
    _attend_pairs(sink_ref, problems, g=g)


def _attn_lat(sink, q, k, v, cache_k, cache_v, layer_slot, n_batch, seq, tok_off):
    qp_w, kv_w = q.shape[1], k.shape[1]
    q_w = qp_w // 2
    g = q_w // kv_w
    rows = BLK * ATTN_BLOCKS_PER_STEP
    nb = seq // rows
    past = cache_k.shape[2]
    assert seq >= 3 * BLK and seq % rows == 0 and tok_off % rows == 0
    kv_spec = pl.BlockSpec((seq, kv_w), lambda b, i: (tok_off // seq + b, 0))
    c_spec = pl.BlockSpec((None, None, past, kv_w), lambda b, i: (b, layer_slot, 0, 0))
    return pl.pallas_call(
        functools.partial(_attn_lat_kernel, g=g, seq=seq),
        grid=(n_batch, nb),
        in_specs=[pl.BlockSpec(memory_space=pltpu.SMEM),
                  pl.BlockSpec((rows, qp_w), lambda b, i: (tok_off // rows + b * nb + i, 0)),
                  kv_spec, kv_spec, c_spec, c_spec],
        out_specs=pl.BlockSpec((rows, q_w), lambda b, i: (b * nb + i, 0)),
        out_shape=jax.ShapeDtypeStruct((n_batch * seq, q_w), BF16),
        compiler_params=_cparams(2),
        name="attn_lat",
    )(sink, q, k, v, cache_k, cache_v)


def _post_ab_kernel(rows_ref, xc_ref, xl_ref, yap_ref, ybp_ref, yas_ref, ybs_ref, w_ref, gm_ref, lg_ref, lb_ref,
                    scf_ref, shf_ref, wr_ref, br_ref, x1_ref, hf_ref, route_ref, gate_ref, cnt_ref, carry_ref,
                    *, n_ctx_tiles, alpha):
    del rows_ref
    is_ctx = pl.program_id(0) < n_ctx_tiles
    a_w = yap_ref.shape[1]

    def proj_out(s):
        rs = s["rows"]
        pick = lambda p_ref, s_ref: pltpu.bitcast(
            jnp.where(is_ctx, pltpu.bitcast(p_ref[rs, :], jnp.uint32), pltpu.bitcast(s_ref[rs, :], jnp.uint32)), BF16)
        ya = pick(yap_ref, yas_ref)
        yb = pick(ybp_ref, ybs_ref)
        return dict(s, y=_dot(ya, w_ref[:a_w, :]) + _dot(yb, w_ref[a_w:, :]))

    def tail_norm(s):
        rs = s["rows"]
        x1, hf = _tail_norm(_group_rows(xc_ref, xl_ref, rs, n_ctx_tiles), s["y"], gm_ref[...], lg_ref[...],
                            lb_ref[...], scf_ref[...], shf_ref[...], alpha)
        x1_ref[rs, :] = x1
        return dict(s, hf=hf, y=None)

    def tail_logits(s):
        packed, lg = _tail_logits(s["hf"], wr_ref[...], br_ref[...])
        hf_ref[s["rows"], :] = packed
        return dict(s, logits=lg, hf=None)

    states = _run_skewed([proj_out, tail_norm, tail_logits],
                         [dict(rows=rs) for rs in _row_chains(xc_ref.shape[0], 2)])
    _tail_route(jnp.concatenate([s["logits"] for s in states], axis=1), route_ref, gate_ref, cnt_ref, carry_ref)


def _tail_out(t, d, tm):
    shapes = [jax.ShapeDtypeStruct((t, d), F32), jax.ShapeDtypeStruct((t, d // 2), jnp.int32),
              jax.ShapeDtypeStruct((2 * TOP_K, t), jnp.int32), jax.ShapeDtypeStruct((t, LANES), F32),
              jax.ShapeDtypeStruct((N_EXPERTS, LANES), F32)]
    specs = [pl.BlockSpec((tm, d), lambda i, r: (i, 0)), pl.BlockSpec((tm, d // 2), lambda i, r: (i, 0)),
             pl.BlockSpec((2 * TOP_K, tm), lambda i, r: (0, i)), pl.BlockSpec((tm, LANES), lambda i, r: (i, 0)),
             pl.BlockSpec((N_EXPERTS, LANES), lambda i, r: (0, 0))]
    return shapes, specs


_TAIL_SCRATCH = [pltpu.VMEM((N_EXPERTS, 1), F32)]


def _post_ab(x, ya_p, yb_p, ya_s, yb_s, w_out, mod3, mod_base, rows, ln_g, ln_b, w_r, b_r, alpha):
    n_ctx_tiles = ya_p.shape[0] // TM
    (xc, xl), x_specs = _group_specs(x, n_ctx_tiles)
    d = xc.shape[1]
    nt = rows.shape[0]
    t = nt * TM
    a_w, q_w = ya_p.shape[1], yb_p.shape[1]
    mod_spec = lambda j: pl.BlockSpec((None, 1, d), lambda i, r: (mod_base + j * 16 + r[i], 0, 0))
    whole = lambda shp: pl.BlockSpec(shp, lambda i, r: (0,) * len(shp))
    ctx = lambda w: pl.BlockSpec((TM, w), lambda i, r: (jnp.minimum(i, n_ctx_tiles - 1), 0))
    lat = lambda w: pl.BlockSpec((TM, w), lambda i, r: (jnp.maximum(i - n_ctx_tiles, 0), 0))
    shapes, specs = _tail_out(t, d, TM)
    grid_spec = pltpu.PrefetchScalarGridSpec(
        num_scalar_prefetch=1, grid=(nt,),
        in_specs=x_specs + [ctx(a_w), ctx(q_w), lat(a_w), lat(q_w),
                            whole(w_out.shape), mod_spec(2), whole((1, d)), whole((1, d)), mod_spec(4), mod_spec(3),
                            whole(w_r.shape), whole(b_r.shape)],
        out_specs=specs,
        scratch_shapes=_TAIL_SCRATCH,
    )
    return pl.pallas_call(
        functools.partial(_post_ab_kernel, n_ctx_tiles=n_ctx_tiles, alpha=alpha),
        grid_spec=grid_spec, out_shape=shapes, compiler_params=_cparams(1), name="post_ab",
    )(rows, xc, xl, ya_p, yb_p, ya_s, yb_s, w_out, mod3, ln_g, ln_b, mod3, mod3, w_r, b_r)


def _unpack_f32_pairs(p):
    u = pltpu.bitcast(p, jnp.uint32)
    return jnp.concatenate([pltpu.bitcast(u << 16, F32), pltpu.bitcast(u & jnp.uint32(0xFFFF0000), F32)], axis=1)


def _ffn_residual(x1, y_packed, gate, g_f, ln_g, ln_b, alpha):
    y = gate[:, 0:1] * _unpack_f32_pairs(y_packed[0])
    for k in range(1, TOP_K):
        y = y + gate[:, k:k + 1] * _unpack_f32_pairs(y_packed[k])
    return _ln(alpha * x1 + g_f * y) * ln_g + ln_b


def _gmlp_kernel(rows_ref, xp_ref, y0_ref, y1_ref, y2_ref, y3_ref, pgate_ref, pgf_ref, plg_ref, plb_ref,
                 scm_ref, shm_ref, win_ref, bin_ref, gv_ref, bv_ref, wsp_ref, bsp_ref, wout_ref,
                 gm_ref, lg_ref, lb_ref, scf_ref, shf_ref, wr_ref, br_ref, x1_ref, hf_ref, route_ref, gate_ref,
                 cnt_ref, carry_ref, *, alpha):
    del rows_ref
    assert (xp_ref.shape[0] // 2) % CHUNK == 0
    c_w = win_ref.shape[1] // 2
    gd = c_w // C_GROUPS

    def norm_in(s):
        rs = s["rows"]
        x = _ffn_residual(xp_ref[rs, :], [r[rs, :] for r in (y0_ref, y1_ref, y2_ref, y3_ref)], pgate_ref[rs, :],
                          pgf_ref[...], plg_ref[...], plb_ref[...], alpha)
        return dict(s, x=x, h=(_ln(x) * (1.0 + scm_ref[...]) + shm_ref[...]).astype(BF16))

    def proj_in(s):
        return dict(s, z=_dot(s["h"], win_ref[...]) + bin_ref[...])

    def gate_split(s):
        z = s["z"]
        z = 0.5 * z * (1.0 + lax.erf(z * (2.0 ** -0.5)))
        return dict(s, z=None, u=z[:, :c_w], v=z[:, c_w:])

    def norm_v(s):
        return dict(s, v=(_ln(s["v"]) * gv_ref[...] + bv_ref[...]).astype(BF16))

    def spatial(s):
        v = s["v"]
        chunks = []
        for n in range(v.shape[0] // CHUNK):
            groups = [_dot(wsp_ref[g], v[n * CHUNK:(n + 1) * CHUNK, g * gd:(g + 1) * gd]) + bsp_ref[g]
                      for g in range(C_GROUPS)]
            chunks.append(jnp.concatenate(groups, axis=1))
        return dict(s, t=(s["u"] * jnp.concatenate(chunks, axis=0)).astype(BF16))

    def proj_out(s):
        return dict(s, y=_dot(s["t"], wout_ref[...]))

    def tail_norm(s):
        x1, hf = _tail_norm(s["x"], s["y"], gm_ref[...], lg_ref[...], lb_ref[...], scf_ref[...], shf_ref[...], alpha)
        x1_ref[s["rows"], :] = x1
        return dict(s, hf=hf, x=None, y=None)

    def tail_logits(s):
        packed, lg = _tail_logits(s["hf"], wr_ref[...], br_ref[...])
        hf_ref[s["rows"], :] = packed
        return dict(s, logits=lg, hf=None)

    states = _run_skewed([norm_in, proj_in, gate_split, norm_v, spatial, proj_out, tail_norm, tail_logits],
                         [dict(rows=rs) for rs in _row_chains(xp_ref.shape[0])])
    _tail_route(jnp.concatenate([s["logits"] for s in states], axis=1), route_ref, gate_ref, cnt_ref, carry_ref)


def _gmlp(prev, mod3, mod_base, rows, w_in, b_in, g_v, b_v, w_sp, b_sp, w_out, ln_g, ln_b, w_r, b_r, alpha):
    x1p, y4, pgate, plg, plb = prev
    t, d = x1p.shape
    nt = rows.shape[0]
    tm = t // nt
    mod_spec = lambda j, base=mod_base: pl.BlockSpec((None, 1, d), lambda i, r: (base + j * 16 + r[i], 0, 0))
    whole = lambda shp: pl.BlockSpec(shp, lambda i, r: (0,) * len(shp))
    y_spec = lambda k: pl.BlockSpec((None, tm, d // 2), lambda i, r: (k, i, 0))
    shapes, specs = _tail_out(t, d, tm)
    grid_spec = pltpu.PrefetchScalarGridSpec(
        num_scalar_prefetch=1, grid=(nt,),
        in_specs=[pl.BlockSpec((tm, d), lambda i, r: (i, 0)), y_spec(0), y_spec(1), y_spec(2), y_spec(3),
                  pl.BlockSpec((tm, LANES), lambda i, r: (i, 0)), mod_spec(5, mod_base - 6 * 16),
                  whole((1, d)), whole((1, d)), mod_spec(1), mod_spec(0),
                  whole(w_in.shape), whole(b_in.shape), whole(g_v.shape), whole(b_v.shape),
                  whole(w_sp.shape), whole(b_sp.shape), whole(w_out.shape),
                  mod_spec(2), whole((1, d)), whole((1, d)), mod_spec(4), mod_spec(3),
                  whole(w_r.shape), whole(b_r.shape)],
        out_specs=specs,
        scratch_shapes=_TAIL_SCRATCH,
    )
    return pl.pallas_call(
        functools.partial(_gmlp_kernel, alpha=alpha),
        grid_spec=grid_spec, out_shape=shapes, compiler_params=_cparams(1, VMEM_LIMIT_GMLP), name="gmlp",
    )(rows, x1p, y4, y4, y4, y4, pgate, mod3, plg, plb, mod3, mod3, w_in, b_in, g_v, b_v, w_sp, b_sp, w_out,
      mod3, ln_g, ln_b, mod3, mod3, w_r, b_r)


def _moe_kernel(be_ref, bf_ref, nx_ref, sl_ref, hf_ref, nu_ref, x_ref, wgu_hbm, bgu_ref, wdn_hbm, bdn_ref, o_ref,
                wgu_f, wdn_f, wgu_s, wdn_s, sem, *, layer):
    i = pl.program_id(0)
    d_ff = wdn_s.shape[0]

    def weight_copies(e, slot):
        return (pltpu.make_async_copy(wgu_hbm.at[layer, e], wgu_f.at[slot], sem.at[0, slot]),
                pltpu.make_async_copy(wdn_hbm.at[layer, e], wdn_f.at[slot], sem.at[1, slot]))

    @pl.when(i < nu_ref[0])
    def _():
        @pl.when(bf_ref[i] == 1)
        def _():
            slot = sl_ref[i]

            @pl.when(i == 0)
            def _():
                for c in weight_copies(be_ref[i], slot):
                    c.start()

            for c in weight_copies(be_ref[i], slot):
                c.wait()
            wgu_s[...] = wgu_f[slot].astype(BF16)
            wdn_s[...] = wdn_f[slot].astype(BF16)

            @pl.when(nx_ref[i] >= 0)
            def _():
                for c in weight_copies(nx_ref[i], 1 - slot):
                    c.start()

        def up(s):
            return dict(s, gu=_dot(_unpack_bf16_pairs(x_ref[s["rows"], :]), wgu_s[...]) + bgu_ref[...])

        def act(s):
            gu = s["gu"]
            gate = jnp.minimum(gu[:, :d_ff], SWIGLU_LIMIT)
            lin = jnp.clip(gu[:, d_ff:], -SWIGLU_LIMIT, SWIGLU_LIMIT)
            glu = gate * jax.nn.sigmoid(SWIGLU_ALPHA * gate)
            return dict(s, gu=None, hid=((lin + 1.0) * glu).astype(BF16))

        def down(s):
            o_ref[s["rows"], :] = _pack_bf16_pairs(_dot(s["hid"], wdn_s[...]) + bdn_ref[...])
            return dict(s, hid=None)

        def run(n_chains, n_rows):
            _run_skewed([up, act, down], [dict(rows=rs) for rs in _row_chains(n_rows, n_chains)])

        @pl.when(hf_ref[i] == 0)
        def _():
            run(2, TM_MOE)

        @pl.when(hf_ref[i] == 1)
        def _():
            run(1, TM_MOE // 2)


def _moe(xs, plan, w_gu, b_gu, w_dn, b_dn, layer):
    n_rows = xs.shape[0]
    depth, n_e, d, ff2 = w_gu.shape
    d_ff = ff2 // 2
    nt = n_rows // TM_MOE
    n_plan = len(plan)
    row_spec = pl.BlockSpec((TM_MOE, d // 2), lambda i, *p: (jnp.minimum(i, p[-1][0] - 1), 0))
    bias_spec = lambda w: pl.BlockSpec((None, None, 1, w), lambda i, *p: (layer, p[0][i], 0, 0))
    grid_spec = pltpu.PrefetchScalarGridSpec(
        num_scalar_prefetch=n_plan, grid=(nt,),
        in_specs=[row_spec, pl.BlockSpec(memory_space=pl.ANY), bias_spec(ff2),
                  pl.BlockSpec(memory_space=pl.ANY), bias_spec(d)],
        out_specs=row_spec,
        scratch_shapes=[pltpu.VMEM((2, d, ff2), F32), pltpu.VMEM((2, d_ff, d), F32),
                        pltpu.VMEM((d, ff2), BF16), pltpu.VMEM((d_ff, d), BF16),
                        pltpu.SemaphoreType.DMA((2, 2))],
    )
    return pl.pallas_call(
        functools.partial(_moe_kernel, layer=layer), grid_spec=grid_spec,
        out_shape=jax.ShapeDtypeStruct((n_rows, d // 2), jnp.int32),
        compiler_params=_cparams(1), name="moe_experts",
    )(*plan, xs, w_gu, b_gu.reshape(depth, n_e, 1, ff2), w_dn, b_dn.reshape(depth, n_e, 1, d))


def _combine_kernel(rows_ref, x_ref, y0_ref, y1_ref, y2_ref, y3_ref, gate_ref, gf_ref, lg_ref, lb_ref, *o_refs,
                    alpha, n_ctx_tiles):
    del rows_ref
    out = _ffn_residual(x_ref[...], [r[...] for r in (y0_ref, y1_ref, y2_ref, y3_ref)], gate_ref[...], gf_ref[...],
                        lg_ref[...], lb_ref[...], alpha)
    if n_ctx_tiles is None:
        o_refs[0][...] = out
    else:
        @pl.when(pl.program_id(0) < n_ctx_tiles)
        def _():
            o_refs[0][...] = out

        @pl.when(pl.program_id(0) >= n_ctx_tiles)
        def _():
            o_refs[1][...] = out


def _combine(x1, y4, gate, mod3, mod_base, rows, ln_g, ln_b, alpha, t_ctx=None):
    t, d = x1.shape
    nt = rows.shape[0]
    tm = t // nt
    y_spec = lambda k: pl.BlockSpec((None, tm, d // 2), lambda i, r: (k, i, 0))
    if t_ctx is None:
        n_ctx_tiles = None
        out_specs = pl.BlockSpec((tm, d), lambda i, r: (i, 0))
        out_shape = jax.ShapeDtypeStruct((t, d), F32)
    else:
        n_ctx_tiles = t_ctx // tm
        out_specs = [pl.BlockSpec((tm, d), lambda i, r: (jnp.minimum(i, n_ctx_tiles - 1), 0)),
                     pl.BlockSpec((tm, d), lambda i, r: (jnp.maximum(i - n_ctx_tiles, 0), 0))]
        out_shape = [jax.ShapeDtypeStruct((t_ctx, d), F32), jax.ShapeDtypeStruct((t - t_ctx, d), F32)]
    grid_spec = pltpu.PrefetchScalarGridSpec(
        num_scalar_prefetch=1, grid=(nt,),
        in_specs=[pl.BlockSpec((tm, d), lambda i, r: (i, 0)), y_spec(0), y_spec(1), y_spec(2), y_spec(3),
                  pl.BlockSpec((tm, LANES), lambda i, r: (i, 0)),
                  pl.BlockSpec((None, 1, d), lambda i, r: (mod_base + 5 * 16 + r[i], 0, 0)),
                  pl.BlockSpec((1, d), lambda i, r: (0, 0)), pl.BlockSpec((1, d), lambda i, r: (0, 0))],
        out_specs=out_specs,
    )
    return pl.pallas_call(
        functools.partial(_combine_kernel, alpha=alpha, n_ctx_tiles=n_ctx_tiles), grid_spec=grid_spec,
        out_shape=out_shape, compiler_params=_cparams(1), name="moe_combine",
    )(rows, x1, y4, y4, y4, y4, gate, mod3, ln_g, ln_b)


def _route_plan(route, counts, n_tiles):
    idx, rank = route[:TOP_K], route[TOP_K:]
    padded = (counts + TM_MOE - 1) // TM_MOE * TM_MOE
    pend = jnp.cumsum(padded)
    pstart = pend - padded
    onehot = idx[:, :, None] == jnp.arange(N_EXPERTS, dtype=jnp.int32)[None, None, :]
    dest = jnp.sum(jnp.where(onehot, pstart[None, None, :], 0), axis=-1) + rank
    n_used = (pend[-1] // TM_MOE).astype(jnp.int32)
    tile_start = jnp.arange(n_tiles, dtype=jnp.int32) * TM_MOE
    blk_e = jnp.sum((tile_start[:, None] >= pend[None, :]).astype(jnp.int32), axis=1)
    last_e = jnp.sum((jnp.maximum(n_used - 1, 0) * TM_MOE >= pend).astype(jnp.int32))
    blk_e = jnp.minimum(jnp.where(jnp.arange(n_tiles) < n_used, blk_e, last_e), N_EXPERTS - 1).astype(jnp.int32)
    blk_first = jnp.concatenate([jnp.ones((1,), jnp.int32), (blk_e[1:] != blk_e[:-1]).astype(jnp.int32)])
    ar = jnp.arange(N_EXPERTS, dtype=jnp.int32)
    later = jnp.logical_and((counts > 0)[None, :], ar[None, :] > ar[:, None])
    next_e = jnp.min(jnp.where(later, ar[None, :], N_EXPERTS), axis=1)
    next_e = jnp.where(next_e == N_EXPERTS, -1, next_e)
    blk_next = jnp.sum(jnp.where(blk_e[:, None] == ar[None, :], next_e[None, :], 0), axis=1).astype(jnp.int32)
    blk_slot = ((jnp.cumsum(blk_first) - 1) % 2).astype(jnp.int32)
    left = jnp.sum(jnp.where(blk_e[:, None] == ar[None, :], (pstart + counts)[None, :], 0), axis=1) - tile_start
    blk_half = (left <= TM_MOE // 2).astype(jnp.int32)
    return dest, (blk_e, blk_first, blk_next, blk_slot, blk_half, n_used.reshape(1))


SC_CORES = 2
SC_SUBCORES = 16


def _sc_mesh():
    return plsc.VectorSubcoreMesh(core_axis_name="core", subcore_axis_name="subcore")


def _sc_scatter_rows(src, dest, n_rows, window):
    n_slots, t = dest.shape
    w = src.shape[1]
    per_worker = t // (SC_CORES * SC_SUBCORES)
    assert per_worker * SC_CORES * SC_SUBCORES == t and per_worker % window == 0

    @functools.partial(pl.kernel, out_type=jax.ShapeDtypeStruct((n_rows, w), src.dtype), mesh=_sc_mesh(),
                       scratch_types=[pltpu.VMEM((n_slots, window), jnp.int32), pltpu.VMEM((window, w), src.dtype)],
                       name="sc_dispatch")
    def scatter(src_hbm, dest_hbm, out_hbm, idx_v, rows_v):
        worker = lax.axis_index("subcore") * SC_CORES + lax.axis_index("core")

        @pl.loop(0, per_worker // window)
        def _(c):
            base = pl.multiple_of(worker * per_worker + c * window, window)
            pltpu.sync_copy(src_hbm.at[pl.ds(base, window)], rows_v)
            pltpu.sync_copy(dest_hbm.at[:, pl.ds(base, window)], idx_v)
            for k in range(n_slots):
                pltpu.sync_copy(rows_v, out_hbm.at[idx_v.at[k]])

    return scatter(src, dest)


def _sc_gather_rows(table, idx, window):
    n = idx.shape[0]
    w = table.shape[1]
    per_worker = n // (SC_CORES * SC_SUBCORES)
    assert per_worker * SC_CORES * SC_SUBCORES == n and per_worker % window == 0

    @functools.partial(pl.kernel, out_type=jax.ShapeDtypeStruct((n, w), table.dtype), mesh=_sc_mesh(),
                       scratch_types=[pltpu.VMEM((window,), jnp.int32), pltpu.VMEM((window, w), table.dtype)],
                       name="sc_collect")
    def gather(table_hbm, idx_hbm, out_hbm, idx_v, rows_v):
        worker = lax.axis_index("subcore") * SC_CORES + lax.axis_index("core")

        @pl.loop(0, per_worker // window)
        def _(c):
            base = pl.multiple_of(worker * per_worker + c * window, window)
            pltpu.sync_copy(idx_hbm.at[pl.ds(base, window)], idx_v)
            pltpu.sync_copy(table_hbm.at[idx_v], rows_v)
            pltpu.sync_copy(rows_v, out_hbm.at[pl.ds(base, window)])

    return gather(table, idx)


def _dft_tables(n):
    k = jnp.arange(n, dtype=jnp.int32)

    def trig(rows):
        ang = ((rows[:, None] * k[None, :]) % n).astype(F32) * (2.0 * math.pi / n)
        return jnp.cos(ang), jnp.sin(ang)

    n2 = 64
    if n < 4 * n2:
        return trig(k)
    ca, sa = (z[:, None, :] for z in trig(jnp.arange(n // n2, dtype=jnp.int32) * n2))
    cb, sb = (z[None, :, :] for z in trig(jnp.arange(n2, dtype=jnp.int32)))
    return (ca * cb - sa * sb).reshape(n, n), (sa * cb + ca * sb).reshape(n, n)


def _rope_tables(n_lat, n_ctx_rows):
    half, quarter = HEAD_DIM // 2, HEAD_DIM // 4
    tpos = jnp.arange(n_lat, dtype=jnp.int32)
    lane = jnp.arange(LANES, dtype=jnp.int32) % HEAD_DIM
    pos = jnp.where(lane[None, :] < half, (tpos // GRID_W)[:, None], (tpos % GRID_W)[:, None]).astype(F32)
    fidx = (lane % quarter).astype(F32)
    freqs = ROPE_THETA ** (-fidx / quarter)
    ang = pos * freqs[None, :]
    cos, sin = jnp.cos(ang), jnp.sin(ang)
    first = (lane % half) < quarter
    sa = jnp.where(first[None, :], -sin, 0.0)
    sb = jnp.where(first[None, :], 0.0, sin)
    ident = lambda v: jnp.full((n_ctx_rows, LANES), v, F32)
    return (jnp.concatenate([ident(1.0), cos]), jnp.concatenate([ident(0.0), sa]), jnp.concatenate([ident(0.0), sb]))


def kernel(x_prompt, x_sample, cache_k_ab, cache_v_ab, c, c_ctx, w_ada, b_ada, ln_mix_g, ln_mix_b, ln_ffn_g, ln_ffn_b, w_in_ab, w_out_ab, sink_ab, w_in_c, b_in_c, ln_v_g, ln_v_b, w_sp, b_sp, w_out_c, w_router, b_router, w_gu, b_gu, w_dn, b_dn):
    n_ctx_b, ctx_seq, d = x_prompt.shape
    n_lat_b, lat_seq, _ = x_sample.shape
    depth = w_ada.shape[0]
    t_ctx, t_lat = n_ctx_b * ctx_seq, n_lat_b * lat_seq
    t = t_ctx + t_lat
    alpha = (2 * depth) ** 0.25
    kv_w = N_KV_HEADS * HEAD_DIM
    a_w = d // 4
    q_w = d - a_w
    assert t_ctx % TM == 0 and lat_seq % TM == 0 and t_ctx % lat_seq == 0 and n_lat_b + 1 <= 16

    cond_rows = lambda tm: jnp.asarray(np.concatenate(
        [np.zeros(t_ctx // tm, np.int32), 1 + np.arange(t_lat // tm, dtype=np.int32) // (lat_seq // tm)]))
    rows, rows_c = cond_rows(TM), cond_rows(TM_COMBINE)
    rblk = jnp.asarray(np.concatenate(
        [np.zeros(t_ctx // TM, np.int32), 1 + np.arange(t_lat // TM, dtype=np.int32) % (lat_seq // TM)]))

    cond16 = jnp.zeros((16, d), F32).at[0].set(c_ctx).at[1:1 + n_lat_b].set(c)
    mod = _ada(cond16, w_ada, b_ada)
    mod3 = mod.reshape(depth, 16, 6, d).transpose(0, 2, 1, 3).reshape(depth * 6 * 16, 1, d)

    cos, sa, sb = _rope_tables(lat_seq, TM)
    cs_ctx, ss_ctx = (z.astype(BF16) for z in _dft_tables(ctx_seq))
    cs_lat, ss_lat = (z.astype(BF16) for z in _dft_tables(lat_seq))
    gd = a_w // A_GROUPS
    cd, sd = _dft_tables(gd)
    eye = jnp.eye(A_GROUPS, dtype=F32)
    bdc, bds = jnp.kron(eye, cd).astype(BF16), jnp.kron(eye, sd).astype(BF16)

    past = cache_k_ab.shape[2]
    cache_k = cache_k_ab.reshape(n_lat_b, -1, past, kv_w).astype(BF16)
    cache_v = cache_v_ab.reshape(n_lat_b, -1, past, kv_w).astype(BF16)

    x = (x_prompt.reshape(t_ctx, d), x_sample.reshape(t_lat, d))
    row2 = lambda v: v.reshape(1, -1)
    ks, vs = [], []
    for l in range(depth):
        j = l // 2
        mod_base = l * 6 * 16
        w_r, b_r = w_router[l].T, b_router[l].reshape(-1, 1)
        if l % 2 == 0:
            ac, as_, q, k32, v32, k, v = _inproj(x, t_ctx // TM, mod3, mod_base, rows, rblk, w_in_ab[j].astype(BF16),
                                                 cos, sa, sb, bdc, bds, a_w, q_w, kv_w)
            ks.append(k32[:t_ctx].reshape(n_ctx_b, ctx_seq, N_KV_HEADS, HEAD_DIM))
            vs.append(v32[:t_ctx].reshape(n_ctx_b, ctx_seq, N_KV_HEADS, HEAD_DIM))
            ya_p = _fourier(ac, as_, cs_ctx, ss_ctx, n_ctx_b, ctx_seq, 0, min(ctx_seq, 512))
            ya_s = _fourier(ac, as_, cs_lat, ss_lat, n_lat_b, lat_seq, t_ctx // lat_seq, min(lat_seq, 1024))
            yb_p = _attn_ctx(sink_ab[j], q, k, v, n_ctx_b, ctx_seq)
            yb_s = _attn_lat(sink_ab[j], q, k, v, cache_k, cache_v, j, n_lat_b, lat_seq, t_ctx)
            x1, hf, route, gate, cnt = _post_ab(x, ya_p, yb_p, ya_s, yb_s, w_out_ab[j].astype(BF16), mod3, mod_base,
                                                rows, row2(ln_mix_g[l]), row2(ln_mix_b[l]), w_r, b_r, alpha)
        else:
            x1, hf, route, gate, cnt = _gmlp(prev, mod3, mod_base, rows, w_in_c[j].astype(BF16), row2(b_in_c[j]),
                                             row2(ln_v_g[j]), row2(ln_v_b[j]), w_sp[j].astype(BF16),
                                             b_sp[j][:, :, None], w_out_c[j].astype(BF16), row2(ln_mix_g[l]),
                                             row2(ln_mix_b[l]), w_r, b_r, alpha)
        n_rows = t * TOP_K + N_EXPERTS * TM_MOE
        dest, plan = _route_plan(route, cnt[:, 0].astype(jnp.int32), n_rows // TM_MOE)
        xs = _sc_scatter_rows(hf, dest, n_rows, 128)
        out_sorted = _moe(xs, plan, w_gu, b_gu, w_dn, b_dn, l)
        y4 = _sc_gather_rows(out_sorted, dest.reshape(-1), 128).reshape(TOP_K, t, d // 2)
        if l % 2 == 0 and l + 1 < depth:
            prev = (x1, y4, gate, row2(ln_ffn_g[l]), row2(ln_ffn_b[l]))
        else:
            x = _combine(x1, y4, gate, mod3, mod_base, rows_c, row2(ln_ffn_g[l]), row2(ln_ffn_b[l]), alpha,
                         t_ctx if l == depth - 1 else None)

    y_prompt = x[0].reshape(n_ctx_b, ctx_seq, d)
    y_sample = x[1].reshape(n_lat_b, lat_seq, d)
    return (y_prompt, y_sample, jnp.stack(ks, axis=1), jnp.stack(vs, axis=1))
```

```python
import functools
import math

import numpy as np
import jax
import jax.numpy as jnp
from jax import lax
from jax.experimental import pallas as pl
from jax.experimental.pallas import tpu as pltpu
from jax.experimental.pallas import tpu_sc as plsc

GRID_W = 64
BLK = 128
WINDOW = 128
HEAD_DIM = 64
A_GROUPS = 4
N_KV_HEADS = 4
C_GROUPS = 8
CHUNK = 128
N_EXPERTS = 32
TOP_K = 4
SWIGLU_LIMIT = 7.0
SWIGLU_ALPHA = 1.702
ROPE_THETA = 10000.0
LN_EPS = 1e-6
NEG_INF = -1e30
LOG2_E = math.log2(math.e)

LANES = 128
TM = 1024
TM_COMBINE = 512
ATTN_BLOCKS_PER_STEP = 8
TM_MOE = 512
VMEM_LIMIT = 52 * 1024 * 1024
VMEM_LIMIT_GMLP = 60 * 1024 * 1024

F32 = jnp.float32
BF16 = jnp.bfloat16


def _cparams(n_axes, vmem_limit=VMEM_LIMIT):
    return pltpu.CompilerParams(dimension_semantics=("arbitrary",) * n_axes, vmem_limit_bytes=vmem_limit)


def _ln(x):
    mu = jnp.mean(x, axis=-1, keepdims=True)
    xc = x - mu
    var = jnp.mean(xc * xc, axis=-1, keepdims=True)
    return xc * lax.rsqrt(var + LN_EPS)


def _dot(a, b):
    return jnp.dot(a, b, preferred_element_type=F32)


def _dot_nt(a, b):
    return lax.dot_general(a, b, (((1,), (1,)), ((), ())), preferred_element_type=F32)


def _split(a):
    hi = a.astype(BF16)
    lo = (a - hi.astype(F32)).astype(BF16)
    return hi, lo


def _pack_bf16_pairs(v):
    n = v.shape[1] // 2
    lo = pltpu.bitcast(v[:, :n].astype(BF16).astype(F32), jnp.uint32) >> 16
    hi = pltpu.bitcast(v[:, n:].astype(BF16).astype(F32), jnp.uint32) & jnp.uint32(0xFFFF0000)
    return pltpu.bitcast(lo | hi, jnp.int32)


def _unpack_bf16_pairs(p):
    u = pltpu.bitcast(p, jnp.uint32)
    lo = pltpu.bitcast(u << 16, F32)
    hi = pltpu.bitcast(u & jnp.uint32(0xFFFF0000), F32)
    return jnp.concatenate([lo, hi], axis=1).astype(BF16)


def _row_chains(n_rows, n_chains=2):
    step = n_rows // n_chains
    return [slice(c * step, (c + 1) * step) for c in range(n_chains)]


def _run_skewed(stages, states):
    states = list(states)
    for step in range(len(stages) + len(states) - 1):
        for c in range(len(states)):
            if 0 <= step - c < len(stages):
                states[c] = stages[step - c](states[c])
    return states


def _tail_norm(x, y, g_m, ln_g, ln_b, sc_f, sh_f, alpha):
    x1 = _ln(alpha * x + g_m * y) * ln_g + ln_b
    return x1, _ln(x1) * (1.0 + sc_f) + sh_f


def _tail_logits(hf, w_r, b_r):
    hf_hi, hf_lo = _split(hf)
    w_hi, w_lo = _split(w_r)
    logits = _dot_nt(w_hi, hf_hi) + (_dot_nt(w_hi, hf_lo) + _dot_nt(w_lo, hf_hi)) + b_r
    return _pack_bf16_pairs(hf), logits


def _tail_route(logits, route_ref, gate_ref, cnt_ref, carry_ref):
    @pl.when(pl.program_id(0) == 0)
    def _():
        carry_ref[...] = jnp.zeros_like(carry_ref)

    tm = logits.shape[1]
    sub = lax.broadcasted_iota(jnp.int32, logits.shape, 0)
    vals = logits
    top_v, top_i = [], []
    for _ in range(TOP_K):
        m = jnp.max(vals, axis=0, keepdims=True)
        am = jnp.min(jnp.where(vals == m, sub, N_EXPERTS), axis=0, keepdims=True)
        top_v.append(m)
        top_i.append(am)
        vals = jnp.where(sub == am, -jnp.inf, vals)
    e = [jnp.exp(v - top_v[0]) for v in top_v]
    denom = e[0] + e[1] + e[2] + e[3]
    gates_t = jnp.concatenate([ek / denom for ek in e] + [jnp.zeros((LANES - TOP_K, tm), F32)], axis=0)
    gate_ref[...] = gates_t.T

    member = jnp.zeros(logits.shape, F32)
    for am in top_i:
        member = jnp.where(sub == am, 1.0, member)
    r_i = lax.broadcasted_iota(jnp.int32, (tm, tm), 0)
    c_i = lax.broadcasted_iota(jnp.int32, (tm, tm), 1)
    earlier = jnp.where(r_i < c_i, 1.0, 0.0).astype(BF16)
    before = _dot(member.astype(BF16), earlier) + carry_ref[...]
    ranks = [jnp.sum(jnp.where(sub == am, before, 0.0), axis=0, keepdims=True).astype(jnp.int32) for am in top_i]
    route_ref[...] = jnp.concatenate(top_i + ranks, axis=0)
    carry = carry_ref[...] + jnp.sum(member, axis=1, keepdims=True)
    carry_ref[...] = carry
    cnt_ref[...] = jnp.broadcast_to(carry, cnt_ref.shape)


def _ada_kernel(cond_ref, w_ref, b_ref, o_ref):
    c = cond_ref[...]
    s = (c * jax.nn.sigmoid(c)).astype(BF16)
    o_ref[...] = _dot(s, w_ref[...].astype(BF16)) + b_ref[...]


def _ada(cond16, w_ada, b_ada):
    depth, d, n = w_ada.shape
    tn = 1536
    return pl.pallas_call(
        _ada_kernel,
        grid=(depth, n // tn),
        in_specs=[
            pl.BlockSpec((16, d), lambda l, j: (0, 0)),
            pl.BlockSpec((None, d, tn), lambda l, j: (l, 0, j)),
            pl.BlockSpec((None, 1, tn), lambda l, j: (l, 0, j)),
        ],
        out_specs=pl.BlockSpec((None, 16, tn), lambda l, j: (l, 0, j)),
        out_shape=jax.ShapeDtypeStruct((depth, 16, n), F32),
        compiler_params=_cparams(2),
        name="ada_mod",
    )(cond16, w_ada, b_ada.reshape(depth, 1, n))


def _group_rows(xc_ref, xl_ref, rows, n_ctx_tiles):
    return jnp.where(pl.program_id(0) < n_ctx_tiles, xc_ref[rows, :], xl_ref[rows, :])


def _group_specs(x, n_ctx_tiles):
    if isinstance(x, tuple):
        arrays, lat = x, (lambda i, *_: (jnp.maximum(i - n_ctx_tiles, 0), 0))
    else:
        arrays, lat = (x, x), (lambda i, *_: (jnp.maximum(i, n_ctx_tiles), 0))
    d = arrays[0].shape[1]
    ctx = lambda i, *_: (jnp.minimum(i, n_ctx_tiles - 1), 0)
    return arrays, [pl.BlockSpec((TM, d), ctx), pl.BlockSpec((TM, d), lat)]


def _inproj_kernel(rows_ref, rblk_ref, xc_ref, xl_ref, sc_ref, sh_ref, w_ref, cos_ref, sa_ref, sb_ref, bdc_ref,
                   bds_ref, ac_ref, as_ref, q_ref, k_ref, v_ref, kb_ref, vb_ref, *, a_w, q_w, kv_w, n_ctx_tiles):
    del rows_ref, rblk_ref
    g = q_w // kv_w

    def norm_in(s):
        x = _group_rows(xc_ref, xl_ref, s["rows"], n_ctx_tiles)
        return dict(s, h=(_ln(x) * (1.0 + sc_ref[...]) + sh_ref[...]).astype(BF16))

    def rope(t, rs):
        w = t.shape[1]
        c, a_, b_ = (jnp.tile(z[rs, :], (1, w // LANES)) for z in (cos_ref, sa_ref, sb_ref))
        nxt = pltpu.roll(t, w - HEAD_DIM // 4, 1)
        prv = pltpu.roll(t, HEAD_DIM // 4, 1)
        return t * c + nxt * a_ + prv * b_

    def proj(s):
        p = _dot(s["h"], w_ref[...])
        return dict(s, h=None, a=p[:, :a_w].astype(BF16), q=p[:, a_w:a_w + q_w], kv=p[:, a_w + q_w:])

    def channel_dft(s):
        ac_ref[s["rows"], :] = _dot(s["a"], bdc_ref[...]).astype(BF16)
        as_ref[s["rows"], :] = _dot(s["a"], bds_ref[...]).astype(BF16)
        return dict(s, a=None)

    def rope_q(s):
        rs = s["rows"]
        q = rope(s["q"], rs) * (HEAD_DIM ** -0.5 * LOG2_E)
        lane = lax.broadcasted_iota(jnp.int32, (q.shape[0], LANES), 1)
        for j in range(q_w // HEAD_DIM):
            tile = q[:, (j // 2) * LANES:(j // 2 + 1) * LANES]
            dst_low = (j // g) % 2 == 0
            if (j % 2 == 0) != dst_low:
                tile = pltpu.roll(tile, HEAD_DIM, 1)
            keep = (lane < HEAD_DIM) if dst_low else (lane >= HEAD_DIM)
            q_ref[rs, j * LANES:(j + 1) * LANES] = jnp.where(keep, tile, 0.0).astype(BF16)
        return dict(s, q=None)

    def rope_k(s):
        rs = s["rows"]
        k = rope(s["kv"][:, :kv_w], rs)
        v = s["kv"][:, kv_w:]
        k_ref[rs, :] = k
        v_ref[rs, :] = v
        kb_ref[rs, :] = k.astype(BF16)
        vb_ref[rs, :] = v.astype(BF16)
        return dict(s, kv=None)

    _run_skewed([norm_in, proj, channel_dft, rope_q, rope_k],
                [dict(rows=rs) for rs in _row_chains(xc_ref.shape[0])])


def _inproj(x, n_ctx_tiles, mod3, mod_base, rows, rblk, w_in, cos, sa, sb, bdc, bds, a_w, q_w, kv_w):
    (xc, xl), x_specs = _group_specs(x, n_ctx_tiles)
    d = xc.shape[1]
    t = rows.shape[0] * TM
    n = w_in.shape[1]
    nt = t // TM
    mod_spec = lambda j: pl.BlockSpec((None, 1, d), lambda i, r, rb: (mod_base + j * 16 + r[i], 0, 0))
    whole = lambda shp: pl.BlockSpec(shp, lambda i, r, rb: (0,) * len(shp))
    rope_spec = pl.BlockSpec((TM, LANES), lambda i, r, rb: (rb[i], 0))
    tok = lambda w: pl.BlockSpec((TM, w), lambda i, r, rb: (i, 0))
    grid_spec = pltpu.PrefetchScalarGridSpec(
        num_scalar_prefetch=2, grid=(nt,),
        in_specs=x_specs + [mod_spec(1), mod_spec(0), whole((d, n)), rope_spec, rope_spec, rope_spec,
                            whole((a_w, a_w)), whole((a_w, a_w))],
        out_specs=[tok(a_w), tok(a_w), tok(2 * q_w), tok(kv_w), tok(kv_w), tok(kv_w), tok(kv_w)],
    )
    return pl.pallas_call(
        functools.partial(_inproj_kernel, a_w=a_w, q_w=q_w, kv_w=kv_w, n_ctx_tiles=n_ctx_tiles),
        grid_spec=grid_spec,
        out_shape=[jax.ShapeDtypeStruct((t, a_w), BF16), jax.ShapeDtypeStruct((t, a_w), BF16),
                   jax.ShapeDtypeStruct((t, 2 * q_w), BF16), jax.ShapeDtypeStruct((t, kv_w), F32),
                   jax.ShapeDtypeStruct((t, kv_w), F32), jax.ShapeDtypeStruct((t, kv_w), BF16),
                   jax.ShapeDtypeStruct((t, kv_w), BF16)],
        compiler_params=_cparams(1),
        name="inproj_ab",
    )(rows, rblk, xc, xl, mod3, mod3, w_in, cos, sa, sb, bdc, bds)


def _fourier_kernel(c_ref, s_ref, ac_ref, as_ref, o_ref, *, scale):
    y = _dot(c_ref[...], ac_ref[...]) - _dot(s_ref[...], as_ref[...])
    o_ref[...] = (y * scale).astype(BF16)


def _fourier(ac, as_, cs, ss, n_batch, seq, blk_off, tm):
    a_w = ac.shape[1]
    nr = seq // tm
    a_spec = pl.BlockSpec((seq, a_w), lambda r, b: (blk_off + b, 0))
    t_spec = pl.BlockSpec((tm, seq), lambda r, b: (r, 0))
    return pl.pallas_call(
        functools.partial(_fourier_kernel, scale=(seq * (a_w // A_GROUPS)) ** -0.5),
        grid=(nr, n_batch),
        in_specs=[t_spec, t_spec, a_spec, a_spec],
        out_specs=pl.BlockSpec((tm, a_w), lambda r, b: (b * nr + r, 0)),
        out_shape=jax.ShapeDtypeStruct((n_batch * seq, a_w), BF16),
        compiler_params=_cparams(2),
        name="fourier_%d" % seq,
    )(cs, ss, ac, as_)


def _attend_pairs(sink_ref, problems, *, g):
    def scores(s):
        pr, h0, p = s["pr"], s["h0"], s["h0"] // (2 * g)
        qp = jnp.concatenate([pr["q"][:, (h0 + j) * LANES:(h0 + j + 1) * LANES] for j in range(g)], axis=0)
        sc = [_dot_nt(qp, kf(p)) for kf in pr["keys"]]
        if pr["mask"] is not None:
            sc[0] = jnp.where(jnp.concatenate([pr["mask"]] * g, axis=0), sc[0], NEG_INF)
        return dict(s, sc=sc)

    def row_max(s):
        rows = s["pr"]["q"].shape[0]
        m = jnp.concatenate([jnp.full((rows, 1), sink_ref[s["h0"] + j] * LOG2_E, F32) for j in range(g)], axis=0)
        sink_col = m
        for sc in s["sc"]:
            m = jnp.maximum(m, jnp.max(sc, axis=-1, keepdims=True))
        return dict(s, m=m, sink=jnp.exp2(sink_col - m))

    def weights(s):
        return dict(s, e=[jnp.exp2(sc - s["m"]).astype(BF16) for sc in s["sc"]], sc=None)

    def weighted_values(s):
        p = s["h0"] // (2 * g)
        acc = None
        for e, vf in zip(s["e"], s["pr"]["values"]):
            v = vf(p)
            pv = _dot(e, jnp.concatenate([v, jnp.ones_like(v)], axis=1))
            acc = pv if acc is None else acc + pv
        return dict(s, acc=acc, e=None)

    def normalise(s):
        acc = s["acc"]
        return dict(s, o=acc[:, :LANES] / (acc[:, LANES:LANES + 1] + s["sink"]), acc=None)

    states = _run_skewed([scores, row_max, weights, weighted_values, normalise],
                         [dict(pr=pr, h0=h * g) for pr in problems for h in range(N_KV_HEADS)])
    for n, pr in enumerate(problems):
        rows = pr["q"].shape[0]
        lane = lax.broadcasted_iota(jnp.int32, (rows, LANES), 1)
        heads = {}
        for s in states[n * N_KV_HEADS:(n + 1) * N_KV_HEADS]:
            for j in range(g):
                blk = s["o"][j * rows:(j + 1) * rows]
                head = s["h0"] + j
                if (((head // g) % 2 == 0) != (head % 2 == 0)):
                    blk = pltpu.roll(blk, HEAD_DIM, 1)
                heads[head] = blk
        for t in range(len(heads) // 2):
            tile = jnp.where(lane < HEAD_DIM, heads[2 * t], heads[2 * t + 1])
            pr["o"][:, t * LANES:(t + 1) * LANES] = tile.astype(BF16)


def _attn_ctx_kernel(sink_ref, q_ref, k_ref, v_ref, o_ref, *, g):
    pair = lambda ref: (lambda p: ref[:, p * LANES:(p + 1) * LANES])
    _attend_pairs(sink_ref, [dict(q=q_ref, keys=[pair(k_ref)], values=[pair(v_ref)], mask=None, o=o_ref)], g=g)


def _attn_ctx(sink, q, k, v, n_batch, seq):
    qp_w, kv_w = q.shape[1], k.shape[1]
    q_w = qp_w // 2
    g = q_w // kv_w
    return pl.pallas_call(
        functools.partial(_attn_ctx_kernel, g=g),
        grid=(n_batch,),
        in_specs=[pl.BlockSpec(memory_space=pltpu.SMEM),
                  pl.BlockSpec((seq, qp_w), lambda b: (b, 0)),
                  pl.BlockSpec((seq, kv_w), lambda b: (b, 0)),
                  pl.BlockSpec((seq, kv_w), lambda b: (b, 0))],
        out_specs=pl.BlockSpec((seq, q_w), lambda b: (b, 0)),
        out_shape=jax.ShapeDtypeStruct((n_batch * seq, q_w), BF16),
        compiler_params=_cparams(1),
        name="attn_ctx",
    )(sink, q, k, v)


def _attn_lat_kernel(sink_ref, q_ref, k_ref, v_ref, ck_ref, cv_ref, o_ref, *, g, seq):
    n_loc = 3 * BLK
    row = lax.broadcasted_iota(jnp.int32, (BLK, n_loc), 0)
    col = lax.broadcasted_iota(jnp.int32, (BLK, n_loc), 1)
    ctx = lambda ref: (lambda p: ref[:, p * LANES:(p + 1) * LANES])
    problems = []
    for sub in range(q_ref.shape[0] // BLK):
        i = pl.program_id(1) * (q_ref.shape[0] // BLK) + sub
        start = pl.multiple_of(jnp.clip((i - 1) * BLK, 0, seq - n_loc), BLK)
        band = jnp.abs(row + (i * BLK - start) - col) <= WINDOW
        loc = lambda ref, start=start: (lambda p: ref[pl.ds(start, n_loc), p * LANES:(p + 1) * LANES])
        rows = slice(sub * BLK, (sub + 1) * BLK)
        problems.append(dict(q=q_ref.at[rows, :], keys=[loc(k_ref), ctx(ck_ref)], values=[loc(v_ref), ctx(cv_ref)],
                             mask=band, o=o_ref.at[rows, :]))
    _attend_pairs(sink_ref, problems, g=g)


def _attn_lat(sink, q, k, v, cache_k, cache_v, layer_slot, n_batch, seq, tok_off):
    qp_w, kv_w = q.shape[1], k.shape[1]
    q_w = qp_w // 2
    g = q_w // kv_w
    rows = BLK * ATTN_BLOCKS_PER_STEP
    nb = seq // rows
    past = cache_k.shape[2]
    assert seq >= 3 * BLK and seq % rows == 0 and tok_off % rows == 0
    kv_spec = pl.BlockSpec((seq, kv_w), lambda b, i: (tok_off // seq + b, 0))
    c_spec = pl.BlockSpec((None, None, past, kv_w), lambda b, i: (b, layer_slot, 0, 0))
    return pl.pallas_call(
        functools.partial(_attn_lat_kernel, g=g, seq=seq),
        grid=(n_batch, nb),
        in_specs=[pl.BlockSpec(memory_space=pltpu.SMEM),
                  pl.BlockSpec((rows, qp_w), lambda b, i: (tok_off // rows + b * nb + i, 0)),
                  kv_spec, kv_spec, c_spec, c_spec],
        out_specs=pl.BlockSpec((rows, q_w), lambda b, i: (b * nb + i, 0)),
        out_shape=jax.ShapeDtypeStruct((n_batch * seq, q_w), BF16),
        compiler_params=_cparams(2),
        name="attn_lat",
    )(sink, q, k, v, cache_k, cache_v)


def _post_ab_kernel(rows_ref, xc_ref, xl_ref, yap_ref, ybp_ref, yas_ref, ybs_ref, w_ref, gm_ref, lg_ref, lb_ref,
                    scf_ref, shf_ref, wr_ref, br_ref, x1_ref, hf_ref, route_ref, gate_ref, cnt_ref, carry_ref,
                    *, n_ctx_tiles, alpha):
    del rows_ref
    is_ctx = pl.program_id(0) < n_ctx_tiles
    a_w = yap_ref.shape[1]

    def proj_out(s):
        rs = s["rows"]
        pick = lambda p_ref, s_ref: pltpu.bitcast(
            jnp.where(is_ctx, pltpu.bitcast(p_ref[rs, :], jnp.uint32), pltpu.bitcast(s_ref[rs, :], jnp.uint32)), BF16)
        ya = pick(yap_ref, yas_ref)
        yb = pick(ybp_ref, ybs_ref)
        return dict(s, y=_dot(ya, w_ref[:a_w, :]) + _dot(yb, w_ref[a_w:, :]))

    def tail_norm(s):
        rs = s["rows"]
        x1, hf = _tail_norm(_group_rows(xc_ref, xl_ref, rs, n_ctx_tiles), s["y"], gm_ref[...], lg_ref[...],
                            lb_ref[...], scf_ref[...], shf_ref[...], alpha)
        x1_ref[rs, :] = x1
        return dict(s, hf=hf, y=None)

    def tail_logits(s):
        packed, lg = _tail_logits(s["hf"], wr_ref[...], br_ref[...])
        hf_ref[s["rows"], :] = packed
        return dict(s, logits=lg, hf=None)

    states = _run_skewed([proj_out, tail_norm, tail_logits],
                         [dict(rows=rs) for rs in _row_chains(xc_ref.shape[0], 2)])
    _tail_route(jnp.concatenate([s["logits"] for s in states], axis=1), route_ref, gate_ref, cnt_ref, carry_ref)


def _tail_out(t, d, tm):
    shapes = [jax.ShapeDtypeStruct((t, d), F32), jax.ShapeDtypeStruct((t, d // 2), jnp.int32),
              jax.ShapeDtypeStruct((2 * TOP_K, t), jnp.int32), jax.ShapeDtypeStruct((t, LANES), F32),
              jax.ShapeDtypeStruct((N_EXPERTS, LANES), F32)]
    specs = [pl.BlockSpec((tm, d), lambda i, r: (i, 0)), pl.BlockSpec((tm, d // 2), lambda i, r: (i, 0)),
             pl.BlockSpec((2 * TOP_K, tm), lambda i, r: (0, i)), pl.BlockSpec((tm, LANES), lambda i, r: (i, 0)),
             pl.BlockSpec((N_EXPERTS, LANES), lambda i, r: (0, 0))]
    return shapes, specs


_TAIL_SCRATCH = [pltpu.VMEM((N_EXPERTS, 1), F32)]


def _post_ab(x, ya_p, yb_p, ya_s, yb_s, w_out, mod3, mod_base, rows, ln_g, ln_b, w_r, b_r, alpha):
    n_ctx_tiles = ya_p.shape[0] // TM
    (xc, xl), x_specs = _group_specs(x, n_ctx_tiles)
    d = xc.shape[1]
    nt = rows.shape[0]
    t = nt * TM
    a_w, q_w = ya_p.shape[1], yb_p.shape[1]
    mod_spec = lambda j: pl.BlockSpec((None, 1, d), lambda i, r: (mod_base + j * 16 + r[i], 0, 0))
    whole = lambda shp: pl.BlockSpec(shp, lambda i, r: (0,) * len(shp))
    ctx = lambda w: pl.BlockSpec((TM, w), lambda i, r: (jnp.minimum(i, n_ctx_tiles - 1), 0))
    lat = lambda w: pl.BlockSpec((TM, w), lambda i, r: (jnp.maximum(i - n_ctx_tiles, 0), 0))
    shapes, specs = _tail_out(t, d, TM)
    grid_spec = pltpu.PrefetchScalarGridSpec(
        num_scalar_prefetch=1, grid=(nt,),
        in_specs=x_specs + [ctx(a_w), ctx(q_w), lat(a_w), lat(q_w),
                            whole(w_out.shape), mod_spec(2), whole((1, d)), whole((1, d)), mod_spec(4), mod_spec(3),
                            whole(w_r.shape), whole(b_r.shape)],
        out_specs=specs,
        scratch_shapes=_TAIL_SCRATCH,
    )
    return pl.pallas_call(
        functools.partial(_post_ab_kernel, n_ctx_tiles=n_ctx_tiles, alpha=alpha),
        grid_spec=grid_spec, out_shape=shapes, compiler_params=_cparams(1), name="post_ab",
    )(rows, xc, xl, ya_p, yb_p, ya_s, yb_s, w_out, mod3, ln_g, ln_b, mod3, mod3, w_r, b_r)


def _unpack_f32_pairs(p):
    u = pltpu.bitcast(p, jnp.uint32)
    return jnp.concatenate([pltpu.bitcast(u << 16, F32), pltpu.bitcast(u & jnp.uint32(0xFFFF0000), F32)], axis=1)


def _ffn_residual(x1, y_packed, gate, g_f, ln_g, ln_b, alpha):
    y = gate[:, 0:1] * _unpack_f32_pairs(y_packed[0])
    for k in range(1, TOP_K):
        y = y + gate[:, k:k + 1] * _unpack_f32_pairs(y_packed[k])
    return _ln(alpha * x1 + g_f * y) * ln_g + ln_b


def _gmlp_kernel(rows_ref, xp_ref, y0_ref, y1_ref, y2_ref, y3_ref, pgate_ref, pgf_ref, plg_ref, plb_ref,
                 scm_ref, shm_ref, win_ref, bin_ref, gv_ref, bv_ref, wsp_ref, bsp_ref, wout_ref,
                 gm_ref, lg_ref, lb_ref, scf_ref, shf_ref, wr_ref, br_ref, x1_ref, hf_ref, route_ref, gate_ref,
                 cnt_ref, carry_ref, *, alpha):
    del rows_ref
    assert (xp_ref.shape[0] // 2) % CHUNK == 0
    c_w = win_ref.shape[1] // 2
    gd = c_w // C_GROUPS

    def norm_in(s):
        rs = s["rows"]
        x = _ffn_residual(xp_ref[rs, :], [r[rs, :] for r in (y0_ref, y1_ref, y2_ref, y3_ref)], pgate_ref[rs, :],
                          pgf_ref[...], plg_ref[...], plb_ref[...], alpha)
        return dict(s, x=x, h=(_ln(x) * (1.0 + scm_ref[...]) + shm_ref[...]).astype(BF16))

    def proj_in(s):
        return dict(s, z=_dot(s["h"], win_ref[...]) + bin_ref[...])

    def gate_split(s):
        z = s["z"]
        z = 0.5 * z * (1.0 + lax.erf(z * (2.0 ** -0.5)))
        return dict(s, z=None, u=z[:, :c_w], v=z[:, c_w:])

    def norm_v(s):
        return dict(s, v=(_ln(s["v"]) * gv_ref[...] + bv_ref[...]).astype(BF16))

    def spatial(s):
        v = s["v"]
        chunks = []
        for n in range(v.shape[0] // CHUNK):
            groups = [_dot(wsp_ref[g], v[n * CHUNK:(n + 1) * CHUNK, g * gd:(g + 1) * gd]) + bsp_ref[g]
                      for g in range(C_GROUPS)]
            chunks.append(jnp.concatenate(groups, axis=1))
        return dict(s, t=(s["u"] * jnp.concatenate(chunks, axis=0)).astype(BF16))

    def proj_out(s):
        return dict(s, y=_dot(s["t"], wout_ref[...]))

    def tail_norm(s):
        x1, hf = _tail_norm(s["x"], s["y"], gm_ref[...], lg_ref[...], lb_ref[...], scf_ref[...], shf_ref[...], alpha)
        x1_ref[s["rows"], :] = x1
        return dict(s, hf=hf, x=None, y=None)

    def tail_logits(s):
        packed, lg = _tail_logits(s["hf"], wr_ref[...], br_ref[...])
        hf_ref[s["rows"], :] = packed
        return dict(s, logits=lg, hf=None)

    states = _run_skewed([norm_in, proj_in, gate_split, norm_v, spatial, proj_out, tail_norm, tail_logits],
                         [dict(rows=rs) for rs in _row_chains(xp_ref.shape[0])])
    _tail_route(jnp.concatenate([s["logits"] for s in states], axis=1), route_ref, gate_ref, cnt_ref, carry_ref)


def _gmlp(prev, mod3, mod_base, rows, w_in, b_in, g_v, b_v, w_sp, b_sp, w_out, ln_g, ln_b, w_r, b_r, alpha):
    x1p, y4, pgate, plg, plb = prev
    t, d = x1p.shape
    nt = rows.shape[0]
    tm = t // nt
    mod_spec = lambda j, base=mod_base: pl.BlockSpec((None, 1, d), lambda i, r: (base + j * 16 + r[i], 0, 0))
    whole = lambda shp: pl.BlockSpec(shp, lambda i, r: (0,) * len(shp))
    y_spec = lambda k: pl.BlockSpec((None, tm, d // 2), lambda i, r: (k, i, 0))
    shapes, specs = _tail_out(t, d, tm)
    grid_spec = pltpu.PrefetchScalarGridSpec(
        num_scalar_prefetch=1, grid=(nt,),
        in_specs=[pl.BlockSpec((tm, d), lambda i, r: (i, 0)), y_spec(0), y_spec(1), y_spec(2), y_spec(3),
                  pl.BlockSpec((tm, LANES), lambda i, r: (i, 0)), mod_spec(5, mod_base - 6 * 16),
                  whole((1, d)), whole((1, d)), mod_spec(1), mod_spec(0),
                  whole(w_in.shape), whole(b_in.shape), whole(g_v.shape), whole(b_v.shape),
                  whole(w_sp.shape), whole(b_sp.shape), whole(w_out.shape),
                  mod_spec(2), whole((1, d)), whole((1, d)), mod_spec(4), mod_spec(3),
                  whole(w_r.shape), whole(b_r.shape)],
        out_specs=specs,
        scratch_shapes=_TAIL_SCRATCH,
    )
    return pl.pallas_call(
        functools.partial(_gmlp_kernel, alpha=alpha),
        grid_spec=grid_spec, out_shape=shapes, compiler_params=_cparams(1, VMEM_LIMIT_GMLP), name="gmlp",
    )(rows, x1p, y4, y4, y4, y4, pgate, mod3, plg, plb, mod3, mod3, w_in, b_in, g_v, b_v, w_sp, b_sp, w_out,
      mod3, ln_g, ln_b, mod3, mod3, w_r, b_r)


def _moe_kernel(be_ref, bf_ref, nx_ref, sl_ref, hf_ref, nu_ref, x_ref, wgu_hbm, bgu_ref, wdn_hbm, bdn_ref, o_ref,
                wgu_f, wdn_f, wgu_s, wdn_s, sem, *, layer):
    i = pl.program_id(0)
    d_ff = wdn_s.shape[0]

    def weight_copies(e, slot):
        return (pltpu.make_async_copy(wgu_hbm.at[layer, e], wgu_f.at[slot], sem.at[0, slot]),
                pltpu.make_async_copy(wdn_hbm.at[layer, e], wdn_f.at[slot], sem.at[1, slot]))

    @pl.when(i < nu_ref[0])
    def _():
        @pl.when(bf_ref[i] == 1)
        def _():
            slot = sl_ref[i]

            @pl.when(i == 0)
            def _():
                for c in weight_copies(be_ref[i], slot):
                    c.start()

            for c in weight_copies(be_ref[i], slot):
                c.wait()
            wgu_s[...] = wgu_f[slot].astype(BF16)
            wdn_s[...] = wdn_f[slot].astype(BF16)

            @pl.when(nx_ref[i] >= 0)
            def _():
                for c in weight_copies(nx_ref[i], 1 - slot):
                    c.start()

        def up(s):
            return dict(s, gu=_dot(_unpack_bf16_pairs(x_ref[s["rows"], :]), wgu_s[...]) + bgu_ref[...])

        def act(s):
            gu = s["gu"]
            gate = jnp.minimum(gu[:, :d_ff], SWIGLU_LIMIT)
            lin = jnp.clip(gu[:, d_ff:], -SWIGLU_LIMIT, SWIGLU_LIMIT)
            glu = gate * jax.nn.sigmoid(SWIGLU_ALPHA * gate)
            return dict(s, gu=None, hid=((lin + 1.0) * glu).astype(BF16))

        def down(s):
            o_ref[s["rows"], :] = _pack_bf16_pairs(_dot(s["hid"], wdn_s[...]) + bdn_ref[...])
            return dict(s, hid=None)

        def run(n_chains, n_rows):
            _run_skewed([up, act, down], [dict(rows=rs) for rs in _row_chains(n_rows, n_chains)])

        @pl.when(hf_ref[i] == 0)
        def _():
            run(2, TM_MOE)

        @pl.when(hf_ref[i] == 1)
        def _():
            run(1, TM_MOE // 2)


def _moe(xs, plan, w_gu, b_gu, w_dn, b_dn, layer):
    n_rows = xs.shape[0]
    depth, n_e, d, ff2 = w_gu.shape
    d_ff = ff2 // 2
    nt = n_rows // TM_MOE
    n_plan = len(plan)
    row_spec = pl.BlockSpec((TM_MOE, d // 2), lambda i, *p: (jnp.minimum(i, p[-1][0] - 1), 0))
    bias_spec = lambda w: pl.BlockSpec((None, None, 1, w), lambda i, *p: (layer, p[0][i], 0, 0))
    grid_spec = pltpu.PrefetchScalarGridSpec(
        num_scalar_prefetch=n_plan, grid=(nt,),
        in_specs=[row_spec, pl.BlockSpec(memory_space=pl.ANY), bias_spec(ff2),
                  pl.BlockSpec(memory_space=pl.ANY), bias_spec(d)],
        out_specs=row_spec,
        scratch_shapes=[pltpu.VMEM((2, d, ff2), F32), pltpu.VMEM((2, d_ff, d), F32),
                        pltpu.VMEM((d, ff2), BF16), pltpu.VMEM((d_ff, d), BF16),
                        pltpu.SemaphoreType.DMA((2, 2))],
    )
    return pl.pallas_call(
        functools.partial(_moe_kernel, layer=layer), grid_spec=grid_spec,
        out_shape=jax.ShapeDtypeStruct((n_rows, d // 2), jnp.int32),
        compiler_params=_cparams(1), name="moe_experts",
    )(*plan, xs, w_gu, b_gu.reshape(depth, n_e, 1, ff2), w_dn, b_dn.reshape(depth, n_e, 1, d))


def _combine_kernel(rows_ref, x_ref, y0_ref, y1_ref, y2_ref, y3_ref, gate_ref, gf_ref, lg_ref, lb_ref, *o_refs,
                    alpha, n_ctx_tiles):
    del rows_ref
    out = _ffn_residual(x_ref[...], [r[...] for r in (y0_ref, y1_ref, y2_ref, y3_ref)], gate_ref[...], gf_ref[...],
                        lg_ref[...], lb_ref[...], alpha)
    if n_ctx_tiles is None:
        o_refs[0][...] = out
    else:
        @pl.when(pl.program_id(0) < n_ctx_tiles)
        def _():
            o_refs[0][...] = out

        @pl.when(pl.program_id(0) >= n_ctx_tiles)
        def _():
            o_refs[1][...] = out


def _combine(x1, y4, gate, mod3, mod_base, rows, ln_g, ln_b, alpha, t_ctx=None):
    t, d = x1.shape
    nt = rows.shape[0]
    tm = t // nt
    y_spec = lambda k: pl.BlockSpec((None, tm, d // 2), lambda i, r: (k, i, 0))
    if t_ctx is None:
        n_ctx_tiles = None
        out_specs = pl.BlockSpec((tm, d), lambda i, r: (i, 0))
        out_shape = jax.ShapeDtypeStruct((t, d), F32)
    else:
        n_ctx_tiles = t_ctx // tm
        out_specs = [pl.BlockSpec((tm, d), lambda i, r: (jnp.minimum(i, n_ctx_tiles - 1), 0)),
                     pl.BlockSpec((tm, d), lambda i, r: (jnp.maximum(i - n_ctx_tiles, 0), 0))]
        out_shape = [jax.ShapeDtypeStruct((t_ctx, d), F32), jax.ShapeDtypeStruct((t - t_ctx, d), F32)]
    grid_spec = pltpu.PrefetchScalarGridSpec(
        num_scalar_prefetch=1, grid=(nt,),
        in_specs=[pl.BlockSpec((tm, d), lambda i, r: (i, 0)), y_spec(0), y_spec(1), y_spec(2), y_spec(3),
                  pl.BlockSpec((tm, LANES), lambda i, r: (i, 0)),
                  pl.BlockSpec((None, 1, d), lambda i, r: (mod_base + 5 * 16 + r[i], 0, 0)),
                  pl.BlockSpec((1, d), lambda i, r: (0, 0)), pl.BlockSpec((1, d), lambda i, r: (0, 0))],
        out_specs=out_specs,
    )
    return pl.pallas_call(
        functools.partial(_combine_kernel, alpha=alpha, n_ctx_tiles=n_ctx_tiles), grid_spec=grid_spec,
        out_shape=out_shape, compiler_params=_cparams(1), name="moe_combine",
    )(rows, x1, y4, y4, y4, y4, gate, mod3, ln_g, ln_b)


def _route_plan(route, counts, n_tiles):
    idx, rank = route[:TOP_K], route[TOP_K:]
    padded = (counts + TM_MOE - 1) // TM_MOE * TM_MOE
    pend = jnp.cumsum(padded)
    pstart = pend - padded
    onehot = idx[:, :, None] == jnp.arange(N_EXPERTS, dtype=jnp.int32)[None, None, :]
    dest = jnp.sum(jnp.where(onehot, pstart[None, None, :], 0), axis=-1) + rank
    n_used = (pend[-1] // TM_MOE).astype(jnp.int32)
    tile_start = jnp.arange(n_tiles, dtype=jnp.int32) * TM_MOE
    blk_e = jnp.sum((tile_start[:, None] >= pend[None, :]).astype(jnp.int32), axis=1)
    last_e = jnp.sum((jnp.maximum(n_used - 1, 0) * TM_MOE >= pend).astype(jnp.int32))
    blk_e = jnp.minimum(jnp.where(jnp.arange(n_tiles) < n_used, blk_e, last_e), N_EXPERTS - 1).astype(jnp.int32)
    blk_first = jnp.concatenate([jnp.ones((1,), jnp.int32), (blk_e[1:] != blk_e[:-1]).astype(jnp.int32)])
    ar = jnp.arange(N_EXPERTS, dtype=jnp.int32)
    later = jnp.logical_and((counts > 0)[None, :], ar[None, :] > ar[:, None])
    next_e = jnp.min(jnp.where(later, ar[None, :], N_EXPERTS), axis=1)
    next_e = jnp.where(next_e == N_EXPERTS, -1, next_e)
    blk_next = jnp.sum(jnp.where(blk_e[:, None] == ar[None, :], next_e[None, :], 0), axis=1).astype(jnp.int32)
    blk_slot = ((jnp.cumsum(blk_first) - 1) % 2).astype(jnp.int32)
    left = jnp.sum(jnp.where(blk_e[:, None] == ar[None, :], (pstart + counts)[None, :], 0), axis=1) - tile_start
    blk_half = (left <= TM_MOE // 2).astype(jnp.int32)
    return dest, (blk_e, blk_first, blk_next, blk_slot, blk_half, n_used.reshape(1))


SC_CORES = 2
SC_SUBCORES = 16


def _sc_mesh():
    return plsc.VectorSubcoreMesh(core_axis_name="core", subcore_axis_name="subcore")


def _sc_scatter_rows(src, dest, n_rows, window):
    n_slots, t = dest.shape
    w = src.shape[1]
    per_worker = t // (SC_CORES * SC_SUBCORES)
    assert per_worker * SC_CORES * SC_SUBCORES == t and per_worker % window == 0

    @functools.partial(pl.kernel, out_type=jax.ShapeDtypeStruct((n_rows, w), src.dtype), mesh=_sc_mesh(),
                       scratch_types=[pltpu.VMEM((n_slots, window), jnp.int32), pltpu.VMEM((window, w), src.dtype)],
                       name="sc_dispatch")
    def scatter(src_hbm, dest_hbm, out_hbm, idx_v, rows_v):
        worker = lax.axis_index("subcore") * SC_CORES + lax.axis_index("core")

        @pl.loop(0, per_worker // window)
        def _(c):
            base = pl.multiple_of(worker * per_worker + c * window, window)
            pltpu.sync_copy(src_hbm.at[pl.ds(base, window)], rows_v)
            pltpu.sync_copy(dest_hbm.at[:, pl.ds(base, window)], idx_v)
            for k in range(n_slots):
                pltpu.sync_copy(rows_v, out_hbm.at[idx_v.at[k]])

    return scatter(src, dest)


def _sc_gather_rows(table, idx, window):
    n = idx.shape[0]
    w = table.shape[1]
    per_worker = n // (SC_CORES * SC_SUBCORES)
    assert per_worker * SC_CORES * SC_SUBCORES == n and per_worker % window == 0

    @functools.partial(pl.kernel, out_type=jax.ShapeDtypeStruct((n, w), table.dtype), mesh=_sc_mesh(),
                       scratch_types=[pltpu.VMEM((window,), jnp.int32), pltpu.VMEM((window, w), table.dtype)],
                       name="sc_collect")
    def gather(table_hbm, idx_hbm, out_hbm, idx_v, rows_v):
        worker = lax.axis_index("subcore") * SC_CORES + lax.axis_index("core")

        @pl.loop(0, per_worker // window)
        def _(c):
            base = pl.multiple_of(worker * per_worker + c * window, window)
            pltpu.sync_copy(idx_hbm.at[pl.ds(base, window)], idx_v)
            pltpu.sync_copy(table_hbm.at[idx_v], rows_v)
            pltpu.sync_copy(rows_v, out_hbm.at[pl.ds(base, window)])

    return gather(table, idx)


def _dft_tables(n):
    k = jnp.arange(n, dtype=jnp.int32)

    def trig(rows):
        ang = ((rows[:, None] * k[None, :]) % n).astype(F32) * (2.0 * math.pi / n)
        return jnp.cos(ang), jnp.sin(ang)

    n2 = 64
    if n < 4 * n2:
        return trig(k)
    ca, sa = (z[:, None, :] for z in trig(jnp.arange(n // n2, dtype=jnp.int32) * n2))
    cb, sb = (z[None, :, :] for z in trig(jnp.arange(n2, dtype=jnp.int32)))
    return (ca * cb - sa * sb).reshape(n, n), (sa * cb + ca * sb).reshape(n, n)


def _rope_tables(n_lat, n_ctx_rows):
    half, quarter = HEAD_DIM // 2, HEAD_DIM // 4
    tpos = jnp.arange(n_lat, dtype=jnp.int32)
    lane = jnp.arange(LANES, dtype=jnp.int32) % HEAD_DIM
    pos = jnp.where(lane[None, :] < half, (tpos // GRID_W)[:, None], (tpos % GRID_W)[:, None]).astype(F32)
    fidx = (lane % quarter).astype(F32)
    freqs = ROPE_THETA ** (-fidx / quarter)
    ang = pos * freqs[None, :]
    cos, sin = jnp.cos(ang), jnp.sin(ang)
    first = (lane % half) < quarter
    sa = jnp.where(first[None, :], -sin, 0.0)
    sb = jnp.where(first[None, :], 0.0, sin)
    ident = lambda v: jnp.full((n_ctx_rows, LANES), v, F32)
    return (jnp.concatenate([ident(1.0), cos]), jnp.concatenate([ident(0.0), sa]), jnp.concatenate([ident(0.0), sb]))


def kernel(x_prompt, x_sample, cache_k_ab, cache_v_ab, c, c_ctx, w_ada, b_ada, ln_mix_g, ln_mix_b, ln_ffn_g, ln_ffn_b, w_in_ab, w_out_ab, sink_ab, w_in_c, b_in_c, ln_v_g, ln_v_b, w_sp, b_sp, w_out_c, w_router, b_router, w_gu, b_gu, w_dn, b_dn):
    n_ctx_b, ctx_seq, d = x_prompt.shape
    n_lat_b, lat_seq, _ = x_sample.shape
    depth = w_ada.shape[0]
    t_ctx, t_lat = n_ctx_b * ctx_seq, n_lat_b * lat_seq
    t = t_ctx + t_lat
    alpha = (2 * depth) ** 0.25
    kv_w = N_KV_HEADS * HEAD_DIM
    a_w = d // 4
    q_w = d - a_w
    assert t_ctx % TM == 0 and lat_seq % TM == 0 and t_ctx % lat_seq == 0 and n_lat_b + 1 <= 16

    cond_rows = lambda tm: jnp.asarray(np.concatenate(
        [np.zeros(t_ctx // tm, np.int32), 1 + np.arange(t_lat // tm, dtype=np.int32) // (lat_seq // tm)]))
    rows, rows_c = cond_rows(TM), cond_rows(TM_COMBINE)
    rblk = jnp.asarray(np.concatenate(
        [np.zeros(t_ctx // TM, np.int32), 1 + np.arange(t_lat // TM, dtype=np.int32) % (lat_seq // TM)]))

    cond16 = jnp.zeros((16, d), F32).at[0].set(c_ctx).at[1:1 + n_lat_b].set(c)
    mod = _ada(cond16, w_ada, b_ada)
    mod3 = mod.reshape(depth, 16, 6, d).transpose(0, 2, 1, 3).reshape(depth * 6 * 16, 1, d)

    cos, sa, sb = _rope_tables(lat_seq, TM)
    cs_ctx, ss_ctx = (z.astype(BF16) for z in _dft_tables(ctx_seq))
    cs_lat, ss_lat = (z.astype(BF16) for z in _dft_tables(lat_seq))
    gd = a_w // A_GROUPS
    cd, sd = _dft_tables(gd)
    eye = jnp.eye(A_GROUPS, dtype=F32)
    bdc, bds = jnp.kron(eye, cd).astype(BF16), jnp.kron(eye, sd).astype(BF16)

    past = cache_k_ab.shape[2]
    cache_k = cache_k_ab.reshape(n_lat_b, -1, past, kv_w).astype(BF16)
    cache_v = cache_v_ab.reshape(n_lat_b, -1, past, kv_w).astype(BF16)

    x = (x_prompt.reshape(t_ctx, d), x_sample.reshape(t_lat, d))
    row2 = lambda v: v.reshape(1, -1)
    ks, vs = [], []
    for l in range(depth):
        j = l // 2
        mod_base = l * 6 * 16
        w_r, b_r = w_router[l].T, b_router[l].reshape(-1, 1)
        if l % 2 == 0:
            ac, as_, q, k32, v32, k, v = _inproj(x, t_ctx // TM, mod3, mod_base, rows, rblk, w_in_ab[j].astype(BF16),
                                                 cos, sa, sb, bdc, bds, a_w, q_w, kv_w)
            ks.append(k32[:t_ctx].reshape(n_ctx_b, ctx_seq, N_KV_HEADS, HEAD_DIM))
            vs.append(v32[:t_ctx].reshape(n_ctx_b, ctx_seq, N_KV_HEADS, HEAD_DIM))
            ya_p = _fourier(ac, as_, cs_ctx, ss_ctx, n_ctx_b, ctx_seq, 0, min(ctx_seq, 512))
            ya_s = _fourier(ac, as_, cs_lat, ss_lat, n_lat_b, lat_seq, t_ctx // lat_seq, min(lat_seq, 1024))
            yb_p = _attn_ctx(sink_ab[j], q, k, v, n_ctx_b, ctx_seq)
            yb_s = _attn_lat(sink_ab[j], q, k, v, cache_k, cache_v, j, n_lat_b, lat_seq, t_ctx)
            x1, hf, route, gate, cnt = _post_ab(x, ya_p, yb_p, ya_s, yb_s, w_out_ab[j].astype(BF16), mod3, mod_base,
                                                rows, row2(ln_mix_g[l]), row2(ln_mix_b[l]), w_r, b_r, alpha)
        else:
            x1, hf, route, gate, cnt = _gmlp(prev, mod3, mod_base, rows, w_in_c[j].astype(BF16), row2(b_in_c[j]),
                                             row2(ln_v_g[j]), row2(ln_v_b[j]), w_sp[j].astype(BF16),
                                             b_sp[j][:, :, None], w_out_c[j].astype(BF16), row2(ln_mix_g[l]),
                                             row2(ln_mix_b[l]), w_r, b_r, alpha)
        n_rows = t * TOP_K + N_EXPERTS * TM_MOE
        dest, plan = _route_plan(route, cnt[:, 0].astype(jnp.int32), n_rows // TM_MOE)
        xs = _sc_scatter_rows(hf, dest, n_rows, 128)
        out_sorted = _moe(xs, plan, w_gu, b_gu, w_dn, b_dn, l)
        y4 = _sc_gather_rows(out_sorted, dest.reshape(-1), 128).reshape(TOP_K, t, d // 2)
        if l % 2 == 0 and l + 1 < depth:
            prev = (x1, y4, gate, row2(ln_ffn_g[l]), row2(ln_ffn_b[l]))
        else:
            x = _combine(x1, y4, gate, mod3, mod_base, rows_c, row2(ln_ffn_g[l]), row2(ln_ffn_b[l]), alpha,
                         t_ctx if l == depth - 1 else None)

    y_prompt = x[0].reshape(n_ctx_b, ctx_seq, d)
    y_sample = x[1].reshape(n_lat_b, lat_seq, d)
    return (y_prompt, y_sample, jnp.stack(ks, axis=1), jnp.stack(vs, axis=1))
```
